```python
import jax, jax.numpy as jnp
from jax import lax
import numpy as np

D_MODEL = 1024
BATCH = 32
SEQ = 2048
DEPTH = 1
DEC_BATCH = 16
DEC_SEQ = 32
PAST_LEN = 2048

CHUNK = 64
CONV_DIM = D_MODEL // 2
CONV_W = 3
N_HEADS = 8
HEAD_DIM = 64
ATT_DIM = N_HEADS * HEAD_DIM
QBLOCK = 128
FORGET_BIAS = 2.0
LN_EPS = 1e-5
ALPHA = (2.0 * DEPTH) ** 0.25
BETA = (8.0 * DEPTH) ** -0.25
IN_COLS = 4 * CONV_DIM + 3 * ATT_DIM + N_HEADS + ATT_DIM + 2 * D_MODEL

kernel_name = "hybrid_shortconv_fox_stream_step"


def _split_points():
    sizes = [CONV_DIM] * 4 + [ATT_DIM] * 3 + [N_HEADS, ATT_DIM]
    pts, acc = [], 0
    for s in sizes:
        acc += s
        pts.append(acc)
    return pts


def _layer_norm(x, g, b):
    xf = x.astype(jnp.float32)
    mu = jnp.mean(xf, axis=-1, keepdims=True)
    var = jnp.mean(jnp.square(xf - mu), axis=-1, keepdims=True)
    y = (xf - mu) * lax.rsqrt(var + LN_EPS) * g.astype(jnp.float32) + b.astype(jnp.float32)
    return y.astype(x.dtype)


def _fox_attention(q, k, v, cq, ck, q_offset):
    b, tq, h, dh = q.shape
    tk = k.shape[1]
    qb = min(QBLOCK, tq)
    nb = tq // qb
    scale = dh ** -0.5
    q_blocks = q.reshape(b, nb, qb, h, dh).swapaxes(0, 1)
    cq_blocks = cq.reshape(b, nb, qb, h).swapaxes(0, 1)
    q_pos = (q_offset + jnp.arange(tq, dtype=jnp.int32)).reshape(nb, qb)
    k_pos = jnp.arange(tk, dtype=jnp.int32)
    ck_t = ck.swapaxes(1, 2)

    def one_block(args):
        qblk, cblk, pblk = args
        logits = jnp.einsum('bqhd,bkhd->bhqk', qblk, k, preferred_element_type=jnp.float32) * scale
        logits = logits + cblk.swapaxes(1, 2)[..., None] - ck_t[:, :, None, :]
        mask = k_pos[None, :] <= pblk[:, None]
        logits = jnp.where(mask, logits, -jnp.inf)
        probs = jax.nn.softmax(logits, axis=-1)
        return jnp.einsum('bhqk,bkhd->bqhd', probs.astype(v.dtype), v)

    out = lax.map(one_block, (q_blocks, cq_blocks, q_pos))
    return out.swapaxes(0, 1).reshape(b, tq, h, dh)


def _trunk_layer(x, w_in, b_f, conv_w, w_proj_a, w_proj_b, w_out, ln_g, ln_b,
                 conv_prev, past_k, past_v, past_logf):
    bsz, t, _ = x.shape
    proj = x @ w_in
    x_a, gate_b, gate_c, z_a, q, k, v, f_logit, z_b, merge = jnp.split(proj, _split_points(), axis=-1)

    u = gate_c * x_a
    if conv_prev is None:
        conv_prev = jnp.zeros((bsz, CONV_W - 1, CONV_DIM), u.dtype)
    full = jnp.concatenate([conv_prev.astype(u.dtype), u], axis=1)
    conv = sum(full[:, i:i + t] * conv_w[i] for i in range(CONV_W))
    y_a = gate_b * conv * jax.nn.silu(z_a)
    new_conv = full[:, -(CONV_W - 1):]

    q = q.reshape(bsz, t, N_HEADS, HEAD_DIM)
    k = k.reshape(bsz, t, N_HEADS, HEAD_DIM)
    v = v.reshape(bsz, t, N_HEADS, HEAD_DIM)
    logf = jax.nn.log_sigmoid(f_logit.astype(jnp.float32) + b_f.astype(jnp.float32))
    if past_k is None:
        c_new = jnp.cumsum(logf, axis=1)
        keys, vals, ck, offset = k, v, c_new, 0
    else:
        c_past = jnp.cumsum(past_logf.astype(jnp.float32), axis=1)
        c_new = c_past[:, -1:] + jnp.cumsum(logf, axis=1)
        keys = jnp.concatenate([past_k.astype(k.dtype), k], axis=1)
        vals = jnp.concatenate([past_v.astype(v.dtype), v], axis=1)
        ck = jnp.concatenate([c_past, c_new], axis=1)
        offset = past_k.shape[1]
    att = _fox_attention(q, keys, vals, c_new, ck, offset).reshape(bsz, t, ATT_DIM)
    y_b = att * jax.nn.silu(z_b)

    g = jax.nn.sigmoid(merge)
    g_a, g_b = g[..., :D_MODEL], g[..., D_MODEL:]
    mixed = g_a * (y_a @ w_proj_a) + g_b * (y_b @ w_proj_b)
    sub = mixed @ w_out
    y = _layer_norm(ALPHA * x + sub, ln_g, ln_b)
    return y, k, v, logf.astype(x.dtype), new_conv


def _run_trunk(x, ln_in_g, ln_in_b, w_in, b_f, conv_w, w_proj_a, w_proj_b, w_out, ln_g, ln_b,
               cache_k=None, cache_v=None, cache_logf=None, state_conv=None):
    h = _layer_norm(x, ln_in_g, ln_in_b)
    ks, vs, lfs, cs = [], [], [], []
    for l in range(DEPTH):
        has_past = cache_k is not None
        h, k, v, lf, cv = _trunk_layer(
            h, w_in[l], b_f[l], conv_w[l], w_proj_a[l], w_proj_b[l], w_out[l], ln_g[l], ln_b[l],
            state_conv[l] if has_past else None,
            cache_k[l] if has_past else None,
            cache_v[l] if has_past else None,
            cache_logf[l] if has_past else None)
        ks.append(k); vs.append(v); lfs.append(lf); cs.append(cv)
    return h, jnp.stack(ks), jnp.stack(vs), jnp.stack(lfs), jnp.stack(cs)


def setup_inputs(seed: int = 0) -> dict:
    key = jax.random.key(seed)
    ks = jax.random.split(key, 18)
    n = jax.random.normal
    f32 = jnp.float32
    return {
        "x_prompt": n(ks[0], (BATCH, SEQ, D_MODEL), f32),
        "x_sample": n(ks[1], (DEC_BATCH, DEC_SEQ, D_MODEL), f32),
        "cache_k": n(ks[2], (DEPTH, DEC_BATCH, PAST_LEN, N_HEADS, HEAD_DIM), f32),
        "cache_v": n(ks[3], (DEPTH, DEC_BATCH, PAST_LEN, N_HEADS, HEAD_DIM), f32),
        "cache_logf": jax.nn.log_sigmoid(FORGET_BIAS + n(ks[4], (DEPTH, DEC_BATCH, PAST_LEN, N_HEADS), f32)),
        "state_conv": n(ks[5], (DEPTH, DEC_BATCH, CONV_W - 1, CONV_DIM), f32),
        "ln_in_g": 1.0 + 0.02 * n(ks[6], (D_MODEL,), f32),
        "ln_in_b": 0.02 * n(ks[7], (D_MODEL,), f32),
        "w_in": n(ks[8], (DEPTH, D_MODEL, IN_COLS), f32) * D_MODEL ** -0.5,
        "b_f": FORGET_BIAS + 0.1 * n(ks[9], (DEPTH, N_HEADS), f32),
        "conv_w": n(ks[10], (DEPTH, CONV_W, CONV_DIM), f32) * CONV_W ** -0.5,
        "w_proj_a": n(ks[11], (DEPTH, CONV_DIM, D_MODEL), f32) * (CONV_DIM ** -0.5 * BETA),
        "w_proj_b": n(ks[12], (DEPTH, ATT_DIM, D_MODEL), f32) * (ATT_DIM ** -0.5 * BETA),
        "w_out": n(ks[13], (DEPTH, D_MODEL, D_MODEL), f32) * (D_MODEL ** -0.5 * BETA),
        "ln_g": 1.0 + 0.02 * n(ks[14], (DEPTH, D_MODEL), f32),
        "ln_b": 0.02 * n(ks[15], (DEPTH, D_MODEL), f32),
    }


def reference(x_prompt, x_sample, cache_k, cache_v, cache_logf, state_conv,
              ln_in_g, ln_in_b, w_in, b_f, conv_w, w_proj_a, w_proj_b, w_out, ln_g, ln_b):
    y_prompt, k_prompt, v_prompt, logf_prompt, conv_prompt = _run_trunk(
        x_prompt, ln_in_g, ln_in_b, w_in, b_f, conv_w, w_proj_a, w_proj_b, w_out, ln_g, ln_b)
    y_sample, k_sample, v_sample, logf_sample, conv_sample = _run_trunk(
        x_sample, ln_in_g, ln_in_b, w_in, b_f, conv_w, w_proj_a, w_proj_b, w_out, ln_g, ln_b,
        cache_k, cache_v, cache_logf, state_conv)
    return (y_prompt, y_sample, k_prompt, v_prompt, logf_prompt, conv_prompt,
            k_sample, v_sample, logf_sample, conv_sample)
```

```python
import functools

import numpy as np
import jax
import jax.numpy as jnp
from jax import lax
from jax.experimental import pallas as pl
from jax.experimental.pallas import tpu as pltpu

D_MODEL = 1024
CONV_DIM = 512
CONV_W = 3
N_HEADS = 8
HEAD_DIM = 64
ATT_DIM = N_HEADS * HEAD_DIM
LN_EPS = 1e-5
DEPTH = 1
ALPHA = (2.0 * DEPTH) ** 0.25
SCALE = HEAD_DIM ** -0.5

LANES = 128
HEAD_BLOCK = LANES
AUG_DIM = N_HEADS * HEAD_BLOCK
N_SPLIT = 3
VMEM_LIMIT_BYTES = 56 * 1024 * 1024

_C_XA, _C_GB, _C_GC, _C_ZA, _C_Q, _C_K, _C_V, _C_ZB, _C_GA, _C_GM = (
    0, 512, 1024, 1536, 2048, 2560, 3072, 3584, 4096, 5120)
MAIN_COLS = 6144
F_COL0 = 4 * CONV_DIM + 3 * ATT_DIM

F32 = jnp.float32
BF16 = jnp.bfloat16


def _routing_constants():
    route = np.zeros((LANES, 2 * AUG_DIM), np.float32)
    ones = np.zeros((1, 2 * AUG_DIM), np.float32)
    for h in range(N_HEADS):
        base = h * HEAD_BLOCK + (HEAD_DIM if h % 2 == 0 else 0)
        for r in range(N_SPLIT):
            route[r * N_HEADS + h, base + 2 * r] = 1.0
            ones[0, base + 2 * r + 1] = 1.0
            ones[0, AUG_DIM + base + 2 * r] = 1.0
            route[r * N_HEADS + h, AUG_DIM + base + 2 * r + 1] = -1.0
    return route, ones


def _layer_norm(x, g, b):
    mu = jnp.mean(x, axis=-1, keepdims=True)
    xc = x - mu
    var = jnp.mean(xc * xc, axis=-1, keepdims=True)
    return xc * lax.rsqrt(var + LN_EPS) * g + b


def _log_sigmoid(x):
    return jnp.minimum(x, 0.0) - jnp.log1p(jnp.exp(-jnp.abs(x)))


def _split3(x):
    hi = x.astype(BF16)
    r1 = x - hi.astype(F32)
    mid = r1.astype(BF16)
    lo = (r1 - mid.astype(F32)).astype(BF16)
    return hi, mid, lo


def _cumsum_rows(x):
    n = x.shape[0]
    row = lax.broadcasted_iota(jnp.int32, x.shape, 0)
    s = 1
    while s < n:
        x = x + jnp.where(row >= s, pltpu.roll(x, s, 0), 0.0)
        s *= 2
    return x


def _extras(c, route):
    lane = lax.broadcasted_iota(jnp.int32, c.shape, 1)
    valid = lane < N_HEADS
    packed = jnp.zeros_like(c)
    for r, piece in enumerate(_split3(c)):
        p32 = jnp.where(valid, piece.astype(F32), 0.0)
        packed = packed + (pltpu.roll(p32, r * N_HEADS, 1) if r else p32)
    return jnp.dot(packed.astype(BF16), route, preferred_element_type=F32)


def _store_augmented(dst_ref, rows, data, extras):
    n = data.shape[0]
    lane = lax.broadcasted_iota(jnp.int32, (n, HEAD_BLOCK), 1)
    for h in range(N_HEADS):
        pair = data[:, (h // 2) * HEAD_BLOCK:(h // 2 + 1) * HEAD_BLOCK]
        ex = extras[:, h * HEAD_BLOCK:(h + 1) * HEAD_BLOCK]
        keep = (lane < HEAD_DIM) if h % 2 == 0 else (lane >= HEAD_DIM)
        dst_ref[rows, h * HEAD_BLOCK:(h + 1) * HEAD_BLOCK] = jnp.where(keep, pair, ex).astype(BF16)


def _attend(h, qa_h, ka, v_pair, m_sc, l_sc, acc_sc, *, mask=None, first=False):
    s = lax.dot_general(qa_h, ka, (((1,), (1,)), ((), ())), preferred_element_type=F32)
    if mask is not None:
        s = jnp.where(mask, s, -jnp.inf)
    m_cur = jnp.max(s, axis=-1, keepdims=True)
    if first:
        m_new = m_cur
        p = jnp.exp(s - m_new)
        l_sc[h] = jnp.sum(p, axis=-1, keepdims=True)
        acc_sc[h] = jnp.dot(p.astype(BF16), v_pair, preferred_element_type=F32)
    else:
        m_prev = m_sc[h]
        m_new = jnp.maximum(m_prev, m_cur)
        alpha = jnp.exp(m_prev - m_new)
        p = jnp.exp(s - m_new)
        l_sc[h] = alpha * l_sc[h] + jnp.sum(p, axis=-1, keepdims=True)
        acc_sc[h] = alpha * acc_sc[h] + jnp.dot(p.astype(BF16), v_pair, preferred_element_type=F32)
    m_sc[h] = m_new


def _trunk_kernel(*refs, tm, n_past, past_chunk):
    if n_past:
        (x_ref, conv0_ref, pk_ref, pv_ref, plf_ref, *rest) = refs
    else:
        (x_ref, conv0_ref, *rest) = refs
    (ln_in_g, ln_in_b, w_main, w_f, b_f, conv_w, w_pa, w_pb, w_out, ln_g, ln_b, route, ones,
     y_ref, k_ref, v_ref, lf_ref, conv_ref,
     kaug_sc, v_sc, h_sc, hb_sc, mix_sc, qa_sc, att_sc, m_sc, l_sc, acc_sc, u_sc, carry_sc) = rest

    j = pl.program_id(1)
    n_chunks = n_past // past_chunk

    @pl.when(j == 0)
    def _start_of_sequence():
        u_sc[0:8, :] = jnp.zeros((8, CONV_DIM), F32)
        u_sc[8 - (CONV_W - 1):8, :] = conv0_ref[...]
        carry_sc[...] = jnp.zeros_like(carry_sc)
        for c in range(n_chunks):
            rows = pl.ds(c * past_chunk, past_chunk)
            cs = _cumsum_rows(plf_ref[rows, :]) + carry_sc[0:1, :]
            carry_sc[...] = jnp.broadcast_to(cs[past_chunk - 1:past_chunk, :], carry_sc.shape)
            ex = _extras(cs, route[:, AUG_DIM:]) + ones[:, AUG_DIM:]
            _store_augmented(kaug_sc, rows, pk_ref[rows, :], ex)
            v_sc[rows, :] = pv_ref[rows, :].astype(BF16)

    h = _layer_norm(x_ref[...], ln_in_g[...], ln_in_b[...])
    h_sc[...] = h
    hb_sc[...] = h.astype(BF16)

    def proj(col, width):
        return jnp.dot(hb_sc[...], w_main[:, col:col + width], preferred_element_type=F32)

    u = proj(_C_GC, CONV_DIM) * proj(_C_XA, CONV_DIM)
    u_sc[8:8 + tm, :] = u
    conv = sum(u_sc[8 - (CONV_W - 1) + i:8 - (CONV_W - 1) + i + tm, :] * conv_w[i:i + 1, :]
               for i in range(CONV_W))
    new_conv = u_sc[8 + tm - (CONV_W - 1):8 + tm, :]
    conv_ref[...] = new_conv
    u_sc[8 - (CONV_W - 1):8, :] = new_conv
    y_a = proj(_C_GB, CONV_DIM) * conv * jax.nn.silu(proj(_C_ZA, CONV_DIM))
    mix_sc[...] = jax.nn.sigmoid(proj(_C_GA, D_MODEL)) * jnp.dot(
        y_a.astype(BF16), w_pa[...], preferred_element_type=F32)

    k = proj(_C_K, ATT_DIM)
    v = proj(_C_V, ATT_DIM)
    k_ref[...] = k
    v_ref[...] = v
    logf = _log_sigmoid(jnp.dot(hb_sc[...], w_f[...], preferred_element_type=F32) + b_f[...])
    lf_ref[...] = logf[:, :N_HEADS]
    c_new = _cumsum_rows(logf) + carry_sc[0:1, :]
    carry_sc[...] = jnp.broadcast_to(c_new[tm - 1:tm, :], carry_sc.shape)
    ex = _extras(c_new, route[...]) + ones[...]
    new_rows = pl.ds(pl.multiple_of(n_past + j * tm, tm), tm)
    _store_augmented(qa_sc, slice(None), proj(_C_Q, ATT_DIM) * SCALE, ex[:, :AUG_DIM])
    _store_augmented(kaug_sc, new_rows, k, ex[:, AUG_DIM:])
    v_sc[new_rows, :] = v.astype(BF16)

    def head_operands(h, rows):
        return (qa_sc[:, h * HEAD_BLOCK:(h + 1) * HEAD_BLOCK],
                kaug_sc[rows, h * HEAD_BLOCK:(h + 1) * HEAD_BLOCK],
                v_sc[rows, (h // 2) * HEAD_BLOCK:(h // 2 + 1) * HEAD_BLOCK])

    causal = (lax.broadcasted_iota(jnp.int32, (tm, tm), 1)
              <= lax.broadcasted_iota(jnp.int32, (tm, tm), 0))
    for h in range(N_HEADS):
        _attend(h, *head_operands(h, new_rows), m_sc, l_sc, acc_sc, mask=causal, first=True)

    def earlier_tile(i, carry):
        rows = pl.ds(pl.multiple_of(n_past + i * tm, tm), tm)
        for h in range(N_HEADS):
            _attend(h, *head_operands(h, rows), m_sc, l_sc, acc_sc)
        return carry

    lax.fori_loop(0, j, earlier_tile, 0)
    for c in range(n_chunks):
        rows = pl.ds(c * past_chunk, past_chunk)
        for h in range(N_HEADS):
            _attend(h, *head_operands(h, rows), m_sc, l_sc, acc_sc)

    lane = lax.broadcasted_iota(jnp.int32, (tm, HEAD_BLOCK), 1)
    for p in range(N_HEADS // 2):
        att_sc[:, p * HEAD_BLOCK:(p + 1) * HEAD_BLOCK] = jnp.where(
            lane < HEAD_DIM, acc_sc[2 * p] / l_sc[2 * p], acc_sc[2 * p + 1] / l_sc[2 * p + 1])

    y_b = att_sc[...] * jax.nn.silu(proj(_C_ZB, ATT_DIM))
    mixed = mix_sc[...] + jax.nn.sigmoid(proj(_C_GM, D_MODEL)) * jnp.dot(
        y_b.astype(BF16), w_pb[...], preferred_element_type=F32)
    sub = jnp.dot(mixed.astype(BF16), w_out[...], preferred_element_type=F32)
    y_ref[...] = _layer_norm(ALPHA * h_sc[...] + sub, ln_g[...], ln_b[...])


def _resident(shape):
    return pl.BlockSpec(shape, lambda b, j: (0,) * len(shape), pipeline_mode=pl.Buffered(1))


def _run_trunk(x, conv0, past, params, *, tm, past_chunk):
    bsz, t, _ = x.shape
    n_past = past[0].shape[1] if past else 0
    n_tiles = t // tm
    assert n_tiles * tm == t and (n_past == 0 or n_past % past_chunk == 0)
    n_keys = n_past + t

    per_tile = lambda width: pl.BlockSpec((None, tm, width), lambda b, j: (b, j, 0))
    per_batch = lambda rows, width: pl.BlockSpec((None, rows, width), lambda b, j: (b, 0, 0))

    in_specs = [per_tile(D_MODEL), per_batch(CONV_W - 1, CONV_DIM)]
    if past:
        in_specs += [per_batch(n_past, ATT_DIM), per_batch(n_past, ATT_DIM), per_batch(n_past, LANES)]
    in_specs += [_resident(p.shape) for p in params]

    out_shape = (jax.ShapeDtypeStruct((bsz, t, D_MODEL), F32),
                 jax.ShapeDtypeStruct((bsz, t, ATT_DIM), F32),
                 jax.ShapeDtypeStruct((bsz, t, ATT_DIM), F32),
                 jax.ShapeDtypeStruct((bsz, t, N_HEADS), F32),
                 jax.ShapeDtypeStruct((bsz, CONV_W - 1, CONV_DIM), F32))
    out_specs = (per_tile(D_MODEL), per_tile(ATT_DIM), per_tile(ATT_DIM), per_tile(N_HEADS),
                 per_batch(CONV_W - 1, CONV_DIM))

    scratch = [
        pltpu.VMEM((n_keys, AUG_DIM), BF16),
        pltpu.VMEM((n_keys, ATT_DIM), BF16),
        pltpu.VMEM((tm, D_MODEL), F32),
        pltpu.VMEM((tm, D_MODEL), BF16),
        pltpu.VMEM((tm, D_MODEL), F32),
        pltpu.VMEM((tm, AUG_DIM), BF16),
        pltpu.VMEM((tm, ATT_DIM), F32),
        pltpu.VMEM((N_HEADS, tm, 1), F32),
        pltpu.VMEM((N_HEADS, tm, 1), F32),
        pltpu.VMEM((N_HEADS, tm, HEAD_BLOCK), F32),
        pltpu.VMEM((8 + tm, CONV_DIM), F32),
        pltpu.VMEM((8, LANES), F32),
    ]
    kern = functools.partial(_trunk_kernel, tm=tm, n_past=n_past, past_chunk=past_chunk)
    return pl.pallas_call(
        kern,
        grid=(bsz, n_tiles),
        in_specs=in_specs,
        out_specs=out_specs,
        out_shape=out_shape,
        scratch_shapes=scratch,
        compiler_params=pltpu.CompilerParams(
            dimension_semantics=("arbitrary", "arbitrary"),
            vmem_limit_bytes=VMEM_LIMIT_BYTES),
        name="trunk_prompt" if not past else "trunk_sample",
    )(x, conv0, *past, *params)


def kernel(x_prompt, x_sample, cache_k, cache_v, cache_logf, state_conv, ln_in_g, ln_in_b, w_in, b_f,
           conv_w, w_proj_a, w_proj_b, w_out, ln_g, ln_b):
    assert w_in.shape[0] == DEPTH
    route_np, ones_np = _routing_constants()
    w = w_in[0]
    row = lambda a: a.reshape(1, -1).astype(F32)
    params = (
        row(ln_in_g), row(ln_in_b),
        jnp.concatenate([w[:, :F_COL0], w[:, F_COL0 + N_HEADS:]], axis=1).astype(BF16),
        jnp.pad(w[:, F_COL0:F_COL0 + N_HEADS], ((0, 0), (0, LANES - N_HEADS))).astype(BF16),
        jnp.pad(row(b_f[0]), ((0, 0), (0, LANES - N_HEADS))),
        conv_w[0].astype(F32),
        w_proj_a[0].astype(BF16), w_proj_b[0].astype(BF16), w_out[0].astype(BF16),
        row(ln_g[0]), row(ln_b[0]),
        jnp.asarray(route_np, BF16), jnp.asarray(ones_np, F32),
    )
    bp, tp, _ = x_prompt.shape
    bs, ts, _ = x_sample.shape
    n_past = cache_k.shape[2]

    y_p, k_p, v_p, lf_p, cv_p = _run_trunk(
        x_prompt, jnp.zeros((bp, CONV_W - 1, CONV_DIM), F32), (), params, tm=256, past_chunk=256)
    past = (cache_k[0].reshape(bs, n_past, ATT_DIM), cache_v[0].reshape(bs, n_past, ATT_DIM),
            jnp.pad(cache_logf[0], ((0, 0), (0, 0), (0, LANES - N_HEADS))))
    y_s, k_s, v_s, lf_s, cv_s = _run_trunk(x_sample, state_conv[0], past, params, tm=ts, past_chunk=256)

    heads = lambda a: a.reshape(1, a.shape[0], a.shape[1], N_HEADS, HEAD_DIM)
    return (y_p, y_s, heads(k_p), heads(v_p), lf_p[None], cv_p[None],
            heads(k_s), heads(v_s), lf_s[None], cv_s[None])
```

```python
import functools

import numpy as np
import jax
import jax.numpy as jnp
from jax import lax
from jax.experimental import pallas as pl
from jax.experimental.pallas import tpu as pltpu

D_MODEL = 1024
CONV_DIM = 512
CONV_W = 3
N_HEADS = 8
HEAD_DIM = 64
ATT_DIM = N_HEADS * HEAD_DIM
LN_EPS = 1e-5
DEPTH = 1
ALPHA = (2.0 * DEPTH) ** 0.25
SCALE = HEAD_DIM ** -0.5

LANES = 128
HEAD_BLOCK = LANES
AUG_DIM = N_HEADS * HEAD_BLOCK
N_SPLIT = 3
VMEM_LIMIT_BYTES = 56 * 1024 * 1024

_C_XA, _C_GB, _C_GC, _C_ZA, _C_Q, _C_K, _C_V, _C_ZB, _C_GA, _C_GM = (
    0, 512, 1024, 1536, 2048, 2560, 3072, 3584, 4096, 5120)
MAIN_COLS = 6144
F_COL0 = 4 * CONV_DIM + 3 * ATT_DIM

F32 = jnp.float32
BF16 = jnp.bfloat16


def _routing_constants():
    route = np.zeros((LANES, 2 * AUG_DIM), np.float32)
    ones = np.zeros((1, 2 * AUG_DIM), np.float32)
    for h in range(N_HEADS):
        base = h * HEAD_BLOCK + (HEAD_DIM if h % 2 == 0 else 0)
        for r in range(N_SPLIT):
            route[r * N_HEADS + h, base + 2 * r] = 1.0
            ones[0, base + 2 * r + 1] = 1.0
            ones[0, AUG_DIM + base + 2 * r] = 1.0
            route[r * N_HEADS + h, AUG_DIM + base + 2 * r + 1] = -1.0
    return route, ones


def _layer_norm(x, g, b):
    mu = jnp.mean(x, axis=-1, keepdims=True)
    xc = x - mu
    var = jnp.mean(xc * xc, axis=-1, keepdims=True)
    return xc * lax.rsqrt(var + LN_EPS) * g + b


def _log_sigmoid(x):
    return jnp.minimum(x, 0.0) - jnp.log1p(jnp.exp(-jnp.abs(x)))


def _split3(x):
    hi = x.astype(BF16)
    r1 = x - hi.astype(F32)
    mid = r1.astype(BF16)
    lo = (r1 - mid.astype(F32)).astype(BF16)
    return hi, mid, lo


def _cumsum_rows(x):
    n = x.shape[0]
    row = lax.broadcasted_iota(jnp.int32, x.shape, 0)
    s = 1
    while s < n:
        x = x + jnp.where(row >= s, pltpu.roll(x, s, 0), 0.0)
        s *= 2
    return x


def _extras(c, route):
    lane = lax.broadcasted_iota(jnp.int32, c.shape, 1)
    valid = lane < N_HEADS
    packed = jnp.zeros_like(c)
    for r, piece in enumerate(_split3(c)):
        p32 = jnp.where(valid, piece.astype(F32), 0.0)
        packed = packed + (pltpu.roll(p32, r * N_HEADS, 1) if r else p32)
    return jnp.dot(packed.astype(BF16), route, preferred_element_type=F32)


def _store_augmented(dst_ref, rows, data, extras):
    n = data.shape[0]
    lane = lax.broadcasted_iota(jnp.int32, (n, HEAD_BLOCK), 1)
    for h in range(N_HEADS):
        pair = data[:, (h // 2) * HEAD_BLOCK:(h // 2 + 1) * HEAD_BLOCK]
        ex = extras[:, h * HEAD_BLOCK:(h + 1) * HEAD_BLOCK]
        keep = (lane < HEAD_DIM) if h % 2 == 0 else (lane >= HEAD_DIM)
        dst_ref[rows, h * HEAD_BLOCK:(h + 1) * HEAD_BLOCK] = jnp.where(keep, pair, ex).astype(BF16)


def _attend(h, qa_h, ka, v_pair, m_sc, l_sc, acc_sc, *, mask=None, first=False):
    s = lax.dot_general(qa_h, ka, (((1,), (1,)), ((), ())), preferred_element_type=F32)
    if mask is not None:
        s = jnp.where(mask, s, -jnp.inf)
    m_cur = jnp.max(s, axis=-1, keepdims=True)
    if first:
        m_new = m_cur
        p = jnp.exp(s - m_new)
        l_sc[h] = jnp.sum(p, axis=-1, keepdims=True)
        acc_sc[h] = jnp.dot(p.astype(BF16), v_pair, preferred_element_type=F32)
    else:
        m_prev = m_sc[h]
        m_new = jnp.maximum(m_prev, m_cur)
        alpha = jnp.exp(m_prev - m_new)
        p = jnp.exp(s - m_new)
        l_sc[h] = alpha * l_sc[h] + jnp.sum(p, axis=-1, keepdims=True)
        acc_sc[h] = alpha * acc_sc[h] + jnp.dot(p.astype(BF16), v_pair, preferred_element_type=F32)
    m_sc[h] = m_new


def _trunk_kernel(*refs, tm, n_past, past_chunk):
    if n_past:
        (x_ref, conv0_ref, pk_ref, pv_ref, plf_ref, *rest) = refs
    else:
        (x_ref, conv0_ref, *rest) = refs
    (ln_in_g, ln_in_b, w_main, w_f, b_f, conv_w, w_pa, w_pb, w_out, ln_g, ln_b, route, ones,
     y_ref, k_ref, v_ref, lf_ref, conv_ref,
     kaug_sc, v_sc, h_sc, hb_sc, mix_sc, qa_sc, att_sc, m_sc, l_sc, acc_sc, u_sc, carry_sc) = rest

    j = pl.program_id(1)
    n_chunks = n_past // past_chunk

    @pl.when(j == 0)
    def _start_of_sequence():
        u_sc[0:8, :] = jnp.zeros((8, CONV_DIM), F32)
        u_sc[8 - (CONV_W - 1):8, :] = conv0_ref[...]
        carry_sc[...] = jnp.zeros_like(carry_sc)
        for c in range(n_chunks):
            rows = pl.ds(c * past_chunk, past_chunk)
            cs = _cumsum_rows(plf_ref[rows, :]) + carry_sc[0:1, :]
            carry_sc[...] = jnp.broadcast_to(cs[past_chunk - 1:past_chunk, :], carry_sc.shape)
            ex = _extras(cs, route[:, AUG_DIM:]) + ones[:, AUG_DIM:]
            _store_augmented(kaug_sc, rows, pk_ref[rows, :], ex)
            v_sc[rows, :] = pv_ref[rows, :].astype(BF16)

    h = _layer_norm(x_ref[...], ln_in_g[...], ln_in_b[...])
    h_sc[...] = h
    hb_sc[...] = h.astype(BF16)

    def proj(col, width):
        return jnp.dot(hb_sc[...], w_main[:, col:col + width], preferred_element_type=F32)

    u = proj(_C_GC, CONV_DIM) * proj(_C_XA, CONV_DIM)
    u_sc[8:8 + tm, :] = u
    conv = sum(u_sc[8 - (CONV_W - 1) + i:8 - (CONV_W - 1) + i + tm, :] * conv_w[i:i + 1, :]
               for i in range(CONV_W))
    new_conv = u_sc[8 + tm - (CONV_W - 1):8 + tm, :]
    conv_ref[...] = new_conv
    u_sc[8 - (CONV_W - 1):8, :] = new_conv
    y_a = proj(_C_GB, CONV_DIM) * conv * jax.nn.silu(proj(_C_ZA, CONV_DIM))
    mix_sc[...] = jax.nn.sigmoid(proj(_C_GA, D_MODEL)) * jnp.dot(
        y_a.astype(BF16), w_pa[...], preferred_element_type=F32)

    k = proj(_C_K, ATT_DIM)
    v = proj(_C_V, ATT_DIM)
    k_ref[...] = k
    v_ref[...] = v
    logf = _log_sigmoid(jnp.dot(hb_sc[...], w_f[...], preferred_element_type=F32) + b_f[...])
    lf_ref[...] = logf[:, :N_HEADS]
    c_new = _cumsum_rows(logf) + carry_sc[0:1, :]
    carry_sc[...] = jnp.broadcast_to(c_new[tm - 1:tm, :], carry_sc.shape)
    ex = _extras(c_new, route[...]) + ones[...]
    new_rows = pl.ds(pl.multiple_of(n_past + j * tm, tm), tm)
    _store_augmented(qa_sc, slice(None), proj(_C_Q, ATT_DIM) * SCALE, ex[:, :AUG_DIM])
    _store_augmented(kaug_sc, new_rows, k, ex[:, AUG_DIM:])
    v_sc[new_rows, :] = v.astype(BF16)

    def head_operands(h, rows):
        return (qa_sc[:, h * HEAD_BLOCK:(h + 1) * HEAD_BLOCK],
                kaug_sc[rows, h * HEAD_BLOCK:(h + 1) * HEAD_BLOCK],
                v_sc[rows, (h // 2) * HEAD_BLOCK:(h // 2 + 1) * HEAD_BLOCK])

    causal = (lax.broadcasted_iota(jnp.int32, (tm, tm), 1)
              <= lax.broadcasted_iota(jnp.int32, (tm, tm), 0))
    for h in range(N_HEADS):
        _attend(h, *head_operands(h, new_rows), m_sc, l_sc, acc_sc, mask=causal, first=True)

    def earlier_tile(i, carry):
        rows = pl.ds(pl.multiple_of(n_past + i * tm, tm), tm)
        for h in range(N_HEADS):
            _attend(h, *head_operands(h, rows), m_sc, l_sc, acc_sc)
        return carry

    lax.fori_loop(0, j, earlier_tile, 0)
    for c in range(n_chunks):
        rows = pl.ds(c * past_chunk, past_chunk)
        for h in range(N_HEADS):
            _attend(h, *head_operands(h, rows), m_sc, l_sc, acc_sc)

    lane = lax.broadcasted_iota(jnp.int32, (tm, HEAD_BLOCK), 1)
    for p in range(N_HEADS // 2):
        att_sc[:, p * HEAD_BLOCK:(p + 1) * HEAD_BLOCK] = jnp.where(
            lane < HEAD_DIM, acc_sc[2 * p] / l_sc[2 * p], acc_sc[2 * p + 1] / l_sc[2 * p + 1])

    y_b = att_sc[...] * jax.nn.silu(proj(_C_ZB, ATT_DIM))
    mixed = mix_sc[...] + jax.nn.sigmoid(proj(_C_GM, D_MODEL)) * jnp.dot(
        y_b.astype(BF16), w_pb[...], preferred_element_type=F32)
    sub = jnp.dot(mixed.astype(BF16), w_out[...], preferred_element_type=F32)
    y_ref[...] = _layer_norm(ALPHA * h_sc[...] + sub, ln_g[...], ln_b[...])


_P_XA, _P_GB, _P_GC, _P_ZA, _P_K, _P_ZB, _P_GA, _P_GM = 0, 512, 1024, 1536, 2048, 2560, 3072, 4096
PROMPT_STD_COLS = 5120
_T_Q, _T_K, _T_V, _T_F = 0, 512, 1024, 1536
F_ROWS = 16
PROMPT_T_ROWS = _T_F + F_ROWS
EXTRA_ROWS = 16


def _cumsum_lanes(x):
    n = x.shape[1]
    lane = lax.broadcasted_iota(jnp.int32, x.shape, 1)
    s = 1
    while s < n:
        x = x + jnp.where(lane >= s, pltpu.roll(x, s, 1), 0.0)
        s *= 2
    return x


def _q_extra_rows(pieces, h, tm):
    r = lax.broadcasted_iota(jnp.int32, (EXTRA_ROWS, tm), 0)
    out = jnp.where((r < 2 * N_SPLIT) & ((r & 1) == 1), 1.0, 0.0)
    for i, piece in enumerate(pieces):
        out = jnp.where(r == 2 * i, jnp.broadcast_to(piece[h:h + 1, :], (EXTRA_ROWS, tm)), out)
    return out.astype(BF16)


SCORE_LOOKAHEAD = 2


def _attention_t(n_key_tiles, tm, kaug_sc, qta_sc, vt_sc, attt_sc):
    keys = n_key_tiles * tm
    causal = (lax.broadcasted_iota(jnp.int32, (tm, tm), 0)
              <= lax.broadcasted_iota(jnp.int32, (tm, tm), 1))

    def scores(hd):
        return jnp.dot(kaug_sc[0:keys, hd * HEAD_BLOCK:(hd + 1) * HEAD_BLOCK], qta_sc[hd],
                       preferred_element_type=F32)

    pending = [scores(hd) for hd in range(SCORE_LOOKAHEAD)]
    for hd in range(N_HEADS):
        s = pending.pop(0)
        if hd + SCORE_LOOKAHEAD < N_HEADS:
            pending.append(scores(hd + SCORE_LOOKAHEAD))
        blocks = [s[i * tm:(i + 1) * tm, :] for i in range(n_key_tiles)]
        blocks[-1] = jnp.where(causal, blocks[-1], -jnp.inf)
        m = functools.reduce(jnp.maximum, [jnp.max(blk, axis=0, keepdims=True) for blk in blocks])
        l = jnp.zeros((1, tm), F32)
        o = jnp.zeros((HEAD_DIM, tm), F32)
        for i, blk in enumerate(blocks):
            p = jnp.exp(blk - m)
            l = l + jnp.sum(p, axis=0, keepdims=True)
            o = o + jnp.dot(vt_sc[i, hd * HEAD_DIM:(hd + 1) * HEAD_DIM, :], p.astype(BF16),
                            preferred_element_type=F32)
        attt_sc[hd * HEAD_DIM:(hd + 1) * HEAD_DIM, :] = o / l


def _prompt_kernel(x_ref, ln_in_g, ln_in_b, w_std, w_fs, b_f_row, w_t, b_f_col, conv_w, w_pa, w_pb, w_out,
                   ln_g, ln_b, route_k, ones_k,
                   y_ref, kt_ref, vt_ref, lft_ref, conv_ref,
                   kaug_sc, vt_sc, h_sc, hb_sc, mix_sc, qta_sc, attt_sc, u_sc,
                   carry_sc, carryt_sc, *, tm, n_tiles):
    j = pl.program_id(1)

    @pl.when(j == 0)
    def _start_of_sequence():
        u_sc[0:8, :] = jnp.zeros((8, CONV_DIM), F32)
        carry_sc[...] = jnp.zeros_like(carry_sc)
        carryt_sc[...] = jnp.zeros_like(carryt_sc)
        qta_sc[...] = jnp.zeros_like(qta_sc)

    h = _layer_norm(x_ref[...], ln_in_g[...], ln_in_b[...])
    h_sc[...] = h
    hb_sc[...] = h.astype(BF16)

    def proj(col, width):
        return jnp.dot(hb_sc[...], w_std[:, col:col + width], preferred_element_type=F32)

    u = proj(_P_GC, CONV_DIM) * proj(_P_XA, CONV_DIM)
    u_sc[8:8 + tm, :] = u
    conv = sum(u_sc[8 - (CONV_W - 1) + i:8 - (CONV_W - 1) + i + tm, :] * conv_w[i:i + 1, :]
               for i in range(CONV_W))
    new_conv = u_sc[8 + tm - (CONV_W - 1):8 + tm, :]
    conv_ref[...] = new_conv
    u_sc[8 - (CONV_W - 1):8, :] = new_conv
    y_a = proj(_P_GB, CONV_DIM) * conv * jax.nn.silu(proj(_P_ZA, CONV_DIM))
    mix_sc[...] = jax.nn.sigmoid(proj(_P_GA, D_MODEL)) * jnp.dot(
        y_a.astype(BF16), w_pa[...], preferred_element_type=F32)

    proj_t = lax.dot_general(w_t[...], hb_sc[...], (((1,), (1,)), ((), ())), preferred_element_type=F32)
    kt = proj_t[_T_K:_T_K + ATT_DIM, :]
    vt = proj_t[_T_V:_T_V + ATT_DIM, :]
    kt_ref[...] = kt
    vt_ref[...] = vt
    vt_sc[j] = vt.astype(BF16)
    logf_t = _log_sigmoid(proj_t[_T_F:_T_F + F_ROWS, :] + b_f_col[...])
    lft_ref[...] = logf_t[:N_HEADS, :]
    c_t = _cumsum_lanes(logf_t) + carryt_sc[:, 0:1]
    carryt_sc[...] = jnp.broadcast_to(c_t[:, tm - 1:tm], carryt_sc.shape)
    cq_pieces = [piece.astype(F32) for piece in _split3(c_t)]
    qt =proj_t[_T_Q:_T_Q + ATT_DIM, :] * SCALE
    for hd in range(N_HEADS):
        data = qt[hd * HEAD_DIM:(hd + 1) * HEAD_DIM, :].astype(BF16)
        extra = _q_extra_rows(cq_pieces, hd, tm)
        if hd % 2 == 0:
            qta_sc[hd, 0:HEAD_DIM, :] = data
            qta_sc[hd, HEAD_DIM:HEAD_DIM + EXTRA_ROWS, :] = extra
        else:
            qta_sc[hd, 0:EXTRA_ROWS, :] = extra
            qta_sc[hd, HEAD_DIM:2 * HEAD_DIM, :] = data

    logf = _log_sigmoid(jnp.dot(hb_sc[...], w_fs[...], preferred_element_type=F32) + b_f_row[...])
    c_new = _cumsum_rows(logf) + carry_sc[0:1, :]
    carry_sc[...] = jnp.broadcast_to(c_new[tm - 1:tm, :], carry_sc.shape)
    ex_k = _extras(c_new, route_k[...]) + ones_k[...]
    new_rows = pl.ds(pl.multiple_of(j * tm, tm), tm)
    _store_augmented(kaug_sc, new_rows, proj(_P_K, ATT_DIM), ex_k)

    for jj in range(n_tiles):
        pl.when(j == jj)(functools.partial(_attention_t, jj + 1, tm, kaug_sc, qta_sc, vt_sc, attt_sc))
    att = attt_sc[...].T

    y_b = att * jax.nn.silu(proj(_P_ZB, ATT_DIM))
    mixed = mix_sc[...] + jax.nn.sigmoid(proj(_P_GM, D_MODEL)) * jnp.dot(
        y_b.astype(BF16), w_pb[...], preferred_element_type=F32)
    sub = jnp.dot(mixed.astype(BF16), w_out[...], preferred_element_type=F32)
    y_ref[...] = _layer_norm(ALPHA * h_sc[...] + sub, ln_g[...], ln_b[...])


def _run_prompt(x, params, *, tm):
    bsz, t, _ = x.shape
    n_tiles = t // tm
    assert n_tiles * tm == t
    per_tile = lambda width: pl.BlockSpec((None, tm, width), lambda b, j: (b, j, 0))
    per_tile_t = lambda rows: pl.BlockSpec((None, rows, tm), lambda b, j: (b, 0, j))
    out_shape = (jax.ShapeDtypeStruct((bsz, t, D_MODEL), F32),
                 jax.ShapeDtypeStruct((bsz, ATT_DIM, t), F32),
                 jax.ShapeDtypeStruct((bsz, ATT_DIM, t), F32),
                 jax.ShapeDtypeStruct((bsz, N_HEADS, t), F32),
                 jax.ShapeDtypeStruct((bsz, CONV_W - 1, CONV_DIM), F32))
    out_specs = (per_tile(D_MODEL), per_tile_t(ATT_DIM), per_tile_t(ATT_DIM), per_tile_t(N_HEADS),
                 pl.BlockSpec((None, CONV_W - 1, CONV_DIM), lambda b, j: (b, 0, 0)))
    scratch = [
        pltpu.VMEM((t, AUG_DIM), BF16),
        pltpu.VMEM((n_tiles, ATT_DIM, tm), BF16),
        pltpu.VMEM((tm, D_MODEL), F32),
        pltpu.VMEM((tm, D_MODEL), BF16),
        pltpu.VMEM((tm, D_MODEL), F32),
        pltpu.VMEM((N_HEADS, HEAD_BLOCK, tm), BF16),
        pltpu.VMEM((ATT_DIM, tm), F32),
        pltpu.VMEM((8 + tm, CONV_DIM), F32),
        pltpu.VMEM((8, LANES), F32),
        pltpu.VMEM((F_ROWS, LANES), F32),
    ]
    return pl.pallas_call(
        functools.partial(_prompt_kernel, tm=tm, n_tiles=n_tiles),
        grid=(bsz, n_tiles),
        in_specs=[per_tile(D_MODEL)] + [_resident(p.shape) for p in params],
        out_specs=out_specs,
        out_shape=out_shape,
        scratch_shapes=scratch,
        compiler_params=pltpu.CompilerParams(
            dimension_semantics=("arbitrary", "arbitrary"),
            vmem_limit_bytes=VMEM_LIMIT_BYTES),
        name="trunk_prompt",
    )(x, *params)


def _resident(shape):
    return pl.BlockSpec(shape, lambda b, j: (0,) * len(shape), pipeline_mode=pl.Buffered(1))


def _run_trunk(x, conv0, past, params, *, tm, past_chunk):
    bsz, t, _ = x.shape
    n_past = past[0].shape[1] if past else 0
    n_tiles = t // tm
    assert n_tiles * tm == t and (n_past == 0 or n_past % past_chunk == 0)
    n_keys = n_past + t

    per_tile = lambda width: pl.BlockSpec((None, tm, width), lambda b, j: (b, j, 0))
    per_batch = lambda rows, width: pl.BlockSpec((None, rows, width), lambda b, j: (b, 0, 0))

    in_specs = [per_tile(D_MODEL), per_batch(CONV_W - 1, CONV_DIM)]
    if past:
        in_specs += [per_batch(n_past, ATT_DIM), per_batch(n_past, ATT_DIM), per_batch(n_past, LANES)]
    in_specs += [_resident(p.shape) for p in params]

    out_shape = (jax.ShapeDtypeStruct((bsz, t, D_MODEL), F32),
                 jax.ShapeDtypeStruct((bsz, t, ATT_DIM), F32),
                 jax.ShapeDtypeStruct((bsz, t, ATT_DIM), F32),
                 jax.ShapeDtypeStruct((bsz, t, N_HEADS), F32),
                 jax.ShapeDtypeStruct((bsz, CONV_W - 1, CONV_DIM), F32))
    out_specs = (per_tile(D_MODEL), per_tile(ATT_DIM), per_tile(ATT_DIM), per_tile(N_HEADS),
                 per_batch(CONV_W - 1, CONV_DIM))

    scratch = [
        pltpu.VMEM((n_keys, AUG_DIM), BF16),
        pltpu.VMEM((n_keys, ATT_DIM), BF16),
        pltpu.VMEM((tm, D_MODEL), F32),
        pltpu.VMEM((tm, D_MODEL), BF16),
        pltpu.VMEM((tm, D_MODEL), F32),
        pltpu.VMEM((tm, AUG_DIM), BF16),
        pltpu.VMEM((tm, ATT_DIM), F32),
        pltpu.VMEM((N_HEADS, tm, 1), F32),
        pltpu.VMEM((N_HEADS, tm, 1), F32),
        pltpu.VMEM((N_HEADS, tm, HEAD_BLOCK), F32),
        pltpu.VMEM((8 + tm, CONV_DIM), F32),
        pltpu.VMEM((8, LANES), F32),
    ]
    kern = functools.partial(_trunk_kernel, tm=tm, n_past=n_past, past_chunk=past_chunk)
    return pl.pallas_call(
        kern,
        grid=(bsz, n_tiles),
        in_specs=in_specs,
        out_specs=out_specs,
        out_shape=out_shape,
        scratch_shapes=scratch,
        compiler_params=pltpu.CompilerParams(
            dimension_semantics=("arbitrary", "arbitrary"),
            vmem_limit_bytes=VMEM_LIMIT_BYTES),
        name="trunk_prompt" if not past else "trunk_sample",
    )(x, conv0, *past, *params)


def kernel(x_prompt, x_sample, cache_k, cache_v, cache_logf, state_conv, ln_in_g, ln_in_b, w_in, b_f,
           conv_w, w_proj_a, w_proj_b, w_out, ln_g, ln_b):
    assert w_in.shape[0] == DEPTH
    route_np, ones_np = _routing_constants()
    w = w_in[0]
    row = lambda a: a.reshape(1, -1).astype(F32)
    params = (
        row(ln_in_g), row(ln_in_b),
        jnp.concatenate([w[:, :F_COL0], w[:, F_COL0 + N_HEADS:]], axis=1).astype(BF16),
        jnp.pad(w[:, F_COL0:F_COL0 + N_HEADS], ((0, 0), (0, LANES - N_HEADS))).astype(BF16),
        jnp.pad(row(b_f[0]), ((0, 0), (0, LANES - N_HEADS))),
        conv_w[0].astype(F32),
        w_proj_a[0].astype(BF16), w_proj_b[0].astype(BF16), w_out[0].astype(BF16),
        row(ln_g[0]), row(ln_b[0]),
        jnp.asarray(route_np, BF16), jnp.asarray(ones_np, F32),
    )
    bp, tp, _ = x_prompt.shape
    bs, ts, _ = x_sample.shape
    n_past = cache_k.shape[2]

    c_q, c_k, c_v, c_zb = F_COL0 - 3 * ATT_DIM, F_COL0 - 2 * ATT_DIM, F_COL0 - ATT_DIM, F_COL0 + N_HEADS
    w_t = jnp.concatenate([w[:, c_q:F_COL0 + N_HEADS],
                           jnp.zeros((D_MODEL, F_ROWS - N_HEADS), w.dtype)], axis=1).T.astype(BF16)
    prompt_params = (
        params[0], params[1],
        jnp.concatenate([w[:, :c_q], w[:, c_k:c_v], w[:, c_zb:]], axis=1).astype(BF16),
        params[3], params[4],
        w_t,
        jnp.pad(b_f[0].astype(F32).reshape(N_HEADS, 1), ((0, F_ROWS - N_HEADS), (0, 0))),
        *params[5:11],
        jnp.asarray(route_np[:, AUG_DIM:], BF16), jnp.asarray(ones_np[:, AUG_DIM:], F32),
    )
    y_p, kt_p, vt_p, lft_p, cv_p = _run_prompt(x_prompt, prompt_params, tm=256)
    past = (cache_k[0].reshape(bs, n_past, ATT_DIM), cache_v[0].reshape(bs, n_past, ATT_DIM),
            jnp.pad(cache_logf[0], ((0, 0), (0, 0), (0, LANES - N_HEADS))))
    y_s, k_s, v_s, lf_s, cv_s = _run_trunk(x_sample, state_conv[0], past, params, tm=ts, past_chunk=256)

    heads = lambda a: a.reshape(1, a.shape[0], a.shape[1], N_HEADS, HEAD_DIM)
    heads_t = lambda a: a.reshape(a.shape[0], N_HEADS, HEAD_DIM, a.shape[2]).transpose(0, 3, 1, 2)[None]
    return (y_p, y_s, heads_t(kt_p), heads_t(vt_p), lft_p.transpose(0, 2, 1)[None], cv_p[None],
            heads(k_s), heads(v_s), lf_s[None], cv_s[None])
```

```python
import functools

import numpy as np
import jax
import jax.numpy as jnp
from jax import lax
from jax.experimental import pallas as pl
from jax.experimental.pallas import tpu as pltpu

D_MODEL = 1024
CONV_DIM = 512
CONV_W = 3
N_HEADS = 8
HEAD_DIM = 64
ATT_DIM = N_HEADS * HEAD_DIM
LN_EPS = 1e-5
DEPTH = 1
ALPHA = (2.0 * DEPTH) ** 0.25
SCALE = HEAD_DIM ** -0.5

LANES = 128
SUBLANES = 8
HEAD_BLOCK = LANES
AUG_DIM = N_HEADS * HEAD_BLOCK
N_SPLIT = 3
EXTRA_ROWS = 16
F_ROWS = 16
VMEM_LIMIT_BYTES = 56 * 1024 * 1024
SCORE_LOOKAHEAD = 2

_W_Q = 4 * CONV_DIM
_W_V = _W_Q + 2 * ATT_DIM
_W_F = _W_Q + 3 * ATT_DIM
_W_ZB = _W_F + N_HEADS
_S_XA, _S_GB, _S_GC, _S_ZA, _S_Q, _S_K, _S_ZB, _S_GA, _S_GM = (
    0, 512, 1024, 1536, 2048, 2560, 3072, 3584, 4608)
_T_Q, _T_K, _T_V, _T_F = 0, 512, 1024, 1536

F32 = jnp.float32
BF16 = jnp.bfloat16


def _routing_constants():
    route = np.zeros((LANES, 2 * AUG_DIM), np.float32)
    ones = np.zeros((1, 2 * AUG_DIM), np.float32)
    for h in range(N_HEADS):
        base = h * HEAD_BLOCK + (HEAD_DIM if h % 2 == 0 else 0)
        for r in range(N_SPLIT):
            route[r * N_HEADS + h, base + 2 * r] = 1.0
            ones[0, base + 2 * r + 1] = 1.0
            ones[0, AUG_DIM + base + 2 * r] = 1.0
            route[r * N_HEADS + h, AUG_DIM + base + 2 * r + 1] = -1.0
    return route, ones


def _layer_norm(x, g, b):
    mu = jnp.mean(x, axis=-1, keepdims=True)
    xc = x - mu
    var = jnp.mean(xc * xc, axis=-1, keepdims=True)
    return xc * lax.rsqrt(var + LN_EPS) * g + b


def _log_sigmoid(x):
    return jnp.minimum(x, 0.0) - jnp.log1p(jnp.exp(-jnp.abs(x)))


def _split3(x):
    hi = x.astype(BF16)
    r1 = x - hi.astype(F32)
    mid = r1.astype(BF16)
    lo = (r1 - mid.astype(F32)).astype(BF16)
    return hi, mid, lo


def _cumsum(x, axis):
    n = x.shape[axis]
    idx = lax.broadcasted_iota(jnp.int32, x.shape, axis)
    s = 1
    while s < n:
        x = x + jnp.where(idx >= s, pltpu.roll(x, s, axis), 0.0)
        s *= 2
    return x


def _extra_lanes(c, route):
    lane = lax.broadcasted_iota(jnp.int32, c.shape, 1)
    valid = lane < N_HEADS
    packed = jnp.zeros_like(c)
    for r, piece in enumerate(_split3(c)):
        p32 = jnp.where(valid, piece.astype(F32), 0.0)
        packed = packed + (pltpu.roll(p32, r * N_HEADS, 1) if r else p32)
    return jnp.dot(packed.astype(BF16), route, preferred_element_type=F32)


def _store_token_major(dst_ref, rows, data, extras):
    n = data.shape[0]
    lane = lax.broadcasted_iota(jnp.int32, (n, HEAD_BLOCK), 1)
    for h in range(N_HEADS):
        pair = data[:, (h // 2) * HEAD_BLOCK:(h // 2 + 1) * HEAD_BLOCK]
        ex = extras[:, h * HEAD_BLOCK:(h + 1) * HEAD_BLOCK]
        keep = (lane < HEAD_DIM) if h % 2 == 0 else (lane >= HEAD_DIM)
        dst_ref[rows, h * HEAD_BLOCK:(h + 1) * HEAD_BLOCK] = jnp.where(keep, pair, ex).astype(BF16)


def _extra_rows(pieces, h, n, key_side):
    r = lax.broadcasted_iota(jnp.int32, (EXTRA_ROWS, n), 0)
    value_row, sign = (1, -1.0) if key_side else (0, 1.0)
    out = jnp.where((r < 2 * N_SPLIT) & ((r & 1) != value_row), 1.0, 0.0)
    for i, piece in enumerate(pieces):
        out = jnp.where(r == 2 * i + value_row,
                        jnp.broadcast_to(sign * piece[h:h + 1, :], (EXTRA_ROWS, n)), out)
    return out.astype(BF16)


def _store_feature_major(dst_ref, h, data, extra):
    if h % 2 == 0:
        dst_ref[h, 0:HEAD_DIM, :] = data
        dst_ref[h, HEAD_DIM:HEAD_DIM + EXTRA_ROWS, :] = extra
    else:
        dst_ref[h, 0:EXTRA_ROWS, :] = extra
        dst_ref[h, HEAD_DIM:2 * HEAD_DIM, :] = data


def _conv_branch(proj, u_sc, conv_w, conv_ref, w_pa, n):
    first = SUBLANES - (CONV_W - 1)
    u_sc[SUBLANES:SUBLANES + n, :] = proj(_S_GC, CONV_DIM) * proj(_S_XA, CONV_DIM)
    conv = sum(u_sc[first + i:first + i + n, :] * conv_w[i:i + 1, :] for i in range(CONV_W))
    new_conv = u_sc[first + n:SUBLANES + n, :]
    conv_ref[...] = new_conv
    u_sc[first:SUBLANES, :] = new_conv
    y_a = proj(_S_GB, CONV_DIM) * conv * jax.nn.silu(proj(_S_ZA, CONV_DIM))
    return jax.nn.sigmoid(proj(_S_GA, D_MODEL)) * jnp.dot(
        y_a.astype(BF16), w_pa[...], preferred_element_type=F32)


def _merge_and_norm(att, mix_a, h, proj, w_pb, w_out, ln_g, ln_b):
    y_b = att * jax.nn.silu(proj(_S_ZB, ATT_DIM))
    mixed = mix_a + jax.nn.sigmoid(proj(_S_GM, D_MODEL)) * jnp.dot(
        y_b.astype(BF16), w_pb[...], preferred_element_type=F32)
    sub = jnp.dot(mixed.astype(BF16), w_out[...], preferred_element_type=F32)
    return _layer_norm(ALPHA * h + sub, ln_g[...], ln_b[...])


def _attention_t(n_key_tiles, tm, kaug_sc, qta_sc, vt_sc, attt_sc):
    keys = n_key_tiles * tm
    causal = (lax.broadcasted_iota(jnp.int32, (tm, tm), 0)
              <= lax.broadcasted_iota(jnp.int32, (tm, tm), 1))

    def scores(hd):
        return jnp.dot(kaug_sc[0:keys, hd * HEAD_BLOCK:(hd + 1) * HEAD_BLOCK], qta_sc[hd],
                       preferred_element_type=F32)

    pending = [scores(hd) for hd in range(SCORE_LOOKAHEAD)]
    for hd in range(N_HEADS):
        s = pending.pop(0)
        if hd + SCORE_LOOKAHEAD < N_HEADS:
            pending.append(scores(hd + SCORE_LOOKAHEAD))
        blocks = [s[i * tm:(i + 1) * tm, :] for i in range(n_key_tiles)]
        blocks[-1] = jnp.where(causal, blocks[-1], -jnp.inf)
        m = functools.reduce(jnp.maximum, [jnp.max(blk, axis=0, keepdims=True) for blk in blocks])
        l = jnp.zeros((1, tm), F32)
        o = jnp.zeros((HEAD_DIM, tm), F32)
        for i, blk in enumerate(blocks):
            p = jnp.exp(blk - m)
            l = l + jnp.sum(p, axis=0, keepdims=True)
            o = o + jnp.dot(vt_sc[i, hd * HEAD_DIM:(hd + 1) * HEAD_DIM, :], p.astype(BF16),
                            preferred_element_type=F32)
        attt_sc[hd * HEAD_DIM:(hd + 1) * HEAD_DIM, :] = o / l


def _prompt_kernel(x_ref, ln_in_g, ln_in_b, w_std, w_fs, b_f_row, w_t, b_f_col, conv_w, w_pa, w_pb, w_out,
                   ln_g, ln_b, route, ones,
                   y_ref, kt_ref, vt_ref, lft_ref, conv_ref,
                   kaug_sc, vt_sc, h_sc, hb_sc, mix_sc, qta_sc, attt_sc, u_sc, carry_sc, carryt_sc,
                   *, tm, n_tiles):
    j = pl.program_id(1)

    @pl.when(j == 0)
    def _start_of_sequence():
        u_sc[0:SUBLANES, :] = jnp.zeros((SUBLANES, CONV_DIM), F32)
        carry_sc[...] = jnp.zeros_like(carry_sc)
        carryt_sc[...] = jnp.zeros_like(carryt_sc)
        qta_sc[...] = jnp.zeros_like(qta_sc)

    h = _layer_norm(x_ref[...], ln_in_g[...], ln_in_b[...])
    h_sc[...] = h
    hb_sc[...] = h.astype(BF16)

    def proj(col, width):
        return jnp.dot(hb_sc[...], w_std[:, col:col + width], preferred_element_type=F32)

    mix_sc[...] = _conv_branch(proj, u_sc, conv_w, conv_ref, w_pa, tm)

    proj_t = lax.dot_general(w_t[...], hb_sc[...], (((1,), (1,)), ((), ())), preferred_element_type=F32)
    kt = proj_t[_T_K:_T_K + ATT_DIM, :]
    vt = proj_t[_T_V:_T_V + ATT_DIM, :]
    kt_ref[...] = kt
    vt_ref[...] = vt
    vt_sc[j] = vt.astype(BF16)
    logf_t = _log_sigmoid(proj_t[_T_F:_T_F + F_ROWS, :] + b_f_col[...])
    lft_ref[...] = logf_t[:N_HEADS, :]
    c_t = _cumsum(logf_t, 1) + carryt_sc[:, 0:1]
    carryt_sc[...] = jnp.broadcast_to(c_t[:, tm - 1:tm], carryt_sc.shape)
    cq_pieces = [piece.astype(F32) for piece in _split3(c_t)]
    qt = proj_t[_T_Q:_T_Q + ATT_DIM, :] * SCALE
    for hd in range(N_HEADS):
        _store_feature_major(qta_sc, hd, qt[hd * HEAD_DIM:(hd + 1) * HEAD_DIM, :].astype(BF16),
                             _extra_rows(cq_pieces, hd, tm, key_side=False))

    logf = _log_sigmoid(jnp.dot(hb_sc[...], w_fs[...], preferred_element_type=F32) + b_f_row[...])
    c_new = _cumsum(logf, 0) + carry_sc[0:1, :]
    carry_sc[...] = jnp.broadcast_to(c_new[tm - 1:tm, :], carry_sc.shape)
    ex_k = _extra_lanes(c_new, route[:, AUG_DIM:]) + ones[:, AUG_DIM:]
    _store_token_major(kaug_sc, pl.ds(pl.multiple_of(j * tm, tm), tm), proj(_S_K, ATT_DIM), ex_k)

    for jj in range(n_tiles):
        pl.when(j == jj)(functools.partial(_attention_t, jj + 1, tm, kaug_sc, qta_sc, vt_sc, attt_sc))

    y_ref[...] = _merge_and_norm(attt_sc[...].T, mix_sc[...], h_sc[...], proj, w_pb, w_out, ln_g, ln_b)


def _resident(shape):
    return pl.BlockSpec(shape, lambda *_: (0,) * len(shape), pipeline_mode=pl.Buffered(1))


def _run_prompt(x, params, *, tm):
    bsz, t, _ = x.shape
    n_tiles = t // tm
    assert n_tiles * tm == t
    per_tile = lambda width: pl.BlockSpec((None, tm, width), lambda b, j: (b, j, 0))
    per_tile_t = lambda rows: pl.BlockSpec((None, rows, tm), lambda b, j: (b, 0, j))
    out_shape = (jax.ShapeDtypeStruct((bsz, t, D_MODEL), F32),
                 jax.ShapeDtypeStruct((bsz, ATT_DIM, t), F32),
                 jax.ShapeDtypeStruct((bsz, ATT_DIM, t), F32),
                 jax.ShapeDtypeStruct((bsz, N_HEADS, t), F32),
                 jax.ShapeDtypeStruct((bsz, CONV_W - 1, CONV_DIM), F32))
    out_specs = (per_tile(D_MODEL), per_tile_t(ATT_DIM), per_tile_t(ATT_DIM), per_tile_t(N_HEADS),
                 pl.BlockSpec((None, CONV_W - 1, CONV_DIM), lambda b, j: (b, 0, 0)))
    scratch = [
        pltpu.VMEM((t, AUG_DIM), BF16),
        pltpu.VMEM((n_tiles, ATT_DIM, tm), BF16),
        pltpu.VMEM((tm, D_MODEL), F32),
        pltpu.VMEM((tm, D_MODEL), BF16),
        pltpu.VMEM((tm, D_MODEL), F32),
        pltpu.VMEM((N_HEADS, HEAD_BLOCK, tm), BF16),
        pltpu.VMEM((ATT_DIM, tm), F32),
        pltpu.VMEM((SUBLANES + tm, CONV_DIM), F32),
        pltpu.VMEM((SUBLANES, LANES), F32),
        pltpu.VMEM((F_ROWS, LANES), F32),
    ]
    return pl.pallas_call(
        functools.partial(_prompt_kernel, tm=tm, n_tiles=n_tiles),
        grid=(bsz, n_tiles),
        in_specs=[per_tile(D_MODEL)] + [_resident(p.shape) for p in params],
        out_specs=out_specs,
        out_shape=out_shape,
        scratch_shapes=scratch,
        compiler_params=pltpu.CompilerParams(
            dimension_semantics=("arbitrary", "arbitrary"),
            vmem_limit_bytes=VMEM_LIMIT_BYTES),
        name="trunk_prompt",
    )(x, *params)


def _sample_kernel(x_ref, conv0_ref, pkt_ref, pvt_ref, plft_ref,
                   ln_in_g, ln_in_b, w_std, w_fs, b_f_row, w_t, b_f_col, conv_w, w_pa, w_pb, w_out,
                   ln_g, ln_b, route, ones,
                   y_ref, kt_ref, vt_ref, lft_ref, conv_ref,
                   kta_sc, ktn_sc, hb_sc, qa_sc, att_sc, u_sc, *, ts, n_past):
    @pl.when(pl.program_id(0) == 0)
    def _zero_padding():
        kta_sc[...] = jnp.zeros_like(kta_sc)
        ktn_sc[...] = jnp.zeros_like(ktn_sc)
        hb_sc[...] = jnp.zeros_like(hb_sc)

    h = _layer_norm(x_ref[...], ln_in_g[...], ln_in_b[...])
    hb_sc[0:ts, :] = h.astype(BF16)

    def proj(col, width):
        return jnp.dot(hb_sc[0:ts, :], w_std[:, col:col + width], preferred_element_type=F32)

    u_sc[0:SUBLANES, :] = jnp.zeros((SUBLANES, CONV_DIM), F32)
    u_sc[SUBLANES - (CONV_W - 1):SUBLANES, :] = conv0_ref[...]
    mix_a = _conv_branch(proj, u_sc, conv_w, conv_ref, w_pa, ts)

    proj_t = lax.dot_general(w_t[_T_K:, :], hb_sc[...], (((1,), (1,)), ((), ())),
                             preferred_element_type=F32)
    kt = proj_t[0:ATT_DIM, :]
    vt = proj_t[ATT_DIM:2 * ATT_DIM, :]
    kt_ref[...] = kt[:, :ts]
    vt_ref[...] = vt[:, :ts]
    logf_t = _log_sigmoid(proj_t[2 * ATT_DIM:2 * ATT_DIM + SUBLANES, :] + b_f_col[0:SUBLANES, :])
    lft_ref[...] = logf_t[:, :ts]

    c_past = _cumsum(plft_ref[...], 1)
    c_last = c_past[:, n_past - 1:n_past]
    c_new_t = _cumsum(logf_t, 1) + c_last
    past_pieces = [piece.astype(F32) for piece in _split3(c_past)]
    new_pieces = [piece.astype(F32) for piece in _split3(c_new_t)]
    for hd in range(N_HEADS):
        rows = slice(hd * HEAD_DIM, (hd + 1) * HEAD_DIM)
        _store_feature_major(kta_sc, hd, pkt_ref[rows, :].astype(BF16),
                             _extra_rows(past_pieces, hd, n_past, key_side=True))
        _store_feature_major(ktn_sc, hd, kt[rows, :].astype(BF16),
                             _extra_rows(new_pieces, hd, LANES, key_side=True))

    eye = (lax.broadcasted_iota(jnp.int32, (SUBLANES, LANES), 0)
           == lax.broadcasted_iota(jnp.int32, (SUBLANES, LANES), 1))
    c_last_row = jnp.sum(jnp.where(eye, jnp.broadcast_to(c_last, (SUBLANES, LANES)), 0.0),
                         axis=0, keepdims=True)
    logf = _log_sigmoid(jnp.dot(hb_sc[0:ts, :], w_fs[...], preferred_element_type=F32) + b_f_row[...])
    c_q = _cumsum(logf, 0) + c_last_row
    ex_q = _extra_lanes(c_q, route[:, :AUG_DIM]) + ones[:, :AUG_DIM]
    _store_token_major(qa_sc, slice(None), proj(_S_Q, ATT_DIM) * SCALE, ex_q)

    causal = (lax.broadcasted_iota(jnp.int32, (ts, LANES), 1)
              <= lax.broadcasted_iota(jnp.int32, (ts, LANES), 0))
    lane = lax.broadcasted_iota(jnp.int32, (ts, HEAD_BLOCK), 1)
    contract_lanes = (((1,), (1,)), ((), ()))
    for pair in range(N_HEADS // 2):
        rows = slice(pair * HEAD_BLOCK, (pair + 1) * HEAD_BLOCK)
        vt_past = pvt_ref[rows, :].astype(BF16)
        vt_new = vt[rows, :].astype(BF16)
        outs = []
        for hd in (2 * pair, 2 * pair + 1):
            qa_h = qa_sc[:, hd * HEAD_BLOCK:(hd + 1) * HEAD_BLOCK]
            s_past = jnp.dot(qa_h, kta_sc[hd], preferred_element_type=F32)
            s_new = jnp.where(causal, jnp.dot(qa_h, ktn_sc[hd], preferred_element_type=F32), -jnp.inf)
            m = jnp.maximum(jnp.max(s_past, axis=-1, keepdims=True), jnp.max(s_new, axis=-1, keepdims=True))
            p_past = jnp.exp(s_past - m)
            p_new = jnp.exp(s_new - m)
            l = jnp.sum(p_past, axis=-1, keepdims=True) + jnp.sum(p_new, axis=-1, keepdims=True)
            o = (lax.dot_general(p_past.astype(BF16), vt_past, contract_lanes, preferred_element_type=F32)
                 + lax.dot_general(p_new.astype(BF16), vt_new, contract_lanes, preferred_element_type=F32))
            outs.append(o / l)
        att_sc[:, rows] = jnp.where(lane < HEAD_DIM, outs[0], outs[1])

    y_ref[...] = _merge_and_norm(att_sc[...], mix_a, h, proj, w_pb, w_out, ln_g, ln_b)


def _run_sample(x, conv0, pkt, pvt, plft, params):
    bsz, ts, _ = x.shape
    n_past = pkt.shape[2]
    assert ts <= LANES and ts % SUBLANES == 0
    per_batch = lambda rows, width: pl.BlockSpec((None, rows, width), lambda b: (b, 0, 0))
    out_shape = (jax.ShapeDtypeStruct((bsz, ts, D_MODEL), F32),
                 jax.ShapeDtypeStruct((bsz, ATT_DIM, ts), F32),
                 jax.ShapeDtypeStruct((bsz, ATT_DIM, ts), F32),
                 jax.ShapeDtypeStruct((bsz, N_HEADS, ts), F32),
                 jax.ShapeDtypeStruct((bsz, CONV_W - 1, CONV_DIM), F32))
    out_specs = (per_batch(ts, D_MODEL), per_batch(ATT_DIM, ts), per_batch(ATT_DIM, ts),
                 per_batch(N_HEADS, ts), per_batch(CONV_W - 1, CONV_DIM))
    in_specs = [per_batch(ts, D_MODEL), per_batch(CONV_W - 1, CONV_DIM), per_batch(ATT_DIM, n_past),
                per_batch(ATT_DIM, n_past), per_batch(N_HEADS, n_past)]
    scratch = [
        pltpu.VMEM((N_HEADS, HEAD_BLOCK, n_past), BF16),
        pltpu.VMEM((N_HEADS, HEAD_BLOCK, LANES), BF16),
        pltpu.VMEM((LANES, D_MODEL), BF16),
        pltpu.VMEM((ts, AUG_DIM), BF16),
        pltpu.VMEM((ts, ATT_DIM), F32),
        pltpu.VMEM((SUBLANES + ts, CONV_DIM), F32),
    ]
    return pl.pallas_call(
        functools.partial(_sample_kernel, ts=ts, n_past=n_past),
        grid=(bsz,),
        in_specs=in_specs + [_resident(p.shape) for p in params],
        out_specs=out_specs,
        out_shape=out_shape,
        scratch_shapes=scratch,
        compiler_params=pltpu.CompilerParams(
            dimension_semantics=("arbitrary",),
            vmem_limit_bytes=VMEM_LIMIT_BYTES),
        name="trunk_sample",
    )(x, conv0, pkt, pvt, plft, *params)


def kernel(x_prompt, x_sample, cache_k, cache_v, cache_logf, state_conv, ln_in_g, ln_in_b, w_in, b_f,
           conv_w, w_proj_a, w_proj_b, w_out, ln_g, ln_b):
    assert w_in.shape[0] == DEPTH
    route_np, ones_np = _routing_constants()
    w = w_in[0]
    row = lambda a: a.reshape(1, -1).astype(F32)
    w_t = jnp.concatenate([w[:, _W_Q:_W_F + N_HEADS],
                           jnp.zeros((D_MODEL, F_ROWS - N_HEADS), w.dtype)], axis=1).T.astype(BF16)
    params = (
        row(ln_in_g), row(ln_in_b),
        jnp.concatenate([w[:, :_W_V], w[:, _W_ZB:]], axis=1).astype(BF16),
        jnp.pad(w[:, _W_F:_W_F + N_HEADS], ((0, 0), (0, LANES - N_HEADS))).astype(BF16),
        jnp.pad(row(b_f[0]), ((0, 0), (0, LANES - N_HEADS))),
        w_t,
        jnp.pad(b_f[0].astype(F32).reshape(N_HEADS, 1), ((0, F_ROWS - N_HEADS), (0, 0))),
        conv_w[0].astype(F32),
        w_proj_a[0].astype(BF16), w_proj_b[0].astype(BF16), w_out[0].astype(BF16),
        row(ln_g[0]), row(ln_b[0]),
        jnp.asarray(route_np, BF16), jnp.asarray(ones_np, F32),
    )
    bs = x_sample.shape[0]
    n_past = cache_k.shape[2]
    feature_major = lambda a: a.transpose(0, 2, 3, 1).reshape(bs, ATT_DIM, n_past)

    y_p, kt_p, vt_p, lft_p, cv_p = _run_prompt(x_prompt, params, tm=256)
    y_s, kt_s, vt_s, lft_s, cv_s = _run_sample(
        x_sample, state_conv[0], feature_major(cache_k[0]), feature_major(cache_v[0]),
        cache_logf[0].transpose(0, 2, 1), params)

    heads_t = lambda a: a.reshape(a.shape[0], N_HEADS, HEAD_DIM, a.shape[2]).transpose(0, 3, 1, 2)[None]
    tokens_t = lambda a: a.transpose(0, 2, 1)[None]
    return (y_p, y_s, heads_t(kt_p), heads_t(vt_p), tokens_t(lft_p), cv_p[None],
            heads_t(kt_s), heads_t(vt_s), tokens_t(lft_s), cv_s[None])
```

```python
import functools

import numpy as np
import jax
import jax.numpy as jnp
from jax import lax
from jax.experimental import pallas as pl
from jax.experimental.pallas import tpu as pltpu

D_MODEL = 1024
CONV_DIM = 512
CONV_W = 3
N_HEADS = 8
HEAD_DIM = 64
ATT_DIM = N_HEADS * HEAD_DIM
LN_EPS = 1e-5
DEPTH = 1
ALPHA = (2.0 * DEPTH) ** 0.25
SCALE = HEAD_DIM ** -0.5

LANES = 128
SUBLANES = 8
HEAD_BLOCK = LANES
AUG_DIM = N_HEADS * HEAD_BLOCK
N_SPLIT = 3
EXTRA_ROWS = 16
F_ROWS = 16
VMEM_LIMIT_BYTES = 56 * 1024 * 1024
V_ROWS = HEAD_DIM + 16
LOG2E = 1.4426950408889634
SCORE_LOOKAHEAD = 2

_W_Q = 4 * CONV_DIM
_W_V = _W_Q + 2 * ATT_DIM
_W_F = _W_Q + 3 * ATT_DIM
_W_ZB = _W_F + N_HEADS
_S_XA, _S_GB, _S_GC, _S_ZA, _S_Q, _S_K, _S_ZB, _S_GA, _S_GM = (
    0, 512, 1024, 1536, 2048, 2560, 3072, 3584, 4608)
_T_Q, _T_K, _T_V, _T_F = 0, 512, 1024, 1536

PROMPT_TILE = 256
PROMPT_PARAMS = ("ln_in_g", "ln_in_b", "w_std", "w_t", "b_f_col", "conv_w", "w_pa", "w_pb", "w_out",
                 "ln_g", "ln_b", "route", "ones", "tri")
SAMPLE_PARAMS = ("ln_in_g", "ln_in_b", "w_std", "w_fs", "b_f_row", "w_t", "b_f_col", "conv_w", "w_pa",
                 "w_pb", "w_out", "ln_g", "ln_b", "route", "ones")

F32 = jnp.float32
BF16 = jnp.bfloat16


def _routing_constants():
    route = np.zeros((LANES, 2 * AUG_DIM), np.float32)
    ones = np.zeros((1, 2 * AUG_DIM), np.float32)
    for h in range(N_HEADS):
        base = h * HEAD_BLOCK + (HEAD_DIM if h % 2 == 0 else 0)
        for r in range(N_SPLIT):
            route[r * N_HEADS + h, base + 2 * r] = 1.0
            ones[0, base + 2 * r + 1] = 1.0
            ones[0, AUG_DIM + base + 2 * r] = 1.0
            route[r * N_HEADS + h, AUG_DIM + base + 2 * r + 1] = -1.0
    return route, ones


def _layer_norm(x, g, b):
    mu = jnp.mean(x, axis=-1, keepdims=True)
    xc = x - mu
    var = jnp.mean(xc * xc, axis=-1, keepdims=True)
    return xc * lax.rsqrt(var + LN_EPS) * g + b


def _log_sigmoid(x):
    return jnp.minimum(x, 0.0) - jnp.log1p(jnp.exp(-jnp.abs(x)))


def _split3(x):
    hi = x.astype(BF16)
    r1 = x - hi.astype(F32)
    mid = r1.astype(BF16)
    lo = (r1 - mid.astype(F32)).astype(BF16)
    return hi, mid, lo


def _cumsum(x, axis):
    n = x.shape[axis]
    idx = lax.broadcasted_iota(jnp.int32, x.shape, axis)
    s = 1
    while s < n:
        x = x + jnp.where(idx >= s, pltpu.roll(x, s, axis), 0.0)
        s *= 2
    return x


def _extra_lanes(c, route):
    lane = lax.broadcasted_iota(jnp.int32, c.shape, 1)
    valid = lane < N_HEADS
    packed = jnp.zeros_like(c)
    for r, piece in enumerate(_split3(c)):
        p32 = jnp.where(valid, piece.astype(F32), 0.0)
        packed = packed + (pltpu.roll(p32, r * N_HEADS, 1) if r else p32)
    return jnp.dot(packed.astype(BF16), route, preferred_element_type=F32)


def _store_token_major(dst_ref, rows, data, extras):
    n = data.shape[0]
    lane = lax.broadcasted_iota(jnp.int32, (n, HEAD_BLOCK), 1)
    for h in range(N_HEADS):
        pair = data[:, (h // 2) * HEAD_BLOCK:(h // 2 + 1) * HEAD_BLOCK]
        ex = extras[:, h * HEAD_BLOCK:(h + 1) * HEAD_BLOCK]
        keep = (lane < HEAD_DIM) if h % 2 == 0 else (lane >= HEAD_DIM)
        dst_ref[rows, h * HEAD_BLOCK:(h + 1) * HEAD_BLOCK] = jnp.where(keep, pair, ex).astype(BF16)


def _extra_rows(pieces, h, n, key_side):
    r = lax.broadcasted_iota(jnp.int32, (EXTRA_ROWS, n), 0)
    value_row, sign = (1, -1.0) if key_side else (0, 1.0)
    out = jnp.where((r < 2 * N_SPLIT) & ((r & 1) != value_row), 1.0, 0.0)
    for i, piece in enumerate(pieces):
        out = jnp.where(r == 2 * i + value_row,
                        jnp.broadcast_to(sign * piece[h:h + 1, :], (EXTRA_ROWS, n)), out)
    return out.astype(BF16)


def _store_feature_major(dst_ref, h, data, extra):
    if h % 2 == 0:
        dst_ref[h, 0:HEAD_DIM, :] = data
        dst_ref[h, HEAD_DIM:HEAD_DIM + EXTRA_ROWS, :] = extra
    else:
        dst_ref[h, 0:EXTRA_ROWS, :] = extra
        dst_ref[h, HEAD_DIM:2 * HEAD_DIM, :] = data


def _conv_branch(proj, u_sc, conv_w, conv_ref, w_pa, n):
    first = SUBLANES - (CONV_W - 1)
    u_sc[SUBLANES:SUBLANES + n, :] = proj(_S_GC, CONV_DIM) * proj(_S_XA, CONV_DIM)
    conv = sum(u_sc[first + i:first + i + n, :] * conv_w[i:i + 1, :] for i in range(CONV_W))
    new_conv = u_sc[first + n:SUBLANES + n, :]
    conv_ref[...] = new_conv
    u_sc[first:SUBLANES, :] = new_conv
    y_a = proj(_S_GB, CONV_DIM) * conv * jax.nn.silu(proj(_S_ZA, CONV_DIM))
    return jax.nn.sigmoid(proj(_S_GA, D_MODEL)) * jnp.dot(
        y_a.astype(BF16), w_pa[...], preferred_element_type=F32)


def _merge_and_norm(att, mix_a, h, silu_zb, gate_b, w_pb, w_out, ln_g, ln_b):
    y_b = att * silu_zb
    mixed = mix_a + gate_b * jnp.dot(
        y_b.astype(BF16), w_pb[...], preferred_element_type=F32)
    sub = jnp.dot(mixed.astype(BF16), w_out[...], preferred_element_type=F32)
    return _layer_norm(ALPHA * h + sub, ln_g[...], ln_b[...])


def _attention_t(n_key_tiles, tm, kaug_sc, qta_sc, vt_sc, attt_sc):
    keys = n_key_tiles * tm
    causal = (lax.broadcasted_iota(jnp.int32, (tm, tm), 0)
              <= lax.broadcasted_iota(jnp.int32, (tm, tm), 1))

    def scores(hd):
        return jnp.dot(kaug_sc[0:keys, hd * HEAD_BLOCK:(hd + 1) * HEAD_BLOCK], qta_sc[hd],
                       preferred_element_type=F32)

    pending = [scores(hd) for hd in range(SCORE_LOOKAHEAD)]
    for hd in range(N_HEADS):
        s = pending.pop(0)
        if hd + SCORE_LOOKAHEAD < N_HEADS:
            pending.append(scores(hd + SCORE_LOOKAHEAD))
        blocks = [s[i * tm:(i + 1) * tm, :] for i in range(n_key_tiles)]
        blocks[-1] = jnp.where(causal, blocks[-1], -jnp.inf)
        m = functools.reduce(jnp.maximum, [jnp.max(blk, axis=0, keepdims=True) for blk in blocks])
        o = jnp.zeros((V_ROWS, tm), F32)
        for i, blk in enumerate(blocks):
            o = o + jnp.dot(vt_sc[i, hd * V_ROWS:(hd + 1) * V_ROWS, :], jnp.exp2(blk - m).astype(BF16),
                            preferred_element_type=F32)
        attt_sc[hd * HEAD_DIM:(hd + 1) * HEAD_DIM, :] = o[0:HEAD_DIM, :] / o[HEAD_DIM:HEAD_DIM + 1, :]


def _prompt_kernel(x_ref, ln_in_g, ln_in_b, w_std, w_t, b_f_col, conv_w, w_pa, w_pb, w_out,
                   ln_g, ln_b, route, ones, tri,
                   y_ref, kt_ref, vt_ref, lft_ref, conv_ref,
                   kaug_sc, vt_sc, h_sc, hb_sc, qta_sc, attt_sc, zb_sc, gm_sc, u_sc, carryt_sc,
                   *, tm, n_tiles):
    j = pl.program_id(1)

    @pl.when(j == 0)
    def _start_of_sequence():
        u_sc[0:SUBLANES, :] = jnp.zeros((SUBLANES, CONV_DIM), F32)
        carryt_sc[...] = jnp.zeros_like(carryt_sc)
        qta_sc[...] = jnp.zeros_like(qta_sc)
        ones_row = jnp.where(lax.broadcasted_iota(jnp.int32, (V_ROWS - HEAD_DIM, tm), 0) == 0, 1.0, 0.0)
        for i in range(n_tiles):
            for hd in range(N_HEADS):
                vt_sc[i, hd * V_ROWS + HEAD_DIM:(hd + 1) * V_ROWS, :] = ones_row.astype(BF16)

    h = _layer_norm(x_ref[...], ln_in_g[...], ln_in_b[...])
    h_sc[...] = h
    hb_sc[...] = h.astype(BF16)

    def proj(col, width):
        return jnp.dot(hb_sc[...], w_std[:, col:col + width], preferred_element_type=F32)

    def proj_t(row, height):
        return lax.dot_general(w_t[row:row + height, :], hb_sc[...], (((1,), (1,)), ((), ())),
                               preferred_element_type=F32)

    vf_t = proj_t(_T_V, ATT_DIM + F_ROWS)
    qk_t = proj_t(_T_Q, 2 * ATT_DIM)
    kt = qk_t[ATT_DIM:2 * ATT_DIM, :]
    vt = vf_t[0:ATT_DIM, :]
    kt_ref[...] = kt
    vt_ref[...] = vt
    for hd in range(N_HEADS):
        vt_sc[j, hd * V_ROWS:hd * V_ROWS + HEAD_DIM, :] = vt[hd * HEAD_DIM:(hd + 1) * HEAD_DIM, :].astype(BF16)
    logf_t = _log_sigmoid(vf_t[ATT_DIM:ATT_DIM + F_ROWS, :] + b_f_col[...])
    lft_ref[...] = logf_t[:N_HEADS, :]
    sums = jnp.dot(jnp.concatenate(_split3(logf_t), axis=0), tri[...], preferred_element_type=F32)
    c_t = sum(sums[r * F_ROWS:(r + 1) * F_ROWS, :] for r in range(N_SPLIT)) + carryt_sc[:, 0:1]
    carryt_sc[...] = jnp.broadcast_to(c_t[:, tm - 1:tm], carryt_sc.shape)
    zb_sc[...] = jax.nn.silu(proj(_S_ZB, ATT_DIM))
    gm_sc[...] = jax.nn.sigmoid(proj(_S_GM, D_MODEL))
    c_pieces = [piece.astype(F32) for piece in _split3(c_t * LOG2E)]
    qt = qk_t[0:ATT_DIM, :] * (SCALE * LOG2E)
    for hd in range(N_HEADS):
        _store_feature_major(qta_sc, hd, qt[hd * HEAD_DIM:(hd + 1) * HEAD_DIM, :].astype(BF16),
                             _extra_rows(c_pieces, hd, tm, key_side=False))

    packed_t = jnp.concatenate([piece[0:N_HEADS, :] for piece in c_pieces]
                               + [jnp.zeros((LANES - N_SPLIT * N_HEADS, tm), F32)], axis=0)
    ex_k = jnp.dot(packed_t.T.astype(BF16), route[:, AUG_DIM:], preferred_element_type=F32) + ones[:, AUG_DIM:]
    _store_token_major(kaug_sc, pl.ds(pl.multiple_of(j * tm, tm), tm), kt.T, ex_k)

    for jj in range(n_tiles):
        pl.when(j == jj)(functools.partial(_attention_t, jj + 1, tm, kaug_sc, qta_sc, vt_sc, attt_sc))

    mix_a = _conv_branch(proj, u_sc, conv_w, conv_ref, w_pa, tm)
    att = attt_sc[...].T
    half = tm // 2
    for r in range(2):
        rows = slice(r * half, (r + 1) * half)
        y_ref[rows, :] = _merge_and_norm(att[rows, :], mix_a[rows, :], h_sc[rows, :], zb_sc[rows, :],
                                         gm_sc[rows, :], w_pb, w_out, ln_g, ln_b)


def _resident(shape):
    return pl.BlockSpec(shape, lambda *_: (0,) * len(shape), pipeline_mode=pl.Buffered(1))


def _run_prompt(x, params, *, tm):
    bsz, t, _ = x.shape
    n_tiles = t // tm
    assert n_tiles * tm == t
    per_tile = lambda width: pl.BlockSpec((None, tm, width), lambda b, j: (b, j, 0))
    per_tile_t = lambda rows: pl.BlockSpec((None, rows, tm), lambda b, j: (b, 0, j))
    out_shape = (jax.ShapeDtypeStruct((bsz, t, D_MODEL), F32),
                 jax.ShapeDtypeStruct((bsz, ATT_DIM, t), F32),
                 jax.ShapeDtypeStruct((bsz, ATT_DIM, t), F32),
                 jax.ShapeDtypeStruct((bsz, N_HEADS, t), F32),
                 jax.ShapeDtypeStruct((bsz, CONV_W - 1, CONV_DIM), F32))
    out_specs = (per_tile(D_MODEL), per_tile_t(ATT_DIM), per_tile_t(ATT_DIM), per_tile_t(N_HEADS),
                 pl.BlockSpec((None, CONV_W - 1, CONV_DIM), lambda b, j: (b, 0, 0)))
    scratch = [
        pltpu.VMEM((t, AUG_DIM), BF16),
        pltpu.VMEM((n_tiles, N_HEADS * V_ROWS, tm), BF16),
        pltpu.VMEM((tm, D_MODEL), F32),
        pltpu.VMEM((tm, D_MODEL), BF16),
        pltpu.VMEM((N_HEADS, HEAD_BLOCK, tm), BF16),
        pltpu.VMEM((ATT_DIM, tm), F32),
        pltpu.VMEM((tm, ATT_DIM), F32),
        pltpu.VMEM((tm, D_MODEL), F32),
        pltpu.VMEM((SUBLANES + tm, CONV_DIM), F32),
        pltpu.VMEM((F_ROWS, LANES), F32),
    ]
    return pl.pallas_call(
        functools.partial(_prompt_kernel, tm=tm, n_tiles=n_tiles),
        grid=(bsz, n_tiles),
        in_specs=[per_tile(D_MODEL)] + [_resident(p.shape) for p in params],
        out_specs=out_specs,
        out_shape=out_shape,
        scratch_shapes=scratch,
        compiler_params=pltpu.CompilerParams(
            dimension_semantics=("arbitrary", "arbitrary"),
            vmem_limit_bytes=VMEM_LIMIT_BYTES),
        name="trunk_prompt",
    )(x, *params)


def _sample_kernel(x_ref, conv0_ref, pkt_ref, pvt_ref, plft_ref,
                   ln_in_g, ln_in_b, w_std, w_fs, b_f_row, w_t, b_f_col, conv_w, w_pa, w_pb, w_out,
                   ln_g, ln_b, route, ones,
                   y_ref, kt_ref, vt_ref, lft_ref, conv_ref,
                   kta_sc, ktn_sc, hb_sc, qa_sc, att_sc, u_sc, *, ts, n_past):
    @pl.when(pl.program_id(0) == 0)
    def _zero_padding():
        kta_sc[...] = jnp.zeros_like(kta_sc)
        ktn_sc[...] = jnp.zeros_like(ktn_sc)
        hb_sc[...] = jnp.zeros_like(hb_sc)

    h = _layer_norm(x_ref[...], ln_in_g[...], ln_in_b[...])
    hb_sc[0:ts, :] = h.astype(BF16)

    def proj(col, width):
        return jnp.dot(hb_sc[0:ts, :], w_std[:, col:col + width], preferred_element_type=F32)

    u_sc[0:SUBLANES, :] = jnp.zeros((SUBLANES, CONV_DIM), F32)
    u_sc[SUBLANES - (CONV_W - 1):SUBLANES, :] = conv0_ref[...]
    mix_a = _conv_branch(proj, u_sc, conv_w, conv_ref, w_pa, ts)

    proj_t = lax.dot_general(w_t[_T_K:, :], hb_sc[...], (((1,), (1,)), ((), ())),
                             preferred_element_type=F32)
    kt = proj_t[0:ATT_DIM, :]
    vt = proj_t[ATT_DIM:2 * ATT_DIM, :]
    kt_ref[...] = kt[:, :ts]
    vt_ref[...] = vt[:, :ts]
    logf_t = _log_sigmoid(proj_t[2 * ATT_DIM:2 * ATT_DIM + SUBLANES, :] + b_f_col[0:SUBLANES, :])
    lft_ref[...] = logf_t[:, :ts]

    c_past = _cumsum(plft_ref[...], 1)
    c_last = c_past[:, n_past - 1:n_past]
    c_new_t = _cumsum(logf_t, 1) + c_last
    past_pieces = [piece.astype(F32) for piece in _split3(c_past)]
    new_pieces = [piece.astype(F32) for piece in _split3(c_new_t)]
    for hd in range(N_HEADS):
        rows = slice(hd * HEAD_DIM, (hd + 1) * HEAD_DIM)
        _store_feature_major(kta_sc, hd, pkt_ref[rows, :].astype(BF16),
                             _extra_rows(past_pieces, hd, n_past, key_side=True))
        _store_feature_major(ktn_sc, hd, kt[rows, :].astype(BF16),
                             _extra_rows(new_pieces, hd, LANES, key_side=True))

    eye = (lax.broadcasted_iota(jnp.int32, (SUBLANES, LANES), 0)
           == lax.broadcasted_iota(jnp.int32, (SUBLANES, LANES), 1))
    c_last_row = jnp.sum(jnp.where(eye, jnp.broadcast_to(c_last, (SUBLANES, LANES)), 0.0),
                         axis=0, keepdims=True)
    logf = _log_sigmoid(jnp.dot(hb_sc[0:ts, :], w_fs[...], preferred_element_type=F32) + b_f_row[...])
    c_q = _cumsum(logf, 0) + c_last_row
    ex_q = _extra_lanes(c_q, route[:, :AUG_DIM]) + ones[:, :AUG_DIM]
    _store_token_major(qa_sc, slice(None), proj(_S_Q, ATT_DIM) * SCALE, ex_q)

    causal = (lax.broadcasted_iota(jnp.int32, (ts, LANES), 1)
              <= lax.broadcasted_iota(jnp.int32, (ts, LANES), 0))
    lane = lax.broadcasted_iota(jnp.int32, (ts, HEAD_BLOCK), 1)
    contract_lanes = (((1,), (1,)), ((), ()))
    for pair in range(N_HEADS // 2):
        rows = slice(pair * HEAD_BLOCK, (pair + 1) * HEAD_BLOCK)
        vt_past = pvt_ref[rows, :].astype(BF16)
        vt_new = vt[rows, :].astype(BF16)
        outs = []
        for hd in (2 * pair, 2 * pair + 1):
            qa_h = qa_sc[:, hd * HEAD_BLOCK:(hd + 1) * HEAD_BLOCK]
            s_past = jnp.dot(qa_h, kta_sc[hd], preferred_element_type=F32)
            s_new = jnp.where(causal, jnp.dot(qa_h, ktn_sc[hd], preferred_element_type=F32), -jnp.inf)
            m = jnp.maximum(jnp.max(s_past, axis=-1, keepdims=True), jnp.max(s_new, axis=-1, keepdims=True))
            p_past = jnp.exp(s_past - m)
            p_new = jnp.exp(s_new - m)
            l = jnp.sum(p_past, axis=-1, keepdims=True) + jnp.sum(p_new, axis=-1, keepdims=True)
            o = (lax.dot_general(p_past.astype(BF16), vt_past, contract_lanes, preferred_element_type=F32)
                 + lax.dot_general(p_new.astype(BF16), vt_new, contract_lanes, preferred_element_type=F32))
            outs.append(o / l)
        att_sc[:, rows] = jnp.where(lane < HEAD_DIM, outs[0], outs[1])

    y_ref[...] = _merge_and_norm(att_sc[...], mix_a, h, jax.nn.silu(proj(_S_ZB, ATT_DIM)),
                                 jax.nn.sigmoid(proj(_S_GM, D_MODEL)), w_pb, w_out, ln_g, ln_b)


def _run_sample(x, conv0, pkt, pvt, plft, params):
    bsz, ts, _ = x.shape
    n_past = pkt.shape[2]
    assert ts <= LANES and ts % SUBLANES == 0
    per_batch = lambda rows, width: pl.BlockSpec((None, rows, width), lambda b: (b, 0, 0))
    out_shape = (jax.ShapeDtypeStruct((bsz, ts, D_MODEL), F32),
                 jax.ShapeDtypeStruct((bsz, ATT_DIM, ts), F32),
                 jax.ShapeDtypeStruct((bsz, ATT_DIM, ts), F32),
                 jax.ShapeDtypeStruct((bsz, N_HEADS, ts), F32),
                 jax.ShapeDtypeStruct((bsz, CONV_W - 1, CONV_DIM), F32))
    out_specs = (per_batch(ts, D_MODEL), per_batch(ATT_DIM, ts), per_batch(ATT_DIM, ts),
                 per_batch(N_HEADS, ts), per_batch(CONV_W - 1, CONV_DIM))
    in_specs = [per_batch(ts, D_MODEL), per_batch(CONV_W - 1, CONV_DIM), per_batch(ATT_DIM, n_past),
                per_batch(ATT_DIM, n_past), per_batch(N_HEADS, n_past)]
    scratch = [
        pltpu.VMEM((N_HEADS, HEAD_BLOCK, n_past), BF16),
        pltpu.VMEM((N_HEADS, HEAD_BLOCK, LANES), BF16),
        pltpu.VMEM((LANES, D_MODEL), BF16),
        pltpu.VMEM((ts, AUG_DIM), BF16),
        pltpu.VMEM((ts, ATT_DIM), F32),
        pltpu.VMEM((SUBLANES + ts, CONV_DIM), F32),
    ]
    return pl.pallas_call(
        functools.partial(_sample_kernel, ts=ts, n_past=n_past),
        grid=(bsz,),
        in_specs=in_specs + [_resident(p.shape) for p in params],
        out_specs=out_specs,
        out_shape=out_shape,
        scratch_shapes=scratch,
        compiler_params=pltpu.CompilerParams(
            dimension_semantics=("arbitrary",),
            vmem_limit_bytes=VMEM_LIMIT_BYTES),
        name="trunk_sample",
    )(x, conv0, pkt, pvt, plft, *params)


def kernel(x_prompt, x_sample, cache_k, cache_v, cache_logf, state_conv, ln_in_g, ln_in_b, w_in, b_f,
           conv_w, w_proj_a, w_proj_b, w_out, ln_g, ln_b):
    assert w_in.shape[0] == DEPTH
    route_np, ones_np = _routing_constants()
    w = w_in[0]
    row = lambda a: a.reshape(1, -1).astype(F32)
    w_t = jnp.concatenate([w[:, _W_Q:_W_F + N_HEADS],
                           jnp.zeros((D_MODEL, F_ROWS - N_HEADS), w.dtype)], axis=1).T.astype(BF16)
    tm = PROMPT_TILE
    p = dict(
        ln_in_g=row(ln_in_g), ln_in_b=row(ln_in_b),
        w_std=jnp.concatenate([w[:, :_W_V], w[:, _W_ZB:]], axis=1).astype(BF16),
        w_fs=jnp.pad(w[:, _W_F:_W_F + N_HEADS], ((0, 0), (0, LANES - N_HEADS))).astype(BF16),
        b_f_row=jnp.pad(row(b_f[0]), ((0, 0), (0, LANES - N_HEADS))),
        w_t=w_t,
        b_f_col=jnp.pad(b_f[0].astype(F32).reshape(N_HEADS, 1), ((0, F_ROWS - N_HEADS), (0, 0))),
        conv_w=conv_w[0].astype(F32),
        w_pa=w_proj_a[0].astype(BF16), w_pb=w_proj_b[0].astype(BF16), w_out=w_out[0].astype(BF16),
        ln_g=row(ln_g[0]), ln_b=row(ln_b[0]),
        route=jnp.asarray(route_np, BF16), ones=jnp.asarray(ones_np, F32),
        tri=jnp.asarray(np.triu(np.ones((tm, tm), np.float32)), BF16),
    )
    bs = x_sample.shape[0]
    n_past = cache_k.shape[2]
    feature_major = lambda a: a.transpose(0, 2, 3, 1).reshape(bs, ATT_DIM, n_past)

    y_p, kt_p, vt_p, lft_p, cv_p = _run_prompt(x_prompt, [p[name] for name in PROMPT_PARAMS], tm=tm)
    y_s, kt_s, vt_s, lft_s, cv_s = _run_sample(
        x_sample, state_conv[0], feature_major(cache_k[0]), feature_major(cache_v[0]),
        cache_logf[0].transpose(0, 2, 1), [p[name] for name in SAMPLE_PARAMS])

    heads_t = lambda a: a.reshape(a.shape[0], N_HEADS, HEAD_DIM, a.shape[2]).transpose(0, 3, 1, 2)[None]
    tokens_t = lambda a: a.transpose(0, 2, 1)[None]
    return (y_p, y_s, heads_t(kt_p), heads_t(vt_p), tokens_t(lft_p), cv_p[None],
            heads_t(kt_s), heads_t(vt_s), tokens_t(lft_s), cv_s[None])
```

```python
import functools

import numpy as np
import jax
import jax.numpy as jnp
from jax import lax
from jax.experimental import pallas as pl
from jax.experimental.pallas import tpu as pltpu

D_MODEL = 1024
CONV_DIM = 512
CONV_W = 3
N_HEADS = 8
HEAD_DIM = 64
ATT_DIM = N_HEADS * HEAD_DIM
LN_EPS = 1e-5
DEPTH = 1
ALPHA = (2.0 * DEPTH) ** 0.25
SCALE = HEAD_DIM ** -0.5

LANES = 128
SUBLANES = 8
HEAD_BLOCK = LANES
AUG_DIM = N_HEADS * HEAD_BLOCK
N_SPLIT = 3
EXTRA_ROWS = 16
F_ROWS = 16
VMEM_LIMIT_BYTES = 56 * 1024 * 1024
V_ROWS = HEAD_DIM + 16
LOG2E = 1.4426950408889634
SCORE_LOOKAHEAD = 4

_W_Q = 4 * CONV_DIM
_W_V = _W_Q + 2 * ATT_DIM
_W_F = _W_Q + 3 * ATT_DIM
_W_ZB = _W_F + N_HEADS
_S_XA, _S_GB, _S_GC, _S_ZA, _S_Q, _S_K, _S_ZB, _S_GA, _S_GM = (
    0, 512, 1024, 1536, 2048, 2560, 3072, 3584, 4608)
_T_Q, _T_K, _T_V, _T_F = 0, 512, 1024, 1536

PROMPT_TILE = 256
PROMPT_PARAMS = ("ln_in_g", "ln_in_b", "w_std", "w_t", "b_f_col", "conv_w", "w_pa", "w_pb", "w_out",
                 "ln_g", "ln_b", "route", "ones", "tri")
SAMPLE_PARAMS = ("ln_in_g", "ln_in_b", "w_std", "w_fs", "b_f_row", "w_t", "b_f_col", "conv_w", "w_pa",
                 "w_pb", "w_out", "ln_g", "ln_b", "route", "ones")

F32 = jnp.float32
BF16 = jnp.bfloat16


def _routing_constants():
    route = np.zeros((LANES, 2 * AUG_DIM), np.float32)
    ones = np.zeros((1, 2 * AUG_DIM), np.float32)
    for h in range(N_HEADS):
        base = h * HEAD_BLOCK + (HEAD_DIM if h % 2 == 0 else 0)
        for r in range(N_SPLIT):
            route[r * N_HEADS + h, base + 2 * r] = 1.0
            ones[0, base + 2 * r + 1] = 1.0
            ones[0, AUG_DIM + base + 2 * r] = 1.0
            route[r * N_HEADS + h, AUG_DIM + base + 2 * r + 1] = -1.0
    return route, ones


def _layer_norm(x, g, b):
    mu = jnp.mean(x, axis=-1, keepdims=True)
    xc = x - mu
    var = jnp.mean(xc * xc, axis=-1, keepdims=True)
    return xc * lax.rsqrt(var + LN_EPS) * g + b


def _log_sigmoid(x):
    return jnp.minimum(x, 0.0) - jnp.log1p(jnp.exp(-jnp.abs(x)))


def _split3(x):
    hi = x.astype(BF16)
    r1 = x - hi.astype(F32)
    mid = r1.astype(BF16)
    lo = (r1 - mid.astype(F32)).astype(BF16)
    return hi, mid, lo


def _cumsum(x, axis):
    n = x.shape[axis]
    idx = lax.broadcasted_iota(jnp.int32, x.shape, axis)
    s = 1
    while s < n:
        x = x + jnp.where(idx >= s, pltpu.roll(x, s, axis), 0.0)
        s *= 2
    return x


def _extra_lanes(c, route):
    lane = lax.broadcasted_iota(jnp.int32, c.shape, 1)
    valid = lane < N_HEADS
    packed = jnp.zeros_like(c)
    for r, piece in enumerate(_split3(c)):
        p32 = jnp.where(valid, piece.astype(F32), 0.0)
        packed = packed + (pltpu.roll(p32, r * N_HEADS, 1) if r else p32)
    return jnp.dot(packed.astype(BF16), route, preferred_element_type=F32)


def _store_token_major(dst_ref, rows, data, extras):
    n = data.shape[0]
    lane = lax.broadcasted_iota(jnp.int32, (n, HEAD_BLOCK), 1)
    for h in range(N_HEADS):
        pair = data[:, (h // 2) * HEAD_BLOCK:(h // 2 + 1) * HEAD_BLOCK]
        ex = extras[:, h * HEAD_BLOCK:(h + 1) * HEAD_BLOCK]
        keep = (lane < HEAD_DIM) if h % 2 == 0 else (lane >= HEAD_DIM)
        dst_ref[rows, h * HEAD_BLOCK:(h + 1) * HEAD_BLOCK] = jnp.where(keep, pair, ex).astype(BF16)


def _extra_rows(pieces, h, n, key_side):
    r = lax.broadcasted_iota(jnp.int32, (EXTRA_ROWS, n), 0)
    value_row, sign = (1, -1.0) if key_side else (0, 1.0)
    out = jnp.where((r < 2 * N_SPLIT) & ((r & 1) != value_row), 1.0, 0.0)
    for i, piece in enumerate(pieces):
        out = jnp.where(r == 2 * i + value_row,
                        jnp.broadcast_to(sign * piece[h:h + 1, :], (EXTRA_ROWS, n)), out)
    return out.astype(BF16)


def _store_feature_major(dst_ref, h, data, extra):
    if h % 2 == 0:
        dst_ref[h, 0:HEAD_DIM, :] = data
        dst_ref[h, HEAD_DIM:HEAD_DIM + EXTRA_ROWS, :] = extra
    else:
        dst_ref[h, 0:EXTRA_ROWS, :] = extra
        dst_ref[h, HEAD_DIM:2 * HEAD_DIM, :] = data


def _conv_branch(proj, u_sc, conv_w, conv_ref, w_pa, n):
    first = SUBLANES - (CONV_W - 1)
    u_sc[SUBLANES:SUBLANES + n, :] = proj(_S_GC, CONV_DIM) * proj(_S_XA, CONV_DIM)
    conv = sum(u_sc[first + i:first + i + n, :] * conv_w[i:i + 1, :] for i in range(CONV_W))
    new_conv = u_sc[first + n:SUBLANES + n, :]
    conv_ref[...] = new_conv
    u_sc[first:SUBLANES, :] = new_conv
    y_a = proj(_S_GB, CONV_DIM) * conv * jax.nn.silu(proj(_S_ZA, CONV_DIM))
    return jax.nn.sigmoid(proj(_S_GA, D_MODEL)) * jnp.dot(
        y_a.astype(BF16), w_pa[...], preferred_element_type=F32)


def _merge_and_norm(att, mix_a, h, silu_zb, gate_b, w_pb, w_out, ln_g, ln_b):
    y_b = att * silu_zb
    mixed = mix_a + gate_b * jnp.dot(
        y_b.astype(BF16), w_pb[...], preferred_element_type=F32)
    sub = jnp.dot(mixed.astype(BF16), w_out[...], preferred_element_type=F32)
    return _layer_norm(ALPHA * h + sub, ln_g[...], ln_b[...])


def _attention_t(n_key_tiles, tm, kaug_sc, qta_sc, vt_sc, attt_sc):
    keys = n_key_tiles * tm
    causal = (lax.broadcasted_iota(jnp.int32, (tm, tm), 0)
              <= lax.broadcasted_iota(jnp.int32, (tm, tm), 1))

    def scores(hd):
        return jnp.dot(kaug_sc[0:keys, hd * HEAD_BLOCK:(hd + 1) * HEAD_BLOCK], qta_sc[hd],
                       preferred_element_type=F32)

    pending = [scores(hd) for hd in range(SCORE_LOOKAHEAD)]
    for hd in range(N_HEADS):
        s = pending.pop(0)
        if hd + SCORE_LOOKAHEAD < N_HEADS:
            pending.append(scores(hd + SCORE_LOOKAHEAD))
        blocks = [s[i * tm:(i + 1) * tm, :] for i in range(n_key_tiles)]
        blocks[-1] = jnp.where(causal, blocks[-1], -jnp.inf)
        m = functools.reduce(jnp.maximum, [jnp.max(blk, axis=0, keepdims=True) for blk in blocks])
        o = jnp.zeros((V_ROWS, tm), F32)
        for i, blk in enumerate(blocks):
            o = o + jnp.dot(vt_sc[i, hd * V_ROWS:(hd + 1) * V_ROWS, :], jnp.exp2(blk - m).astype(BF16),
                            preferred_element_type=F32)
        attt_sc[hd * HEAD_DIM:(hd + 1) * HEAD_DIM, :] = o[0:HEAD_DIM, :] / o[HEAD_DIM:HEAD_DIM + 1, :]


def _prompt_kernel(x_ref, ln_in_g, ln_in_b, w_std, w_t, b_f_col, conv_w, w_pa, w_pb, w_out,
                   ln_g, ln_b, route, ones, tri,
                   y_ref, kt_ref, vt_ref, lft_ref, conv_ref,
                   kaug_sc, vt_sc, h_sc, hb_sc, z_sc, qta_sc, attt_sc, zb_sc, gm_sc, u_sc, carryt_sc,
                   *, tm, n_tiles):
    t = pl.program_id(1)

    def norm_in(slot):
        h = _layer_norm(x_ref[...], ln_in_g[...], ln_in_b[...])
        h_sc[slot] = h
        hb_sc[slot] = h.astype(BF16)

    def norm_out():
        y_ref[...] = _layer_norm(z_sc[...], ln_g[...], ln_b[...])

    @pl.when(t == 0)
    def _start_of_sequence():
        u_sc[0:SUBLANES, :] = jnp.zeros((SUBLANES, CONV_DIM), F32)
        carryt_sc[...] = jnp.zeros_like(carryt_sc)
        qta_sc[...] = jnp.zeros_like(qta_sc)
        z_sc[...] = jnp.zeros_like(z_sc)
        ones_row = jnp.where(lax.broadcasted_iota(jnp.int32, (V_ROWS - HEAD_DIM, tm), 0) == 0, 1.0, 0.0)
        for i in range(n_tiles):
            for hd in range(N_HEADS):
                vt_sc[i, hd * V_ROWS + HEAD_DIM:(hd + 1) * V_ROWS, :] = ones_row.astype(BF16)
        norm_in(0)

    @pl.when((t >= 1) & (t <= n_tiles))
    def _mixers():
        j = t - 1
        cur = j & 1

        def proj(col, width):
            return jnp.dot(hb_sc[cur], w_std[:, col:col + width], preferred_element_type=F32)

        def proj_t(row, height):
            return lax.dot_general(w_t[row:row + height, :], hb_sc[cur], (((1,), (1,)), ((), ())),
                                   preferred_element_type=F32)

        norm_out()

        vf_t = proj_t(_T_V, ATT_DIM + F_ROWS)
        qk_t = proj_t(_T_Q, 2 * ATT_DIM)
        kt = qk_t[ATT_DIM:2 * ATT_DIM, :]
        vt = vf_t[0:ATT_DIM, :]
        kt_ref[...] = kt
        vt_ref[...] = vt
        for hd in range(N_HEADS):
            vt_sc[j, hd * V_ROWS:hd * V_ROWS + HEAD_DIM, :] = (
                vt[hd * HEAD_DIM:(hd + 1) * HEAD_DIM, :].astype(BF16))
        logf_t = _log_sigmoid(vf_t[ATT_DIM:ATT_DIM + F_ROWS, :] + b_f_col[...])
        lft_ref[...] = logf_t[:N_HEADS, :]
        sums = jnp.dot(jnp.concatenate(_split3(logf_t), axis=0), tri[...], preferred_element_type=F32)
        c_t = sum(sums[r * F_ROWS:(r + 1) * F_ROWS, :] for r in range(N_SPLIT)) + carryt_sc[:, 0:1]
        carryt_sc[...] = jnp.broadcast_to(c_t[:, tm - 1:tm], carryt_sc.shape)
        zb_sc[...] = jax.nn.silu(proj(_S_ZB, ATT_DIM))
        gm_sc[...] = jax.nn.sigmoid(proj(_S_GM, D_MODEL))
        c_pieces = [piece.astype(F32) for piece in _split3(c_t * LOG2E)]
        qt = qk_t[0:ATT_DIM, :] * (SCALE * LOG2E)
        for hd in range(N_HEADS):
            _store_feature_major(qta_sc, hd, qt[hd * HEAD_DIM:(hd + 1) * HEAD_DIM, :].astype(BF16),
                                 _extra_rows(c_pieces, hd, tm, key_side=False))

        packed_t = jnp.concatenate([piece[0:N_HEADS, :] for piece in c_pieces]
                                   + [jnp.zeros((LANES - N_SPLIT * N_HEADS, tm), F32)], axis=0)
        ex_k = (jnp.dot(packed_t.T.astype(BF16), route[:, AUG_DIM:], preferred_element_type=F32)
                + ones[:, AUG_DIM:])
        _store_token_major(kaug_sc, pl.ds(pl.multiple_of(j * tm, tm), tm), kt.T, ex_k)

        norm_in(t & 1)

        for jj in range(n_tiles):
            pl.when(j == jj)(functools.partial(_attention_t, jj + 1, tm, kaug_sc, qta_sc, vt_sc, attt_sc))

        mix_a = _conv_branch(proj, u_sc, conv_w, conv_ref, w_pa, tm)
        y_b = attt_sc[...].T * zb_sc[...]
        mixed = mix_a + gm_sc[...] * jnp.dot(y_b.astype(BF16), w_pb[...], preferred_element_type=F32)
        z_sc[...] = ALPHA * h_sc[cur] + jnp.dot(mixed.astype(BF16), w_out[...], preferred_element_type=F32)

    @pl.when(t == n_tiles + 1)
    def _drain():
        norm_out()


def _resident(shape):
    return pl.BlockSpec(shape, lambda *_: (0,) * len(shape), pipeline_mode=pl.Buffered(1))


def _run_prompt(x, params, *, tm):
    bsz, t, _ = x.shape
    n_tiles = t // tm
    assert n_tiles * tm == t
    tile = lambda s, lag: jnp.clip(s - lag, 0, n_tiles - 1)
    per_tile = lambda width, lag: pl.BlockSpec((None, tm, width), lambda b, s: (b, tile(s, lag), 0))
    per_tile_t = lambda rows: pl.BlockSpec((None, rows, tm), lambda b, s: (b, 0, tile(s, 1)))
    out_shape = (jax.ShapeDtypeStruct((bsz, t, D_MODEL), F32),
                 jax.ShapeDtypeStruct((bsz, ATT_DIM, t), F32),
                 jax.ShapeDtypeStruct((bsz, ATT_DIM, t), F32),
                 jax.ShapeDtypeStruct((bsz, N_HEADS, t), F32),
                 jax.ShapeDtypeStruct((bsz, CONV_W - 1, CONV_DIM), F32))
    out_specs = (per_tile(D_MODEL, 2), per_tile_t(ATT_DIM), per_tile_t(ATT_DIM), per_tile_t(N_HEADS),
                 pl.BlockSpec((None, CONV_W - 1, CONV_DIM), lambda b, s: (b, 0, 0)))
    scratch = [
        pltpu.VMEM((t, AUG_DIM), BF16),
        pltpu.VMEM((n_tiles, N_HEADS * V_ROWS, tm), BF16),
        pltpu.VMEM((2, tm, D_MODEL), F32),
        pltpu.VMEM((2, tm, D_MODEL), BF16),
        pltpu.VMEM((tm, D_MODEL), F32),
        pltpu.VMEM((N_HEADS, HEAD_BLOCK, tm), BF16),
        pltpu.VMEM((ATT_DIM, tm), F32),
        pltpu.VMEM((tm, ATT_DIM), F32),
        pltpu.VMEM((tm, D_MODEL), F32),
        pltpu.VMEM((SUBLANES + tm, CONV_DIM), F32),
        pltpu.VMEM((F_ROWS, LANES), F32),
    ]
    return pl.pallas_call(
        functools.partial(_prompt_kernel, tm=tm, n_tiles=n_tiles),
        grid=(bsz, n_tiles + 2),
        in_specs=[per_tile(D_MODEL, 0)] + [_resident(p.shape) for p in params],
        out_specs=out_specs,
        out_shape=out_shape,
        scratch_shapes=scratch,
        compiler_params=pltpu.CompilerParams(
            dimension_semantics=("arbitrary", "arbitrary"),
            vmem_limit_bytes=VMEM_LIMIT_BYTES),
        name="trunk_prompt",
    )(x, *params)


def _sample_kernel(x_ref, conv0_ref, pkt_ref, pvt_ref, plft_ref,
                   ln_in_g, ln_in_b, w_std, w_fs, b_f_row, w_t, b_f_col, conv_w, w_pa, w_pb, w_out,
                   ln_g, ln_b, route, ones,
                   y_ref, kt_ref, vt_ref, lft_ref, conv_ref,
                   kta_sc, ktn_sc, hb_sc, qa_sc, att_sc, u_sc, *, ts, n_past):
    @pl.when(pl.program_id(0) == 0)
    def _zero_padding():
        kta_sc[...] = jnp.zeros_like(kta_sc)
        ktn_sc[...] = jnp.zeros_like(ktn_sc)
        hb_sc[...] = jnp.zeros_like(hb_sc)

    h = _layer_norm(x_ref[...], ln_in_g[...], ln_in_b[...])
    hb_sc[0:ts, :] = h.astype(BF16)

    def proj(col, width):
        return jnp.dot(hb_sc[0:ts, :], w_std[:, col:col + width], preferred_element_type=F32)

    u_sc[0:SUBLANES, :] = jnp.zeros((SUBLANES, CONV_DIM), F32)
    u_sc[SUBLANES - (CONV_W - 1):SUBLANES, :] = conv0_ref[...]
    mix_a = _conv_branch(proj, u_sc, conv_w, conv_ref, w_pa, ts)

    proj_t = lax.dot_general(w_t[_T_K:, :], hb_sc[...], (((1,), (1,)), ((), ())),
                             preferred_element_type=F32)
    kt = proj_t[0:ATT_DIM, :]
    vt = proj_t[ATT_DIM:2 * ATT_DIM, :]
    kt_ref[...] = kt[:, :ts]
    vt_ref[...] = vt[:, :ts]
    logf_t = _log_sigmoid(proj_t[2 * ATT_DIM:2 * ATT_DIM + SUBLANES, :] + b_f_col[0:SUBLANES, :])
    lft_ref[...] = logf_t[:, :ts]

    c_past = _cumsum(plft_ref[...], 1)
    c_last = c_past[:, n_past - 1:n_past]
    c_new_t = _cumsum(logf_t, 1) + c_last
    past_pieces = [piece.astype(F32) for piece in _split3(c_past)]
    new_pieces = [piece.astype(F32) for piece in _split3(c_new_t)]
    for hd in range(N_HEADS):
        rows = slice(hd * HEAD_DIM, (hd + 1) * HEAD_DIM)
        _store_feature_major(kta_sc, hd, pkt_ref[rows, :].astype(BF16),
                             _extra_rows(past_pieces, hd, n_past, key_side=True))
        _store_feature_major(ktn_sc, hd, kt[rows, :].astype(BF16),
                             _extra_rows(new_pieces, hd, LANES, key_side=True))

    eye = (lax.broadcasted_iota(jnp.int32, (SUBLANES, LANES), 0)
           == lax.broadcasted_iota(jnp.int32, (SUBLANES, LANES), 1))
    c_last_row = jnp.sum(jnp.where(eye, jnp.broadcast_to(c_last, (SUBLANES, LANES)), 0.0),
                         axis=0, keepdims=True)
    logf = _log_sigmoid(jnp.dot(hb_sc[0:ts, :], w_fs[...], preferred_element_type=F32) + b_f_row[...])
    c_q = _cumsum(logf, 0) + c_last_row
    ex_q = _extra_lanes(c_q, route[:, :AUG_DIM]) + ones[:, :AUG_DIM]
    _store_token_major(qa_sc, slice(None), proj(_S_Q, ATT_DIM) * SCALE, ex_q)

    causal = (lax.broadcasted_iota(jnp.int32, (ts, LANES), 1)
              <= lax.broadcasted_iota(jnp.int32, (ts, LANES), 0))
    lane = lax.broadcasted_iota(jnp.int32, (ts, HEAD_BLOCK), 1)
    contract_lanes = (((1,), (1,)), ((), ()))
    for pair in range(N_HEADS // 2):
        rows = slice(pair * HEAD_BLOCK, (pair + 1) * HEAD_BLOCK)
        vt_past = pvt_ref[rows, :].astype(BF16)
        vt_new = vt[rows, :].astype(BF16)
        outs = []
        for hd in (2 * pair, 2 * pair + 1):
            qa_h = qa_sc[:, hd * HEAD_BLOCK:(hd + 1) * HEAD_BLOCK]
            s_past = jnp.dot(qa_h, kta_sc[hd], preferred_element_type=F32)
            s_new = jnp.where(causal, jnp.dot(qa_h, ktn_sc[hd], preferred_element_type=F32), -jnp.inf)
            m = jnp.maximum(jnp.max(s_past, axis=-1, keepdims=True), jnp.max(s_new, axis=-1, keepdims=True))
            p_past = jnp.exp(s_past - m)
            p_new = jnp.exp(s_new - m)
            l = jnp.sum(p_past, axis=-1, keepdims=True) + jnp.sum(p_new, axis=-1, keepdims=True)
            o = (lax.dot_general(p_past.astype(BF16), vt_past, contract_lanes, preferred_element_type=F32)
                 + lax.dot_general(p_new.astype(BF16), vt_new, contract_lanes, preferred_element_type=F32))
            outs.append(o / l)
        att_sc[:, rows] = jnp.where(lane < HEAD_DIM, outs[0], outs[1])

    y_ref[...] = _merge_and_norm(att_sc[...], mix_a, h, jax.nn.silu(proj(_S_ZB, ATT_DIM)),
                                 jax.nn.sigmoid(proj(_S_GM, D_MODEL)), w_pb, w_out, ln_g, ln_b)


def _run_sample(x, conv0, pkt, pvt, plft, params):
    bsz, ts, _ = x.shape
    n_past = pkt.shape[2]
    assert ts <= LANES and ts % SUBLANES == 0
    per_batch = lambda rows, width: pl.BlockSpec((None, rows, width), lambda b: (b, 0, 0))
    out_shape = (jax.ShapeDtypeStruct((bsz, ts, D_MODEL), F32),
                 jax.ShapeDtypeStruct((bsz, ATT_DIM, ts), F32),
                 jax.ShapeDtypeStruct((bsz, ATT_DIM, ts), F32),
                 jax.ShapeDtypeStruct((bsz, N_HEADS, ts), F32),
                 jax.ShapeDtypeStruct((bsz, CONV_W - 1, CONV_DIM), F32))
    out_specs = (per_batch(ts, D_MODEL), per_batch(ATT_DIM, ts), per_batch(ATT_DIM, ts),
                 per_batch(N_HEADS, ts), per_batch(CONV_W - 1, CONV_DIM))
    in_specs = [per_batch(ts, D_MODEL), per_batch(CONV_W - 1, CONV_DIM), per_batch(ATT_DIM, n_past),
                per_batch(ATT_DIM, n_past), per_batch(N_HEADS, n_past)]
    scratch = [
        pltpu.VMEM((N_HEADS, HEAD_BLOCK, n_past), BF16),
        pltpu.VMEM((N_HEADS, HEAD_BLOCK, LANES), BF16),
        pltpu.VMEM((LANES, D_MODEL), BF16),
        pltpu.VMEM((ts, AUG_DIM), BF16),
        pltpu.VMEM((ts, ATT_DIM), F32),
        pltpu.VMEM((SUBLANES + ts, CONV_DIM), F32),
    ]
    return pl.pallas_call(
        functools.partial(_sample_kernel, ts=ts, n_past=n_past),
        grid=(bsz,),
        in_specs=in_specs + [_resident(p.shape) for p in params],
        out_specs=out_specs,
        out_shape=out_shape,
        scratch_shapes=scratch,
        compiler_params=pltpu.CompilerParams(
            dimension_semantics=("arbitrary",),
            vmem_limit_bytes=VMEM_LIMIT_BYTES),
        name="trunk_sample",
    )(x, conv0, pkt, pvt, plft, *params)


def kernel(x_prompt, x_sample, cache_k, cache_v, cache_logf, state_conv, ln_in_g, ln_in_b, w_in, b_f,
           conv_w, w_proj_a, w_proj_b, w_out, ln_g, ln_b):
    assert w_in.shape[0] == DEPTH
    route_np, ones_np = _routing_constants()
    w = w_in[0]
    row = lambda a: a.reshape(1, -1).astype(F32)
    w_t = jnp.concatenate([w[:, _W_Q:_W_F + N_HEADS],
                           jnp.zeros((D_MODEL, F_ROWS - N_HEADS), w.dtype)], axis=1).T.astype(BF16)
    tm = PROMPT_TILE
    p = dict(
        ln_in_g=row(ln_in_g), ln_in_b=row(ln_in_b),
        w_std=jnp.concatenate([w[:, :_W_V], w[:, _W_ZB:]], axis=1).astype(BF16),
        w_fs=jnp.pad(w[:, _W_F:_W_F + N_HEADS], ((0, 0), (0, LANES - N_HEADS))).astype(BF16),
        b_f_row=jnp.pad(row(b_f[0]), ((0, 0), (0, LANES - N_HEADS))),
        w_t=w_t,
        b_f_col=jnp.pad(b_f[0].astype(F32).reshape(N_HEADS, 1), ((0, F_ROWS - N_HEADS), (0, 0))),
        conv_w=conv_w[0].astype(F32),
        w_pa=w_proj_a[0].astype(BF16), w_pb=w_proj_b[0].astype(BF16), w_out=w_out[0].astype(BF16),
        ln_g=row(ln_g[0]), ln_b=row(ln_b[0]),
        route=jnp.asarray(route_np, BF16), ones=jnp.asarray(ones_np, F32),
        tri=jnp.asarray(np.triu(np.ones((tm, tm), np.float32)), BF16),
    )
    bs = x_sample.shape[0]
    n_past = cache_k.shape[2]
    feature_major = lambda a: a.transpose(0, 2, 3, 1).reshape(bs, ATT_DIM, n_past)

    y_p, kt_p, vt_p, lft_p, cv_p = _run_prompt(x_prompt, [p[name] for name in PROMPT_PARAMS], tm=tm)
    y_s, kt_s, vt_s, lft_s, cv_s = _run_sample(
        x_sample, state_conv[0], feature_major(cache_k[0]), feature_major(cache_v[0]),
        cache_logf[0].transpose(0, 2, 1), [p[name] for name in SAMPLE_PARAMS])

    heads_t = lambda a: a.reshape(a.shape[0], N_HEADS, HEAD_DIM, a.shape[2]).transpose(0, 3, 1, 2)[None]
    tokens_t = lambda a: a.transpose(0, 2, 1)[None]
    return (y_p, y_s, heads_t(kt_p), heads_t(vt_p), tokens_t(lft_p), cv_p[None],
            heads_t(kt_s), heads_t(vt_s), tokens_t(lft_s), cv_s[None])
```

```python
import functools

import numpy as np
import jax
import jax.numpy as jnp
from jax import lax
from jax.experimental import pallas as pl
from jax.experimental.pallas import tpu as pltpu

D_MODEL = 1024
CONV_DIM = 512
CONV_W = 3
N_HEADS = 8
HEAD_DIM = 64
ATT_DIM = N_HEADS * HEAD_DIM
LN_EPS = 1e-5
DEPTH = 1
ALPHA = (2.0 * DEPTH) ** 0.25
SCALE = HEAD_DIM ** -0.5

LANES = 128
SUBLANES = 8
HEAD_BLOCK = LANES
AUG_DIM = N_HEADS * HEAD_BLOCK
N_SPLIT = 3
EXTRA_ROWS = 16
F_ROWS = 16
VMEM_LIMIT_BYTES = 56 * 1024 * 1024
V_ROWS = HEAD_DIM + 16
LOG2E = 1.4426950408889634
SCORE_LOOKAHEAD = 4

_W_Q = 4 * CONV_DIM
_W_V = _W_Q + 2 * ATT_DIM
_W_F = _W_Q + 3 * ATT_DIM
_W_ZB = _W_F + N_HEADS
_S_XA, _S_GB, _S_GC, _S_ZA, _S_Q, _S_K, _S_ZB, _S_GA, _S_GM = (
    0, 512, 1024, 1536, 2048, 2560, 3072, 3584, 4608)
_T_Q, _T_K, _T_V, _T_F = 0, 512, 1024, 1536

PROMPT_TILE = 256
PROMPT_PARAMS = ("ln_in_g", "ln_in_b", "w_std", "w_t", "b_f_col", "conv_w", "w_pa", "w_pb", "w_out",
                 "ln_g", "ln_b", "route", "ones", "tri")
SAMPLE_PARAMS = ("ln_in_g", "ln_in_b", "w_std", "w_fs", "b_f_row", "w_t", "b_f_col", "conv_w", "w_pa",
                 "w_pb", "w_out", "ln_g", "ln_b", "route", "ones")

F32 = jnp.float32
BF16 = jnp.bfloat16


def _routing_constants():
    route = np.zeros((LANES, 2 * AUG_DIM), np.float32)
    ones = np.zeros((1, 2 * AUG_DIM), np.float32)
    for h in range(N_HEADS):
        base = h * HEAD_BLOCK + (HEAD_DIM if h % 2 == 0 else 0)
        for r in range(N_SPLIT):
            route[r * N_HEADS + h, base + 2 * r] = 1.0
            ones[0, base + 2 * r + 1] = 1.0
            ones[0, AUG_DIM + base + 2 * r] = 1.0
            route[r * N_HEADS + h, AUG_DIM + base + 2 * r + 1] = -1.0
    return route, ones


def _layer_norm(x, g, b):
    mu = jnp.mean(x, axis=-1, keepdims=True)
    xc = x - mu
    var = jnp.mean(xc * xc, axis=-1, keepdims=True)
    return xc * lax.rsqrt(var + LN_EPS) * g + b


def _log_sigmoid(x):
    return jnp.minimum(x, 0.0) - jnp.log1p(jnp.exp(-jnp.abs(x)))


def _split3(x):
    hi = x.astype(BF16)
    r1 = x - hi.astype(F32)
    mid = r1.astype(BF16)
    lo = (r1 - mid.astype(F32)).astype(BF16)
    return hi, mid, lo


def _cumsum(x, axis):
    n = x.shape[axis]
    idx = lax.broadcasted_iota(jnp.int32, x.shape, axis)
    s = 1
    while s < n:
        x = x + jnp.where(idx >= s, pltpu.roll(x, s, axis), 0.0)
        s *= 2
    return x


def _extra_lanes(c, route):
    lane = lax.broadcasted_iota(jnp.int32, c.shape, 1)
    valid = lane < N_HEADS
    packed = jnp.zeros_like(c)
    for r, piece in enumerate(_split3(c)):
        p32 = jnp.where(valid, piece.astype(F32), 0.0)
        packed = packed + (pltpu.roll(p32, r * N_HEADS, 1) if r else p32)
    return jnp.dot(packed.astype(BF16), route, preferred_element_type=F32)


def _store_token_major(dst_ref, rows, data, extras):
    n = data.shape[0]
    lane = lax.broadcasted_iota(jnp.int32, (n, HEAD_BLOCK), 1)
    for h in range(N_HEADS):
        pair = data[:, (h // 2) * HEAD_BLOCK:(h // 2 + 1) * HEAD_BLOCK]
        ex = extras[:, h * HEAD_BLOCK:(h + 1) * HEAD_BLOCK]
        keep = (lane < HEAD_DIM) if h % 2 == 0 else (lane >= HEAD_DIM)
        dst_ref[rows, h * HEAD_BLOCK:(h + 1) * HEAD_BLOCK] = jnp.where(keep, pair, ex).astype(BF16)


def _extra_rows(pieces, h, n, key_side):
    r = lax.broadcasted_iota(jnp.int32, (EXTRA_ROWS, n), 0)
    value_row, sign = (1, -1.0) if key_side else (0, 1.0)
    out = jnp.where((r < 2 * N_SPLIT) & ((r & 1) != value_row), 1.0, 0.0)
    for i, piece in enumerate(pieces):
        out = jnp.where(r == 2 * i + value_row,
                        jnp.broadcast_to(sign * piece[h:h + 1, :], (EXTRA_ROWS, n)), out)
    return out.astype(BF16)


def _store_feature_major(dst_ref, h, data, extra):
    if h % 2 == 0:
        dst_ref[h, 0:HEAD_DIM, :] = data
        dst_ref[h, HEAD_DIM:HEAD_DIM + EXTRA_ROWS, :] = extra
    else:
        dst_ref[h, 0:EXTRA_ROWS, :] = extra
        dst_ref[h, HEAD_DIM:2 * HEAD_DIM, :] = data


def _conv_branch(proj, u_sc, conv_w, conv_ref, w_pa, n):
    first = SUBLANES - (CONV_W - 1)
    u_sc[SUBLANES:SUBLANES + n, :] = proj(_S_GC, CONV_DIM) * proj(_S_XA, CONV_DIM)
    conv = sum(u_sc[first + i:first + i + n, :] * conv_w[i:i + 1, :] for i in range(CONV_W))
    new_conv = u_sc[first + n:SUBLANES + n, :]
    conv_ref[...] = new_conv
    u_sc[first:SUBLANES, :] = new_conv
    y_a = proj(_S_GB, CONV_DIM) * conv * jax.nn.silu(proj(_S_ZA, CONV_DIM))
    return jax.nn.sigmoid(proj(_S_GA, D_MODEL)) * jnp.dot(
        y_a.astype(BF16), w_pa[...], preferred_element_type=F32)


def _merge_and_norm(att, mix_a, h, silu_zb, gate_b, w_pb, w_out, ln_g, ln_b):
    y_b = att * silu_zb
    mixed = mix_a + gate_b * jnp.dot(
        y_b.astype(BF16), w_pb[...], preferred_element_type=F32)
    sub = jnp.dot(mixed.astype(BF16), w_out[...], preferred_element_type=F32)
    return _layer_norm(ALPHA * h + sub, ln_g[...], ln_b[...])


def _attention_t(n_key_tiles, tm, kaug_sc, qta_sc, vt_sc, attt_sc):
    keys = n_key_tiles * tm
    causal = (lax.broadcasted_iota(jnp.int32, (tm, tm), 0)
              <= lax.broadcasted_iota(jnp.int32, (tm, tm), 1))

    def scores(hd):
        return jnp.dot(kaug_sc[0:keys, hd * HEAD_BLOCK:(hd + 1) * HEAD_BLOCK], qta_sc[hd],
                       preferred_element_type=F32)

    pending = [scores(hd) for hd in range(SCORE_LOOKAHEAD)]
    for hd in range(N_HEADS):
        s = pending.pop(0)
        if hd + SCORE_LOOKAHEAD < N_HEADS:
            pending.append(scores(hd + SCORE_LOOKAHEAD))
        blocks = [s[i * tm:(i + 1) * tm, :] for i in range(n_key_tiles)]
        blocks[-1] = jnp.where(causal, blocks[-1], -jnp.inf)
        m = functools.reduce(jnp.maximum, [jnp.max(blk, axis=0, keepdims=True) for blk in blocks])
        o = jnp.zeros((V_ROWS, tm), F32)
        for i, blk in enumerate(blocks):
            o = o + jnp.dot(vt_sc[i, hd * V_ROWS:(hd + 1) * V_ROWS, :], jnp.exp2(blk - m).astype(BF16),
                            preferred_element_type=F32)
        attt_sc[hd * HEAD_DIM:(hd + 1) * HEAD_DIM, :] = o[0:HEAD_DIM, :] / o[HEAD_DIM:HEAD_DIM + 1, :]


def _prompt_kernel(x_ref, ln_in_g, ln_in_b, w_std, w_t, b_f_col, conv_w, w_pa, w_pb, w_out,
                   ln_g, ln_b, route, ones, tri,
                   y_ref, kt_ref, vt_ref, lft_ref, conv_ref,
                   kaug_sc, vt_sc, h_sc, hb_sc, z_sc, qta_sc, attt_sc, zb_sc, gm_sc, u_sc, carryt_sc,
                   *, tm, n_tiles, n_total):
    t = pl.program_id(0)

    def norm_in(slot):
        h = _layer_norm(x_ref[...], ln_in_g[...], ln_in_b[...])
        h_sc[slot] = h
        hb_sc[slot] = h.astype(BF16)

    def norm_out():
        y_ref[...] = _layer_norm(z_sc[...], ln_g[...], ln_b[...])

    @pl.when(t == 0)
    def _fill():
        z_sc[...] = jnp.zeros_like(z_sc)
        norm_in(0)

    @pl.when((t >= 1) & (t <= n_total))
    def _mixers():
        j = lax.rem(t - 1, n_tiles)
        cur = (t - 1) & 1

        @pl.when(j == 0)
        def _start_of_sequence():
            u_sc[0:SUBLANES, :] = jnp.zeros((SUBLANES, CONV_DIM), F32)
            carryt_sc[...] = jnp.zeros_like(carryt_sc)
            qta_sc[...] = jnp.zeros_like(qta_sc)
            ones_row = jnp.where(lax.broadcasted_iota(jnp.int32, (V_ROWS - HEAD_DIM, tm), 0) == 0, 1.0, 0.0)
            for i in range(n_tiles):
                for hd in range(N_HEADS):
                    vt_sc[i, hd * V_ROWS + HEAD_DIM:(hd + 1) * V_ROWS, :] = ones_row.astype(BF16)

        def proj(col, width):
            return jnp.dot(hb_sc[cur], w_std[:, col:col + width], preferred_element_type=F32)

        def proj_t(row, height):
            return lax.dot_general(w_t[row:row + height, :], hb_sc[cur], (((1,), (1,)), ((), ())),
                                   preferred_element_type=F32)

        norm_out()

        vf_t = proj_t(_T_V, ATT_DIM + F_ROWS)
        qk_t = proj_t(_T_Q, 2 * ATT_DIM)
        kt = qk_t[ATT_DIM:2 * ATT_DIM, :]
        vt = vf_t[0:ATT_DIM, :]
        kt_ref[...] = kt
        vt_ref[...] = vt
        for hd in range(N_HEADS):
            vt_sc[j, hd * V_ROWS:hd * V_ROWS + HEAD_DIM, :] = (
                vt[hd * HEAD_DIM:(hd + 1) * HEAD_DIM, :].astype(BF16))
        logf_t = _log_sigmoid(vf_t[ATT_DIM:ATT_DIM + F_ROWS, :] + b_f_col[...])
        lft_ref[...] = logf_t[:N_HEADS, :]
        sums = jnp.dot(jnp.concatenate(_split3(logf_t), axis=0), tri[...], preferred_element_type=F32)
        c_t = sum(sums[r * F_ROWS:(r + 1) * F_ROWS, :] for r in range(N_SPLIT)) + carryt_sc[:, 0:1]
        carryt_sc[...] = jnp.broadcast_to(c_t[:, tm - 1:tm], carryt_sc.shape)
        zb_sc[...] = jax.nn.silu(proj(_S_ZB, ATT_DIM))
        gm_sc[...] = jax.nn.sigmoid(proj(_S_GM, D_MODEL))
        c_pieces = [piece.astype(F32) for piece in _split3(c_t * LOG2E)]
        qt = qk_t[0:ATT_DIM, :] * (SCALE * LOG2E)
        for hd in range(N_HEADS):
            _store_feature_major(qta_sc, hd, qt[hd * HEAD_DIM:(hd + 1) * HEAD_DIM, :].astype(BF16),
                                 _extra_rows(c_pieces, hd, tm, key_side=False))

        packed_t = jnp.concatenate([piece[0:N_HEADS, :] for piece in c_pieces]
                                   + [jnp.zeros((LANES - N_SPLIT * N_HEADS, tm), F32)], axis=0)
        ex_k = (jnp.dot(packed_t.T.astype(BF16), route[:, AUG_DIM:], preferred_element_type=F32)
                + ones[:, AUG_DIM:])
        _store_token_major(kaug_sc, pl.ds(pl.multiple_of(j * tm, tm), tm), kt.T, ex_k)

        norm_in(t & 1)

        for jj in range(n_tiles):
            pl.when(j == jj)(functools.partial(_attention_t, jj + 1, tm, kaug_sc, qta_sc, vt_sc, attt_sc))

        mix_a = _conv_branch(proj, u_sc, conv_w, conv_ref, w_pa, tm)
        y_b = attt_sc[...].T * zb_sc[...]
        mixed = mix_a + gm_sc[...] * jnp.dot(y_b.astype(BF16), w_pb[...], preferred_element_type=F32)
        z_sc[...] = ALPHA * h_sc[cur] + jnp.dot(mixed.astype(BF16), w_out[...], preferred_element_type=F32)

    @pl.when(t == n_total + 1)
    def _drain():
        norm_out()


def _resident(shape):
    return pl.BlockSpec(shape, lambda *_: (0,) * len(shape), pipeline_mode=pl.Buffered(1))


def _run_prompt(x, params, *, tm):
    bsz, t, _ = x.shape
    n_tiles = t // tm
    assert n_tiles * tm == t
    n_total = bsz * n_tiles
    tile = lambda s, lag: jnp.clip(s - lag, 0, n_total - 1)
    per_tile = lambda width, lag: pl.BlockSpec(
        (None, tm, width), lambda s: (tile(s, lag) // n_tiles, tile(s, lag) % n_tiles, 0))
    per_tile_t = lambda rows: pl.BlockSpec(
        (None, rows, tm), lambda s: (tile(s, 1) // n_tiles, 0, tile(s, 1) % n_tiles))
    out_shape = (jax.ShapeDtypeStruct((bsz, t, D_MODEL), F32),
                 jax.ShapeDtypeStruct((bsz, ATT_DIM, t), F32),
                 jax.ShapeDtypeStruct((bsz, ATT_DIM, t), F32),
                 jax.ShapeDtypeStruct((bsz, N_HEADS, t), F32),
                 jax.ShapeDtypeStruct((bsz, CONV_W - 1, CONV_DIM), F32))
    out_specs = (per_tile(D_MODEL, 2), per_tile_t(ATT_DIM), per_tile_t(ATT_DIM), per_tile_t(N_HEADS),
                 pl.BlockSpec((None, CONV_W - 1, CONV_DIM), lambda s: (tile(s, 1) // n_tiles, 0, 0)))
    scratch = [
        pltpu.VMEM((t, AUG_DIM), BF16),
        pltpu.VMEM((n_tiles, N_HEADS * V_ROWS, tm), BF16),
        pltpu.VMEM((2, tm, D_MODEL), F32),
        pltpu.VMEM((2, tm, D_MODEL), BF16),
        pltpu.VMEM((tm, D_MODEL), F32),
        pltpu.VMEM((N_HEADS, HEAD_BLOCK, tm), BF16),
        pltpu.VMEM((ATT_DIM, tm), F32),
        pltpu.VMEM((tm, ATT_DIM), F32),
        pltpu.VMEM((tm, D_MODEL), F32),
        pltpu.VMEM((SUBLANES + tm, CONV_DIM), F32),
        pltpu.VMEM((F_ROWS, LANES), F32),
    ]
    return pl.pallas_call(
        functools.partial(_prompt_kernel, tm=tm, n_tiles=n_tiles, n_total=n_total),
        grid=(n_total + 2,),
        in_specs=[per_tile(D_MODEL, 0)] + [_resident(p.shape) for p in params],
        out_specs=out_specs,
        out_shape=out_shape,
        scratch_shapes=scratch,
        compiler_params=pltpu.CompilerParams(
            dimension_semantics=("arbitrary",),
            vmem_limit_bytes=VMEM_LIMIT_BYTES),
        name="trunk_prompt",
    )(x, *params)


def _sample_kernel(x_ref, conv0_ref, pkt_ref, pvt_ref, plft_ref,
                   ln_in_g, ln_in_b, w_std, w_fs, b_f_row, w_t, b_f_col, conv_w, w_pa, w_pb, w_out,
                   ln_g, ln_b, route, ones,
                   y_ref, kt_ref, vt_ref, lft_ref, conv_ref,
                   kta_sc, ktn_sc, hb_sc, qa_sc, att_sc, u_sc, *, ts, n_past):
    @pl.when(pl.program_id(0) == 0)
    def _zero_padding():
        kta_sc[...] = jnp.zeros_like(kta_sc)
        ktn_sc[...] = jnp.zeros_like(ktn_sc)
        hb_sc[...] = jnp.zeros_like(hb_sc)

    h = _layer_norm(x_ref[...], ln_in_g[...], ln_in_b[...])
    hb_sc[0:ts, :] = h.astype(BF16)

    def proj(col, width):
        return jnp.dot(hb_sc[0:ts, :], w_std[:, col:col + width], preferred_element_type=F32)

    u_sc[0:SUBLANES, :] = jnp.zeros((SUBLANES, CONV_DIM), F32)
    u_sc[SUBLANES - (CONV_W - 1):SUBLANES, :] = conv0_ref[...]
    mix_a = _conv_branch(proj, u_sc, conv_w, conv_ref, w_pa, ts)

    proj_t = lax.dot_general(w_t[_T_K:, :], hb_sc[...], (((1,), (1,)), ((), ())),
                             preferred_element_type=F32)
    kt = proj_t[0:ATT_DIM, :]
    vt = proj_t[ATT_DIM:2 * ATT_DIM, :]
    kt_ref[...] = kt[:, :ts]
    vt_ref[...] = vt[:, :ts]
    logf_t = _log_sigmoid(proj_t[2 * ATT_DIM:2 * ATT_DIM + SUBLANES, :] + b_f_col[0:SUBLANES, :])
    lft_ref[...] = logf_t[:, :ts]

    c_past = _cumsum(plft_ref[...], 1)
    c_last = c_past[:, n_past - 1:n_past]
    c_new_t = _cumsum(logf_t, 1) + c_last
    past_pieces = [piece.astype(F32) for piece in _split3(c_past)]
    new_pieces = [piece.astype(F32) for piece in _split3(c_new_t)]
    for hd in range(N_HEADS):
        rows = slice(hd * HEAD_DIM, (hd + 1) * HEAD_DIM)
        _store_feature_major(kta_sc, hd, pkt_ref[rows, :].astype(BF16),
                             _extra_rows(past_pieces, hd, n_past, key_side=True))
        _store_feature_major(ktn_sc, hd, kt[rows, :].astype(BF16),
                             _extra_rows(new_pieces, hd, LANES, key_side=True))

    eye = (lax.broadcasted_iota(jnp.int32, (SUBLANES, LANES), 0)
           == lax.broadcasted_iota(jnp.int32, (SUBLANES, LANES), 1))
    c_last_row = jnp.sum(jnp.where(eye, jnp.broadcast_to(c_last, (SUBLANES, LANES)), 0.0),
                         axis=0, keepdims=True)
    logf = _log_sigmoid(jnp.dot(hb_sc[0:ts, :], w_fs[...], preferred_element_type=F32) + b_f_row[...])
    c_q = _cumsum(logf, 0) + c_last_row
    ex_q = _extra_lanes(c_q, route[:, :AUG_DIM]) + ones[:, :AUG_DIM]
    _store_token_major(qa_sc, slice(None), proj(_S_Q, ATT_DIM) * SCALE, ex_q)

    causal = (lax.broadcasted_iota(jnp.int32, (ts, LANES), 1)
              <= lax.broadcasted_iota(jnp.int32, (ts, LANES), 0))
    lane = lax.broadcasted_iota(jnp.int32, (ts, HEAD_BLOCK), 1)
    contract_lanes = (((1,), (1,)), ((), ()))
    for pair in range(N_HEADS // 2):
        rows = slice(pair * HEAD_BLOCK, (pair + 1) * HEAD_BLOCK)
        vt_past = pvt_ref[rows, :].astype(BF16)
        vt_new = vt[rows, :].astype(BF16)
        outs = []
        for hd in (2 * pair, 2 * pair + 1):
            qa_h = qa_sc[:, hd * HEAD_BLOCK:(hd + 1) * HEAD_BLOCK]
            s_past = jnp.dot(qa_h, kta_sc[hd], preferred_element_type=F32)
            s_new = jnp.where(causal, jnp.dot(qa_h, ktn_sc[hd], preferred_element_type=F32), -jnp.inf)
            m = jnp.maximum(jnp.max(s_past, axis=-1, keepdims=True), jnp.max(s_new, axis=-1, keepdims=True))
            p_past = jnp.exp(s_past - m)
            p_new = jnp.exp(s_new - m)
            l = jnp.sum(p_past, axis=-1, keepdims=True) + jnp.sum(p_new, axis=-1, keepdims=True)
            o = (lax.dot_general(p_past.astype(BF16), vt_past, contract_lanes, preferred_element_type=F32)
                 + lax.dot_general(p_new.astype(BF16), vt_new, contract_lanes, preferred_element_type=F32))
            outs.append(o / l)
        att_sc[:, rows] = jnp.where(lane < HEAD_DIM, outs[0], outs[1])

    y_ref[...] = _merge_and_norm(att_sc[...], mix_a, h, jax.nn.silu(proj(_S_ZB, ATT_DIM)),
                                 jax.nn.sigmoid(proj(_S_GM, D_MODEL)), w_pb, w_out, ln_g, ln_b)


def _run_sample(x, conv0, pkt, pvt, plft, params):
    bsz, ts, _ = x.shape
    n_past = pkt.shape[2]
    assert ts <= LANES and ts % SUBLANES == 0
    per_batch = lambda rows, width: pl.BlockSpec((None, rows, width), lambda b: (b, 0, 0))
    out_shape = (jax.ShapeDtypeStruct((bsz, ts, D_MODEL), F32),
                 jax.ShapeDtypeStruct((bsz, ATT_DIM, ts), F32),
                 jax.ShapeDtypeStruct((bsz, ATT_DIM, ts), F32),
                 jax.ShapeDtypeStruct((bsz, N_HEADS, ts), F32),
                 jax.ShapeDtypeStruct((bsz, CONV_W - 1, CONV_DIM), F32))
    out_specs = (per_batch(ts, D_MODEL), per_batch(ATT_DIM, ts), per_batch(ATT_DIM, ts),
                 per_batch(N_HEADS, ts), per_batch(CONV_W - 1, CONV_DIM))
    in_specs = [per_batch(ts, D_MODEL), per_batch(CONV_W - 1, CONV_DIM), per_batch(ATT_DIM, n_past),
                per_batch(ATT_DIM, n_past), per_batch(N_HEADS, n_past)]
    scratch = [
        pltpu.VMEM((N_HEADS, HEAD_BLOCK, n_past), BF16),
        pltpu.VMEM((N_HEADS, HEAD_BLOCK, LANES), BF16),
        pltpu.VMEM((LANES, D_MODEL), BF16),
        pltpu.VMEM((ts, AUG_DIM), BF16),
        pltpu.VMEM((ts, ATT_DIM), F32),
        pltpu.VMEM((SUBLANES + ts, CONV_DIM), F32),
    ]
    return pl.pallas_call(
        functools.partial(_sample_kernel, ts=ts, n_past=n_past),
        grid=(bsz,),
        in_specs=in_specs + [_resident(p.shape) for p in params],
        out_specs=out_specs,
        out_shape=out_shape,
        scratch_shapes=scratch,
        compiler_params=pltpu.CompilerParams(
            dimension_semantics=("arbitrary",),
            vmem_limit_bytes=VMEM_LIMIT_BYTES),
        name="trunk_sample",
    )(x, conv0, pkt, pvt, plft, *params)


def kernel(x_prompt, x_sample, cache_k, cache_v, cache_logf, state_conv, ln_in_g, ln_in_b, w_in, b_f,
           conv_w, w_proj_a, w_proj_b, w_out, ln_g, ln_b):
    assert w_in.shape[0] == DEPTH
    route_np, ones_np = _routing_constants()
    w = w_in[0]
    row = lambda a: a.reshape(1, -1).astype(F32)
    w_t = jnp.concatenate([w[:, _W_Q:_W_F + N_HEADS],
                           jnp.zeros((D_MODEL, F_ROWS - N_HEADS), w.dtype)], axis=1).T.astype(BF16)
    tm = PROMPT_TILE
    p = dict(
        ln_in_g=row(ln_in_g), ln_in_b=row(ln_in_b),
        w_std=jnp.concatenate([w[:, :_W_V], w[:, _W_ZB:]], axis=1).astype(BF16),
        w_fs=jnp.pad(w[:, _W_F:_W_F + N_HEADS], ((0, 0), (0, LANES - N_HEADS))).astype(BF16),
        b_f_row=jnp.pad(row(b_f[0]), ((0, 0), (0, LANES - N_HEADS))),
        w_t=w_t,
        b_f_col=jnp.pad(b_f[0].astype(F32).reshape(N_HEADS, 1), ((0, F_ROWS - N_HEADS), (0, 0))),
        conv_w=conv_w[0].astype(F32),
        w_pa=w_proj_a[0].astype(BF16), w_pb=w_proj_b[0].astype(BF16), w_out=w_out[0].astype(BF16),
        ln_g=row(ln_g[0]), ln_b=row(ln_b[0]),
        route=jnp.asarray(route_np, BF16), ones=jnp.asarray(ones_np, F32),
        tri=jnp.asarray(np.triu(np.ones((tm, tm), np.float32)), BF16),
    )
    bs = x_sample.shape[0]
    n_past = cache_k.shape[2]
    feature_major = lambda a: a.transpose(0, 2, 3, 1).reshape(bs, ATT_DIM, n_past)

    y_p, kt_p, vt_p, lft_p, cv_p = _run_prompt(x_prompt, [p[name] for name in PROMPT_PARAMS], tm=tm)
    y_s, kt_s, vt_s, lft_s, cv_s = _run_sample(
        x_sample, state_conv[0], feature_major(cache_k[0]), feature_major(cache_v[0]),
        cache_logf[0].transpose(0, 2, 1), [p[name] for name in SAMPLE_PARAMS])

    heads_t = lambda a: a.reshape(a.shape[0], N_HEADS, HEAD_DIM, a.shape[2]).transpose(0, 3, 1, 2)[None]
    tokens_t = lambda a: a.transpose(0, 2, 1)[None]
    return (y_p, y_s, heads_t(kt_p), heads_t(vt_p), tokens_t(lft_p), cv_p[None],
            heads_t(kt_s), heads_t(vt_s), tokens_t(lft_s), cv_s[None])
```

```python
import functools

import numpy as np
import jax
import jax.numpy as jnp
from jax import lax
from jax.experimental import pallas as pl
from jax.experimental.pallas import tpu as pltpu

D_MODEL = 1024
CONV_DIM = 512
CONV_W = 3
N_HEADS = 8
HEAD_DIM = 64
ATT_DIM = N_HEADS * HEAD_DIM
LN_EPS = 1e-5
DEPTH = 1
ALPHA = (2.0 * DEPTH) ** 0.25
SCALE = HEAD_DIM ** -0.5

LANES = 128
SUBLANES = 8
HEAD_BLOCK = LANES
AUG_DIM = N_HEADS * HEAD_BLOCK
N_SPLIT = 3
EXTRA_ROWS = 16
F_ROWS = 16
VMEM_LIMIT_BYTES = 56 * 1024 * 1024
V_ROWS = HEAD_DIM + 16
LOG2E = 1.4426950408889634
SCORE_LOOKAHEAD = 4

_W_Q = 4 * CONV_DIM
_W_V = _W_Q + 2 * ATT_DIM
_W_F = _W_Q + 3 * ATT_DIM
_W_ZB = _W_F + N_HEADS
_S_XA, _S_GB, _S_GC, _S_ZA, _S_Q, _S_K, _S_ZB, _S_GA, _S_GM = (
    0, 512, 1024, 1536, 2048, 2560, 3072, 3584, 4608)
_T_Q, _T_K, _T_V, _T_F = 0, 512, 1024, 1536

PROMPT_TILE = 256
PROMPT_PARAMS = ("ln_in_g", "ln_in_b", "w_std", "w_t", "b_f_col", "conv_w", "w_pa", "w_pb", "w_out",
                 "ln_g", "ln_b", "route", "ones", "tri")
SAMPLE_PARAMS = ("ln_in_g", "ln_in_b", "w_std", "w_fs", "b_f_row", "w_t", "b_f_col", "conv_w", "w_pa",
                 "w_pb", "w_out", "ln_g", "ln_b", "route", "ones")

F32 = jnp.float32
BF16 = jnp.bfloat16


def _routing_constants():
    route = np.zeros((LANES, 2 * AUG_DIM), np.float32)
    ones = np.zeros((1, 2 * AUG_DIM), np.float32)
    for h in range(N_HEADS):
        base = h * HEAD_BLOCK + (HEAD_DIM if h % 2 == 0 else 0)
        for r in range(N_SPLIT):
            route[r * N_HEADS + h, base + 2 * r] = 1.0
            ones[0, base + 2 * r + 1] = 1.0
            ones[0, AUG_DIM + base + 2 * r] = 1.0
            route[r * N_HEADS + h, AUG_DIM + base + 2 * r + 1] = -1.0
    return route, ones


def _layer_norm(x, g, b):
    mu = jnp.mean(x, axis=-1, keepdims=True)
    xc = x - mu
    var = jnp.mean(xc * xc, axis=-1, keepdims=True)
    return xc * lax.rsqrt(var + LN_EPS) * g + b


def _log_sigmoid(x):
    return jnp.minimum(x, 0.0) - jnp.log1p(jnp.exp(-jnp.abs(x)))


def _split3(x):
    hi = x.astype(BF16)
    r1 = x - hi.astype(F32)
    mid = r1.astype(BF16)
    lo = (r1 - mid.astype(F32)).astype(BF16)
    return hi, mid, lo


def _cumsum(x, axis):
    n = x.shape[axis]
    idx = lax.broadcasted_iota(jnp.int32, x.shape, axis)
    s = 1
    while s < n:
        x = x + jnp.where(idx >= s, pltpu.roll(x, s, axis), 0.0)
        s *= 2
    return x


def _extra_lanes(c, route):
    lane = lax.broadcasted_iota(jnp.int32, c.shape, 1)
    valid = lane < N_HEADS
    packed = jnp.zeros_like(c)
    for r, piece in enumerate(_split3(c)):
        p32 = jnp.where(valid, piece.astype(F32), 0.0)
        packed = packed + (pltpu.roll(p32, r * N_HEADS, 1) if r else p32)
    return jnp.dot(packed.astype(BF16), route, preferred_element_type=F32)


def _store_token_major(dst_ref, rows, data, extras):
    n = data.shape[0]
    lane = lax.broadcasted_iota(jnp.int32, (n, HEAD_BLOCK), 1)
    for h in range(N_HEADS):
        pair = data[:, (h // 2) * HEAD_BLOCK:(h // 2 + 1) * HEAD_BLOCK]
        ex = extras[:, h * HEAD_BLOCK:(h + 1) * HEAD_BLOCK]
        keep = (lane < HEAD_DIM) if h % 2 == 0 else (lane >= HEAD_DIM)
        dst_ref[rows, h * HEAD_BLOCK:(h + 1) * HEAD_BLOCK] = jnp.where(keep, pair, ex).astype(BF16)


def _extra_rows(pieces, h, n, key_side):
    r = lax.broadcasted_iota(jnp.int32, (EXTRA_ROWS, n), 0)
    value_row, sign = (1, -1.0) if key_side else (0, 1.0)
    out = jnp.where((r < 2 * N_SPLIT) & ((r & 1) != value_row), 1.0, 0.0)
    for i, piece in enumerate(pieces):
        out = jnp.where(r == 2 * i + value_row,
                        jnp.broadcast_to(sign * piece[h:h + 1, :], (EXTRA_ROWS, n)), out)
    return out.astype(BF16)


def _store_feature_major(dst_ref, h, data, extra):
    if h % 2 == 0:
        dst_ref[h, 0:HEAD_DIM, :] = data
        dst_ref[h, HEAD_DIM:HEAD_DIM + EXTRA_ROWS, :] = extra
    else:
        dst_ref[h, 0:EXTRA_ROWS, :] = extra
        dst_ref[h, HEAD_DIM:2 * HEAD_DIM, :] = data


def _conv_branch(proj, u_sc, conv_w, conv_ref, w_pa, n):
    first = SUBLANES - (CONV_W - 1)
    u_sc[SUBLANES:SUBLANES + n, :] = proj(_S_GC, CONV_DIM) * proj(_S_XA, CONV_DIM)
    conv = sum(u_sc[first + i:first + i + n, :] * conv_w[i:i + 1, :] for i in range(CONV_W))
    new_conv = u_sc[first + n:SUBLANES + n, :]
    conv_ref[...] = new_conv
    u_sc[first:SUBLANES, :] = new_conv
    y_a = proj(_S_GB, CONV_DIM) * conv * jax.nn.silu(proj(_S_ZA, CONV_DIM))
    return jax.nn.sigmoid(proj(_S_GA, D_MODEL)) * jnp.dot(
        y_a.astype(BF16), w_pa[...], preferred_element_type=F32)


def _merge_and_norm(att, mix_a, h, silu_zb, gate_b, w_pb, w_out, ln_g, ln_b):
    y_b = att * silu_zb
    mixed = mix_a + gate_b * jnp.dot(
        y_b.astype(BF16), w_pb[...], preferred_element_type=F32)
    sub = jnp.dot(mixed.astype(BF16), w_out[...], preferred_element_type=F32)
    return _layer_norm(ALPHA * h + sub, ln_g[...], ln_b[...])


def _conv_branch_stages(proj, u_sc, ya_sc, mix_sc, conv_w, conv_ref, w_pa, n):
    first = SUBLANES - (CONV_W - 1)

    def conv_inputs():
        u_sc[SUBLANES:SUBLANES + n, :] = proj(_S_GC, CONV_DIM) * proj(_S_XA, CONV_DIM)

    def gated_conv():
        conv = sum(u_sc[first + i:first + i + n, :] * conv_w[i:i + 1, :] for i in range(CONV_W))
        new_conv = u_sc[first + n:SUBLANES + n, :]
        conv_ref[...] = new_conv
        u_sc[first:SUBLANES, :] = new_conv
        ya_sc[...] = (proj(_S_GB, CONV_DIM) * conv * jax.nn.silu(proj(_S_ZA, CONV_DIM))).astype(BF16)

    def gate():
        mix_sc[...] = jax.nn.sigmoid(proj(_S_GA, D_MODEL))

    def project():
        mix_sc[...] = mix_sc[...] * jnp.dot(ya_sc[...], w_pa[...], preferred_element_type=F32)

    return [conv_inputs, gated_conv, gate, project]


def _attention_t(n_key_tiles, tm, kaug_sc, qta_sc, vt_sc, attt_sc, fillers):
    fillers = list(fillers)
    keys = n_key_tiles * tm
    causal = (lax.broadcasted_iota(jnp.int32, (tm, tm), 0)
              <= lax.broadcasted_iota(jnp.int32, (tm, tm), 1))

    def scores(hd):
        return jnp.dot(kaug_sc[0:keys, hd * HEAD_BLOCK:(hd + 1) * HEAD_BLOCK], qta_sc[hd],
                       preferred_element_type=F32)

    pending = [scores(hd) for hd in range(SCORE_LOOKAHEAD)]
    for hd in range(N_HEADS):
        s = pending.pop(0)
        if hd + SCORE_LOOKAHEAD < N_HEADS:
            pending.append(scores(hd + SCORE_LOOKAHEAD))
        blocks = [s[i * tm:(i + 1) * tm, :] for i in range(n_key_tiles)]
        blocks[-1] = jnp.where(causal, blocks[-1], -jnp.inf)
        m = functools.reduce(jnp.maximum, [jnp.max(blk, axis=0, keepdims=True) for blk in blocks])
        o = jnp.zeros((V_ROWS, tm), F32)
        for i, blk in enumerate(blocks):
            o = o + jnp.dot(vt_sc[i, hd * V_ROWS:(hd + 1) * V_ROWS, :], jnp.exp2(blk - m).astype(BF16),
                            preferred_element_type=F32)
        attt_sc[hd * HEAD_DIM:(hd + 1) * HEAD_DIM, :] = o[0:HEAD_DIM, :] / o[HEAD_DIM:HEAD_DIM + 1, :]
        if hd % 2 == 0 and fillers:
            fillers.pop(0)()
    for stage in fillers:
        stage()


def _prompt_kernel(x_ref, ln_in_g, ln_in_b, w_std, w_t, b_f_col, conv_w, w_pa, w_pb, w_out,
                   ln_g, ln_b, route, ones, tri,
                   y_ref, kt_ref, vt_ref, lft_ref, conv_ref,
                   kaug_sc, vt_sc, h_sc, hb_sc, z_sc, qta_sc, attt_sc, zb_sc, gm_sc, ya_sc, mix_sc, u_sc,
                   carryt_sc,
                   *, tm, n_tiles, n_total):
    t = pl.program_id(0)

    def norm_in(slot):
        h = _layer_norm(x_ref[...], ln_in_g[...], ln_in_b[...])
        h_sc[slot] = h
        hb_sc[slot] = h.astype(BF16)

    def norm_out():
        y_ref[...] = _layer_norm(z_sc[...], ln_g[...], ln_b[...])

    @pl.when(t == 0)
    def _fill():
        z_sc[...] = jnp.zeros_like(z_sc)
        norm_in(0)

    @pl.when((t >= 1) & (t <= n_total))
    def _mixers():
        j = lax.rem(t - 1, n_tiles)
        cur = (t - 1) & 1

        @pl.when(j == 0)
        def _start_of_sequence():
            u_sc[0:SUBLANES, :] = jnp.zeros((SUBLANES, CONV_DIM), F32)
            carryt_sc[...] = jnp.zeros_like(carryt_sc)
            qta_sc[...] = jnp.zeros_like(qta_sc)
            ones_row = jnp.where(lax.broadcasted_iota(jnp.int32, (V_ROWS - HEAD_DIM, tm), 0) == 0, 1.0, 0.0)
            for i in range(n_tiles):
                for hd in range(N_HEADS):
                    vt_sc[i, hd * V_ROWS + HEAD_DIM:(hd + 1) * V_ROWS, :] = ones_row.astype(BF16)

        def proj(col, width):
            return jnp.dot(hb_sc[cur], w_std[:, col:col + width], preferred_element_type=F32)

        def proj_t(row, height):
            return lax.dot_general(w_t[row:row + height, :], hb_sc[cur], (((1,), (1,)), ((), ())),
                                   preferred_element_type=F32)

        norm_out()

        vf_t = proj_t(_T_V, ATT_DIM + F_ROWS)
        qk_t = proj_t(_T_Q, 2 * ATT_DIM)
        kt = qk_t[ATT_DIM:2 * ATT_DIM, :]
        vt = vf_t[0:ATT_DIM, :]
        kt_ref[...] = kt
        vt_ref[...] = vt
        for hd in range(N_HEADS):
            vt_sc[j, hd * V_ROWS:hd * V_ROWS + HEAD_DIM, :] = (
                vt[hd * HEAD_DIM:(hd + 1) * HEAD_DIM, :].astype(BF16))
        logf_t = _log_sigmoid(vf_t[ATT_DIM:ATT_DIM + F_ROWS, :] + b_f_col[...])
        lft_ref[...] = logf_t[:N_HEADS, :]
        sums = jnp.dot(jnp.concatenate(_split3(logf_t), axis=0), tri[...], preferred_element_type=F32)
        c_t = sum(sums[r * F_ROWS:(r + 1) * F_ROWS, :] for r in range(N_SPLIT)) + carryt_sc[:, 0:1]
        carryt_sc[...] = jnp.broadcast_to(c_t[:, tm - 1:tm], carryt_sc.shape)
        zb_sc[...] = jax.nn.silu(proj(_S_ZB, ATT_DIM))
        gm_sc[...] = jax.nn.sigmoid(proj(_S_GM, D_MODEL))
        c_pieces = [piece.astype(F32) for piece in _split3(c_t * LOG2E)]
        qt = qk_t[0:ATT_DIM, :] * (SCALE * LOG2E)
        for hd in range(N_HEADS):
            _store_feature_major(qta_sc, hd, qt[hd * HEAD_DIM:(hd + 1) * HEAD_DIM, :].astype(BF16),
                                 _extra_rows(c_pieces, hd, tm, key_side=False))

        packed_t = jnp.concatenate([piece[0:N_HEADS, :] for piece in c_pieces]
                                   + [jnp.zeros((LANES - N_SPLIT * N_HEADS, tm), F32)], axis=0)
        ex_k = (jnp.dot(packed_t.T.astype(BF16), route[:, AUG_DIM:], preferred_element_type=F32)
                + ones[:, AUG_DIM:])
        _store_token_major(kaug_sc, pl.ds(pl.multiple_of(j * tm, tm), tm), kt.T, ex_k)

        norm_in(t & 1)

        for jj in range(n_tiles):
            pl.when(j == jj)(functools.partial(
                _attention_t, jj + 1, tm, kaug_sc, qta_sc, vt_sc, attt_sc,
                _conv_branch_stages(proj, u_sc, ya_sc, mix_sc, conv_w, conv_ref, w_pa, tm)))

        y_b = attt_sc[...].T * zb_sc[...]
        mixed = mix_sc[...] + gm_sc[...] * jnp.dot(y_b.astype(BF16), w_pb[...], preferred_element_type=F32)
        z_sc[...] = ALPHA * h_sc[cur] + jnp.dot(mixed.astype(BF16), w_out[...], preferred_element_type=F32)

    @pl.when(t == n_total + 1)
    def _drain():
        norm_out()


def _resident(shape):
    return pl.BlockSpec(shape, lambda *_: (0,) * len(shape), pipeline_mode=pl.Buffered(1))


def _run_prompt(x, params, *, tm):
    bsz, t, _ = x.shape
    n_tiles = t // tm
    assert n_tiles * tm == t
    n_total = bsz * n_tiles
    tile = lambda s, lag: jnp.clip(s - lag, 0, n_total - 1)
    per_tile = lambda width, lag: pl.BlockSpec(
        (None, tm, width), lambda s: (tile(s, lag) // n_tiles, tile(s, lag) % n_tiles, 0))
    per_tile_t = lambda rows: pl.BlockSpec(
        (None, rows, tm), lambda s: (tile(s, 1) // n_tiles, 0, tile(s, 1) % n_tiles))
    out_shape = (jax.ShapeDtypeStruct((bsz, t, D_MODEL), F32),
                 jax.ShapeDtypeStruct((bsz, ATT_DIM, t), F32),
                 jax.ShapeDtypeStruct((bsz, ATT_DIM, t), F32),
                 jax.ShapeDtypeStruct((bsz, N_HEADS, t), F32),
                 jax.ShapeDtypeStruct((bsz, CONV_W - 1, CONV_DIM), F32))
    out_specs = (per_tile(D_MODEL, 2), per_tile_t(ATT_DIM), per_tile_t(ATT_DIM), per_tile_t(N_HEADS),
                 pl.BlockSpec((None, CONV_W - 1, CONV_DIM), lambda s: (tile(s, 1) // n_tiles, 0, 0)))
    scratch = [
        pltpu.VMEM((t, AUG_DIM), BF16),
        pltpu.VMEM((n_tiles, N_HEADS * V_ROWS, tm), BF16),
        pltpu.VMEM((2, tm, D_MODEL), F32),
        pltpu.VMEM((2, tm, D_MODEL), BF16),
        pltpu.VMEM((tm, D_MODEL), F32),
        pltpu.VMEM((N_HEADS, HEAD_BLOCK, tm), BF16),
        pltpu.VMEM((ATT_DIM, tm), F32),
        pltpu.VMEM((tm, ATT_DIM), F32),
        pltpu.VMEM((tm, D_MODEL), F32),
        pltpu.VMEM((tm, CONV_DIM), BF16),
        pltpu.VMEM((tm, D_MODEL), F32),
        pltpu.VMEM((SUBLANES + tm, CONV_DIM), F32),
        pltpu.VMEM((F_ROWS, LANES), F32),
    ]
    return pl.pallas_call(
        functools.partial(_prompt_kernel, tm=tm, n_tiles=n_tiles, n_total=n_total),
        grid=(n_total + 2,),
        in_specs=[per_tile(D_MODEL, 0)] + [_resident(p.shape) for p in params],
        out_specs=out_specs,
        out_shape=out_shape,
        scratch_shapes=scratch,
        compiler_params=pltpu.CompilerParams(
            dimension_semantics=("arbitrary",),
            vmem_limit_bytes=VMEM_LIMIT_BYTES),
        name="trunk_prompt",
    )(x, *params)


def _sample_kernel(x_ref, conv0_ref, pkt_ref, pvt_ref, plft_ref,
                   ln_in_g, ln_in_b, w_std, w_fs, b_f_row, w_t, b_f_col, conv_w, w_pa, w_pb, w_out,
                   ln_g, ln_b, route, ones,
                   y_ref, kt_ref, vt_ref, lft_ref, conv_ref,
                   kta_sc, ktn_sc, hb_sc, qa_sc, att_sc, u_sc, *, ts, n_past):
    @pl.when(pl.program_id(0) == 0)
    def _zero_padding():
        kta_sc[...] = jnp.zeros_like(kta_sc)
        ktn_sc[...] = jnp.zeros_like(ktn_sc)
        hb_sc[...] = jnp.zeros_like(hb_sc)

    h = _layer_norm(x_ref[...], ln_in_g[...], ln_in_b[...])
    hb_sc[0:ts, :] = h.astype(BF16)

    def proj(col, width):
        return jnp.dot(hb_sc[0:ts, :], w_std[:, col:col + width], preferred_element_type=F32)

    u_sc[0:SUBLANES, :] = jnp.zeros((SUBLANES, CONV_DIM), F32)
    u_sc[SUBLANES - (CONV_W - 1):SUBLANES, :] = conv0_ref[...]
    mix_a = _conv_branch(proj, u_sc, conv_w, conv_ref, w_pa, ts)

    proj_t = lax.dot_general(w_t[_T_K:, :], hb_sc[...], (((1,), (1,)), ((), ())),
                             preferred_element_type=F32)
    kt = proj_t[0:ATT_DIM, :]
    vt = proj_t[ATT_DIM:2 * ATT_DIM, :]
    kt_ref[...] = kt[:, :ts]
    vt_ref[...] = vt[:, :ts]
    logf_t = _log_sigmoid(proj_t[2 * ATT_DIM:2 * ATT_DIM + SUBLANES, :] + b_f_col[0:SUBLANES, :])
    lft_ref[...] = logf_t[:, :ts]

    c_past = _cumsum(plft_ref[...], 1)
    c_last = c_past[:, n_past - 1:n_past]
    c_new_t = _cumsum(logf_t, 1) + c_last
    past_pieces = [piece.astype(F32) for piece in _split3(c_past)]
    new_pieces = [piece.astype(F32) for piece in _split3(c_new_t)]
    for hd in range(N_HEADS):
        rows = slice(hd * HEAD_DIM, (hd + 1) * HEAD_DIM)
        _store_feature_major(kta_sc, hd, pkt_ref[rows, :].astype(BF16),
                             _extra_rows(past_pieces, hd, n_past, key_side=True))
        _store_feature_major(ktn_sc, hd, kt[rows, :].astype(BF16),
                             _extra_rows(new_pieces, hd, LANES, key_side=True))

    eye = (lax.broadcasted_iota(jnp.int32, (SUBLANES, LANES), 0)
           == lax.broadcasted_iota(jnp.int32, (SUBLANES, LANES), 1))
    c_last_row = jnp.sum(jnp.where(eye, jnp.broadcast_to(c_last, (SUBLANES, LANES)), 0.0),
                         axis=0, keepdims=True)
    logf = _log_sigmoid(jnp.dot(hb_sc[0:ts, :], w_fs[...], preferred_element_type=F32) + b_f_row[...])
    c_q = _cumsum(logf, 0) + c_last_row
    ex_q = _extra_lanes(c_q, route[:, :AUG_DIM]) + ones[:, :AUG_DIM]
    _store_token_major(qa_sc, slice(None), proj(_S_Q, ATT_DIM) * SCALE, ex_q)

    causal = (lax.broadcasted_iota(jnp.int32, (ts, LANES), 1)
              <= lax.broadcasted_iota(jnp.int32, (ts, LANES), 0))
    lane = lax.broadcasted_iota(jnp.int32, (ts, HEAD_BLOCK), 1)
    contract_lanes = (((1,), (1,)), ((), ()))
    for pair in range(N_HEADS // 2):
        rows = slice(pair * HEAD_BLOCK, (pair + 1) * HEAD_BLOCK)
        vt_past = pvt_ref[rows, :].astype(BF16)
        vt_new = vt[rows, :].astype(BF16)
        outs = []
        for hd in (2 * pair, 2 * pair + 1):
            qa_h = qa_sc[:, hd * HEAD_BLOCK:(hd + 1) * HEAD_BLOCK]
            s_past = jnp.dot(qa_h, kta_sc[hd], preferred_element_type=F32)
            s_new = jnp.where(causal, jnp.dot(qa_h, ktn_sc[hd], preferred_element_type=F32), -jnp.inf)
            m = jnp.maximum(jnp.max(s_past, axis=-1, keepdims=True), jnp.max(s_new, axis=-1, keepdims=True))
            p_past = jnp.exp(s_past - m)
            p_new = jnp.exp(s_new - m)
            l = jnp.sum(p_past, axis=-1, keepdims=True) + jnp.sum(p_new, axis=-1, keepdims=True)
            o = (lax.dot_general(p_past.astype(BF16), vt_past, contract_lanes, preferred_element_type=F32)
                 + lax.dot_general(p_new.astype(BF16), vt_new, contract_lanes, preferred_element_type=F32))
            outs.append(o / l)
        att_sc[:, rows] = jnp.where(lane < HEAD_DIM, outs[0], outs[1])

    y_ref[...] = _merge_and_norm(att_sc[...], mix_a, h, jax.nn.silu(proj(_S_ZB, ATT_DIM)),
                                 jax.nn.sigmoid(proj(_S_GM, D_MODEL)), w_pb, w_out, ln_g, ln_b)


def _run_sample(x, conv0, pkt, pvt, plft, params):
    bsz, ts, _ = x.shape
    n_past = pkt.shape[2]
    assert ts <= LANES and ts % SUBLANES == 0
    per_batch = lambda rows, width: pl.BlockSpec((None, rows, width), lambda b: (b, 0, 0))
    out_shape = (jax.ShapeDtypeStruct((bsz, ts, D_MODEL), F32),
                 jax.ShapeDtypeStruct((bsz, ATT_DIM, ts), F32),
                 jax.ShapeDtypeStruct((bsz, ATT_DIM, ts), F32),
                 jax.ShapeDtypeStruct((bsz, N_HEADS, ts), F32),
                 jax.ShapeDtypeStruct((bsz, CONV_W - 1, CONV_DIM), F32))
    out_specs = (per_batch(ts, D_MODEL), per_batch(ATT_DIM, ts), per_batch(ATT_DIM, ts),
                 per_batch(N_HEADS, ts), per_batch(CONV_W - 1, CONV_DIM))
    in_specs = [per_batch(ts, D_MODEL), per_batch(CONV_W - 1, CONV_DIM), per_batch(ATT_DIM, n_past),
                per_batch(ATT_DIM, n_past), per_batch(N_HEADS, n_past)]
    scratch = [
        pltpu.VMEM((N_HEADS, HEAD_BLOCK, n_past), BF16),
        pltpu.VMEM((N_HEADS, HEAD_BLOCK, LANES), BF16),
        pltpu.VMEM((LANES, D_MODEL), BF16),
        pltpu.VMEM((ts, AUG_DIM), BF16),
        pltpu.VMEM((ts, ATT_DIM), F32),
        pltpu.VMEM((SUBLANES + ts, CONV_DIM), F32),
    ]
    return pl.pallas_call(
        functools.partial(_sample_kernel, ts=ts, n_past=n_past),
        grid=(bsz,),
        in_specs=in_specs + [_resident(p.shape) for p in params],
        out_specs=out_specs,
        out_shape=out_shape,
        scratch_shapes=scratch,
        compiler_params=pltpu.CompilerParams(
            dimension_semantics=("arbitrary",),
            vmem_limit_bytes=VMEM_LIMIT_BYTES),
        name="trunk_sample",
    )(x, conv0, pkt, pvt, plft, *params)


def kernel(x_prompt, x_sample, cache_k, cache_v, cache_logf, state_conv, ln_in_g, ln_in_b, w_in, b_f,
           conv_w, w_proj_a, w_proj_b, w_out, ln_g, ln_b):
    assert w_in.shape[0] == DEPTH
    route_np, ones_np = _routing_constants()
    w = w_in[0]
    row = lambda a: a.reshape(1, -1).astype(F32)
    w_t = jnp.concatenate([w[:, _W_Q:_W_F + N_HEADS],
                           jnp.zeros((D_MODEL, F_ROWS - N_HEADS), w.dtype)], axis=1).T.astype(BF16)
    tm = PROMPT_TILE
    p = dict(
        ln_in_g=row(ln_in_g), ln_in_b=row(ln_in_b),
        w_std=jnp.concatenate([w[:, :_W_V], w[:, _W_ZB:]], axis=1).astype(BF16),
        w_fs=jnp.pad(w[:, _W_F:_W_F + N_HEADS], ((0, 0), (0, LANES - N_HEADS))).astype(BF16),
        b_f_row=jnp.pad(row(b_f[0]), ((0, 0), (0, LANES - N_HEADS))),
        w_t=w_t,
        b_f_col=jnp.pad(b_f[0].astype(F32).reshape(N_HEADS, 1), ((0, F_ROWS - N_HEADS), (0, 0))),
        conv_w=conv_w[0].astype(F32),
        w_pa=w_proj_a[0].astype(BF16), w_pb=w_proj_b[0].astype(BF16), w_out=w_out[0].astype(BF16),
        ln_g=row(ln_g[0]), ln_b=row(ln_b[0]),
        route=jnp.asarray(route_np, BF16), ones=jnp.asarray(ones_np, F32),
        tri=jnp.asarray(np.triu(np.ones((tm, tm), np.float32)), BF16),
    )
    bs = x_sample.shape[0]
    n_past = cache_k.shape[2]
    feature_major = lambda a: a.transpose(0, 2, 3, 1).reshape(bs, ATT_DIM, n_past)

    y_p, kt_p, vt_p, lft_p, cv_p = _run_prompt(x_prompt, [p[name] for name in PROMPT_PARAMS], tm=tm)
    y_s, kt_s, vt_s, lft_s, cv_s = _run_sample(
        x_sample, state_conv[0], feature_major(cache_k[0]), feature_major(cache_v[0]),
        cache_logf[0].transpose(0, 2, 1), [p[name] for name in SAMPLE_PARAMS])

    heads_t = lambda a: a.reshape(a.shape[0], N_HEADS, HEAD_DIM, a.shape[2]).transpose(0, 3, 1, 2)[None]
    tokens_t = lambda a: a.transpose(0, 2, 1)[None]
    return (y_p, y_s, heads_t(kt_p), heads_t(vt_p), tokens_t(lft_p), cv_p[None],
            heads_t(kt_s), heads_t(vt_s), tokens_t(lft_s), cv_s[None])
```

```python
import functools

import numpy as np
import jax
import jax.numpy as jnp
from jax import lax
from jax.experimental import pallas as pl
from jax.experimental.pallas import tpu as pltpu

D_MODEL = 1024
CONV_DIM = 512
CONV_W = 3
N_HEADS = 8
HEAD_DIM = 64
ATT_DIM = N_HEADS * HEAD_DIM
LN_EPS = 1e-5
DEPTH = 1
ALPHA = (2.0 * DEPTH) ** 0.25
SCALE = HEAD_DIM ** -0.5

LANES = 128
SUBLANES = 8
HEAD_BLOCK = LANES
AUG_DIM = N_HEADS * HEAD_BLOCK
N_SPLIT = 3
EXTRA_ROWS = 16
F_ROWS = 16
VMEM_LIMIT_BYTES = 56 * 1024 * 1024
V_ROWS = HEAD_DIM + 16
LOG2E = 1.4426950408889634
SCORE_LOOKAHEAD = 4

_W_Q = 4 * CONV_DIM
_W_V = _W_Q + 2 * ATT_DIM
_W_F = _W_Q + 3 * ATT_DIM
_W_ZB = _W_F + N_HEADS
_S_XA, _S_GB, _S_GC, _S_ZA, _S_Q, _S_K, _S_ZB, _S_GA, _S_GM = (
    0, 512, 1024, 1536, 2048, 2560, 3072, 3584, 4608)
_T_Q, _T_K, _T_V, _T_F = 0, 512, 1024, 1536

PROMPT_TILE = 256
PROMPT_PARAMS = ("ln_in_g", "ln_in_b", "w_std", "w_t", "b_f_col", "conv_w", "w_pa", "w_pb", "w_out",
                 "ln_g", "ln_b", "route", "ones", "tri")
SAMPLE_PARAMS = ("ln_in_g", "ln_in_b", "w_std", "w_fs", "b_f_row", "w_t", "b_f_col", "conv_w", "w_pa",
                 "w_pb", "w_out", "ln_g", "ln_b", "route", "ones")

F32 = jnp.float32
BF16 = jnp.bfloat16


def _routing_constants():
    route = np.zeros((LANES, 2 * AUG_DIM), np.float32)
    ones = np.zeros((1, 2 * AUG_DIM), np.float32)
    for h in range(N_HEADS):
        base = h * HEAD_BLOCK + (HEAD_DIM if h % 2 == 0 else 0)
        for r in range(N_SPLIT):
            route[r * N_HEADS + h, base + 2 * r] = 1.0
            ones[0, base + 2 * r + 1] = 1.0
            ones[0, AUG_DIM + base + 2 * r] = 1.0
            route[r * N_HEADS + h, AUG_DIM + base + 2 * r + 1] = -1.0
    return route, ones


def _layer_norm(x, g, b):
    mu = jnp.mean(x, axis=-1, keepdims=True)
    xc = x - mu
    var = jnp.mean(xc * xc, axis=-1, keepdims=True)
    return xc * lax.rsqrt(var + LN_EPS) * g + b


def _log_sigmoid(x):
    return jnp.minimum(x, 0.0) - jnp.log1p(jnp.exp(-jnp.abs(x)))


def _split3(x):
    hi = x.astype(BF16)
    r1 = x - hi.astype(F32)
    mid = r1.astype(BF16)
    lo = (r1 - mid.astype(F32)).astype(BF16)
    return hi, mid, lo


def _cumsum(x, axis):
    n = x.shape[axis]
    idx = lax.broadcasted_iota(jnp.int32, x.shape, axis)
    s = 1
    while s < n:
        x = x + jnp.where(idx >= s, pltpu.roll(x, s, axis), 0.0)
        s *= 2
    return x


def _extra_lanes(c, route):
    lane = lax.broadcasted_iota(jnp.int32, c.shape, 1)
    valid = lane < N_HEADS
    packed = jnp.zeros_like(c)
    for r, piece in enumerate(_split3(c)):
        p32 = jnp.where(valid, piece.astype(F32), 0.0)
        packed = packed + (pltpu.roll(p32, r * N_HEADS, 1) if r else p32)
    return jnp.dot(packed.astype(BF16), route, preferred_element_type=F32)


def _store_token_major(dst_ref, rows, data, extras):
    n = data.shape[0]
    lane = lax.broadcasted_iota(jnp.int32, (n, HEAD_BLOCK), 1)
    for h in range(N_HEADS):
        pair = data[:, (h // 2) * HEAD_BLOCK:(h // 2 + 1) * HEAD_BLOCK]
        ex = extras[:, h * HEAD_BLOCK:(h + 1) * HEAD_BLOCK]
        keep = (lane < HEAD_DIM) if h % 2 == 0 else (lane >= HEAD_DIM)
        dst_ref[rows, h * HEAD_BLOCK:(h + 1) * HEAD_BLOCK] = jnp.where(keep, pair, ex).astype(BF16)


def _extra_rows(pieces, h, n, key_side):
    r = lax.broadcasted_iota(jnp.int32, (EXTRA_ROWS, n), 0)
    value_row, sign = (1, -1.0) if key_side else (0, 1.0)
    out = jnp.where((r < 2 * N_SPLIT) & ((r & 1) != value_row), 1.0, 0.0)
    for i, piece in enumerate(pieces):
        out = jnp.where(r == 2 * i + value_row,
                        jnp.broadcast_to(sign * piece[h:h + 1, :], (EXTRA_ROWS, n)), out)
    return out.astype(BF16)


def _store_feature_major(dst_ref, h, data, extra):
    if h % 2 == 0:
        dst_ref[h, 0:HEAD_DIM, :] = data
        dst_ref[h, HEAD_DIM:HEAD_DIM + EXTRA_ROWS, :] = extra
    else:
        dst_ref[h, 0:EXTRA_ROWS, :] = extra
        dst_ref[h, HEAD_DIM:2 * HEAD_DIM, :] = data


def _conv_branch(proj, u_sc, conv_w, conv_ref, w_pa, n):
    first = SUBLANES - (CONV_W - 1)
    u_sc[SUBLANES:SUBLANES + n, :] = proj(_S_GC, CONV_DIM) * proj(_S_XA, CONV_DIM)
    conv = sum(u_sc[first + i:first + i + n, :] * conv_w[i:i + 1, :] for i in range(CONV_W))
    new_conv = u_sc[first + n:SUBLANES + n, :]
    conv_ref[...] = new_conv
    u_sc[first:SUBLANES, :] = new_conv
    y_a = proj(_S_GB, CONV_DIM) * conv * jax.nn.silu(proj(_S_ZA, CONV_DIM))
    return jax.nn.sigmoid(proj(_S_GA, D_MODEL)) * jnp.dot(
        y_a.astype(BF16), w_pa[...], preferred_element_type=F32)


def _merge_and_norm(att, mix_a, h, silu_zb, gate_b, w_pb, w_out, ln_g, ln_b):
    y_b = att * silu_zb
    mixed = mix_a + gate_b * jnp.dot(
        y_b.astype(BF16), w_pb[...], preferred_element_type=F32)
    sub = jnp.dot(mixed.astype(BF16), w_out[...], preferred_element_type=F32)
    return _layer_norm(ALPHA * h + sub, ln_g[...], ln_b[...])


def _attention_t(n_key_tiles, tm, kaug_sc, qta_sc, vt_sc, attt_sc):
    keys = n_key_tiles * tm
    causal = (lax.broadcasted_iota(jnp.int32, (tm, tm), 0)
              <= lax.broadcasted_iota(jnp.int32, (tm, tm), 1))

    def scores(hd):
        return jnp.dot(kaug_sc[0:keys, hd * HEAD_BLOCK:(hd + 1) * HEAD_BLOCK], qta_sc[hd],
                       preferred_element_type=F32)

    pending = [scores(hd) for hd in range(SCORE_LOOKAHEAD)]
    for hd in range(N_HEADS):
        s = pending.pop(0)
        if hd + SCORE_LOOKAHEAD < N_HEADS:
            pending.append(scores(hd + SCORE_LOOKAHEAD))
        blocks = [s[i * tm:(i + 1) * tm, :] for i in range(n_key_tiles)]
        blocks[-1] = jnp.where(causal, blocks[-1], -jnp.inf)
        m = functools.reduce(jnp.maximum, [jnp.max(blk, axis=0, keepdims=True) for blk in blocks])
        o = jnp.zeros((V_ROWS, tm), F32)
        for i, blk in enumerate(blocks):
            o = o + jnp.dot(vt_sc[i, hd * V_ROWS:(hd + 1) * V_ROWS, :], jnp.exp2(blk - m).astype(BF16),
                            preferred_element_type=F32)
        attt_sc[hd * HEAD_DIM:(hd + 1) * HEAD_DIM, :] = o[0:HEAD_DIM, :] / o[HEAD_DIM:HEAD_DIM + 1, :]


def _prompt_kernel(x_ref, ln_in_g, ln_in_b, w_std, w_t, b_f_col, conv_w, w_pa, w_pb, w_out,
                   ln_g, ln_b, route, ones, tri,
                   y_ref, kt_ref, vt_ref, lft_ref, conv_ref,
                   kaug_sc, vt_sc, h_sc, hb_sc, z_sc, qta_sc, attt_sc, zb_sc, gm_sc, u_sc, carryt_sc,
                   *, tm, n_tiles, n_total):
    t = pl.program_id(0)

    def norm_in(slot):
        h = _layer_norm(x_ref[...], ln_in_g[...], ln_in_b[...])
        h_sc[slot] = h
        hb_sc[slot] = h.astype(BF16)

    def norm_out():
        y_ref[...] = _layer_norm(z_sc[...], ln_g[...], ln_b[...])

    @pl.when(t == 0)
    def _fill():
        z_sc[...] = jnp.zeros_like(z_sc)
        norm_in(0)

    @pl.when((t >= 1) & (t <= n_total))
    def _mixers():
        j = lax.rem(t - 1, n_tiles)
        cur = (t - 1) & 1

        @pl.when(j == 0)
        def _start_of_sequence():
            u_sc[0:SUBLANES, :] = jnp.zeros((SUBLANES, CONV_DIM), F32)
            carryt_sc[...] = jnp.zeros_like(carryt_sc)
            qta_sc[...] = jnp.zeros_like(qta_sc)
            ones_row = jnp.where(lax.broadcasted_iota(jnp.int32, (V_ROWS - HEAD_DIM, tm), 0) == 0, 1.0, 0.0)
            for i in range(n_tiles):
                for hd in range(N_HEADS):
                    vt_sc[i, hd * V_ROWS + HEAD_DIM:(hd + 1) * V_ROWS, :] = ones_row.astype(BF16)

        def proj(col, width):
            return jnp.dot(hb_sc[cur], w_std[:, col:col + width], preferred_element_type=F32)

        def proj_t(row, height):
            return lax.dot_general(w_t[row:row + height, :], hb_sc[cur], (((1,), (1,)), ((), ())),
                                   preferred_element_type=F32)

        norm_out()

        vf_t = proj_t(_T_V, ATT_DIM + F_ROWS)
        qk_t = proj_t(_T_Q, 2 * ATT_DIM)
        kt = qk_t[ATT_DIM:2 * ATT_DIM, :]
        vt = vf_t[0:ATT_DIM, :]
        kt_ref[...] = kt
        vt_ref[...] = vt
        for hd in range(N_HEADS):
            vt_sc[j, hd * V_ROWS:hd * V_ROWS + HEAD_DIM, :] = (
                vt[hd * HEAD_DIM:(hd + 1) * HEAD_DIM, :].astype(BF16))
        logf_t = _log_sigmoid(vf_t[ATT_DIM:ATT_DIM + F_ROWS, :] + b_f_col[...])
        lft_ref[...] = logf_t[:N_HEADS, :]
        sums = jnp.dot(jnp.concatenate(_split3(logf_t), axis=0), tri[...], preferred_element_type=F32)
        c_t = sum(sums[r * F_ROWS:(r + 1) * F_ROWS, :] for r in range(N_SPLIT)) + carryt_sc[:, 0:1]
        carryt_sc[...] = jnp.broadcast_to(c_t[:, tm - 1:tm], carryt_sc.shape)
        zb_sc[...] = jax.nn.silu(proj(_S_ZB, ATT_DIM))
        gm_sc[...] = jax.nn.sigmoid(proj(_S_GM, D_MODEL))
        c_pieces = [piece.astype(F32) for piece in _split3(c_t * LOG2E)]
        qt = qk_t[0:ATT_DIM, :] * (SCALE * LOG2E)
        for hd in range(N_HEADS):
            _store_feature_major(qta_sc, hd, qt[hd * HEAD_DIM:(hd + 1) * HEAD_DIM, :].astype(BF16),
                                 _extra_rows(c_pieces, hd, tm, key_side=False))

        packed_t = jnp.concatenate([piece[0:N_HEADS, :] for piece in c_pieces]
                                   + [jnp.zeros((LANES - N_SPLIT * N_HEADS, tm), F32)], axis=0)
        ex_k = (jnp.dot(packed_t.T.astype(BF16), route[:, AUG_DIM:], preferred_element_type=F32)
                + ones[:, AUG_DIM:])
        _store_token_major(kaug_sc, pl.ds(pl.multiple_of(j * tm, tm), tm), kt.T, ex_k)

        norm_in(t & 1)

        for jj in range(n_tiles):
            pl.when(j == jj)(functools.partial(_attention_t, jj + 1, tm, kaug_sc, qta_sc, vt_sc, attt_sc))

        mix_a = _conv_branch(proj, u_sc, conv_w, conv_ref, w_pa, tm)
        y_b = attt_sc[...].T * zb_sc[...]
        mixed = mix_a + gm_sc[...] * jnp.dot(y_b.astype(BF16), w_pb[...], preferred_element_type=F32)
        z_sc[...] = ALPHA * h_sc[cur] + jnp.dot(mixed.astype(BF16), w_out[...], preferred_element_type=F32)

    @pl.when(t == n_total + 1)
    def _drain():
        norm_out()


def _resident(shape):
    return pl.BlockSpec(shape, lambda *_: (0,) * len(shape), pipeline_mode=pl.Buffered(1))


def _run_prompt(x, params, *, tm):
    bsz, t, _ = x.shape
    n_tiles = t // tm
    assert n_tiles * tm == t
    n_total = bsz * n_tiles
    tile = lambda s, lag: jnp.clip(s - lag, 0, n_total - 1)
    per_tile = lambda width, lag: pl.BlockSpec(
        (None, tm, width), lambda s: (tile(s, lag) // n_tiles, tile(s, lag) % n_tiles, 0))
    per_tile_t = lambda rows: pl.BlockSpec(
        (None, rows, tm), lambda s: (tile(s, 1) // n_tiles, 0, tile(s, 1) % n_tiles))
    out_shape = (jax.ShapeDtypeStruct((bsz, t, D_MODEL), F32),
                 jax.ShapeDtypeStruct((bsz, ATT_DIM, t), F32),
                 jax.ShapeDtypeStruct((bsz, ATT_DIM, t), F32),
                 jax.ShapeDtypeStruct((bsz, N_HEADS, t), F32),
                 jax.ShapeDtypeStruct((bsz, CONV_W - 1, CONV_DIM), F32))
    out_specs = (per_tile(D_MODEL, 2), per_tile_t(ATT_DIM), per_tile_t(ATT_DIM), per_tile_t(N_HEADS),
                 pl.BlockSpec((None, CONV_W - 1, CONV_DIM), lambda s: (tile(s, 1) // n_tiles, 0, 0)))
    scratch = [
        pltpu.VMEM((t, AUG_DIM), BF16),
        pltpu.VMEM((n_tiles, N_HEADS * V_ROWS, tm), BF16),
        pltpu.VMEM((2, tm, D_MODEL), F32),
        pltpu.VMEM((2, tm, D_MODEL), BF16),
        pltpu.VMEM((tm, D_MODEL), F32),
        pltpu.VMEM((N_HEADS, HEAD_BLOCK, tm), BF16),
        pltpu.VMEM((ATT_DIM, tm), F32),
        pltpu.VMEM((tm, ATT_DIM), F32),
        pltpu.VMEM((tm, D_MODEL), F32),
        pltpu.VMEM((SUBLANES + tm, CONV_DIM), F32),
        pltpu.VMEM((F_ROWS, LANES), F32),
    ]
    return pl.pallas_call(
        functools.partial(_prompt_kernel, tm=tm, n_tiles=n_tiles, n_total=n_total),
        grid=(n_total + 2,),
        in_specs=[per_tile(D_MODEL, 0)] + [_resident(p.shape) for p in params],
        out_specs=out_specs,
        out_shape=out_shape,
        scratch_shapes=scratch,
        compiler_params=pltpu.CompilerParams(
            dimension_semantics=("arbitrary",),
            vmem_limit_bytes=VMEM_LIMIT_BYTES),
        name="trunk_prompt",
    )(x, *params)


def _sample_kernel(x_ref, conv0_ref, pkt_ref, pvt_ref, plft_ref,
                   ln_in_g, ln_in_b, w_std, w_fs, b_f_row, w_t, b_f_col, conv_w, w_pa, w_pb, w_out,
                   ln_g, ln_b, route, ones,
                   y_ref, kt_ref, vt_ref, lft_ref, conv_ref,
                   kta_sc, ktn_sc, hb_sc, qa_sc, att_sc, u_sc, *, ts, n_past):
    @pl.when(pl.program_id(0) == 0)
    def _zero_padding():
        kta_sc[...] = jnp.zeros_like(kta_sc)
        ktn_sc[...] = jnp.zeros_like(ktn_sc)
        hb_sc[...] = jnp.zeros_like(hb_sc)

    h = _layer_norm(x_ref[...], ln_in_g[...], ln_in_b[...])
    hb_sc[0:ts, :] = h.astype(BF16)

    def proj(col, width):
        return jnp.dot(hb_sc[0:ts, :], w_std[:, col:col + width], preferred_element_type=F32)

    u_sc[0:SUBLANES, :] = jnp.zeros((SUBLANES, CONV_DIM), F32)
    u_sc[SUBLANES - (CONV_W - 1):SUBLANES, :] = conv0_ref[...]
    mix_a = _conv_branch(proj, u_sc, conv_w, conv_ref, w_pa, ts)

    def proj_t(row, height):
        return lax.dot_general(w_t[row:row + height, :], hb_sc[...], (((1,), (1,)), ((), ())),
                               preferred_element_type=F32)

    kt = proj_t(_T_K, ATT_DIM)
    vf_t = proj_t(_T_V, ATT_DIM + F_ROWS)
    vt = vf_t[0:ATT_DIM, :]
    kt_ref[...] = kt[:, :ts]
    vt_ref[...] = vt[:, :ts]
    logf_t = _log_sigmoid(vf_t[ATT_DIM:ATT_DIM + SUBLANES, :] + b_f_col[0:SUBLANES, :])
    lft_ref[...] = logf_t[:, :ts]

    c_past = _cumsum(plft_ref[...], 1)
    c_last = c_past[:, n_past - 1:n_past]
    c_new_t = _cumsum(logf_t, 1) + c_last
    past_pieces = [piece.astype(F32) for piece in _split3(c_past)]
    new_pieces = [piece.astype(F32) for piece in _split3(c_new_t)]
    for hd in range(N_HEADS):
        rows = slice(hd * HEAD_DIM, (hd + 1) * HEAD_DIM)
        _store_feature_major(kta_sc, hd, pkt_ref[rows, :].astype(BF16),
                             _extra_rows(past_pieces, hd, n_past, key_side=True))
        _store_feature_major(ktn_sc, hd, kt[rows, :].astype(BF16),
                             _extra_rows(new_pieces, hd, LANES, key_side=True))

    eye = (lax.broadcasted_iota(jnp.int32, (SUBLANES, LANES), 0)
           == lax.broadcasted_iota(jnp.int32, (SUBLANES, LANES), 1))
    c_last_row = jnp.sum(jnp.where(eye, jnp.broadcast_to(c_last, (SUBLANES, LANES)), 0.0),
                         axis=0, keepdims=True)
    logf = _log_sigmoid(jnp.dot(hb_sc[0:ts, :], w_fs[...], preferred_element_type=F32) + b_f_row[...])
    c_q = _cumsum(logf, 0) + c_last_row
    ex_q = _extra_lanes(c_q, route[:, :AUG_DIM]) + ones[:, :AUG_DIM]
    _store_token_major(qa_sc, slice(None), proj(_S_Q, ATT_DIM) * SCALE, ex_q)

    causal = (lax.broadcasted_iota(jnp.int32, (ts, LANES), 1)
              <= lax.broadcasted_iota(jnp.int32, (ts, LANES), 0))
    lane = lax.broadcasted_iota(jnp.int32, (ts, HEAD_BLOCK), 1)
    contract_lanes = (((1,), (1,)), ((), ()))
    scores = []
    for hd in range(N_HEADS):
        qa_h = qa_sc[:, hd * HEAD_BLOCK:(hd + 1) * HEAD_BLOCK]
        scores.append((jnp.dot(qa_h, kta_sc[hd], preferred_element_type=F32),
                       jnp.dot(qa_h, ktn_sc[hd], preferred_element_type=F32)))
    for pair in range(N_HEADS // 2):
        rows = slice(pair * HEAD_BLOCK, (pair + 1) * HEAD_BLOCK)
        vt_past = pvt_ref[rows, :].astype(BF16)
        vt_new = vt[rows, :].astype(BF16)
        outs = []
        for hd in (2 * pair, 2 * pair + 1):
            s_past = scores[hd][0]
            s_new = jnp.where(causal, scores[hd][1], -jnp.inf)
            m = jnp.maximum(jnp.max(s_past, axis=-1, keepdims=True), jnp.max(s_new, axis=-1, keepdims=True))
            p_past = jnp.exp(s_past - m)
            p_new = jnp.exp(s_new - m)
            l = jnp.sum(p_past, axis=-1, keepdims=True) + jnp.sum(p_new, axis=-1, keepdims=True)
            o = (lax.dot_general(p_past.astype(BF16), vt_past, contract_lanes, preferred_element_type=F32)
                 + lax.dot_general(p_new.astype(BF16), vt_new, contract_lanes, preferred_element_type=F32))
            outs.append(o / l)
        att_sc[:, rows] = jnp.where(lane < HEAD_DIM, outs[0], outs[1])

    y_ref[...] = _merge_and_norm(att_sc[...], mix_a, h, jax.nn.silu(proj(_S_ZB, ATT_DIM)),
                                 jax.nn.sigmoid(proj(_S_GM, D_MODEL)), w_pb, w_out, ln_g, ln_b)


def _run_sample(x, conv0, pkt, pvt, plft, params):
    bsz, ts, _ = x.shape
    n_past = pkt.shape[2]
    assert ts <= LANES and ts % SUBLANES == 0
    per_batch = lambda rows, width: pl.BlockSpec((None, rows, width), lambda b: (b, 0, 0))
    out_shape = (jax.ShapeDtypeStruct((bsz, ts, D_MODEL), F32),
                 jax.ShapeDtypeStruct((bsz, ATT_DIM, ts), F32),
                 jax.ShapeDtypeStruct((bsz, ATT_DIM, ts), F32),
                 jax.ShapeDtypeStruct((bsz, N_HEADS, ts), F32),
                 jax.ShapeDtypeStruct((bsz, CONV_W - 1, CONV_DIM), F32))
    out_specs = (per_batch(ts, D_MODEL), per_batch(ATT_DIM, ts), per_batch(ATT_DIM, ts),
                 per_batch(N_HEADS, ts), per_batch(CONV_W - 1, CONV_DIM))
    in_specs = [per_batch(ts, D_MODEL), per_batch(CONV_W - 1, CONV_DIM), per_batch(ATT_DIM, n_past),
                per_batch(ATT_DIM, n_past), per_batch(N_HEADS, n_past)]
    scratch = [
        pltpu.VMEM((N_HEADS, HEAD_BLOCK, n_past), BF16),
        pltpu.VMEM((N_HEADS, HEAD_BLOCK, LANES), BF16),
        pltpu.VMEM((LANES, D_MODEL), BF16),
        pltpu.VMEM((ts, AUG_DIM), BF16),
        pltpu.VMEM((ts, ATT_DIM), F32),
        pltpu.VMEM((SUBLANES + ts, CONV_DIM), F32),
    ]
    return pl.pallas_call(
        functools.partial(_sample_kernel, ts=ts, n_past=n_past),
        grid=(bsz,),
        in_specs=in_specs + [_resident(p.shape) for p in params],
        out_specs=out_specs,
        out_shape=out_shape,
        scratch_shapes=scratch,
        compiler_params=pltpu.CompilerParams(
            dimension_semantics=("arbitrary",),
            vmem_limit_bytes=VMEM_LIMIT_BYTES),
        name="trunk_sample",
    )(x, conv0, pkt, pvt, plft, *params)


def kernel(x_prompt, x_sample, cache_k, cache_v, cache_logf, state_conv, ln_in_g, ln_in_b, w_in, b_f,
           conv_w, w_proj_a, w_proj_b, w_out, ln_g, ln_b):
    assert w_in.shape[0] == DEPTH
    route_np, ones_np = _routing_constants()
    w = w_in[0]
    row = lambda a: a.reshape(1, -1).astype(F32)
    w_t = jnp.concatenate([w[:, _W_Q:_W_F + N_HEADS],
                           jnp.zeros((D_MODEL, F_ROWS - N_HEADS), w.dtype)], axis=1).T.astype(BF16)
    tm = PROMPT_TILE
    p = dict(
        ln_in_g=row(ln_in_g), ln_in_b=row(ln_in_b),
        w_std=jnp.concatenate([w[:, :_W_V], w[:, _W_ZB:]], axis=1).astype(BF16),
        w_fs=jnp.pad(w[:, _W_F:_W_F + N_HEADS], ((0, 0), (0, LANES - N_HEADS))).astype(BF16),
        b_f_row=jnp.pad(row(b_f[0]), ((0, 0), (0, LANES - N_HEADS))),
        w_t=w_t,
        b_f_col=jnp.pad(b_f[0].astype(F32).reshape(N_HEADS, 1), ((0, F_ROWS - N_HEADS), (0, 0))),
        conv_w=conv_w[0].astype(F32),
        w_pa=w_proj_a[0].astype(BF16), w_pb=w_proj_b[0].astype(BF16), w_out=w_out[0].astype(BF16),
        ln_g=row(ln_g[0]), ln_b=row(ln_b[0]),
        route=jnp.asarray(route_np, BF16), ones=jnp.asarray(ones_np, F32),
        tri=jnp.asarray(np.triu(np.ones((tm, tm), np.float32)), BF16),
    )
    bs = x_sample.shape[0]
    n_past = cache_k.shape[2]
    feature_major = lambda a: a.transpose(0, 2, 3, 1).reshape(bs, ATT_DIM, n_past)

    y_p, kt_p, vt_p, lft_p, cv_p = _run_prompt(x_prompt, [p[name] for name in PROMPT_PARAMS], tm=tm)
    y_s, kt_s, vt_s, lft_s, cv_s = _run_sample(
        x_sample, state_conv[0], feature_major(cache_k[0]), feature_major(cache_v[0]),
        cache_logf[0].transpose(0, 2, 1), [p[name] for name in SAMPLE_PARAMS])

    heads_t = lambda a: a.reshape(a.shape[0], N_HEADS, HEAD_DIM, a.shape[2]).transpose(0, 3, 1, 2)[None]
    tokens_t = lambda a: a.transpose(0, 2, 1)[None]
    return (y_p, y_s, heads_t(kt_p), heads_t(vt_p), tokens_t(lft_p), cv_p[None],
            heads_t(kt_s), heads_t(vt_s), tokens_t(lft_s), cv_s[None])
```

```python
import functools

import numpy as np
import jax
import jax.numpy as jnp
from jax import lax
from jax.experimental import pallas as pl
from jax.experimental.pallas import tpu as pltpu

D_MODEL = 1024
CONV_DIM = 512
CONV_W = 3
N_HEADS = 8
HEAD_DIM = 64
ATT_DIM = N_HEADS * HEAD_DIM
LN_EPS = 1e-5
DEPTH = 1
ALPHA = (2.0 * DEPTH) ** 0.25
SCALE = HEAD_DIM ** -0.5

LANES = 128
SUBLANES = 8
HEAD_BLOCK = LANES
AUG_DIM = N_HEADS * HEAD_BLOCK
N_SPLIT = 3
EXTRA_ROWS = 16
F_ROWS = 16
VMEM_LIMIT_BYTES = 56 * 1024 * 1024
V_ROWS = HEAD_DIM + 16
LOG2E = 1.4426950408889634
TAIL_SHARE = 2
SCORE_LOOKAHEAD = 4

_W_Q = 4 * CONV_DIM
_W_V = _W_Q + 2 * ATT_DIM
_W_F = _W_Q + 3 * ATT_DIM
_W_ZB = _W_F + N_HEADS
_S_XA, _S_GB, _S_GC, _S_ZA, _S_Q, _S_K, _S_ZB, _S_GA, _S_GM = (
    0, 512, 1024, 1536, 2048, 2560, 3072, 3584, 4608)
_T_Q, _T_K, _T_V, _T_F = 0, 512, 1024, 1536

PROMPT_TILE = 256
PROMPT_PARAMS = ("ln_in_g", "ln_in_b", "w_std", "w_t", "b_f_col", "conv_w", "w_pa", "w_pb", "w_out",
                 "ln_g", "ln_b", "route", "ones", "tri")
SAMPLE_PARAMS = ("ln_in_g", "ln_in_b", "w_std", "w_fs", "b_f_row", "w_t", "b_f_col", "conv_w", "w_pa",
                 "w_pb", "w_out", "ln_g", "ln_b", "route", "ones")

F32 = jnp.float32
BF16 = jnp.bfloat16


def _routing_constants():
    route = np.zeros((LANES, 2 * AUG_DIM), np.float32)
    ones = np.zeros((1, 2 * AUG_DIM), np.float32)
    for h in range(N_HEADS):
        base = h * HEAD_BLOCK + (HEAD_DIM if h % 2 == 0 else 0)
        for r in range(N_SPLIT):
            route[r * N_HEADS + h, base + 2 * r] = 1.0
            ones[0, base + 2 * r + 1] = 1.0
            ones[0, AUG_DIM + base + 2 * r] = 1.0
            route[r * N_HEADS + h, AUG_DIM + base + 2 * r + 1] = -1.0
    return route, ones


def _layer_norm(x, g, b):
    mu = jnp.mean(x, axis=-1, keepdims=True)
    xc = x - mu
    var = jnp.mean(xc * xc, axis=-1, keepdims=True)
    return xc * lax.rsqrt(var + LN_EPS) * g + b


def _log_sigmoid(x):
    return jnp.minimum(x, 0.0) - jnp.log1p(jnp.exp(-jnp.abs(x)))


def _split3(x):
    hi = x.astype(BF16)
    r1 = x - hi.astype(F32)
    mid = r1.astype(BF16)
    lo = (r1 - mid.astype(F32)).astype(BF16)
    return hi, mid, lo


def _cumsum(x, axis):
    n = x.shape[axis]
    idx = lax.broadcasted_iota(jnp.int32, x.shape, axis)
    s = 1
    while s < n:
        x = x + jnp.where(idx >= s, pltpu.roll(x, s, axis), 0.0)
        s *= 2
    return x


def _extra_lanes(c, route):
    lane = lax.broadcasted_iota(jnp.int32, c.shape, 1)
    valid = lane < N_HEADS
    packed = jnp.zeros_like(c)
    for r, piece in enumerate(_split3(c)):
        p32 = jnp.where(valid, piece.astype(F32), 0.0)
        packed = packed + (pltpu.roll(p32, r * N_HEADS, 1) if r else p32)
    return jnp.dot(packed.astype(BF16), route, preferred_element_type=F32)


def _store_token_major(dst_ref, rows, data, extras):
    n = data.shape[0]
    lane = lax.broadcasted_iota(jnp.int32, (n, HEAD_BLOCK), 1)
    for h in range(N_HEADS):
        pair = data[:, (h // 2) * HEAD_BLOCK:(h // 2 + 1) * HEAD_BLOCK]
        ex = extras[:, h * HEAD_BLOCK:(h + 1) * HEAD_BLOCK]
        keep = (lane < HEAD_DIM) if h % 2 == 0 else (lane >= HEAD_DIM)
        dst_ref[rows, h * HEAD_BLOCK:(h + 1) * HEAD_BLOCK] = jnp.where(keep, pair, ex).astype(BF16)


def _extra_rows(pieces, h, n, key_side):
    r = lax.broadcasted_iota(jnp.int32, (EXTRA_ROWS, n), 0)
    value_row, sign = (1, -1.0) if key_side else (0, 1.0)
    out = jnp.where((r < 2 * N_SPLIT) & ((r & 1) != value_row), 1.0, 0.0)
    for i, piece in enumerate(pieces):
        out = jnp.where(r == 2 * i + value_row,
                        jnp.broadcast_to(sign * piece[h:h + 1, :], (EXTRA_ROWS, n)), out)
    return out.astype(BF16)


def _store_feature_major(dst_ref, h, data, extra):
    if h % 2 == 0:
        dst_ref[h, 0:HEAD_DIM, :] = data
        dst_ref[h, HEAD_DIM:HEAD_DIM + EXTRA_ROWS, :] = extra
    else:
        dst_ref[h, 0:EXTRA_ROWS, :] = extra
        dst_ref[h, HEAD_DIM:2 * HEAD_DIM, :] = data


def _conv_branch(proj, u_sc, conv_w, conv_ref, w_pa, n):
    first = SUBLANES - (CONV_W - 1)
    u_sc[SUBLANES:SUBLANES + n, :] = proj(_S_GC, CONV_DIM) * proj(_S_XA, CONV_DIM)
    conv = sum(u_sc[first + i:first + i + n, :] * conv_w[i:i + 1, :] for i in range(CONV_W))
    new_conv = u_sc[first + n:SUBLANES + n, :]
    conv_ref[...] = new_conv
    u_sc[first:SUBLANES, :] = new_conv
    y_a = proj(_S_GB, CONV_DIM) * conv * jax.nn.silu(proj(_S_ZA, CONV_DIM))
    return jax.nn.sigmoid(proj(_S_GA, D_MODEL)) * jnp.dot(
        y_a.astype(BF16), w_pa[...], preferred_element_type=F32)


def _merge_and_norm(att, mix_a, h, silu_zb, gate_b, w_pb, w_out, ln_g, ln_b):
    y_b = att * silu_zb
    mixed = mix_a + gate_b * jnp.dot(
        y_b.astype(BF16), w_pb[...], preferred_element_type=F32)
    sub = jnp.dot(mixed.astype(BF16), w_out[...], preferred_element_type=F32)
    return _layer_norm(ALPHA * h + sub, ln_g[...], ln_b[...])


def _attention_t(j_lo, j_hi, j, tm, kaug_sc, qta_sc, vt_sc, attt_sc):
    n_key_tiles = j_hi + 1
    keys = n_key_tiles * tm
    key_in_tile = lax.broadcasted_iota(jnp.int32, (tm, tm), 0)
    query_in_tile = lax.broadcasted_iota(jnp.int32, (tm, tm), 1)
    tile_of_query = j if j_hi > j_lo else j_lo
    visible = {i: key_in_tile + (i - tile_of_query) * tm <= query_in_tile for i in range(j_lo, n_key_tiles)}

    def scores(hd):
        return jnp.dot(kaug_sc[0:keys, hd * HEAD_BLOCK:(hd + 1) * HEAD_BLOCK], qta_sc[hd],
                       preferred_element_type=F32)

    pending = [scores(hd) for hd in range(SCORE_LOOKAHEAD)]
    for hd in range(N_HEADS):
        s = pending.pop(0)
        if hd + SCORE_LOOKAHEAD < N_HEADS:
            pending.append(scores(hd + SCORE_LOOKAHEAD))
        blocks = [s[i * tm:(i + 1) * tm, :] for i in range(n_key_tiles)]
        for i, mask in visible.items():
            blocks[i] = jnp.where(mask, blocks[i], -jnp.inf)
        m = functools.reduce(jnp.maximum, [jnp.max(blk, axis=0, keepdims=True) for blk in blocks])
        o = jnp.zeros((V_ROWS, tm), F32)
        for i, blk in enumerate(blocks):
            o = o + jnp.dot(vt_sc[i, hd * V_ROWS:(hd + 1) * V_ROWS, :], jnp.exp2(blk - m).astype(BF16),
                            preferred_element_type=F32)
        attt_sc[hd * HEAD_DIM:(hd + 1) * HEAD_DIM, :] = o[0:HEAD_DIM, :] / o[HEAD_DIM:HEAD_DIM + 1, :]


def _prompt_kernel(x_ref, ln_in_g, ln_in_b, w_std, w_t, b_f_col, conv_w, w_pa, w_pb, w_out,
                   ln_g, ln_b, route, ones, tri,
                   y_ref, kt_ref, vt_ref, lft_ref, conv_ref,
                   kaug_sc, vt_sc, h_sc, hb_sc, z_sc, qta_sc, attt_sc, zb_sc, gm_sc, u_sc, carryt_sc,
                   *, tm, n_tiles, n_total):
    t = pl.program_id(0)

    def norm_in(slot):
        h = _layer_norm(x_ref[...], ln_in_g[...], ln_in_b[...])
        h_sc[slot] = h
        hb_sc[slot] = h.astype(BF16)

    def norm_out():
        y_ref[...] = _layer_norm(z_sc[...], ln_g[...], ln_b[...])

    @pl.when(t == 0)
    def _fill():
        z_sc[...] = jnp.zeros_like(z_sc)
        kaug_sc[...] = jnp.zeros_like(kaug_sc)
        vt_sc[...] = jnp.zeros_like(vt_sc)
        norm_in(0)

    @pl.when((t >= 1) & (t <= n_total))
    def _mixers():
        j = lax.rem(t - 1, n_tiles)
        cur = (t - 1) & 1

        @pl.when(j == 0)
        def _start_of_sequence():
            u_sc[0:SUBLANES, :] = jnp.zeros((SUBLANES, CONV_DIM), F32)
            carryt_sc[...] = jnp.zeros_like(carryt_sc)
            qta_sc[...] = jnp.zeros_like(qta_sc)
            ones_row = jnp.where(lax.broadcasted_iota(jnp.int32, (V_ROWS - HEAD_DIM, tm), 0) == 0, 1.0, 0.0)
            for i in range(n_tiles):
                for hd in range(N_HEADS):
                    vt_sc[i, hd * V_ROWS + HEAD_DIM:(hd + 1) * V_ROWS, :] = ones_row.astype(BF16)

        def proj(col, width):
            return jnp.dot(hb_sc[cur], w_std[:, col:col + width], preferred_element_type=F32)

        def proj_t(row, height):
            return lax.dot_general(w_t[row:row + height, :], hb_sc[cur], (((1,), (1,)), ((), ())),
                                   preferred_element_type=F32)

        norm_out()

        vf_t = proj_t(_T_V, ATT_DIM + F_ROWS)
        qk_t = proj_t(_T_Q, 2 * ATT_DIM)
        kt = qk_t[ATT_DIM:2 * ATT_DIM, :]
        vt = vf_t[0:ATT_DIM, :]
        kt_ref[...] = kt
        vt_ref[...] = vt
        for hd in range(N_HEADS):
            vt_sc[j, hd * V_ROWS:hd * V_ROWS + HEAD_DIM, :] = (
                vt[hd * HEAD_DIM:(hd + 1) * HEAD_DIM, :].astype(BF16))
        logf_t = _log_sigmoid(vf_t[ATT_DIM:ATT_DIM + F_ROWS, :] + b_f_col[...])
        lft_ref[...] = logf_t[:N_HEADS, :]
        sums = jnp.dot(jnp.concatenate(_split3(logf_t), axis=0), tri[...], preferred_element_type=F32)
        c_t = sum(sums[r * F_ROWS:(r + 1) * F_ROWS, :] for r in range(N_SPLIT)) + carryt_sc[:, 0:1]
        carryt_sc[...] = jnp.broadcast_to(c_t[:, tm - 1:tm], carryt_sc.shape)
        zb_sc[...] = jax.nn.silu(proj(_S_ZB, ATT_DIM))
        gm_sc[...] = jax.nn.sigmoid(proj(_S_GM, D_MODEL))
        c_pieces = [piece.astype(F32) for piece in _split3(c_t * LOG2E)]
        qt = qk_t[0:ATT_DIM, :] * (SCALE * LOG2E)
        for hd in range(N_HEADS):
            _store_feature_major(qta_sc, hd, qt[hd * HEAD_DIM:(hd + 1) * HEAD_DIM, :].astype(BF16),
                                 _extra_rows(c_pieces, hd, tm, key_side=False))

        packed_t = jnp.concatenate([piece[0:N_HEADS, :] for piece in c_pieces]
                                   + [jnp.zeros((LANES - N_SPLIT * N_HEADS, tm), F32)], axis=0)
        ex_k = (jnp.dot(packed_t.T.astype(BF16), route[:, AUG_DIM:], preferred_element_type=F32)
                + ones[:, AUG_DIM:])
        _store_token_major(kaug_sc, pl.ds(pl.multiple_of(j * tm, tm), tm), kt.T, ex_k)

        norm_in(t & 1)

        spans = [(jj, jj) for jj in range(n_tiles - TAIL_SHARE)] + [(n_tiles - TAIL_SHARE, n_tiles - 1)]
        for j_lo, j_hi in spans:
            pl.when((j >= j_lo) & (j <= j_hi))(functools.partial(
                _attention_t, j_lo, j_hi, j, tm, kaug_sc, qta_sc, vt_sc, attt_sc))

        mix_a = _conv_branch(proj, u_sc, conv_w, conv_ref, w_pa, tm)
        y_b = attt_sc[...].T * zb_sc[...]
        mixed = mix_a + gm_sc[...] * jnp.dot(y_b.astype(BF16), w_pb[...], preferred_element_type=F32)
        z_sc[...] = ALPHA * h_sc[cur] + jnp.dot(mixed.astype(BF16), w_out[...], preferred_element_type=F32)

    @pl.when(t == n_total + 1)
    def _drain():
        norm_out()


def _resident(shape):
    return pl.BlockSpec(shape, lambda *_: (0,) * len(shape), pipeline_mode=pl.Buffered(1))


def _run_prompt(x, params, *, tm):
    bsz, t, _ = x.shape
    n_tiles = t // tm
    assert n_tiles * tm == t
    n_total = bsz * n_tiles
    tile = lambda s, lag: jnp.clip(s - lag, 0, n_total - 1)
    per_tile = lambda width, lag: pl.BlockSpec(
        (None, tm, width), lambda s: (tile(s, lag) // n_tiles, tile(s, lag) % n_tiles, 0))
    per_tile_t = lambda rows: pl.BlockSpec(
        (None, rows, tm), lambda s: (tile(s, 1) // n_tiles, 0, tile(s, 1) % n_tiles))
    out_shape = (jax.ShapeDtypeStruct((bsz, t, D_MODEL), F32),
                 jax.ShapeDtypeStruct((bsz, ATT_DIM, t), F32),
                 jax.ShapeDtypeStruct((bsz, ATT_DIM, t), F32),
                 jax.ShapeDtypeStruct((bsz, N_HEADS, t), F32),
                 jax.ShapeDtypeStruct((bsz, CONV_W - 1, CONV_DIM), F32))
    out_specs = (per_tile(D_MODEL, 2), per_tile_t(ATT_DIM), per_tile_t(ATT_DIM), per_tile_t(N_HEADS),
                 pl.BlockSpec((None, CONV_W - 1, CONV_DIM), lambda s: (tile(s, 1) // n_tiles, 0, 0)))
    scratch = [
        pltpu.VMEM((t, AUG_DIM), BF16),
        pltpu.VMEM((n_tiles, N_HEADS * V_ROWS, tm), BF16),
        pltpu.VMEM((2, tm, D_MODEL), F32),
        pltpu.VMEM((2, tm, D_MODEL), BF16),
        pltpu.VMEM((tm, D_MODEL), F32),
        pltpu.VMEM((N_HEADS, HEAD_BLOCK, tm), BF16),
        pltpu.VMEM((ATT_DIM, tm), F32),
        pltpu.VMEM((tm, ATT_DIM), F32),
        pltpu.VMEM((tm, D_MODEL), F32),
        pltpu.VMEM((SUBLANES + tm, CONV_DIM), F32),
        pltpu.VMEM((F_ROWS, LANES), F32),
    ]
    return pl.pallas_call(
        functools.partial(_prompt_kernel, tm=tm, n_tiles=n_tiles, n_total=n_total),
        grid=(n_total + 2,),
        in_specs=[per_tile(D_MODEL, 0)] + [_resident(p.shape) for p in params],
        out_specs=out_specs,
        out_shape=out_shape,
        scratch_shapes=scratch,
        compiler_params=pltpu.CompilerParams(
            dimension_semantics=("arbitrary",),
            vmem_limit_bytes=VMEM_LIMIT_BYTES),
        name="trunk_prompt",
    )(x, *params)


def _sample_kernel(x_ref, conv0_ref, pkt_ref, pvt_ref, plft_ref,
                   ln_in_g, ln_in_b, w_std, w_fs, b_f_row, w_t, b_f_col, conv_w, w_pa, w_pb, w_out,
                   ln_g, ln_b, route, ones,
                   y_ref, kt_ref, vt_ref, lft_ref, conv_ref,
                   kta_sc, ktn_sc, hb_sc, qa_sc, att_sc, u_sc, *, ts, n_past):
    @pl.when(pl.program_id(0) == 0)
    def _zero_padding():
        kta_sc[...] = jnp.zeros_like(kta_sc)
        ktn_sc[...] = jnp.zeros_like(ktn_sc)
        hb_sc[...] = jnp.zeros_like(hb_sc)

    h = _layer_norm(x_ref[...], ln_in_g[...], ln_in_b[...])
    hb_sc[0:ts, :] = h.astype(BF16)

    def proj(col, width):
        return jnp.dot(hb_sc[0:ts, :], w_std[:, col:col + width], preferred_element_type=F32)

    u_sc[0:SUBLANES, :] = jnp.zeros((SUBLANES, CONV_DIM), F32)
    u_sc[SUBLANES - (CONV_W - 1):SUBLANES, :] = conv0_ref[...]
    mix_a = _conv_branch(proj, u_sc, conv_w, conv_ref, w_pa, ts)

    def proj_t(row, height):
        return lax.dot_general(w_t[row:row + height, :], hb_sc[...], (((1,), (1,)), ((), ())),
                               preferred_element_type=F32)

    kt = proj_t(_T_K, ATT_DIM)
    vf_t = proj_t(_T_V, ATT_DIM + F_ROWS)
    vt = vf_t[0:ATT_DIM, :]
    kt_ref[...] = kt[:, :ts]
    vt_ref[...] = vt[:, :ts]
    logf_t = _log_sigmoid(vf_t[ATT_DIM:ATT_DIM + SUBLANES, :] + b_f_col[0:SUBLANES, :])
    lft_ref[...] = logf_t[:, :ts]

    c_past = _cumsum(plft_ref[...], 1)
    c_last = c_past[:, n_past - 1:n_past]
    c_new_t = _cumsum(logf_t, 1) + c_last
    past_pieces = [piece.astype(F32) for piece in _split3(c_past)]
    new_pieces = [piece.astype(F32) for piece in _split3(c_new_t)]
    for hd in range(N_HEADS):
        rows = slice(hd * HEAD_DIM, (hd + 1) * HEAD_DIM)
        _store_feature_major(kta_sc, hd, pkt_ref[rows, :].astype(BF16),
                             _extra_rows(past_pieces, hd, n_past, key_side=True))
        _store_feature_major(ktn_sc, hd, kt[rows, :].astype(BF16),
                             _extra_rows(new_pieces, hd, LANES, key_side=True))

    eye = (lax.broadcasted_iota(jnp.int32, (SUBLANES, LANES), 0)
           == lax.broadcasted_iota(jnp.int32, (SUBLANES, LANES), 1))
    c_last_row = jnp.sum(jnp.where(eye, jnp.broadcast_to(c_last, (SUBLANES, LANES)), 0.0),
                         axis=0, keepdims=True)
    logf = _log_sigmoid(jnp.dot(hb_sc[0:ts, :], w_fs[...], preferred_element_type=F32) + b_f_row[...])
    c_q = _cumsum(logf, 0) + c_last_row
    ex_q = _extra_lanes(c_q, route[:, :AUG_DIM]) + ones[:, :AUG_DIM]
    _store_token_major(qa_sc, slice(None), proj(_S_Q, ATT_DIM) * SCALE, ex_q)

    causal = (lax.broadcasted_iota(jnp.int32, (ts, LANES), 1)
              <= lax.broadcasted_iota(jnp.int32, (ts, LANES), 0))
    lane = lax.broadcasted_iota(jnp.int32, (ts, HEAD_BLOCK), 1)
    contract_lanes = (((1,), (1,)), ((), ()))
    scores = []
    for hd in range(N_HEADS):
        qa_h = qa_sc[:, hd * HEAD_BLOCK:(hd + 1) * HEAD_BLOCK]
        scores.append((jnp.dot(qa_h, kta_sc[hd], preferred_element_type=F32),
                       jnp.dot(qa_h, ktn_sc[hd], preferred_element_type=F32)))
    for pair in range(N_HEADS // 2):
        rows = slice(pair * HEAD_BLOCK, (pair + 1) * HEAD_BLOCK)
        vt_past = pvt_ref[rows, :].astype(BF16)
        vt_new = vt[rows, :].astype(BF16)
        outs = []
        for hd in (2 * pair, 2 * pair + 1):
            s_past = scores[hd][0]
            s_new = jnp.where(causal, scores[hd][1], -jnp.inf)
            m = jnp.maximum(jnp.max(s_past, axis=-1, keepdims=True), jnp.max(s_new, axis=-1, keepdims=True))
            p_past = jnp.exp(s_past - m)
            p_new = jnp.exp(s_new - m)
            l = jnp.sum(p_past, axis=-1, keepdims=True) + jnp.sum(p_new, axis=-1, keepdims=True)
            o = (lax.dot_general(p_past.astype(BF16), vt_past, contract_lanes, preferred_element_type=F32)
                 + lax.dot_general(p_new.astype(BF16), vt_new, contract_lanes, preferred_element_type=F32))
            outs.append(o / l)
        att_sc[:, rows] = jnp.where(lane < HEAD_DIM, outs[0], outs[1])

    y_ref[...] = _merge_and_norm(att_sc[...], mix_a, h, jax.nn.silu(proj(_S_ZB, ATT_DIM)),
                                 jax.nn.sigmoid(proj(_S_GM, D_MODEL)), w_pb, w_out, ln_g, ln_b)


def _run_sample(x, conv0, pkt, pvt, plft, params):
    bsz, ts, _ = x.shape
    n_past = pkt.shape[2]
    assert ts <= LANES and ts % SUBLANES == 0
    per_batch = lambda rows, width: pl.BlockSpec((None, rows, width), lambda b: (b, 0, 0))
    out_shape = (jax.ShapeDtypeStruct((bsz, ts, D_MODEL), F32),
                 jax.ShapeDtypeStruct((bsz, ATT_DIM, ts), F32),
                 jax.ShapeDtypeStruct((bsz, ATT_DIM, ts), F32),
                 jax.ShapeDtypeStruct((bsz, N_HEADS, ts), F32),
                 jax.ShapeDtypeStruct((bsz, CONV_W - 1, CONV_DIM), F32))
    out_specs = (per_batch(ts, D_MODEL), per_batch(ATT_DIM, ts), per_batch(ATT_DIM, ts),
                 per_batch(N_HEADS, ts), per_batch(CONV_W - 1, CONV_DIM))
    in_specs = [per_batch(ts, D_MODEL), per_batch(CONV_W - 1, CONV_DIM), per_batch(ATT_DIM, n_past),
                per_batch(ATT_DIM, n_past), per_batch(N_HEADS, n_past)]
    scratch = [
        pltpu.VMEM((N_HEADS, HEAD_BLOCK, n_past), BF16),
        pltpu.VMEM((N_HEADS, HEAD_BLOCK, LANES), BF16),
        pltpu.VMEM((LANES, D_MODEL), BF16),
        pltpu.VMEM((ts, AUG_DIM), BF16),
        pltpu.VMEM((ts, ATT_DIM), F32),
        pltpu.VMEM((SUBLANES + ts, CONV_DIM), F32),
    ]
    return pl.pallas_call(
        functools.partial(_sample_kernel, ts=ts, n_past=n_past),
        grid=(bsz,),
        in_specs=in_specs + [_resident(p.shape) for p in params],
        out_specs=out_specs,
        out_shape=out_shape,
        scratch_shapes=scratch,
        compiler_params=pltpu.CompilerParams(
            dimension_semantics=("arbitrary",),
            vmem_limit_bytes=VMEM_LIMIT_BYTES),
        name="trunk_sample",
    )(x, conv0, pkt, pvt, plft, *params)


def kernel(x_prompt, x_sample, cache_k, cache_v, cache_logf, state_conv, ln_in_g, ln_in_b, w_in, b_f,
           conv_w, w_proj_a, w_proj_b, w_out, ln_g, ln_b):
    assert w_in.shape[0] == DEPTH
    route_np, ones_np = _routing_constants()
    w = w_in[0]
    row = lambda a: a.reshape(1, -1).astype(F32)
    w_t = jnp.concatenate([w[:, _W_Q:_W_F + N_HEADS],
                           jnp.zeros((D_MODEL, F_ROWS - N_HEADS), w.dtype)], axis=1).T.astype(BF16)
    tm = PROMPT_TILE
    p = dict(
        ln_in_g=row(ln_in_g), ln_in_b=row(ln_in_b),
        w_std=jnp.concatenate([w[:, :_W_V], w[:, _W_ZB:]], axis=1).astype(BF16),
        w_fs=jnp.pad(w[:, _W_F:_W_F + N_HEADS], ((0, 0), (0, LANES - N_HEADS))).astype(BF16),
        b_f_row=jnp.pad(row(b_f[0]), ((0, 0), (0, LANES - N_HEADS))),
        w_t=w_t,
        b_f_col=jnp.pad(b_f[0].astype(F32).reshape(N_HEADS, 1), ((0, F_ROWS - N_HEADS), (0, 0))),
        conv_w=conv_w[0].astype(F32),
        w_pa=w_proj_a[0].astype(BF16), w_pb=w_proj_b[0].astype(BF16), w_out=w_out[0].astype(BF16),
        ln_g=row(ln_g[0]), ln_b=row(ln_b[0]),
        route=jnp.asarray(route_np, BF16), ones=jnp.asarray(ones_np, F32),
        tri=jnp.asarray(np.triu(np.ones((tm, tm), np.float32)), BF16),
    )
    bs = x_sample.shape[0]
    n_past = cache_k.shape[2]
    feature_major = lambda a: a.transpose(0, 2, 3, 1).reshape(bs, ATT_DIM, n_past)

    y_p, kt_p, vt_p, lft_p, cv_p = _run_prompt(x_prompt, [p[name] for name in PROMPT_PARAMS], tm=tm)
    y_s, kt_s, vt_s, lft_s, cv_s = _run_sample(
        x_sample, state_conv[0], feature_major(cache_k[0]), feature_major(cache_v[0]),
        cache_logf[0].transpose(0, 2, 1), [p[name] for name in SAMPLE_PARAMS])

    heads_t = lambda a: a.reshape(a.shape[0], N_HEADS, HEAD_DIM, a.shape[2]).transpose(0, 3, 1, 2)[None]
    tokens_t = lambda a: a.transpose(0, 2, 1)[None]
    return (y_p, y_s, heads_t(kt_p), heads_t(vt_p), tokens_t(lft_p), cv_p[None],
            heads_t(kt_s), heads_t(vt_s), tokens_t(lft_s), cv_s[None])
```

```python
import functools

import numpy as np
import jax
import jax.numpy as jnp
from jax import lax
from jax.experimental import pallas as pl
from jax.experimental.pallas import tpu as pltpu

D_MODEL = 1024
CONV_DIM = 512
CONV_W = 3
N_HEADS = 8
HEAD_DIM = 64
ATT_DIM = N_HEADS * HEAD_DIM
LN_EPS = 1e-5
DEPTH = 1
ALPHA = (2.0 * DEPTH) ** 0.25
SCALE = HEAD_DIM ** -0.5

LANES = 128
SUBLANES = 8
HEAD_BLOCK = LANES
AUG_DIM = N_HEADS * HEAD_BLOCK
N_SPLIT = 3
ONES_LANE = N_SPLIT * N_HEADS
EXTRA_ROWS = 16
F_ROWS = 16
VMEM_LIMIT_BYTES = 56 * 1024 * 1024
V_ROWS = HEAD_DIM + 16
LOG2E = 1.4426950408889634
PIPE_SLOTS = 3
TAIL_SHARE = 1
SCORE_LOOKAHEAD = 4

_W_Q = 4 * CONV_DIM
_W_V = _W_Q + 2 * ATT_DIM
_W_F = _W_Q + 3 * ATT_DIM
_W_ZB = _W_F + N_HEADS
_S_XA, _S_GB, _S_GC, _S_ZA, _S_Q, _S_K, _S_ZB, _S_GA, _S_GM = (
    0, 512, 1024, 1536, 2048, 2560, 3072, 3584, 4608)
_T_Q, _T_K, _T_V, _T_F = 0, 512, 1024, 1536

PROMPT_TILE = 256
PROMPT_PARAMS = ("ln_in_g", "ln_in_b", "w_std", "w_t", "b_f_col", "conv_w", "w_pa", "w_pb", "w_out",
                 "ln_g", "ln_b", "route", "tri")
SAMPLE_PARAMS = ("ln_in_g", "ln_in_b", "w_std", "w_fs", "b_f_row", "w_t", "b_f_col", "conv_w", "w_pa",
                 "w_pb", "w_out", "ln_g", "ln_b", "route", "ones")

F32 = jnp.float32
BF16 = jnp.bfloat16


def _routing_constants():
    route = np.zeros((LANES, 2 * AUG_DIM), np.float32)
    ones = np.zeros((1, 2 * AUG_DIM), np.float32)
    for h in range(N_HEADS):
        base = h * HEAD_BLOCK + (HEAD_DIM if h % 2 == 0 else 0)
        for r in range(N_SPLIT):
            route[r * N_HEADS + h, base + 2 * r] = 1.0
            ones[0, base + 2 * r + 1] = 1.0
            ones[0, AUG_DIM + base + 2 * r] = 1.0
            route[r * N_HEADS + h, AUG_DIM + base + 2 * r + 1] = -1.0
    route[ONES_LANE, :] = ones[0, :]
    return route, ones


def _layer_norm(x, g, b):
    mu = jnp.mean(x, axis=-1, keepdims=True)
    xc = x - mu
    var = jnp.mean(xc * xc, axis=-1, keepdims=True)
    return xc * lax.rsqrt(var + LN_EPS) * g + b


def _log_sigmoid(x):
    return jnp.minimum(x, 0.0) - jnp.log1p(jnp.exp(-jnp.abs(x)))


def _split3(x):
    hi = x.astype(BF16)
    r1 = x - hi.astype(F32)
    mid = r1.astype(BF16)
    lo = (r1 - mid.astype(F32)).astype(BF16)
    return hi, mid, lo


def _cumsum(x, axis):
    n = x.shape[axis]
    idx = lax.broadcasted_iota(jnp.int32, x.shape, axis)
    s = 1
    while s < n:
        x = x + jnp.where(idx >= s, pltpu.roll(x, s, axis), 0.0)
        s *= 2
    return x


def _extra_lanes(c, route):
    lane = lax.broadcasted_iota(jnp.int32, c.shape, 1)
    valid = lane < N_HEADS
    packed = jnp.zeros_like(c)
    for r, piece in enumerate(_split3(c)):
        p32 = jnp.where(valid, piece.astype(F32), 0.0)
        packed = packed + (pltpu.roll(p32, r * N_HEADS, 1) if r else p32)
    return jnp.dot(packed.astype(BF16), route, preferred_element_type=F32)


def _store_token_major(dst_ref, rows, data, extras):
    n = data.shape[0]
    lane = lax.broadcasted_iota(jnp.int32, (n, HEAD_BLOCK), 1)
    for h in range(N_HEADS):
        pair = data[:, (h // 2) * HEAD_BLOCK:(h // 2 + 1) * HEAD_BLOCK]
        ex = extras[:, h * HEAD_BLOCK:(h + 1) * HEAD_BLOCK]
        keep = (lane < HEAD_DIM) if h % 2 == 0 else (lane >= HEAD_DIM)
        dst_ref[rows, h * HEAD_BLOCK:(h + 1) * HEAD_BLOCK] = jnp.where(keep, pair, ex).astype(BF16)


def _extra_rows(pieces, h, n, key_side):
    r = lax.broadcasted_iota(jnp.int32, (EXTRA_ROWS, n), 0)
    value_row, sign = (1, -1.0) if key_side else (0, 1.0)
    out = jnp.where((r < 2 * N_SPLIT) & ((r & 1) != value_row), 1.0, 0.0)
    for i, piece in enumerate(pieces):
        out = jnp.where(r == 2 * i + value_row,
                        jnp.broadcast_to(sign * piece[h:h + 1, :], (EXTRA_ROWS, n)), out)
    return out.astype(BF16)


def _store_feature_major(dst_ref, h, data, extra):
    if h % 2 == 0:
        dst_ref[h, 0:HEAD_DIM, :] = data
        dst_ref[h, HEAD_DIM:HEAD_DIM + EXTRA_ROWS, :] = extra
    else:
        dst_ref[h, 0:EXTRA_ROWS, :] = extra
        dst_ref[h, HEAD_DIM:2 * HEAD_DIM, :] = data


def _conv_branch(proj, u_sc, conv_w, conv_ref, w_pa, n):
    first = SUBLANES - (CONV_W - 1)
    u_sc[SUBLANES:SUBLANES + n, :] = proj(_S_GC, CONV_DIM) * proj(_S_XA, CONV_DIM)
    conv = sum(u_sc[first + i:first + i + n, :] * conv_w[i:i + 1, :] for i in range(CONV_W))
    new_conv = u_sc[first + n:SUBLANES + n, :]
    conv_ref[...] = new_conv
    u_sc[first:SUBLANES, :] = new_conv
    y_a = proj(_S_GB, CONV_DIM) * conv * jax.nn.silu(proj(_S_ZA, CONV_DIM))
    return jax.nn.sigmoid(proj(_S_GA, D_MODEL)) * jnp.dot(
        y_a.astype(BF16), w_pa[...], preferred_element_type=F32)


def _merge_and_norm(att, mix_a, h, silu_zb, gate_b, w_pb, w_out, ln_g, ln_b):
    y_b = att * silu_zb
    mixed = mix_a + gate_b * jnp.dot(
        y_b.astype(BF16), w_pb[...], preferred_element_type=F32)
    sub = jnp.dot(mixed.astype(BF16), w_out[...], preferred_element_type=F32)
    return _layer_norm(ALPHA * h + sub, ln_g[...], ln_b[...])


def _attention_t(j_lo, j_hi, j, tm, kaug_sc, qta_sc, vt_sc, attt_sc):
    n_key_tiles = j_hi + 1
    keys = n_key_tiles * tm
    key_in_tile = lax.broadcasted_iota(jnp.int32, (tm, tm), 0)
    query_in_tile = lax.broadcasted_iota(jnp.int32, (tm, tm), 1)
    tile_of_query = j if j_hi > j_lo else j_lo
    visible = {i: key_in_tile + (i - tile_of_query) * tm <= query_in_tile for i in range(j_lo, n_key_tiles)}

    def scores(hd):
        return jnp.dot(kaug_sc[0:keys, hd * HEAD_BLOCK:(hd + 1) * HEAD_BLOCK], qta_sc[hd],
                       preferred_element_type=F32)

    pending = [scores(hd) for hd in range(SCORE_LOOKAHEAD)]
    for hd in range(N_HEADS):
        s = pending.pop(0)
        if hd + SCORE_LOOKAHEAD < N_HEADS:
            pending.append(scores(hd + SCORE_LOOKAHEAD))
        blocks = [s[i * tm:(i + 1) * tm, :] for i in range(n_key_tiles)]
        for i, mask in visible.items():
            blocks[i] = jnp.where(mask, blocks[i], -jnp.inf)
        m = functools.reduce(jnp.maximum, [jnp.max(blk, axis=0, keepdims=True) for blk in blocks])
        o = jnp.zeros((V_ROWS, tm), F32)
        for i, blk in enumerate(blocks):
            o = o + jnp.dot(vt_sc[i, hd * V_ROWS:(hd + 1) * V_ROWS, :], jnp.exp2(blk - m).astype(BF16),
                            preferred_element_type=F32)
        attt_sc[hd * HEAD_DIM:(hd + 1) * HEAD_DIM, :] = o[0:HEAD_DIM, :] / o[HEAD_DIM:HEAD_DIM + 1, :]


def _prompt_kernel(x_ref, ln_in_g, ln_in_b, w_std, w_t, b_f_col, conv_w, w_pa, w_pb, w_out,
                   ln_g, ln_b, route, tri,
                   y_ref, kt_ref, vt_ref, lft_ref, conv_ref,
                   kaug_sc, vt_sc, h_sc, hb_sc, z_sc, qta_sc, attt_sc, zb_sc, gm_sc, u_sc, carryt_sc,
                   *, tm, n_tiles, n_total):
    t = pl.program_id(0)

    def norm_in():
        h = _layer_norm(x_ref[...], ln_in_g[...], ln_in_b[...])
        h_sc[lax.rem(t, PIPE_SLOTS)] = h
        hb_sc[lax.rem(t, PIPE_SLOTS)] = h.astype(BF16)

    def norm_out():
        y_ref[...] = _layer_norm(z_sc[...], ln_g[...], ln_b[...])

    @pl.when(t == 0)
    def _fill():
        for ref in (z_sc, kaug_sc, vt_sc, h_sc, hb_sc, attt_sc, zb_sc, gm_sc, u_sc, qta_sc, carryt_sc):
            ref[...] = jnp.zeros_like(ref)
        ones_row = jnp.where(lax.broadcasted_iota(jnp.int32, (V_ROWS - HEAD_DIM, tm), 0) == 0, 1.0, 0.0)
        for i in range(n_tiles):
            for hd in range(N_HEADS):
                vt_sc[i, hd * V_ROWS + HEAD_DIM:(hd + 1) * V_ROWS, :] = ones_row.astype(BF16)
        norm_in()

    @pl.when((t >= 1) & (t <= n_total + 1))
    def _mixers():
        tile_a = t - 1
        tile_b = t - 2
        ja = lax.rem(tile_a, n_tiles)
        jb = lax.rem(tile_b, n_tiles)
        slot_a = lax.rem(tile_a, PIPE_SLOTS)
        slot_b = lax.rem(t + 1, PIPE_SLOTS)
        gate_a = tile_a & 1
        gate_b = t & 1

        @pl.when(ja == 0)
        def _start_of_sequence():
            carryt_sc[...] = jnp.zeros_like(carryt_sc)

        @pl.when(jb == 0)
        def _start_of_conv():
            u_sc[0:SUBLANES, :] = jnp.zeros((SUBLANES, CONV_DIM), F32)

        spans = [(jj, jj) for jj in range(n_tiles - TAIL_SHARE)] + [(n_tiles - TAIL_SHARE, n_tiles - 1)]
        for j_lo, j_hi in spans:
            pl.when((t >= 2) & (jb >= j_lo) & (jb <= j_hi))(functools.partial(
                _attention_t, j_lo, j_hi, jb, tm, kaug_sc, qta_sc, vt_sc, attt_sc))

        def proj(slot, col, width):
            return jnp.dot(hb_sc[slot], w_std[:, col:col + width], preferred_element_type=F32)

        def proj_t(row, height):
            return lax.dot_general(w_t[row:row + height, :], hb_sc[slot_a], (((1,), (1,)), ((), ())),
                                   preferred_element_type=F32)

        norm_out()

        vf_t = proj_t(_T_V, ATT_DIM + F_ROWS)
        qk_t = proj_t(_T_Q, 2 * ATT_DIM)
        kt = qk_t[ATT_DIM:2 * ATT_DIM, :]
        vt = vf_t[0:ATT_DIM, :]
        kt_ref[...] = kt
        vt_ref[...] = vt
        logf_t = _log_sigmoid(vf_t[ATT_DIM:ATT_DIM + F_ROWS, :] + b_f_col[...])
        lft_ref[...] = logf_t[:N_HEADS, :]
        sums = jnp.dot(jnp.concatenate(_split3(logf_t), axis=0), tri[...], preferred_element_type=F32)

        mix_a = _conv_branch(functools.partial(proj, slot_b), u_sc, conv_w, conv_ref, w_pa, tm)

        zb_sc[gate_a] = jax.nn.silu(proj(slot_a, _S_ZB, ATT_DIM))
        gm_sc[gate_a] = jax.nn.sigmoid(proj(slot_a, _S_GM, D_MODEL))

        y_b = attt_sc[...].T * zb_sc[gate_b]
        mixed = mix_a + gm_sc[gate_b] * jnp.dot(y_b.astype(BF16), w_pb[...], preferred_element_type=F32)

        c_t = sum(sums[r * F_ROWS:(r + 1) * F_ROWS, :] for r in range(N_SPLIT)) + carryt_sc[:, 0:1]
        carryt_sc[...] = jnp.broadcast_to(c_t[:, tm - 1:tm], carryt_sc.shape)
        c_pieces = [piece.astype(F32) for piece in _split3(c_t * LOG2E)]
        qt = qk_t[0:ATT_DIM, :] * (SCALE * LOG2E)
        for hd in range(N_HEADS):
            _store_feature_major(qta_sc, hd, qt[hd * HEAD_DIM:(hd + 1) * HEAD_DIM, :].astype(BF16),
                                 _extra_rows(c_pieces, hd, tm, key_side=False))
            vt_sc[ja, hd * V_ROWS:hd * V_ROWS + HEAD_DIM, :] = (
                vt[hd * HEAD_DIM:(hd + 1) * HEAD_DIM, :].astype(BF16))
        ones_and_pad = jnp.where(lax.broadcasted_iota(jnp.int32, (LANES - ONES_LANE, tm), 0) == 0, 1.0, 0.0)
        packed_t = jnp.concatenate([piece[0:N_HEADS, :] for piece in c_pieces] + [ones_and_pad], axis=0)
        ex_k = jnp.dot(packed_t.T.astype(BF16), route[:, AUG_DIM:], preferred_element_type=F32)
        _store_token_major(kaug_sc, pl.ds(pl.multiple_of(ja * tm, tm), tm), kt.T, ex_k)

        z_sc[...] = ALPHA * h_sc[slot_b] + jnp.dot(mixed.astype(BF16), w_out[...], preferred_element_type=F32)

        norm_in()

    @pl.when(t == n_total + 2)
    def _drain():
        norm_out()


def _resident(shape):
    return pl.BlockSpec(shape, lambda *_: (0,) * len(shape), pipeline_mode=pl.Buffered(1))


def _run_prompt(x, params, *, tm):
    bsz, t, _ = x.shape
    n_tiles = t // tm
    assert n_tiles * tm == t
    n_total = bsz * n_tiles
    tile = lambda s, lag: jnp.clip(s - lag, 0, n_total - 1)
    per_tile = lambda width, lag: pl.BlockSpec(
        (None, tm, width), lambda s: (tile(s, lag) // n_tiles, tile(s, lag) % n_tiles, 0))
    per_tile_t = lambda rows: pl.BlockSpec(
        (None, rows, tm), lambda s: (tile(s, 1) // n_tiles, 0, tile(s, 1) % n_tiles))
    out_shape = (jax.ShapeDtypeStruct((bsz, t, D_MODEL), F32),
                 jax.ShapeDtypeStruct((bsz, ATT_DIM, t), F32),
                 jax.ShapeDtypeStruct((bsz, ATT_DIM, t), F32),
                 jax.ShapeDtypeStruct((bsz, N_HEADS, t), F32),
                 jax.ShapeDtypeStruct((bsz, CONV_W - 1, CONV_DIM), F32))
    out_specs = (per_tile(D_MODEL, 3), per_tile_t(ATT_DIM), per_tile_t(ATT_DIM), per_tile_t(N_HEADS),
                 pl.BlockSpec((None, CONV_W - 1, CONV_DIM), lambda s: (tile(s, 2) // n_tiles, 0, 0)))
    scratch = [
        pltpu.VMEM((t, AUG_DIM), BF16),
        pltpu.VMEM((n_tiles, N_HEADS * V_ROWS, tm), BF16),
        pltpu.VMEM((PIPE_SLOTS, tm, D_MODEL), F32),
        pltpu.VMEM((PIPE_SLOTS, tm, D_MODEL), BF16),
        pltpu.VMEM((tm, D_MODEL), F32),
        pltpu.VMEM((N_HEADS, HEAD_BLOCK, tm), BF16),
        pltpu.VMEM((ATT_DIM, tm), F32),
        pltpu.VMEM((2, tm, ATT_DIM), F32),
        pltpu.VMEM((2, tm, D_MODEL), F32),
        pltpu.VMEM((SUBLANES + tm, CONV_DIM), F32),
        pltpu.VMEM((F_ROWS, LANES), F32),
    ]
    return pl.pallas_call(
        functools.partial(_prompt_kernel, tm=tm, n_tiles=n_tiles, n_total=n_total),
        grid=(n_total + PIPE_SLOTS,),
        in_specs=[per_tile(D_MODEL, 0)] + [_resident(p.shape) for p in params],
        out_specs=out_specs,
        out_shape=out_shape,
        scratch_shapes=scratch,
        compiler_params=pltpu.CompilerParams(
            dimension_semantics=("arbitrary",),
            vmem_limit_bytes=VMEM_LIMIT_BYTES),
        name="trunk_prompt",
    )(x, *params)


def _sample_kernel(x_ref, conv0_ref, pkt_ref, pvt_ref, plft_ref,
                   ln_in_g, ln_in_b, w_std, w_fs, b_f_row, w_t, b_f_col, conv_w, w_pa, w_pb, w_out,
                   ln_g, ln_b, route, ones,
                   y_ref, kt_ref, vt_ref, lft_ref, conv_ref,
                   kta_sc, ktn_sc, hb_sc, qa_sc, att_sc, u_sc, *, ts, n_past):
    @pl.when(pl.program_id(0) == 0)
    def _zero_padding():
        kta_sc[...] = jnp.zeros_like(kta_sc)
        ktn_sc[...] = jnp.zeros_like(ktn_sc)
        hb_sc[...] = jnp.zeros_like(hb_sc)

    h = _layer_norm(x_ref[...], ln_in_g[...], ln_in_b[...])
    hb_sc[0:ts, :] = h.astype(BF16)

    def proj(col, width):
        return jnp.dot(hb_sc[0:ts, :], w_std[:, col:col + width], preferred_element_type=F32)

    u_sc[0:SUBLANES, :] = jnp.zeros((SUBLANES, CONV_DIM), F32)
    u_sc[SUBLANES - (CONV_W - 1):SUBLANES, :] = conv0_ref[...]
    mix_a = _conv_branch(proj, u_sc, conv_w, conv_ref, w_pa, ts)

    def proj_t(row, height):
        return lax.dot_general(w_t[row:row + height, :], hb_sc[...], (((1,), (1,)), ((), ())),
                               preferred_element_type=F32)

    kt = proj_t(_T_K, ATT_DIM)
    vf_t = proj_t(_T_V, ATT_DIM + F_ROWS)
    vt = vf_t[0:ATT_DIM, :]
    kt_ref[...] = kt[:, :ts]
    vt_ref[...] = vt[:, :ts]
    logf_t = _log_sigmoid(vf_t[ATT_DIM:ATT_DIM + SUBLANES, :] + b_f_col[0:SUBLANES, :])
    lft_ref[...] = logf_t[:, :ts]

    c_past = _cumsum(plft_ref[...], 1)
    c_last = c_past[:, n_past - 1:n_past]
    c_new_t = _cumsum(logf_t, 1) + c_last
    past_pieces = [piece.astype(F32) for piece in _split3(c_past)]
    new_pieces = [piece.astype(F32) for piece in _split3(c_new_t)]
    for hd in range(N_HEADS):
        rows = slice(hd * HEAD_DIM, (hd + 1) * HEAD_DIM)
        _store_feature_major(kta_sc, hd, pkt_ref[rows, :].astype(BF16),
                             _extra_rows(past_pieces, hd, n_past, key_side=True))
        _store_feature_major(ktn_sc, hd, kt[rows, :].astype(BF16),
                             _extra_rows(new_pieces, hd, LANES, key_side=True))

    eye = (lax.broadcasted_iota(jnp.int32, (SUBLANES, LANES), 0)
           == lax.broadcasted_iota(jnp.int32, (SUBLANES, LANES), 1))
    c_last_row = jnp.sum(jnp.where(eye, jnp.broadcast_to(c_last, (SUBLANES, LANES)), 0.0),
                         axis=0, keepdims=True)
    logf = _log_sigmoid(jnp.dot(hb_sc[0:ts, :], w_fs[...], preferred_element_type=F32) + b_f_row[...])
    c_q = _cumsum(logf, 0) + c_last_row
    ex_q = _extra_lanes(c_q, route[:, :AUG_DIM]) + ones[:, :AUG_DIM]
    _store_token_major(qa_sc, slice(None), proj(_S_Q, ATT_DIM) * SCALE, ex_q)

    causal = (lax.broadcasted_iota(jnp.int32, (ts, LANES), 1)
              <= lax.broadcasted_iota(jnp.int32, (ts, LANES), 0))
    lane = lax.broadcasted_iota(jnp.int32, (ts, HEAD_BLOCK), 1)
    contract_lanes = (((1,), (1,)), ((), ()))
    scores = []
    for hd in range(N_HEADS):
        qa_h = qa_sc[:, hd * HEAD_BLOCK:(hd + 1) * HEAD_BLOCK]
        scores.append((jnp.dot(qa_h, kta_sc[hd], preferred_element_type=F32),
                       jnp.dot(qa_h, ktn_sc[hd], preferred_element_type=F32)))
    for pair in range(N_HEADS // 2):
        rows = slice(pair * HEAD_BLOCK, (pair + 1) * HEAD_BLOCK)
        vt_past = pvt_ref[rows, :].astype(BF16)
        vt_new = vt[rows, :].astype(BF16)
        outs = []
        for hd in (2 * pair, 2 * pair + 1):
            s_past = scores[hd][0]
            s_new = jnp.where(causal, scores[hd][1], -jnp.inf)
            m = jnp.maximum(jnp.max(s_past, axis=-1, keepdims=True), jnp.max(s_new, axis=-1, keepdims=True))
            p_past = jnp.exp(s_past - m)
            p_new = jnp.exp(s_new - m)
            l = jnp.sum(p_past, axis=-1, keepdims=True) + jnp.sum(p_new, axis=-1, keepdims=True)
            o = (lax.dot_general(p_past.astype(BF16), vt_past, contract_lanes, preferred_element_type=F32)
                 + lax.dot_general(p_new.astype(BF16), vt_new, contract_lanes, preferred_element_type=F32))
            outs.append(o / l)
        att_sc[:, rows] = jnp.where(lane < HEAD_DIM, outs[0], outs[1])

    y_ref[...] = _merge_and_norm(att_sc[...], mix_a, h, jax.nn.silu(proj(_S_ZB, ATT_DIM)),
                                 jax.nn.sigmoid(proj(_S_GM, D_MODEL)), w_pb, w_out, ln_g, ln_b)


def _run_sample(x, conv0, pkt, pvt, plft, params):
    bsz, ts, _ = x.shape
    n_past = pkt.shape[2]
    assert ts <= LANES and ts % SUBLANES == 0
    per_batch = lambda rows, width: pl.BlockSpec((None, rows, width), lambda b: (b, 0, 0))
    out_shape = (jax.ShapeDtypeStruct((bsz, ts, D_MODEL), F32),
                 jax.ShapeDtypeStruct((bsz, ATT_DIM, ts), F32),
                 jax.ShapeDtypeStruct((bsz, ATT_DIM, ts), F32),
                 jax.ShapeDtypeStruct((bsz, N_HEADS, ts), F32),
                 jax.ShapeDtypeStruct((bsz, CONV_W - 1, CONV_DIM), F32))
    out_specs = (per_batch(ts, D_MODEL), per_batch(ATT_DIM, ts), per_batch(ATT_DIM, ts),
                 per_batch(N_HEADS, ts), per_batch(CONV_W - 1, CONV_DIM))
    in_specs = [per_batch(ts, D_MODEL), per_batch(CONV_W - 1, CONV_DIM), per_batch(ATT_DIM, n_past),
                per_batch(ATT_DIM, n_past), per_batch(N_HEADS, n_past)]
    scratch = [
        pltpu.VMEM((N_HEADS, HEAD_BLOCK, n_past), BF16),
        pltpu.VMEM((N_HEADS, HEAD_BLOCK, LANES), BF16),
        pltpu.VMEM((LANES, D_MODEL), BF16),
        pltpu.VMEM((ts, AUG_DIM), BF16),
        pltpu.VMEM((ts, ATT_DIM), F32),
        pltpu.VMEM((SUBLANES + ts, CONV_DIM), F32),
    ]
    return pl.pallas_call(
        functools.partial(_sample_kernel, ts=ts, n_past=n_past),
        grid=(bsz,),
        in_specs=in_specs + [_resident(p.shape) for p in params],
        out_specs=out_specs,
        out_shape=out_shape,
        scratch_shapes=scratch,
        compiler_params=pltpu.CompilerParams(
            dimension_semantics=("arbitrary",),
            vmem_limit_bytes=VMEM_LIMIT_BYTES),
        name="trunk_sample",
    )(x, conv0, pkt, pvt, plft, *params)


def kernel(x_prompt, x_sample, cache_k, cache_v, cache_logf, state_conv, ln_in_g, ln_in_b, w_in, b_f,
           conv_w, w_proj_a, w_proj_b, w_out, ln_g, ln_b):
    assert w_in.shape[0] == DEPTH
    route_np, ones_np = _routing_constants()
    w = w_in[0]
    row = lambda a: a.reshape(1, -1).astype(F32)
    w_t = jnp.concatenate([w[:, _W_Q:_W_F + N_HEADS],
                           jnp.zeros((D_MODEL, F_ROWS - N_HEADS), w.dtype)], axis=1).T.astype(BF16)
    tm = PROMPT_TILE
    p = dict(
        ln_in_g=row(ln_in_g), ln_in_b=row(ln_in_b),
        w_std=jnp.concatenate([w[:, :_W_V], w[:, _W_ZB:]], axis=1).astype(BF16),
        w_fs=jnp.pad(w[:, _W_F:_W_F + N_HEADS], ((0, 0), (0, LANES - N_HEADS))).astype(BF16),
        b_f_row=jnp.pad(row(b_f[0]), ((0, 0), (0, LANES - N_HEADS))),
        w_t=w_t,
        b_f_col=jnp.pad(b_f[0].astype(F32).reshape(N_HEADS, 1), ((0, F_ROWS - N_HEADS), (0, 0))),
        conv_w=conv_w[0].astype(F32),
        w_pa=w_proj_a[0].astype(BF16), w_pb=w_proj_b[0].astype(BF16), w_out=w_out[0].astype(BF16),
        ln_g=row(ln_g[0]), ln_b=row(ln_b[0]),
        route=jnp.asarray(route_np, BF16), ones=jnp.asarray(ones_np, F32),
        tri=jnp.asarray(np.triu(np.ones((tm, tm), np.float32)), BF16),
    )
    bs = x_sample.shape[0]
    n_past = cache_k.shape[2]
    feature_major = lambda a: a.transpose(0, 2, 3, 1).reshape(bs, ATT_DIM, n_past)

    y_p, kt_p, vt_p, lft_p, cv_p = _run_prompt(x_prompt, [p[name] for name in PROMPT_PARAMS], tm=tm)
    y_s, kt_s, vt_s, lft_s, cv_s = _run_sample(
        x_sample, state_conv[0], feature_major(cache_k[0]), feature_major(cache_v[0]),
        cache_logf[0].transpose(0, 2, 1), [p[name] for name in SAMPLE_PARAMS])

    heads_t = lambda a: a.reshape(a.shape[0], N_HEADS, HEAD_DIM, a.shape[2]).transpose(0, 3, 1, 2)[None]
    tokens_t = lambda a: a.transpose(0, 2, 1)[None]
    return (y_p, y_s, heads_t(kt_p), heads_t(vt_p), tokens_t(lft_p), cv_p[None],
            heads_t(kt_s), heads_t(vt_s), tokens_t(lft_s), cv_s[None])
```

```python
import functools

import numpy as np
import jax
import jax.numpy as jnp
from jax import lax
from jax.experimental import pallas as pl
from jax.experimental.pallas import tpu as pltpu

D_MODEL = 1024
CONV_DIM = 512
CONV_W = 3
N_HEADS = 8
HEAD_DIM = 64
ATT_DIM = N_HEADS * HEAD_DIM
LN_EPS = 1e-5
DEPTH = 1
ALPHA = (2.0 * DEPTH) ** 0.25
SCALE = HEAD_DIM ** -0.5

LANES = 128
SUBLANES = 8
HEAD_BLOCK = LANES
AUG_DIM = N_HEADS * HEAD_BLOCK
N_SPLIT = 3
ONES_LANE = N_SPLIT * N_HEADS
EXTRA_ROWS = 16
F_ROWS = 16
VMEM_LIMIT_BYTES = 56 * 1024 * 1024
V_ROWS = HEAD_DIM + 16
LOG2E = 1.4426950408889634
PIPE_SLOTS = 3
TAIL_SHARE = 1
SCORE_LOOKAHEAD = 4

_W_Q = 4 * CONV_DIM
_W_V = _W_Q + 2 * ATT_DIM
_W_F = _W_Q + 3 * ATT_DIM
_W_ZB = _W_F + N_HEADS
_S_XA, _S_GB, _S_GC, _S_ZA, _S_Q, _S_K, _S_ZB, _S_GA, _S_GM = (
    0, 512, 1024, 1536, 2048, 2560, 3072, 3584, 4608)
_T_Q, _T_K, _T_V, _T_F = 0, 512, 1024, 1536

PROMPT_TILE = 256
PROMPT_PARAMS = ("ln_in_g", "ln_in_b", "w_lo", "w_hi", "w_t", "b_f_col", "conv_w", "w_pa", "w_pb", "w_out",
                 "ln_g", "ln_b", "route", "tri")
SAMPLE_PARAMS = ("ln_in_g", "ln_in_b", "w_lo", "w_hi", "w_fs", "b_f_row", "w_t", "b_f_col", "conv_w", "w_pa",
                 "w_pb", "w_out", "ln_g", "ln_b", "route", "ones")

F32 = jnp.float32
BF16 = jnp.bfloat16


def _routing_constants():
    route = np.zeros((LANES, 2 * AUG_DIM), np.float32)
    ones = np.zeros((1, 2 * AUG_DIM), np.float32)
    for h in range(N_HEADS):
        base = h * HEAD_BLOCK + (HEAD_DIM if h % 2 == 0 else 0)
        for r in range(N_SPLIT):
            route[r * N_HEADS + h, base + 2 * r] = 1.0
            ones[0, base + 2 * r + 1] = 1.0
            ones[0, AUG_DIM + base + 2 * r] = 1.0
            route[r * N_HEADS + h, AUG_DIM + base + 2 * r + 1] = -1.0
    route[ONES_LANE, :] = ones[0, :]
    return route, ones


def _layer_norm(x, g, b):
    mu = jnp.mean(x, axis=-1, keepdims=True)
    xc = x - mu
    var = jnp.mean(xc * xc, axis=-1, keepdims=True)
    return xc * lax.rsqrt(var + LN_EPS) * g + b


def _log_sigmoid(x):
    return jnp.minimum(x, 0.0) - jnp.log1p(jnp.exp(-jnp.abs(x)))


def _split3(x):
    hi = x.astype(BF16)
    r1 = x - hi.astype(F32)
    mid = r1.astype(BF16)
    lo = (r1 - mid.astype(F32)).astype(BF16)
    return hi, mid, lo


def _cumsum(x, axis):
    n = x.shape[axis]
    idx = lax.broadcasted_iota(jnp.int32, x.shape, axis)
    s = 1
    while s < n:
        x = x + jnp.where(idx >= s, pltpu.roll(x, s, axis), 0.0)
        s *= 2
    return x


def _extra_lanes(c, route):
    lane = lax.broadcasted_iota(jnp.int32, c.shape, 1)
    valid = lane < N_HEADS
    packed = jnp.zeros_like(c)
    for r, piece in enumerate(_split3(c)):
        p32 = jnp.where(valid, piece.astype(F32), 0.0)
        packed = packed + (pltpu.roll(p32, r * N_HEADS, 1) if r else p32)
    return jnp.dot(packed.astype(BF16), route, preferred_element_type=F32)


def _store_token_major(dst_ref, rows, data, extras):
    n = data.shape[0]
    lane = lax.broadcasted_iota(jnp.int32, (n, HEAD_BLOCK), 1)
    for h in range(N_HEADS):
        pair = data[:, (h // 2) * HEAD_BLOCK:(h // 2 + 1) * HEAD_BLOCK]
        ex = extras[:, h * HEAD_BLOCK:(h + 1) * HEAD_BLOCK]
        keep = (lane < HEAD_DIM) if h % 2 == 0 else (lane >= HEAD_DIM)
        dst_ref[rows, h * HEAD_BLOCK:(h + 1) * HEAD_BLOCK] = jnp.where(keep, pair, ex).astype(BF16)


def _extra_rows(pieces, h, n, key_side):
    r = lax.broadcasted_iota(jnp.int32, (EXTRA_ROWS, n), 0)
    value_row, sign = (1, -1.0) if key_side else (0, 1.0)
    out = jnp.where((r < 2 * N_SPLIT) & ((r & 1) != value_row), 1.0, 0.0)
    for i, piece in enumerate(pieces):
        out = jnp.where(r == 2 * i + value_row,
                        jnp.broadcast_to(sign * piece[h:h + 1, :], (EXTRA_ROWS, n)), out)
    return out.astype(BF16)


def _store_feature_major(dst_ref, h, data, extra):
    if h % 2 == 0:
        dst_ref[h, 0:HEAD_DIM, :] = data
        dst_ref[h, HEAD_DIM:HEAD_DIM + EXTRA_ROWS, :] = extra
    else:
        dst_ref[h, 0:EXTRA_ROWS, :] = extra
        dst_ref[h, HEAD_DIM:2 * HEAD_DIM, :] = data


def _std_columns(w_lo, w_hi, col, width):
    split = w_lo.shape[1]
    assert col + width <= split or col >= split
    return w_lo[:, col:col + width] if col < split else w_hi[:, col - split:col - split + width]


def _conv_branch(proj, u_sc, conv_w, conv_ref, w_pa, n):
    first = SUBLANES - (CONV_W - 1)
    u_sc[SUBLANES:SUBLANES + n, :] = proj(_S_GC, CONV_DIM) * proj(_S_XA, CONV_DIM)
    conv = sum(u_sc[first + i:first + i + n, :] * conv_w[i:i + 1, :] for i in range(CONV_W))
    new_conv = u_sc[first + n:SUBLANES + n, :]
    conv_ref[...] = new_conv
    u_sc[first:SUBLANES, :] = new_conv
    y_a = proj(_S_GB, CONV_DIM) * conv * jax.nn.silu(proj(_S_ZA, CONV_DIM))
    return jax.nn.sigmoid(proj(_S_GA, D_MODEL)) * jnp.dot(
        y_a.astype(BF16), w_pa[...], preferred_element_type=F32)


def _merge_and_norm(att, mix_a, h, silu_zb, gate_b, w_pb, w_out, ln_g, ln_b):
    y_b = att * silu_zb
    mixed = mix_a + gate_b * jnp.dot(
        y_b.astype(BF16), w_pb[...], preferred_element_type=F32)
    sub = jnp.dot(mixed.astype(BF16), w_out[...], preferred_element_type=F32)
    return _layer_norm(ALPHA * h + sub, ln_g[...], ln_b[...])


def _attention_t(j_lo, j_hi, j, tm, kaug_sc, qta_sc, vt_sc, attt_sc):
    n_key_tiles = j_hi + 1
    keys = n_key_tiles * tm
    key_in_tile = lax.broadcasted_iota(jnp.int32, (tm, tm), 0)
    query_in_tile = lax.broadcasted_iota(jnp.int32, (tm, tm), 1)
    tile_of_query = j if j_hi > j_lo else j_lo
    visible = {i: key_in_tile + (i - tile_of_query) * tm <= query_in_tile for i in range(j_lo, n_key_tiles)}

    def scores(hd):
        return jnp.dot(kaug_sc[0:keys, hd * HEAD_BLOCK:(hd + 1) * HEAD_BLOCK], qta_sc[hd],
                       preferred_element_type=F32)

    pending = [scores(hd) for hd in range(SCORE_LOOKAHEAD)]
    for hd in range(N_HEADS):
        s = pending.pop(0)
        if hd + SCORE_LOOKAHEAD < N_HEADS:
            pending.append(scores(hd + SCORE_LOOKAHEAD))
        blocks = [s[i * tm:(i + 1) * tm, :] for i in range(n_key_tiles)]
        for i, mask in visible.items():
            blocks[i] = jnp.where(mask, blocks[i], -jnp.inf)
        m = functools.reduce(jnp.maximum, [jnp.max(blk, axis=0, keepdims=True) for blk in blocks])
        o = jnp.zeros((V_ROWS, tm), F32)
        for i, blk in enumerate(blocks):
            o = o + jnp.dot(vt_sc[i, hd * V_ROWS:(hd + 1) * V_ROWS, :], jnp.exp2(blk - m).astype(BF16),
                            preferred_element_type=F32)
        attt_sc[hd * HEAD_DIM:(hd + 1) * HEAD_DIM, :] = o[0:HEAD_DIM, :] / o[HEAD_DIM:HEAD_DIM + 1, :]


def _prompt_kernel(x_ref, ln_in_g, ln_in_b, w_lo, w_hi, w_t, b_f_col, conv_w, w_pa, w_pb, w_out,
                   ln_g, ln_b, route, tri,
                   y_ref, kt_ref, vt_ref, lft_ref, conv_ref,
                   kaug_sc, vt_sc, h_sc, hb_sc, z_sc, qta_sc, attt_sc, zb_sc, gm_sc, u_sc, carryt_sc,
                   *, tm, n_tiles, n_total):
    t = pl.program_id(0)

    def norm_in():
        h = _layer_norm(x_ref[...], ln_in_g[...], ln_in_b[...])
        h_sc[lax.rem(t, PIPE_SLOTS)] = h
        hb_sc[lax.rem(t, PIPE_SLOTS)] = h.astype(BF16)

    def norm_out():
        y_ref[...] = _layer_norm(z_sc[...], ln_g[...], ln_b[...])

    @pl.when(t == 0)
    def _fill():
        for ref in (z_sc, kaug_sc, vt_sc, h_sc, hb_sc, attt_sc, zb_sc, gm_sc, u_sc, qta_sc, carryt_sc):
            ref[...] = jnp.zeros_like(ref)
        ones_row = jnp.where(lax.broadcasted_iota(jnp.int32, (V_ROWS - HEAD_DIM, tm), 0) == 0, 1.0, 0.0)
        for i in range(n_tiles):
            for hd in range(N_HEADS):
                vt_sc[i, hd * V_ROWS + HEAD_DIM:(hd + 1) * V_ROWS, :] = ones_row.astype(BF16)
        norm_in()

    @pl.when((t >= 1) & (t <= n_total + 1))
    def _mixers():
        tile_a = t - 1
        tile_b = t - 2
        ja = lax.rem(tile_a, n_tiles)
        jb = lax.rem(tile_b, n_tiles)
        slot_a = lax.rem(tile_a, PIPE_SLOTS)
        slot_b = lax.rem(t + 1, PIPE_SLOTS)
        gate_a = tile_a & 1
        gate_b = t & 1

        @pl.when(ja == 0)
        def _start_of_sequence():
            carryt_sc[...] = jnp.zeros_like(carryt_sc)

        @pl.when(jb == 0)
        def _start_of_conv():
            u_sc[0:SUBLANES, :] = jnp.zeros((SUBLANES, CONV_DIM), F32)

        spans = [(jj, jj) for jj in range(n_tiles - TAIL_SHARE)] + [(n_tiles - TAIL_SHARE, n_tiles - 1)]
        for j_lo, j_hi in spans:
            pl.when((t >= 2) & (jb >= j_lo) & (jb <= j_hi))(functools.partial(
                _attention_t, j_lo, j_hi, jb, tm, kaug_sc, qta_sc, vt_sc, attt_sc))

        def proj(slot, col, width):
            return jnp.dot(hb_sc[slot], _std_columns(w_lo, w_hi, col, width), preferred_element_type=F32)

        def proj_t(row, height):
            return lax.dot_general(w_t[row:row + height, :], hb_sc[slot_a], (((1,), (1,)), ((), ())),
                                   preferred_element_type=F32)

        norm_out()

        vf_t = proj_t(_T_V, ATT_DIM + F_ROWS)
        qk_t = proj_t(_T_Q, 2 * ATT_DIM)
        kt = qk_t[ATT_DIM:2 * ATT_DIM, :]
        vt = vf_t[0:ATT_DIM, :]
        kt_ref[...] = kt
        vt_ref[...] = vt
        logf_t = _log_sigmoid(vf_t[ATT_DIM:ATT_DIM + F_ROWS, :] + b_f_col[...])
        lft_ref[...] = logf_t[:N_HEADS, :]
        sums = jnp.dot(jnp.concatenate(_split3(logf_t), axis=0), tri[...], preferred_element_type=F32)

        mix_a = _conv_branch(functools.partial(proj, slot_b), u_sc, conv_w, conv_ref, w_pa, tm)

        zb_sc[gate_a] = jax.nn.silu(proj(slot_a, _S_ZB, ATT_DIM))
        gm_sc[gate_a] = jax.nn.sigmoid(proj(slot_a, _S_GM, D_MODEL))

        y_b = attt_sc[...].T * zb_sc[gate_b]
        mixed = mix_a + gm_sc[gate_b] * jnp.dot(y_b.astype(BF16), w_pb[...], preferred_element_type=F32)

        c_t = sum(sums[r * F_ROWS:(r + 1) * F_ROWS, :] for r in range(N_SPLIT)) + carryt_sc[:, 0:1]
        carryt_sc[...] = jnp.broadcast_to(c_t[:, tm - 1:tm], carryt_sc.shape)
        c_pieces = [piece.astype(F32) for piece in _split3(c_t * LOG2E)]
        qt = qk_t[0:ATT_DIM, :] * (SCALE * LOG2E)
        for hd in range(N_HEADS):
            _store_feature_major(qta_sc, hd, qt[hd * HEAD_DIM:(hd + 1) * HEAD_DIM, :].astype(BF16),
                                 _extra_rows(c_pieces, hd, tm, key_side=False))
            vt_sc[ja, hd * V_ROWS:hd * V_ROWS + HEAD_DIM, :] = (
                vt[hd * HEAD_DIM:(hd + 1) * HEAD_DIM, :].astype(BF16))
        ones_and_pad = jnp.where(lax.broadcasted_iota(jnp.int32, (LANES - ONES_LANE, tm), 0) == 0, 1.0, 0.0)
        packed_t = jnp.concatenate([piece[0:N_HEADS, :] for piece in c_pieces] + [ones_and_pad], axis=0)
        ex_k = jnp.dot(packed_t.T.astype(BF16), route[:, AUG_DIM:], preferred_element_type=F32)
        _store_token_major(kaug_sc, pl.ds(pl.multiple_of(ja * tm, tm), tm), kt.T, ex_k)

        z_sc[...] = ALPHA * h_sc[slot_b] + jnp.dot(mixed.astype(BF16), w_out[...], preferred_element_type=F32)

        norm_in()

    @pl.when(t == n_total + 2)
    def _drain():
        norm_out()


def _resident(shape):
    return pl.BlockSpec(shape, lambda *_: (0,) * len(shape), pipeline_mode=pl.Buffered(1))


def _run_prompt(x, params, *, tm):
    bsz, t, _ = x.shape
    n_tiles = t // tm
    assert n_tiles * tm == t
    n_total = bsz * n_tiles
    tile = lambda s, lag: jnp.clip(s - lag, 0, n_total - 1)
    per_tile = lambda width, lag: pl.BlockSpec(
        (None, tm, width), lambda s: (tile(s, lag) // n_tiles, tile(s, lag) % n_tiles, 0))
    per_tile_t = lambda rows: pl.BlockSpec(
        (None, rows, tm), lambda s: (tile(s, 1) // n_tiles, 0, tile(s, 1) % n_tiles))
    out_shape = (jax.ShapeDtypeStruct((bsz, t, D_MODEL), F32),
                 jax.ShapeDtypeStruct((bsz, ATT_DIM, t), F32),
                 jax.ShapeDtypeStruct((bsz, ATT_DIM, t), F32),
                 jax.ShapeDtypeStruct((bsz, N_HEADS, t), F32),
                 jax.ShapeDtypeStruct((bsz, CONV_W - 1, CONV_DIM), F32))
    out_specs = (per_tile(D_MODEL, 3), per_tile_t(ATT_DIM), per_tile_t(ATT_DIM), per_tile_t(N_HEADS),
                 pl.BlockSpec((None, CONV_W - 1, CONV_DIM), lambda s: (tile(s, 2) // n_tiles, 0, 0)))
    scratch = [
        pltpu.VMEM((t, AUG_DIM), BF16),
        pltpu.VMEM((n_tiles, N_HEADS * V_ROWS, tm), BF16),
        pltpu.VMEM((PIPE_SLOTS, tm, D_MODEL), F32),
        pltpu.VMEM((PIPE_SLOTS, tm, D_MODEL), BF16),
        pltpu.VMEM((tm, D_MODEL), F32),
        pltpu.VMEM((N_HEADS, HEAD_BLOCK, tm), BF16),
        pltpu.VMEM((ATT_DIM, tm), F32),
        pltpu.VMEM((2, tm, ATT_DIM), F32),
        pltpu.VMEM((2, tm, D_MODEL), F32),
        pltpu.VMEM((SUBLANES + tm, CONV_DIM), F32),
        pltpu.VMEM((F_ROWS, LANES), F32),
    ]
    return pl.pallas_call(
        functools.partial(_prompt_kernel, tm=tm, n_tiles=n_tiles, n_total=n_total),
        grid=(n_total + PIPE_SLOTS,),
        in_specs=[per_tile(D_MODEL, 0)] + [_resident(p.shape) for p in params],
        out_specs=out_specs,
        out_shape=out_shape,
        scratch_shapes=scratch,
        compiler_params=pltpu.CompilerParams(
            dimension_semantics=("arbitrary",),
            vmem_limit_bytes=VMEM_LIMIT_BYTES),
        name="trunk_prompt",
    )(x, *params)


def _sample_kernel(x_ref, conv0_ref, pkt_ref, pvt_ref, plft_ref,
                   ln_in_g, ln_in_b, w_lo, w_hi, w_fs, b_f_row, w_t, b_f_col, conv_w, w_pa, w_pb, w_out,
                   ln_g, ln_b, route, ones,
                   y_ref, kt_ref, vt_ref, lft_ref, conv_ref,
                   kta_sc, ktn_sc, hb_sc, qa_sc, att_sc, u_sc, *, ts, n_past):
    @pl.when(pl.program_id(0) == 0)
    def _zero_padding():
        kta_sc[...] = jnp.zeros_like(kta_sc)
        ktn_sc[...] = jnp.zeros_like(ktn_sc)
        hb_sc[...] = jnp.zeros_like(hb_sc)

    h = _layer_norm(x_ref[...], ln_in_g[...], ln_in_b[...])
    hb_sc[0:ts, :] = h.astype(BF16)

    def proj(col, width):
        return jnp.dot(hb_sc[0:ts, :], _std_columns(w_lo, w_hi, col, width), preferred_element_type=F32)

    u_sc[0:SUBLANES, :] = jnp.zeros((SUBLANES, CONV_DIM), F32)
    u_sc[SUBLANES - (CONV_W - 1):SUBLANES, :] = conv0_ref[...]
    mix_a = _conv_branch(proj, u_sc, conv_w, conv_ref, w_pa, ts)

    def proj_t(row, height):
        return lax.dot_general(w_t[row:row + height, :], hb_sc[...], (((1,), (1,)), ((), ())),
                               preferred_element_type=F32)

    kt = proj_t(_T_K, ATT_DIM)
    vf_t = proj_t(_T_V, ATT_DIM + F_ROWS)
    vt = vf_t[0:ATT_DIM, :]
    kt_ref[...] = kt[:, :ts]
    vt_ref[...] = vt[:, :ts]
    logf_t = _log_sigmoid(vf_t[ATT_DIM:ATT_DIM + SUBLANES, :] + b_f_col[0:SUBLANES, :])
    lft_ref[...] = logf_t[:, :ts]

    c_past = _cumsum(plft_ref[...], 1)
    c_last = c_past[:, n_past - 1:n_past]
    c_new_t = _cumsum(logf_t, 1) + c_last
    past_pieces = [piece.astype(F32) for piece in _split3(c_past)]
    new_pieces = [piece.astype(F32) for piece in _split3(c_new_t)]
    for hd in range(N_HEADS):
        rows = slice(hd * HEAD_DIM, (hd + 1) * HEAD_DIM)
        _store_feature_major(kta_sc, hd, pkt_ref[rows, :].astype(BF16),
                             _extra_rows(past_pieces, hd, n_past, key_side=True))
        _store_feature_major(ktn_sc, hd, kt[rows, :].astype(BF16),
                             _extra_rows(new_pieces, hd, LANES, key_side=True))

    eye = (lax.broadcasted_iota(jnp.int32, (SUBLANES, LANES), 0)
           == lax.broadcasted_iota(jnp.int32, (SUBLANES, LANES), 1))
    c_last_row = jnp.sum(jnp.where(eye, jnp.broadcast_to(c_last, (SUBLANES, LANES)), 0.0),
                         axis=0, keepdims=True)
    logf = _log_sigmoid(jnp.dot(hb_sc[0:ts, :], w_fs[...], preferred_element_type=F32) + b_f_row[...])
    c_q = _cumsum(logf, 0) + c_last_row
    ex_q = _extra_lanes(c_q, route[:, :AUG_DIM]) + ones[:, :AUG_DIM]
    _store_token_major(qa_sc, slice(None), proj(_S_Q, ATT_DIM) * SCALE, ex_q)

    causal = (lax.broadcasted_iota(jnp.int32, (ts, LANES), 1)
              <= lax.broadcasted_iota(jnp.int32, (ts, LANES), 0))
    lane = lax.broadcasted_iota(jnp.int32, (ts, HEAD_BLOCK), 1)
    contract_lanes = (((1,), (1,)), ((), ()))
    scores = []
    for hd in range(N_HEADS):
        qa_h = qa_sc[:, hd * HEAD_BLOCK:(hd + 1) * HEAD_BLOCK]
        scores.append((jnp.dot(qa_h, kta_sc[hd], preferred_element_type=F32),
                       jnp.dot(qa_h, ktn_sc[hd], preferred_element_type=F32)))
    for pair in range(N_HEADS // 2):
        rows = slice(pair * HEAD_BLOCK, (pair + 1) * HEAD_BLOCK)
        vt_past = pvt_ref[rows, :].astype(BF16)
        vt_new = vt[rows, :].astype(BF16)
        outs = []
        for hd in (2 * pair, 2 * pair + 1):
            s_past = scores[hd][0]
            s_new = jnp.where(causal, scores[hd][1], -jnp.inf)
            m = jnp.maximum(jnp.max(s_past, axis=-1, keepdims=True), jnp.max(s_new, axis=-1, keepdims=True))
            p_past = jnp.exp(s_past - m)
            p_new = jnp.exp(s_new - m)
            l = jnp.sum(p_past, axis=-1, keepdims=True) + jnp.sum(p_new, axis=-1, keepdims=True)
            o = (lax.dot_general(p_past.astype(BF16), vt_past, contract_lanes, preferred_element_type=F32)
                 + lax.dot_general(p_new.astype(BF16), vt_new, contract_lanes, preferred_element_type=F32))
            outs.append(o / l)
        att_sc[:, rows] = jnp.where(lane < HEAD_DIM, outs[0], outs[1])

    y_ref[...] = _merge_and_norm(att_sc[...], mix_a, h, jax.nn.silu(proj(_S_ZB, ATT_DIM)),
                                 jax.nn.sigmoid(proj(_S_GM, D_MODEL)), w_pb, w_out, ln_g, ln_b)


def _run_sample(x, conv0, pkt, pvt, plft, params):
    bsz, ts, _ = x.shape
    n_past = pkt.shape[2]
    assert ts <= LANES and ts % SUBLANES == 0
    per_batch = lambda rows, width: pl.BlockSpec((None, rows, width), lambda b: (b, 0, 0))
    out_shape = (jax.ShapeDtypeStruct((bsz, ts, D_MODEL), F32),
                 jax.ShapeDtypeStruct((bsz, ATT_DIM, ts), F32),
                 jax.ShapeDtypeStruct((bsz, ATT_DIM, ts), F32),
                 jax.ShapeDtypeStruct((bsz, N_HEADS, ts), F32),
                 jax.ShapeDtypeStruct((bsz, CONV_W - 1, CONV_DIM), F32))
    out_specs = (per_batch(ts, D_MODEL), per_batch(ATT_DIM, ts), per_batch(ATT_DIM, ts),
                 per_batch(N_HEADS, ts), per_batch(CONV_W - 1, CONV_DIM))
    in_specs = [per_batch(ts, D_MODEL), per_batch(CONV_W - 1, CONV_DIM), per_batch(ATT_DIM, n_past),
                per_batch(ATT_DIM, n_past), per_batch(N_HEADS, n_past)]
    scratch = [
        pltpu.VMEM((N_HEADS, HEAD_BLOCK, n_past), BF16),
        pltpu.VMEM((N_HEADS, HEAD_BLOCK, LANES), BF16),
        pltpu.VMEM((LANES, D_MODEL), BF16),
        pltpu.VMEM((ts, AUG_DIM), BF16),
        pltpu.VMEM((ts, ATT_DIM), F32),
        pltpu.VMEM((SUBLANES + ts, CONV_DIM), F32),
    ]
    return pl.pallas_call(
        functools.partial(_sample_kernel, ts=ts, n_past=n_past),
        grid=(bsz,),
        in_specs=in_specs + [_resident(p.shape) for p in params],
        out_specs=out_specs,
        out_shape=out_shape,
        scratch_shapes=scratch,
        compiler_params=pltpu.CompilerParams(
            dimension_semantics=("arbitrary",),
            vmem_limit_bytes=VMEM_LIMIT_BYTES),
        name="trunk_sample",
    )(x, conv0, pkt, pvt, plft, *params)


def kernel(x_prompt, x_sample, cache_k, cache_v, cache_logf, state_conv, ln_in_g, ln_in_b, w_in, b_f,
           conv_w, w_proj_a, w_proj_b, w_out, ln_g, ln_b):
    assert w_in.shape[0] == DEPTH
    route_np, ones_np = _routing_constants()
    w = w_in[0]
    row = lambda a: a.reshape(1, -1).astype(F32)
    w_t = jnp.concatenate([w[:, _W_Q:_W_F + N_HEADS],
                           jnp.zeros((D_MODEL, F_ROWS - N_HEADS), w.dtype)], axis=1).T.astype(BF16)
    tm = PROMPT_TILE
    p = dict(
        ln_in_g=row(ln_in_g), ln_in_b=row(ln_in_b),
        w_lo=w[:, :_W_V].astype(BF16), w_hi=w[:, _W_ZB:].astype(BF16),
        w_fs=jnp.pad(w[:, _W_F:_W_F + N_HEADS], ((0, 0), (0, LANES - N_HEADS))).astype(BF16),
        b_f_row=jnp.pad(row(b_f[0]), ((0, 0), (0, LANES - N_HEADS))),
        w_t=w_t,
        b_f_col=jnp.pad(b_f[0].astype(F32).reshape(N_HEADS, 1), ((0, F_ROWS - N_HEADS), (0, 0))),
        conv_w=conv_w[0].astype(F32),
        w_pa=w_proj_a[0].astype(BF16), w_pb=w_proj_b[0].astype(BF16), w_out=w_out[0].astype(BF16),
        ln_g=row(ln_g[0]), ln_b=row(ln_b[0]),
        route=jnp.asarray(route_np, BF16), ones=jnp.asarray(ones_np, F32),
        tri=jnp.asarray(np.triu(np.ones((tm, tm), np.float32)), BF16),
    )
    bs = x_sample.shape[0]
    n_past = cache_k.shape[2]
    feature_major = lambda a: a.transpose(0, 2, 3, 1).reshape(bs, ATT_DIM, n_past)

    y_p, kt_p, vt_p, lft_p, cv_p = _run_prompt(x_prompt, [p[name] for name in PROMPT_PARAMS], tm=tm)
    y_s, kt_s, vt_s, lft_s, cv_s = _run_sample(
        x_sample, state_conv[0], feature_major(cache_k[0]), feature_major(cache_v[0]),
        cache_logf[0].transpose(0, 2, 1), [p[name] for name in SAMPLE_PARAMS])

    heads_t = lambda a: a.reshape(a.shape[0], N_HEADS, HEAD_DIM, a.shape[2]).transpose(0, 3, 1, 2)[None]
    tokens_t = lambda a: a.transpose(0, 2, 1)[None]
    return (y_p, y_s, heads_t(kt_p), heads_t(vt_p), tokens_t(lft_p), cv_p[None],
            heads_t(kt_s), heads_t(vt_s), tokens_t(lft_s), cv_s[None])
```

```python
import functools

import numpy as np
import jax
import jax.numpy as jnp
from jax import lax
from jax.experimental import pallas as pl
from jax.experimental.pallas import tpu as pltpu

D_MODEL = 1024
CONV_DIM = 512
CONV_W = 3
N_HEADS = 8
HEAD_DIM = 64
ATT_DIM = N_HEADS * HEAD_DIM
LN_EPS = 1e-5
DEPTH = 1
ALPHA = (2.0 * DEPTH) ** 0.25
SCALE = HEAD_DIM ** -0.5

LANES = 128
SUBLANES = 8
HEAD_BLOCK = LANES
AUG_DIM = N_HEADS * HEAD_BLOCK
N_SPLIT = 3
ONES_LANE = N_SPLIT * N_HEADS
EXTRA_ROWS = 16
F_ROWS = 16
VMEM_LIMIT_BYTES = 56 * 1024 * 1024
V_ROWS = HEAD_DIM + 16
LOG2E = 1.4426950408889634
PIPE_SLOTS = 3
TAIL_SHARE = 1
SCORE_LOOKAHEAD = 3

_W_Q = 4 * CONV_DIM
_W_V = _W_Q + 2 * ATT_DIM
_W_F = _W_Q + 3 * ATT_DIM
_W_ZB = _W_F + N_HEADS
_S_XA, _S_GB, _S_GC, _S_ZA, _S_Q, _S_K, _S_ZB, _S_GA, _S_GM = (
    0, 512, 1024, 1536, 2048, 2560, 3072, 3584, 4608)
_T_Q, _T_K, _T_V, _T_F = 0, 512, 1024, 1536

PROMPT_TILE = 256
PROMPT_PARAMS = ("ln_in_g", "ln_in_b", "w_lo", "w_hi", "w_t", "b_f_col", "conv_w", "w_pa", "w_pb", "w_out",
                 "ln_g", "ln_b", "route", "tri")
SAMPLE_PARAMS = ("ln_in_g", "ln_in_b", "w_lo", "w_hi", "w_fs", "b_f_row", "w_t", "b_f_col", "conv_w", "w_pa",
                 "w_pb", "w_out", "ln_g", "ln_b", "route", "ones")

F32 = jnp.float32
BF16 = jnp.bfloat16


def _routing_constants():
    route = np.zeros((LANES, 2 * AUG_DIM), np.float32)
    ones = np.zeros((1, 2 * AUG_DIM), np.float32)
    for h in range(N_HEADS):
        base = h * HEAD_BLOCK + (HEAD_DIM if h % 2 == 0 else 0)
        for r in range(N_SPLIT):
            route[r * N_HEADS + h, base + 2 * r] = 1.0
            ones[0, base + 2 * r + 1] = 1.0
            ones[0, AUG_DIM + base + 2 * r] = 1.0
            route[r * N_HEADS + h, AUG_DIM + base + 2 * r + 1] = -1.0
    route[ONES_LANE, :] = ones[0, :]
    return route, ones


def _layer_norm(x, g, b):
    mu = jnp.mean(x, axis=-1, keepdims=True)
    xc = x - mu
    var = jnp.mean(xc * xc, axis=-1, keepdims=True)
    return xc * lax.rsqrt(var + LN_EPS) * g + b


def _log_sigmoid(x):
    return jnp.minimum(x, 0.0) - jnp.log1p(jnp.exp(-jnp.abs(x)))


def _split3(x):
    hi = x.astype(BF16)
    r1 = x - hi.astype(F32)
    mid = r1.astype(BF16)
    lo = (r1 - mid.astype(F32)).astype(BF16)
    return hi, mid, lo


def _cumsum(x, axis):
    n = x.shape[axis]
    idx = lax.broadcasted_iota(jnp.int32, x.shape, axis)
    s = 1
    while s < n:
        x = x + jnp.where(idx >= s, pltpu.roll(x, s, axis), 0.0)
        s *= 2
    return x


def _extra_lanes(c, route):
    lane = lax.broadcasted_iota(jnp.int32, c.shape, 1)
    valid = lane < N_HEADS
    packed = jnp.zeros_like(c)
    for r, piece in enumerate(_split3(c)):
        p32 = jnp.where(valid, piece.astype(F32), 0.0)
        packed = packed + (pltpu.roll(p32, r * N_HEADS, 1) if r else p32)
    return jnp.dot(packed.astype(BF16), route, preferred_element_type=F32)


def _store_token_major(dst_ref, rows, data, extras):
    n = data.shape[0]
    lane = lax.broadcasted_iota(jnp.int32, (n, HEAD_BLOCK), 1)
    for h in range(N_HEADS):
        pair = data[:, (h // 2) * HEAD_BLOCK:(h // 2 + 1) * HEAD_BLOCK]
        ex = extras[:, h * HEAD_BLOCK:(h + 1) * HEAD_BLOCK]
        keep = (lane < HEAD_DIM) if h % 2 == 0 else (lane >= HEAD_DIM)
        dst_ref[rows, h * HEAD_BLOCK:(h + 1) * HEAD_BLOCK] = jnp.where(keep, pair, ex).astype(BF16)


def _extra_rows(pieces, h, n, key_side):
    r = lax.broadcasted_iota(jnp.int32, (EXTRA_ROWS, n), 0)
    value_row, sign = (1, -1.0) if key_side else (0, 1.0)
    out = jnp.where((r < 2 * N_SPLIT) & ((r & 1) != value_row), 1.0, 0.0)
    for i, piece in enumerate(pieces):
        out = jnp.where(r == 2 * i + value_row,
                        jnp.broadcast_to(sign * piece[h:h + 1, :], (EXTRA_ROWS, n)), out)
    return out.astype(BF16)


def _store_feature_major(dst_ref, h, data, extra):
    if h % 2 == 0:
        dst_ref[h, 0:HEAD_DIM, :] = data
        dst_ref[h, HEAD_DIM:HEAD_DIM + EXTRA_ROWS, :] = extra
    else:
        dst_ref[h, 0:EXTRA_ROWS, :] = extra
        dst_ref[h, HEAD_DIM:2 * HEAD_DIM, :] = data


def _std_columns(w_lo, w_hi, col, width):
    split = w_lo.shape[1]
    assert col + width <= split or col >= split
    return w_lo[:, col:col + width] if col < split else w_hi[:, col - split:col - split + width]


def _conv_branch(proj, u_sc, conv_w, conv_ref, w_pa, n):
    first = SUBLANES - (CONV_W - 1)
    u_sc[SUBLANES:SUBLANES + n, :] = proj(_S_GC, CONV_DIM) * proj(_S_XA, CONV_DIM)
    conv = sum(u_sc[first + i:first + i + n, :] * conv_w[i:i + 1, :] for i in range(CONV_W))
    new_conv = u_sc[first + n:SUBLANES + n, :]
    conv_ref[...] = new_conv
    u_sc[first:SUBLANES, :] = new_conv
    y_a = proj(_S_GB, CONV_DIM) * conv * jax.nn.silu(proj(_S_ZA, CONV_DIM))
    return jax.nn.sigmoid(proj(_S_GA, D_MODEL)) * jnp.dot(
        y_a.astype(BF16), w_pa[...], preferred_element_type=F32)


def _merge_and_norm(att, mix_a, h, silu_zb, gate_b, w_pb, w_out, ln_g, ln_b):
    y_b = att * silu_zb
    mixed = mix_a + gate_b * jnp.dot(
        y_b.astype(BF16), w_pb[...], preferred_element_type=F32)
    sub = jnp.dot(mixed.astype(BF16), w_out[...], preferred_element_type=F32)
    return _layer_norm(ALPHA * h + sub, ln_g[...], ln_b[...])


def _attention_t(j_lo, j_hi, j, tm, kaug_sc, qta_sc, vt_sc, attt_sc):
    n_key_tiles = j_hi + 1
    keys = n_key_tiles * tm
    key_in_tile = lax.broadcasted_iota(jnp.int32, (tm, tm), 0)
    query_in_tile = lax.broadcasted_iota(jnp.int32, (tm, tm), 1)
    tile_of_query = j if j_hi > j_lo else j_lo
    visible = {i: key_in_tile + (i - tile_of_query) * tm <= query_in_tile for i in range(j_lo, n_key_tiles)}

    def scores(hd):
        return jnp.dot(kaug_sc[0:keys, hd * HEAD_BLOCK:(hd + 1) * HEAD_BLOCK], qta_sc[hd],
                       preferred_element_type=F32)

    pending = [scores(hd) for hd in range(SCORE_LOOKAHEAD)]
    for hd in range(N_HEADS):
        s = pending.pop(0)
        if hd + SCORE_LOOKAHEAD < N_HEADS:
            pending.append(scores(hd + SCORE_LOOKAHEAD))
        blocks = [s[i * tm:(i + 1) * tm, :] for i in range(n_key_tiles)]
        for i, mask in visible.items():
            blocks[i] = jnp.where(mask, blocks[i], -jnp.inf)
        m = functools.reduce(jnp.maximum, [jnp.max(blk, axis=0, keepdims=True) for blk in blocks])
        o = jnp.zeros((V_ROWS, tm), F32)
        for i, blk in enumerate(blocks):
            o = o + jnp.dot(vt_sc[i, hd * V_ROWS:(hd + 1) * V_ROWS, :], jnp.exp2(blk - m).astype(BF16),
                            preferred_element_type=F32)
        attt_sc[hd * HEAD_DIM:(hd + 1) * HEAD_DIM, :] = o[0:HEAD_DIM, :] / o[HEAD_DIM:HEAD_DIM + 1, :]


def _prompt_kernel(x_ref, ln_in_g, ln_in_b, w_lo, w_hi, w_t, b_f_col, conv_w, w_pa, w_pb, w_out,
                   ln_g, ln_b, route, tri,
                   y_ref, kt_ref, vt_ref, lft_ref, conv_ref,
                   kaug_sc, vt_sc, h_sc, hb_sc, z_sc, qta_sc, attt_sc, zb_sc, gm_sc, u_sc, carryt_sc,
                   *, tm, n_tiles, n_total):
    t = pl.program_id(0)

    def norm_in():
        h = _layer_norm(x_ref[...], ln_in_g[...], ln_in_b[...])
        h_sc[lax.rem(t, PIPE_SLOTS)] = h
        hb_sc[lax.rem(t, PIPE_SLOTS)] = h.astype(BF16)

    def norm_out():
        y_ref[...] = _layer_norm(z_sc[...], ln_g[...], ln_b[...])

    @pl.when(t == 0)
    def _fill():
        for ref in (z_sc, kaug_sc, vt_sc, h_sc, hb_sc, attt_sc, zb_sc, gm_sc, u_sc, qta_sc, carryt_sc):
            ref[...] = jnp.zeros_like(ref)
        ones_row = jnp.where(lax.broadcasted_iota(jnp.int32, (V_ROWS - HEAD_DIM, tm), 0) == 0, 1.0, 0.0)
        for i in range(n_tiles):
            for hd in range(N_HEADS):
                vt_sc[i, hd * V_ROWS + HEAD_DIM:(hd + 1) * V_ROWS, :] = ones_row.astype(BF16)
        norm_in()

    @pl.when((t >= 1) & (t <= n_total + 1))
    def _mixers():
        tile_a = t - 1
        tile_b = t - 2
        ja = lax.rem(tile_a, n_tiles)
        jb = lax.rem(tile_b, n_tiles)
        slot_a = lax.rem(tile_a, PIPE_SLOTS)
        slot_b = lax.rem(t + 1, PIPE_SLOTS)
        gate_a = tile_a & 1
        gate_b = t & 1

        @pl.when(ja == 0)
        def _start_of_sequence():
            carryt_sc[...] = jnp.zeros_like(carryt_sc)

        @pl.when(jb == 0)
        def _start_of_conv():
            u_sc[0:SUBLANES, :] = jnp.zeros((SUBLANES, CONV_DIM), F32)

        spans = [(jj, jj) for jj in range(n_tiles - TAIL_SHARE)] + [(n_tiles - TAIL_SHARE, n_tiles - 1)]
        for j_lo, j_hi in spans:
            pl.when((t >= 2) & (jb >= j_lo) & (jb <= j_hi))(functools.partial(
                _attention_t, j_lo, j_hi, jb, tm, kaug_sc, qta_sc, vt_sc, attt_sc))

        def proj(slot, col, width):
            return jnp.dot(hb_sc[slot], _std_columns(w_lo, w_hi, col, width), preferred_element_type=F32)

        def proj_t(row, height):
            return lax.dot_general(w_t[row:row + height, :], hb_sc[slot_a], (((1,), (1,)), ((), ())),
                                   preferred_element_type=F32)

        norm_out()

        vf_t = proj_t(_T_V, ATT_DIM + F_ROWS)
        qk_t = proj_t(_T_Q, 2 * ATT_DIM)
        kt = qk_t[ATT_DIM:2 * ATT_DIM, :]
        vt = vf_t[0:ATT_DIM, :]
        kt_ref[...] = kt
        vt_ref[...] = vt
        logf_t = _log_sigmoid(vf_t[ATT_DIM:ATT_DIM + F_ROWS, :] + b_f_col[...])
        lft_ref[...] = logf_t[:N_HEADS, :]
        sums = jnp.dot(jnp.concatenate(_split3(logf_t), axis=0), tri[...], preferred_element_type=F32)

        mix_a = _conv_branch(functools.partial(proj, slot_b), u_sc, conv_w, conv_ref, w_pa, tm)

        zb_sc[gate_a] = jax.nn.silu(proj(slot_a, _S_ZB, ATT_DIM))
        gm_sc[gate_a] = jax.nn.sigmoid(proj(slot_a, _S_GM, D_MODEL))

        y_b = attt_sc[...].T * zb_sc[gate_b]
        mixed = mix_a + gm_sc[gate_b] * jnp.dot(y_b.astype(BF16), w_pb[...], preferred_element_type=F32)

        c_t = sum(sums[r * F_ROWS:(r + 1) * F_ROWS, :] for r in range(N_SPLIT)) + carryt_sc[:, 0:1]
        carryt_sc[...] = jnp.broadcast_to(c_t[:, tm - 1:tm], carryt_sc.shape)
        c_pieces = [piece.astype(F32) for piece in _split3(c_t * LOG2E)]
        qt = qk_t[0:ATT_DIM, :] * (SCALE * LOG2E)
        for hd in range(N_HEADS):
            _store_feature_major(qta_sc, hd, qt[hd * HEAD_DIM:(hd + 1) * HEAD_DIM, :].astype(BF16),
                                 _extra_rows(c_pieces, hd, tm, key_side=False))
            vt_sc[ja, hd * V_ROWS:hd * V_ROWS + HEAD_DIM, :] = (
                vt[hd * HEAD_DIM:(hd + 1) * HEAD_DIM, :].astype(BF16))
        ones_and_pad = jnp.where(lax.broadcasted_iota(jnp.int32, (LANES - ONES_LANE, tm), 0) == 0, 1.0, 0.0)
        packed_t = jnp.concatenate([piece[0:N_HEADS, :] for piece in c_pieces] + [ones_and_pad], axis=0)
        ex_k = jnp.dot(packed_t.T.astype(BF16), route[:, AUG_DIM:], preferred_element_type=F32)
        _store_token_major(kaug_sc, pl.ds(pl.multiple_of(ja * tm, tm), tm), kt.T, ex_k)

        z_sc[...] = ALPHA * h_sc[slot_b] + jnp.dot(mixed.astype(BF16), w_out[...], preferred_element_type=F32)

        norm_in()

    @pl.when(t == n_total + 2)
    def _drain():
        norm_out()


def _resident(shape):
    return pl.BlockSpec(shape, lambda *_: (0,) * len(shape), pipeline_mode=pl.Buffered(1))


def _run_prompt(x, params, *, tm):
    bsz, t, _ = x.shape
    n_tiles = t // tm
    assert n_tiles * tm == t
    n_total = bsz * n_tiles
    tile = lambda s, lag: jnp.clip(s - lag, 0, n_total - 1)
    per_tile = lambda width, lag: pl.BlockSpec(
        (None, tm, width), lambda s: (tile(s, lag) // n_tiles, tile(s, lag) % n_tiles, 0))
    per_tile_t = lambda rows: pl.BlockSpec(
        (None, rows, tm), lambda s: (tile(s, 1) // n_tiles, 0, tile(s, 1) % n_tiles))
    out_shape = (jax.ShapeDtypeStruct((bsz, t, D_MODEL), F32),
                 jax.ShapeDtypeStruct((bsz, ATT_DIM, t), F32),
                 jax.ShapeDtypeStruct((bsz, ATT_DIM, t), F32),
                 jax.ShapeDtypeStruct((bsz, N_HEADS, t), F32),
                 jax.ShapeDtypeStruct((bsz, CONV_W - 1, CONV_DIM), F32))
    out_specs = (per_tile(D_MODEL, 3), per_tile_t(ATT_DIM), per_tile_t(ATT_DIM), per_tile_t(N_HEADS),
                 pl.BlockSpec((None, CONV_W - 1, CONV_DIM), lambda s: (tile(s, 2) // n_tiles, 0, 0)))
    scratch = [
        pltpu.VMEM((t, AUG_DIM), BF16),
        pltpu.VMEM((n_tiles, N_HEADS * V_ROWS, tm), BF16),
        pltpu.VMEM((PIPE_SLOTS, tm, D_MODEL), F32),
        pltpu.VMEM((PIPE_SLOTS, tm, D_MODEL), BF16),
        pltpu.VMEM((tm, D_MODEL), F32),
        pltpu.VMEM((N_HEADS, HEAD_BLOCK, tm), BF16),
        pltpu.VMEM((ATT_DIM, tm), F32),
        pltpu.VMEM((2, tm, ATT_DIM), F32),
        pltpu.VMEM((2, tm, D_MODEL), F32),
        pltpu.VMEM((SUBLANES + tm, CONV_DIM), F32),
        pltpu.VMEM((F_ROWS, LANES), F32),
    ]
    return pl.pallas_call(
        functools.partial(_prompt_kernel, tm=tm, n_tiles=n_tiles, n_total=n_total),
        grid=(n_total + PIPE_SLOTS,),
        in_specs=[per_tile(D_MODEL, 0)] + [_resident(p.shape) for p in params],
        out_specs=out_specs,
        out_shape=out_shape,
        scratch_shapes=scratch,
        compiler_params=pltpu.CompilerParams(
            dimension_semantics=("arbitrary",),
            vmem_limit_bytes=VMEM_LIMIT_BYTES),
        name="trunk_prompt",
    )(x, *params)


def _sample_kernel(x_ref, conv0_ref, pkt_ref, pvt_ref, plft_ref,
                   ln_in_g, ln_in_b, w_lo, w_hi, w_fs, b_f_row, w_t, b_f_col, conv_w, w_pa, w_pb, w_out,
                   ln_g, ln_b, route, ones,
                   y_ref, kt_ref, vt_ref, lft_ref, conv_ref,
                   kta_sc, ktn_sc, hb_sc, qa_sc, att_sc, u_sc, *, ts, n_past):
    @pl.when(pl.program_id(0) == 0)
    def _zero_padding():
        kta_sc[...] = jnp.zeros_like(kta_sc)
        ktn_sc[...] = jnp.zeros_like(ktn_sc)
        hb_sc[...] = jnp.zeros_like(hb_sc)

    h = _layer_norm(x_ref[...], ln_in_g[...], ln_in_b[...])
    hb_sc[0:ts, :] = h.astype(BF16)

    def proj(col, width):
        return jnp.dot(hb_sc[0:ts, :], _std_columns(w_lo, w_hi, col, width), preferred_element_type=F32)

    u_sc[0:SUBLANES, :] = jnp.zeros((SUBLANES, CONV_DIM), F32)
    u_sc[SUBLANES - (CONV_W - 1):SUBLANES, :] = conv0_ref[...]
    mix_a = _conv_branch(proj, u_sc, conv_w, conv_ref, w_pa, ts)

    def proj_t(row, height):
        return lax.dot_general(w_t[row:row + height, :], hb_sc[...], (((1,), (1,)), ((), ())),
                               preferred_element_type=F32)

    kt = proj_t(_T_K, ATT_DIM)
    vf_t = proj_t(_T_V, ATT_DIM + F_ROWS)
    vt = vf_t[0:ATT_DIM, :]
    kt_ref[...] = kt[:, :ts]
    vt_ref[...] = vt[:, :ts]
    logf_t = _log_sigmoid(vf_t[ATT_DIM:ATT_DIM + SUBLANES, :] + b_f_col[0:SUBLANES, :])
    lft_ref[...] = logf_t[:, :ts]

    c_past = _cumsum(plft_ref[...], 1)
    c_last = c_past[:, n_past - 1:n_past]
    c_new_t = _cumsum(logf_t, 1) + c_last
    past_pieces = [piece.astype(F32) for piece in _split3(c_past)]
    new_pieces = [piece.astype(F32) for piece in _split3(c_new_t)]
    for hd in range(N_HEADS):
        rows = slice(hd * HEAD_DIM, (hd + 1) * HEAD_DIM)
        _store_feature_major(kta_sc, hd, pkt_ref[rows, :].astype(BF16),
                             _extra_rows(past_pieces, hd, n_past, key_side=True))
        _store_feature_major(ktn_sc, hd, kt[rows, :].astype(BF16),
                             _extra_rows(new_pieces, hd, LANES, key_side=True))

    eye = (lax.broadcasted_iota(jnp.int32, (SUBLANES, LANES), 0)
           == lax.broadcasted_iota(jnp.int32, (SUBLANES, LANES), 1))
    c_last_row = jnp.sum(jnp.where(eye, jnp.broadcast_to(c_last, (SUBLANES, LANES)), 0.0),
                         axis=0, keepdims=True)
    logf = _log_sigmoid(jnp.dot(hb_sc[0:ts, :], w_fs[...], preferred_element_type=F32) + b_f_row[...])
    c_q = _cumsum(logf, 0) + c_last_row
    ex_q = _extra_lanes(c_q, route[:, :AUG_DIM]) + ones[:, :AUG_DIM]
    _store_token_major(qa_sc, slice(None), proj(_S_Q, ATT_DIM) * SCALE, ex_q)

    causal = (lax.broadcasted_iota(jnp.int32, (ts, LANES), 1)
              <= lax.broadcasted_iota(jnp.int32, (ts, LANES), 0))
    lane = lax.broadcasted_iota(jnp.int32, (ts, HEAD_BLOCK), 1)
    contract_lanes = (((1,), (1,)), ((), ()))
    scores = []
    for hd in range(N_HEADS):
        qa_h = qa_sc[:, hd * HEAD_BLOCK:(hd + 1) * HEAD_BLOCK]
        scores.append((jnp.dot(qa_h, kta_sc[hd], preferred_element_type=F32),
                       jnp.dot(qa_h, ktn_sc[hd], preferred_element_type=F32)))
    for pair in range(N_HEADS // 2):
        rows = slice(pair * HEAD_BLOCK, (pair + 1) * HEAD_BLOCK)
        vt_past = pvt_ref[rows, :].astype(BF16)
        vt_new = vt[rows, :].astype(BF16)
        outs = []
        for hd in (2 * pair, 2 * pair + 1):
            s_past = scores[hd][0]
            s_new = jnp.where(causal, scores[hd][1], -jnp.inf)
            m = jnp.maximum(jnp.max(s_past, axis=-1, keepdims=True), jnp.max(s_new, axis=-1, keepdims=True))
            p_past = jnp.exp(s_past - m)
            p_new = jnp.exp(s_new - m)
            l = jnp.sum(p_past, axis=-1, keepdims=True) + jnp.sum(p_new, axis=-1, keepdims=True)
            o = (lax.dot_general(p_past.astype(BF16), vt_past, contract_lanes, preferred_element_type=F32)
                 + lax.dot_general(p_new.astype(BF16), vt_new, contract_lanes, preferred_element_type=F32))
            outs.append(o / l)
        att_sc[:, rows] = jnp.where(lane < HEAD_DIM, outs[0], outs[1])

    y_ref[...] = _merge_and_norm(att_sc[...], mix_a, h, jax.nn.silu(proj(_S_ZB, ATT_DIM)),
                                 jax.nn.sigmoid(proj(_S_GM, D_MODEL)), w_pb, w_out, ln_g, ln_b)


def _run_sample(x, conv0, pkt, pvt, plft, params):
    bsz, ts, _ = x.shape
    n_past = pkt.shape[2]
    assert ts <= LANES and ts % SUBLANES == 0
    per_batch = lambda rows, width: pl.BlockSpec((None, rows, width), lambda b: (b, 0, 0))
    out_shape = (jax.ShapeDtypeStruct((bsz, ts, D_MODEL), F32),
                 jax.ShapeDtypeStruct((bsz, ATT_DIM, ts), F32),
                 jax.ShapeDtypeStruct((bsz, ATT_DIM, ts), F32),
                 jax.ShapeDtypeStruct((bsz, N_HEADS, ts), F32),
                 jax.ShapeDtypeStruct((bsz, CONV_W - 1, CONV_DIM), F32))
    out_specs = (per_batch(ts, D_MODEL), per_batch(ATT_DIM, ts), per_batch(ATT_DIM, ts),
                 per_batch(N_HEADS, ts), per_batch(CONV_W - 1, CONV_DIM))
    in_specs = [per_batch(ts, D_MODEL), per_batch(CONV_W - 1, CONV_DIM), per_batch(ATT_DIM, n_past),
                per_batch(ATT_DIM, n_past), per_batch(N_HEADS, n_past)]
    scratch = [
        pltpu.VMEM((N_HEADS, HEAD_BLOCK, n_past), BF16),
        pltpu.VMEM((N_HEADS, HEAD_BLOCK, LANES), BF16),
        pltpu.VMEM((LANES, D_MODEL), BF16),
        pltpu.VMEM((ts, AUG_DIM), BF16),
        pltpu.VMEM((ts, ATT_DIM), F32),
        pltpu.VMEM((SUBLANES + ts, CONV_DIM), F32),
    ]
    return pl.pallas_call(
        functools.partial(_sample_kernel, ts=ts, n_past=n_past),
        grid=(bsz,),
        in_specs=in_specs + [_resident(p.shape) for p in params],
        out_specs=out_specs,
        out_shape=out_shape,
        scratch_shapes=scratch,
        compiler_params=pltpu.CompilerParams(
            dimension_semantics=("arbitrary",),
            vmem_limit_bytes=VMEM_LIMIT_BYTES),
        name="trunk_sample",
    )(x, conv0, pkt, pvt, plft, *params)


def kernel(x_prompt, x_sample, cache_k, cache_v, cache_logf, state_conv, ln_in_g, ln_in_b, w_in, b_f,
           conv_w, w_proj_a, w_proj_b, w_out, ln_g, ln_b):
    assert w_in.shape[0] == DEPTH
    route_np, ones_np = _routing_constants()
    w = w_in[0]
    row = lambda a: a.reshape(1, -1).astype(F32)
    w_t = jnp.concatenate([w[:, _W_Q:_W_F + N_HEADS],
                           jnp.zeros((D_MODEL, F_ROWS - N_HEADS), w.dtype)], axis=1).T.astype(BF16)
    tm = PROMPT_TILE
    p = dict(
        ln_in_g=row(ln_in_g), ln_in_b=row(ln_in_b),
        w_lo=w[:, :_W_V].astype(BF16), w_hi=w[:, _W_ZB:].astype(BF16),
        w_fs=jnp.pad(w[:, _W_F:_W_F + N_HEADS], ((0, 0), (0, LANES - N_HEADS))).astype(BF16),
        b_f_row=jnp.pad(row(b_f[0]), ((0, 0), (0, LANES - N_HEADS))),
        w_t=w_t,
        b_f_col=jnp.pad(b_f[0].astype(F32).reshape(N_HEADS, 1), ((0, F_ROWS - N_HEADS), (0, 0))),
        conv_w=conv_w[0].astype(F32),
        w_pa=w_proj_a[0].astype(BF16), w_pb=w_proj_b[0].astype(BF16), w_out=w_out[0].astype(BF16),
        ln_g=row(ln_g[0]), ln_b=row(ln_b[0]),
        route=jnp.asarray(route_np, BF16), ones=jnp.asarray(ones_np, F32),
        tri=jnp.asarray(np.triu(np.ones((tm, tm), np.float32)), BF16),
    )
    bs = x_sample.shape[0]
    n_past = cache_k.shape[2]
    feature_major = lambda a: a.transpose(0, 2, 3, 1).reshape(bs, ATT_DIM, n_past)

    y_p, kt_p, vt_p, lft_p, cv_p = _run_prompt(x_prompt, [p[name] for name in PROMPT_PARAMS], tm=tm)
    y_s, kt_s, vt_s, lft_s, cv_s = _run_sample(
        x_sample, state_conv[0], feature_major(cache_k[0]), feature_major(cache_v[0]),
        cache_logf[0].transpose(0, 2, 1), [p[name] for name in SAMPLE_PARAMS])

    heads_t = lambda a: a.reshape(a.shape[0], N_HEADS, HEAD_DIM, a.shape[2]).transpose(0, 3, 1, 2)[None]
    tokens_t = lambda a: a.transpose(0, 2, 1)[None]
    return (y_p, y_s, heads_t(kt_p), heads_t(vt_p), tokens_t(lft_p), cv_p[None],
            heads_t(kt_s), heads_t(vt_s), tokens_t(lft_s), cv_s[None])
```

```python
import functools

import numpy as np
import jax
import jax.numpy as jnp
from jax import lax
from jax.experimental import pallas as pl
from jax.experimental.pallas import tpu as pltpu

D_MODEL = 1024
CONV_DIM = 512
CONV_W = 3
N_HEADS = 8
HEAD_DIM = 64
ATT_DIM = N_HEADS * HEAD_DIM
LN_EPS = 1e-5
DEPTH = 1
ALPHA = (2.0 * DEPTH) ** 0.25
SCALE = HEAD_DIM ** -0.5

LANES = 128
SUBLANES = 8
HEAD_BLOCK = LANES
AUG_DIM = N_HEADS * HEAD_BLOCK
N_SPLIT = 3
ONES_LANE = N_SPLIT * N_HEADS
EXTRA_ROWS = 16
F_ROWS = 16
VMEM_LIMIT_BYTES = 56 * 1024 * 1024
V_ROWS = HEAD_DIM + 16
LOG2E = 1.4426950408889634
PIPE_SLOTS = 3
TAIL_SHARE = 1
SCORE_LOOKAHEAD = 4
SHORT_BRANCH_TILES = 2

_W_Q = 4 * CONV_DIM
_W_V = _W_Q + 2 * ATT_DIM
_W_F = _W_Q + 3 * ATT_DIM
_W_ZB = _W_F + N_HEADS
_S_XA, _S_GB, _S_GC, _S_ZA, _S_Q, _S_K, _S_ZB, _S_GA, _S_GM = (
    0, 512, 1024, 1536, 2048, 2560, 3072, 3584, 4608)
_T_Q, _T_K, _T_V, _T_F = 0, 512, 1024, 1536

PROMPT_TILE = 256
PROMPT_PARAMS = ("ln_in_g", "ln_in_b", "w_lo", "w_hi", "w_t", "b_f_col", "conv_w", "w_pa", "w_pb", "w_out",
                 "ln_g", "ln_b", "route", "tri")
SAMPLE_PARAMS = ("ln_in_g", "ln_in_b", "w_lo", "w_hi", "w_fs", "b_f_row", "w_t", "b_f_col", "conv_w", "w_pa",
                 "w_pb", "w_out", "ln_g", "ln_b", "route", "ones")

F32 = jnp.float32
BF16 = jnp.bfloat16


def _routing_constants():
    route = np.zeros((LANES, 2 * AUG_DIM), np.float32)
    ones = np.zeros((1, 2 * AUG_DIM), np.float32)
    for h in range(N_HEADS):
        base = h * HEAD_BLOCK + (HEAD_DIM if h % 2 == 0 else 0)
        for r in range(N_SPLIT):
            route[r * N_HEADS + h, base + 2 * r] = 1.0
            ones[0, base + 2 * r + 1] = 1.0
            ones[0, AUG_DIM + base + 2 * r] = 1.0
            route[r * N_HEADS + h, AUG_DIM + base + 2 * r + 1] = -1.0
    route[ONES_LANE, :] = ones[0, :]
    return route, ones


def _layer_norm(x, g, b):
    mu = jnp.mean(x, axis=-1, keepdims=True)
    xc = x - mu
    var = jnp.mean(xc * xc, axis=-1, keepdims=True)
    return xc * lax.rsqrt(var + LN_EPS) * g + b


def _log_sigmoid(x):
    return jnp.minimum(x, 0.0) - jnp.log1p(jnp.exp(-jnp.abs(x)))


def _split3(x):
    hi = x.astype(BF16)
    r1 = x - hi.astype(F32)
    mid = r1.astype(BF16)
    lo = (r1 - mid.astype(F32)).astype(BF16)
    return hi, mid, lo


def _cumsum(x, axis):
    n = x.shape[axis]
    idx = lax.broadcasted_iota(jnp.int32, x.shape, axis)
    s = 1
    while s < n:
        x = x + jnp.where(idx >= s, pltpu.roll(x, s, axis), 0.0)
        s *= 2
    return x


def _extra_lanes(c, route):
    lane = lax.broadcasted_iota(jnp.int32, c.shape, 1)
    valid = lane < N_HEADS
    packed = jnp.zeros_like(c)
    for r, piece in enumerate(_split3(c)):
        p32 = jnp.where(valid, piece.astype(F32), 0.0)
        packed = packed + (pltpu.roll(p32, r * N_HEADS, 1) if r else p32)
    return jnp.dot(packed.astype(BF16), route, preferred_element_type=F32)


def _store_token_major(dst_ref, rows, data, extras):
    n = data.shape[0]
    lane = lax.broadcasted_iota(jnp.int32, (n, HEAD_BLOCK), 1)
    for h in range(N_HEADS):
        pair = data[:, (h // 2) * HEAD_BLOCK:(h // 2 + 1) * HEAD_BLOCK]
        ex = extras[:, h * HEAD_BLOCK:(h + 1) * HEAD_BLOCK]
        keep = (lane < HEAD_DIM) if h % 2 == 0 else (lane >= HEAD_DIM)
        dst_ref[rows, h * HEAD_BLOCK:(h + 1) * HEAD_BLOCK] = jnp.where(keep, pair, ex).astype(BF16)


def _extra_rows(pieces, h, n, key_side):
    r = lax.broadcasted_iota(jnp.int32, (EXTRA_ROWS, n), 0)
    value_row, sign = (1, -1.0) if key_side else (0, 1.0)
    out = jnp.where((r < 2 * N_SPLIT) & ((r & 1) != value_row), 1.0, 0.0)
    for i, piece in enumerate(pieces):
        out = jnp.where(r == 2 * i + value_row,
                        jnp.broadcast_to(sign * piece[h:h + 1, :], (EXTRA_ROWS, n)), out)
    return out.astype(BF16)


def _store_feature_major(dst_ref, h, data, extra):
    if h % 2 == 0:
        dst_ref[h, 0:HEAD_DIM, :] = data
        dst_ref[h, HEAD_DIM:HEAD_DIM + EXTRA_ROWS, :] = extra
    else:
        dst_ref[h, 0:EXTRA_ROWS, :] = extra
        dst_ref[h, HEAD_DIM:2 * HEAD_DIM, :] = data


def _std_columns(w_lo, w_hi, col, width):
    split = w_lo.shape[1]
    assert col + width <= split or col >= split
    return w_lo[:, col:col + width] if col < split else w_hi[:, col - split:col - split + width]


def _conv_branch(proj, u_sc, conv_w, conv_ref, w_pa, n):
    first = SUBLANES - (CONV_W - 1)
    u_sc[SUBLANES:SUBLANES + n, :] = proj(_S_GC, CONV_DIM) * proj(_S_XA, CONV_DIM)
    conv = sum(u_sc[first + i:first + i + n, :] * conv_w[i:i + 1, :] for i in range(CONV_W))
    new_conv = u_sc[first + n:SUBLANES + n, :]
    conv_ref[...] = new_conv
    u_sc[first:SUBLANES, :] = new_conv
    y_a = proj(_S_GB, CONV_DIM) * conv * jax.nn.silu(proj(_S_ZA, CONV_DIM))
    return jax.nn.sigmoid(proj(_S_GA, D_MODEL)) * jnp.dot(
        y_a.astype(BF16), w_pa[...], preferred_element_type=F32)


def _merge_and_norm(att, mix_a, h, silu_zb, gate_b, w_pb, w_out, ln_g, ln_b):
    y_b = att * silu_zb
    mixed = mix_a + gate_b * jnp.dot(
        y_b.astype(BF16), w_pb[...], preferred_element_type=F32)
    sub = jnp.dot(mixed.astype(BF16), w_out[...], preferred_element_type=F32)
    return _layer_norm(ALPHA * h + sub, ln_g[...], ln_b[...])


def _attention_t(j_lo, j_hi, j, tm, kaug_sc, qta_sc, vt_sc, attt_sc):
    n_key_tiles = j_hi + 1
    keys = n_key_tiles * tm
    key_in_tile = lax.broadcasted_iota(jnp.int32, (tm, tm), 0)
    query_in_tile = lax.broadcasted_iota(jnp.int32, (tm, tm), 1)
    tile_of_query = j if j_hi > j_lo else j_lo
    visible = {i: key_in_tile + (i - tile_of_query) * tm <= query_in_tile for i in range(j_lo, n_key_tiles)}

    def scores(hd):
        return jnp.dot(kaug_sc[0:keys, hd * HEAD_BLOCK:(hd + 1) * HEAD_BLOCK], qta_sc[hd],
                       preferred_element_type=F32)

    lookahead = N_HEADS if n_key_tiles <= SHORT_BRANCH_TILES else SCORE_LOOKAHEAD
    pending = [scores(hd) for hd in range(lookahead)]
    for hd in range(N_HEADS):
        s = pending.pop(0)
        if hd + lookahead < N_HEADS:
            pending.append(scores(hd + lookahead))
        blocks = [s[i * tm:(i + 1) * tm, :] for i in range(n_key_tiles)]
        for i, mask in visible.items():
            blocks[i] = jnp.where(mask, blocks[i], -jnp.inf)
        m = functools.reduce(jnp.maximum, [jnp.max(blk, axis=0, keepdims=True) for blk in blocks])
        o = jnp.zeros((V_ROWS, tm), F32)
        for i, blk in enumerate(blocks):
            o = o + jnp.dot(vt_sc[i, hd * V_ROWS:(hd + 1) * V_ROWS, :], jnp.exp2(blk - m).astype(BF16),
                            preferred_element_type=F32)
        attt_sc[hd * HEAD_DIM:(hd + 1) * HEAD_DIM, :] = o[0:HEAD_DIM, :] / o[HEAD_DIM:HEAD_DIM + 1, :]


def _prompt_kernel(x_ref, ln_in_g, ln_in_b, w_lo, w_hi, w_t, b_f_col, conv_w, w_pa, w_pb, w_out,
                   ln_g, ln_b, route, tri,
                   y_ref, kt_ref, vt_ref, lft_ref, conv_ref,
                   kaug_sc, vt_sc, h_sc, hb_sc, z_sc, qta_sc, attt_sc, zb_sc, gm_sc, u_sc, carryt_sc,
                   *, tm, n_tiles, n_total):
    t = pl.program_id(0)

    def norm_in():
        h = _layer_norm(x_ref[...], ln_in_g[...], ln_in_b[...])
        h_sc[lax.rem(t, PIPE_SLOTS)] = h
        hb_sc[lax.rem(t, PIPE_SLOTS)] = h.astype(BF16)

    def norm_out():
        y_ref[...] = _layer_norm(z_sc[...], ln_g[...], ln_b[...])

    @pl.when(t == 0)
    def _fill():
        for ref in (z_sc, kaug_sc, vt_sc, h_sc, hb_sc, attt_sc, zb_sc, gm_sc, u_sc, qta_sc, carryt_sc):
            ref[...] = jnp.zeros_like(ref)
        ones_row = jnp.where(lax.broadcasted_iota(jnp.int32, (V_ROWS - HEAD_DIM, tm), 0) == 0, 1.0, 0.0)
        for i in range(n_tiles):
            for hd in range(N_HEADS):
                vt_sc[i, hd * V_ROWS + HEAD_DIM:(hd + 1) * V_ROWS, :] = ones_row.astype(BF16)
        norm_in()

    @pl.when((t >= 1) & (t <= n_total + 1))
    def _mixers():
        tile_a = t - 1
        tile_b = t - 2
        ja = lax.rem(tile_a, n_tiles)
        jb = lax.rem(tile_b, n_tiles)
        slot_a = lax.rem(tile_a, PIPE_SLOTS)
        slot_b = lax.rem(t + 1, PIPE_SLOTS)
        gate_a = tile_a & 1
        gate_b = t & 1

        @pl.when(ja == 0)
        def _start_of_sequence():
            carryt_sc[...] = jnp.zeros_like(carryt_sc)

        @pl.when(jb == 0)
        def _start_of_conv():
            u_sc[0:SUBLANES, :] = jnp.zeros((SUBLANES, CONV_DIM), F32)

        spans = [(jj, jj) for jj in range(n_tiles - TAIL_SHARE)] + [(n_tiles - TAIL_SHARE, n_tiles - 1)]
        for j_lo, j_hi in spans:
            pl.when((t >= 2) & (jb >= j_lo) & (jb <= j_hi))(functools.partial(
                _attention_t, j_lo, j_hi, jb, tm, kaug_sc, qta_sc, vt_sc, attt_sc))

        def proj(slot, col, width):
            return jnp.dot(hb_sc[slot], _std_columns(w_lo, w_hi, col, width), preferred_element_type=F32)

        def proj_t(row, height):
            return lax.dot_general(w_t[row:row + height, :], hb_sc[slot_a], (((1,), (1,)), ((), ())),
                                   preferred_element_type=F32)

        norm_out()

        vf_t = proj_t(_T_V, ATT_DIM + F_ROWS)
        qk_t = proj_t(_T_Q, 2 * ATT_DIM)
        kt = qk_t[ATT_DIM:2 * ATT_DIM, :]
        vt = vf_t[0:ATT_DIM, :]
        kt_ref[...] = kt
        vt_ref[...] = vt
        logf_t = _log_sigmoid(vf_t[ATT_DIM:ATT_DIM + F_ROWS, :] + b_f_col[...])
        lft_ref[...] = logf_t[:N_HEADS, :]
        sums = jnp.dot(jnp.concatenate(_split3(logf_t), axis=0), tri[...], preferred_element_type=F32)

        mix_a = _conv_branch(functools.partial(proj, slot_b), u_sc, conv_w, conv_ref, w_pa, tm)

        zb_sc[gate_a] = jax.nn.silu(proj(slot_a, _S_ZB, ATT_DIM))
        gm_sc[gate_a] = jax.nn.sigmoid(proj(slot_a, _S_GM, D_MODEL))

        y_b = attt_sc[...].T * zb_sc[gate_b]
        mixed = mix_a + gm_sc[gate_b] * jnp.dot(y_b.astype(BF16), w_pb[...], preferred_element_type=F32)

        c_t = sum(sums[r * F_ROWS:(r + 1) * F_ROWS, :] for r in range(N_SPLIT)) + carryt_sc[:, 0:1]
        carryt_sc[...] = jnp.broadcast_to(c_t[:, tm - 1:tm], carryt_sc.shape)
        c_pieces = [piece.astype(F32) for piece in _split3(c_t * LOG2E)]
        qt = qk_t[0:ATT_DIM, :] * (SCALE * LOG2E)
        for hd in range(N_HEADS):
            _store_feature_major(qta_sc, hd, qt[hd * HEAD_DIM:(hd + 1) * HEAD_DIM, :].astype(BF16),
                                 _extra_rows(c_pieces, hd, tm, key_side=False))
            vt_sc[ja, hd * V_ROWS:hd * V_ROWS + HEAD_DIM, :] = (
                vt[hd * HEAD_DIM:(hd + 1) * HEAD_DIM, :].astype(BF16))
        ones_and_pad = jnp.where(lax.broadcasted_iota(jnp.int32, (LANES - ONES_LANE, tm), 0) == 0, 1.0, 0.0)
        packed_t = jnp.concatenate([piece[0:N_HEADS, :] for piece in c_pieces] + [ones_and_pad], axis=0)
        ex_k = jnp.dot(packed_t.T.astype(BF16), route[:, AUG_DIM:], preferred_element_type=F32)
        _store_token_major(kaug_sc, pl.ds(pl.multiple_of(ja * tm, tm), tm), kt.T, ex_k)

        z_sc[...] = ALPHA * h_sc[slot_b] + jnp.dot(mixed.astype(BF16), w_out[...], preferred_element_type=F32)

        norm_in()

    @pl.when(t == n_total + 2)
    def _drain():
        norm_out()


def _resident(shape):
    return pl.BlockSpec(shape, lambda *_: (0,) * len(shape), pipeline_mode=pl.Buffered(1))


def _run_prompt(x, params, *, tm):
    bsz, t, _ = x.shape
    n_tiles = t // tm
    assert n_tiles * tm == t
    n_total = bsz * n_tiles
    tile = lambda s, lag: jnp.clip(s - lag, 0, n_total - 1)
    per_tile = lambda width, lag: pl.BlockSpec(
        (None, tm, width), lambda s: (tile(s, lag) // n_tiles, tile(s, lag) % n_tiles, 0))
    per_tile_t = lambda rows: pl.BlockSpec(
        (None, rows, tm), lambda s: (tile(s, 1) // n_tiles, 0, tile(s, 1) % n_tiles))
    out_shape = (jax.ShapeDtypeStruct((bsz, t, D_MODEL), F32),
                 jax.ShapeDtypeStruct((bsz, ATT_DIM, t), F32),
                 jax.ShapeDtypeStruct((bsz, ATT_DIM, t), F32),
                 jax.ShapeDtypeStruct((bsz, N_HEADS, t), F32),
                 jax.ShapeDtypeStruct((bsz, CONV_W - 1, CONV_DIM), F32))
    out_specs = (per_tile(D_MODEL, 3), per_tile_t(ATT_DIM), per_tile_t(ATT_DIM), per_tile_t(N_HEADS),
                 pl.BlockSpec((None, CONV_W - 1, CONV_DIM), lambda s: (tile(s, 2) // n_tiles, 0, 0)))
    scratch = [
        pltpu.VMEM((t, AUG_DIM), BF16),
        pltpu.VMEM((n_tiles, N_HEADS * V_ROWS, tm), BF16),
        pltpu.VMEM((PIPE_SLOTS, tm, D_MODEL), F32),
        pltpu.VMEM((PIPE_SLOTS, tm, D_MODEL), BF16),
        pltpu.VMEM((tm, D_MODEL), F32),
        pltpu.VMEM((N_HEADS, HEAD_BLOCK, tm), BF16),
        pltpu.VMEM((ATT_DIM, tm), F32),
        pltpu.VMEM((2, tm, ATT_DIM), F32),
        pltpu.VMEM((2, tm, D_MODEL), F32),
        pltpu.VMEM((SUBLANES + tm, CONV_DIM), F32),
        pltpu.VMEM((F_ROWS, LANES), F32),
    ]
    return pl.pallas_call(
        functools.partial(_prompt_kernel, tm=tm, n_tiles=n_tiles, n_total=n_total),
        grid=(n_total + PIPE_SLOTS,),
        in_specs=[per_tile(D_MODEL, 0)] + [_resident(p.shape) for p in params],
        out_specs=out_specs,
        out_shape=out_shape,
        scratch_shapes=scratch,
        compiler_params=pltpu.CompilerParams(
            dimension_semantics=("arbitrary",),
            vmem_limit_bytes=VMEM_LIMIT_BYTES),
        name="trunk_prompt",
    )(x, *params)


def _sample_kernel(x_ref, conv0_ref, pkt_ref, pvt_ref, plft_ref,
                   ln_in_g, ln_in_b, w_lo, w_hi, w_fs, b_f_row, w_t, b_f_col, conv_w, w_pa, w_pb, w_out,
                   ln_g, ln_b, route, ones,
                   y_ref, kt_ref, vt_ref, lft_ref, conv_ref,
                   kta_sc, ktn_sc, hb_sc, qa_sc, att_sc, u_sc, *, ts, n_past):
    @pl.when(pl.program_id(0) == 0)
    def _zero_padding():
        kta_sc[...] = jnp.zeros_like(kta_sc)
        ktn_sc[...] = jnp.zeros_like(ktn_sc)
        hb_sc[...] = jnp.zeros_like(hb_sc)

    h = _layer_norm(x_ref[...], ln_in_g[...], ln_in_b[...])
    hb_sc[0:ts, :] = h.astype(BF16)

    def proj(col, width):
        return jnp.dot(hb_sc[0:ts, :], _std_columns(w_lo, w_hi, col, width), preferred_element_type=F32)

    u_sc[0:SUBLANES, :] = jnp.zeros((SUBLANES, CONV_DIM), F32)
    u_sc[SUBLANES - (CONV_W - 1):SUBLANES, :] = conv0_ref[...]
    mix_a = _conv_branch(proj, u_sc, conv_w, conv_ref, w_pa, ts)

    def proj_t(row, height):
        return lax.dot_general(w_t[row:row + height, :], hb_sc[...], (((1,), (1,)), ((), ())),
                               preferred_element_type=F32)

    kt = proj_t(_T_K, ATT_DIM)
    vf_t = proj_t(_T_V, ATT_DIM + F_ROWS)
    vt = vf_t[0:ATT_DIM, :]
    kt_ref[...] = kt[:, :ts]
    vt_ref[...] = vt[:, :ts]
    logf_t = _log_sigmoid(vf_t[ATT_DIM:ATT_DIM + SUBLANES, :] + b_f_col[0:SUBLANES, :])
    lft_ref[...] = logf_t[:, :ts]

    c_past = _cumsum(plft_ref[...], 1)
    c_last = c_past[:, n_past - 1:n_past]
    c_new_t = _cumsum(logf_t, 1) + c_last
    past_pieces = [piece.astype(F32) for piece in _split3(c_past)]
    new_pieces = [piece.astype(F32) for piece in _split3(c_new_t)]
    for hd in range(N_HEADS):
        rows = slice(hd * HEAD_DIM, (hd + 1) * HEAD_DIM)
        _store_feature_major(kta_sc, hd, pkt_ref[rows, :].astype(BF16),
                             _extra_rows(past_pieces, hd, n_past, key_side=True))
        _store_feature_major(ktn_sc, hd, kt[rows, :].astype(BF16),
                             _extra_rows(new_pieces, hd, LANES, key_side=True))

    eye = (lax.broadcasted_iota(jnp.int32, (SUBLANES, LANES), 0)
           == lax.broadcasted_iota(jnp.int32, (SUBLANES, LANES), 1))
    c_last_row = jnp.sum(jnp.where(eye, jnp.broadcast_to(c_last, (SUBLANES, LANES)), 0.0),
                         axis=0, keepdims=True)
    logf = _log_sigmoid(jnp.dot(hb_sc[0:ts, :], w_fs[...], preferred_element_type=F32) + b_f_row[...])
    c_q = _cumsum(logf, 0) + c_last_row
    ex_q = _extra_lanes(c_q, route[:, :AUG_DIM]) + ones[:, :AUG_DIM]
    _store_token_major(qa_sc, slice(None), proj(_S_Q, ATT_DIM) * SCALE, ex_q)

    causal = (lax.broadcasted_iota(jnp.int32, (ts, LANES), 1)
              <= lax.broadcasted_iota(jnp.int32, (ts, LANES), 0))
    lane = lax.broadcasted_iota(jnp.int32, (ts, HEAD_BLOCK), 1)
    contract_lanes = (((1,), (1,)), ((), ()))
    scores = []
    for hd in range(N_HEADS):
        qa_h = qa_sc[:, hd * HEAD_BLOCK:(hd + 1) * HEAD_BLOCK]
        scores.append((jnp.dot(qa_h, kta_sc[hd], preferred_element_type=F32),
                       jnp.dot(qa_h, ktn_sc[hd], preferred_element_type=F32)))
    for pair in range(N_HEADS // 2):
        rows = slice(pair * HEAD_BLOCK, (pair + 1) * HEAD_BLOCK)
        vt_past = pvt_ref[rows, :].astype(BF16)
        vt_new = vt[rows, :].astype(BF16)
        outs = []
        for hd in (2 * pair, 2 * pair + 1):
            s_past = scores[hd][0]
            s_new = jnp.where(causal, scores[hd][1], -jnp.inf)
            m = jnp.maximum(jnp.max(s_past, axis=-1, keepdims=True), jnp.max(s_new, axis=-1, keepdims=True))
            p_past = jnp.exp(s_past - m)
            p_new = jnp.exp(s_new - m)
            l = jnp.sum(p_past, axis=-1, keepdims=True) + jnp.sum(p_new, axis=-1, keepdims=True)
            o = (lax.dot_general(p_past.astype(BF16), vt_past, contract_lanes, preferred_element_type=F32)
                 + lax.dot_general(p_new.astype(BF16), vt_new, contract_lanes, preferred_element_type=F32))
            outs.append(o / l)
        att_sc[:, rows] = jnp.where(lane < HEAD_DIM, outs[0], outs[1])

    y_ref[...] = _merge_and_norm(att_sc[...], mix_a, h, jax.nn.silu(proj(_S_ZB, ATT_DIM)),
                                 jax.nn.sigmoid(proj(_S_GM, D_MODEL)), w_pb, w_out, ln_g, ln_b)


def _run_sample(x, conv0, pkt, pvt, plft, params):
    bsz, ts, _ = x.shape
    n_past = pkt.shape[2]
    assert ts <= LANES and ts % SUBLANES == 0
    per_batch = lambda rows, width: pl.BlockSpec((None, rows, width), lambda b: (b, 0, 0))
    out_shape = (jax.ShapeDtypeStruct((bsz, ts, D_MODEL), F32),
                 jax.ShapeDtypeStruct((bsz, ATT_DIM, ts), F32),
                 jax.ShapeDtypeStruct((bsz, ATT_DIM, ts), F32),
                 jax.ShapeDtypeStruct((bsz, N_HEADS, ts), F32),
                 jax.ShapeDtypeStruct((bsz, CONV_W - 1, CONV_DIM), F32))
    out_specs = (per_batch(ts, D_MODEL), per_batch(ATT_DIM, ts), per_batch(ATT_DIM, ts),
                 per_batch(N_HEADS, ts), per_batch(CONV_W - 1, CONV_DIM))
    in_specs = [per_batch(ts, D_MODEL), per_batch(CONV_W - 1, CONV_DIM), per_batch(ATT_DIM, n_past),
                per_batch(ATT_DIM, n_past), per_batch(N_HEADS, n_past)]
    scratch = [
        pltpu.VMEM((N_HEADS, HEAD_BLOCK, n_past), BF16),
        pltpu.VMEM((N_HEADS, HEAD_BLOCK, LANES), BF16),
        pltpu.VMEM((LANES, D_MODEL), BF16),
        pltpu.VMEM((ts, AUG_DIM), BF16),
        pltpu.VMEM((ts, ATT_DIM), F32),
        pltpu.VMEM((SUBLANES + ts, CONV_DIM), F32),
    ]
    return pl.pallas_call(
        functools.partial(_sample_kernel, ts=ts, n_past=n_past),
        grid=(bsz,),
        in_specs=in_specs + [_resident(p.shape) for p in params],
        out_specs=out_specs,
        out_shape=out_shape,
        scratch_shapes=scratch,
        compiler_params=pltpu.CompilerParams(
            dimension_semantics=("arbitrary",),
            vmem_limit_bytes=VMEM_LIMIT_BYTES),
        name="trunk_sample",
    )(x, conv0, pkt, pvt, plft, *params)


def kernel(x_prompt, x_sample, cache_k, cache_v, cache_logf, state_conv, ln_in_g, ln_in_b, w_in, b_f,
           conv_w, w_proj_a, w_proj_b, w_out, ln_g, ln_b):
    assert w_in.shape[0] == DEPTH
    route_np, ones_np = _routing_constants()
    w = w_in[0]
    row = lambda a: a.reshape(1, -1).astype(F32)
    w_t = jnp.concatenate([w[:, _W_Q:_W_F + N_HEADS],
                           jnp.zeros((D_MODEL, F_ROWS - N_HEADS), w.dtype)], axis=1).T.astype(BF16)
    tm = PROMPT_TILE
    p = dict(
        ln_in_g=row(ln_in_g), ln_in_b=row(ln_in_b),
        w_lo=w[:, :_W_V].astype(BF16), w_hi=w[:, _W_ZB:].astype(BF16),
        w_fs=jnp.pad(w[:, _W_F:_W_F + N_HEADS], ((0, 0), (0, LANES - N_HEADS))).astype(BF16),
        b_f_row=jnp.pad(row(b_f[0]), ((0, 0), (0, LANES - N_HEADS))),
        w_t=w_t,
        b_f_col=jnp.pad(b_f[0].astype(F32).reshape(N_HEADS, 1), ((0, F_ROWS - N_HEADS), (0, 0))),
        conv_w=conv_w[0].astype(F32),
        w_pa=w_proj_a[0].astype(BF16), w_pb=w_proj_b[0].astype(BF16), w_out=w_out[0].astype(BF16),
        ln_g=row(ln_g[0]), ln_b=row(ln_b[0]),
        route=jnp.asarray(route_np, BF16), ones=jnp.asarray(ones_np, F32),
        tri=jnp.asarray(np.triu(np.ones((tm, tm), np.float32)), BF16),
    )
    bs = x_sample.shape[0]
    n_past = cache_k.shape[2]
    feature_major = lambda a: a.transpose(0, 2, 3, 1).reshape(bs, ATT_DIM, n_past)

    y_p, kt_p, vt_p, lft_p, cv_p = _run_prompt(x_prompt, [p[name] for name in PROMPT_PARAMS], tm=tm)
    y_s, kt_s, vt_s, lft_s, cv_s = _run_sample(
        x_sample, state_conv[0], feature_major(cache_k[0]), feature_major(cache_v[0]),
        cache_logf[0].transpose(0, 2, 1), [p[name] for name in SAMPLE_PARAMS])

    heads_t = lambda a: a.reshape(a.shape[0], N_HEADS, HEAD_DIM, a.shape[2]).transpose(0, 3, 1, 2)[None]
    tokens_t = lambda a: a.transpose(0, 2, 1)[None]
    return (y_p, y_s, heads_t(kt_p), heads_t(vt_p), tokens_t(lft_p), cv_p[None],
            heads_t(kt_s), heads_t(vt_s), tokens_t(lft_s), cv_s[None])
```

```python
import functools

import numpy as np
import jax
import jax.numpy as jnp
from jax import lax
from jax.experimental import pallas as pl
from jax.experimental.pallas import tpu as pltpu

D_MODEL = 1024
CONV_DIM = 512
CONV_W = 3
N_HEADS = 8
HEAD_DIM = 64
ATT_DIM = N_HEADS * HEAD_DIM
LN_EPS = 1e-5
DEPTH = 1
ALPHA = (2.0 * DEPTH) ** 0.25
SCALE = HEAD_DIM ** -0.5

LANES = 128
SUBLANES = 8
HEAD_BLOCK = LANES
AUG_DIM = N_HEADS * HEAD_BLOCK
N_SPLIT = 3
ONES_LANE = N_SPLIT * N_HEADS
EXTRA_ROWS = 16
F_ROWS = 16
VMEM_LIMIT_BYTES = 56 * 1024 * 1024
V_ROWS = HEAD_DIM + 16
LOG2E = 1.4426950408889634
PIPE_SLOTS = 3
TAIL_SHARE = 1
SCORE_LOOKAHEAD = 4

_W_Q = 4 * CONV_DIM
_W_V = _W_Q + 2 * ATT_DIM
_W_F = _W_Q + 3 * ATT_DIM
_W_ZB = _W_F + N_HEADS
_S_XA, _S_GB, _S_GC, _S_ZA, _S_Q, _S_K, _S_ZB, _S_GA, _S_GM = (
    0, 512, 1024, 1536, 2048, 2560, 3072, 3584, 4608)
_T_Q, _T_K, _T_V, _T_F = 0, 512, 1024, 1536

PROMPT_TILE = 256
PROMPT_PARAMS = ("ln_in_g", "ln_in_b", "w_lo", "w_hi", "w_t", "b_f_col", "conv_w", "w_pa", "w_pb", "w_out",
                 "ln_g", "ln_b", "route", "tri")
SAMPLE_PARAMS = ("ln_in_g", "ln_in_b", "w_lo", "w_hi", "w_fs", "b_f_row", "w_t", "b_f_col", "conv_w", "w_pa",
                 "w_pb", "w_out", "ln_g", "ln_b", "route", "ones")

F32 = jnp.float32
BF16 = jnp.bfloat16


def _routing_constants():
    route = np.zeros((LANES, 2 * AUG_DIM), np.float32)
    ones = np.zeros((1, 2 * AUG_DIM), np.float32)
    for h in range(N_HEADS):
        base = h * HEAD_BLOCK + (HEAD_DIM if h % 2 == 0 else 0)
        for r in range(N_SPLIT):
            route[r * N_HEADS + h, base + 2 * r] = 1.0
            ones[0, base + 2 * r + 1] = 1.0
            ones[0, AUG_DIM + base + 2 * r] = 1.0
            route[r * N_HEADS + h, AUG_DIM + base + 2 * r + 1] = -1.0
    route[ONES_LANE, :] = ones[0, :]
    return route, ones


def _layer_norm(x, g, b):
    mu = jnp.mean(x, axis=-1, keepdims=True)
    xc = x - mu
    var = jnp.mean(xc * xc, axis=-1, keepdims=True)
    return xc * lax.rsqrt(var + LN_EPS) * g + b


def _log_sigmoid(x):
    return jnp.minimum(x, 0.0) - jnp.log1p(jnp.exp(-jnp.abs(x)))


def _split3(x):
    hi = x.astype(BF16)
    r1 = x - hi.astype(F32)
    mid = r1.astype(BF16)
    lo = (r1 - mid.astype(F32)).astype(BF16)
    return hi, mid, lo


def _cumsum(x, axis):
    n = x.shape[axis]
    idx = lax.broadcasted_iota(jnp.int32, x.shape, axis)
    s = 1
    while s < n:
        x = x + jnp.where(idx >= s, pltpu.roll(x, s, axis), 0.0)
        s *= 2
    return x


def _extra_lanes(c, route):
    lane = lax.broadcasted_iota(jnp.int32, c.shape, 1)
    valid = lane < N_HEADS
    packed = jnp.zeros_like(c)
    for r, piece in enumerate(_split3(c)):
        p32 = jnp.where(valid, piece.astype(F32), 0.0)
        packed = packed + (pltpu.roll(p32, r * N_HEADS, 1) if r else p32)
    return jnp.dot(packed.astype(BF16), route, preferred_element_type=F32)


def _store_token_major(dst_ref, rows, data, extras):
    n = data.shape[0]
    lane = lax.broadcasted_iota(jnp.int32, (n, HEAD_BLOCK), 1)
    for h in range(N_HEADS):
        pair = data[:, (h // 2) * HEAD_BLOCK:(h // 2 + 1) * HEAD_BLOCK]
        ex = extras[:, h * HEAD_BLOCK:(h + 1) * HEAD_BLOCK]
        keep = (lane < HEAD_DIM) if h % 2 == 0 else (lane >= HEAD_DIM)
        dst_ref[rows, h * HEAD_BLOCK:(h + 1) * HEAD_BLOCK] = jnp.where(keep, pair, ex).astype(BF16)


def _extra_rows(pieces, h, n, key_side):
    r = lax.broadcasted_iota(jnp.int32, (EXTRA_ROWS, n), 0)
    value_row, sign = (1, -1.0) if key_side else (0, 1.0)
    out = jnp.where((r < 2 * N_SPLIT) & ((r & 1) != value_row), 1.0, 0.0)
    for i, piece in enumerate(pieces):
        out = jnp.where(r == 2 * i + value_row,
                        jnp.broadcast_to(sign * piece[h:h + 1, :], (EXTRA_ROWS, n)), out)
    return out.astype(BF16)


def _store_feature_major(dst_ref, h, data, extra):
    if h % 2 == 0:
        dst_ref[h, 0:HEAD_DIM, :] = data
        dst_ref[h, HEAD_DIM:HEAD_DIM + EXTRA_ROWS, :] = extra
    else:
        dst_ref[h, 0:EXTRA_ROWS, :] = extra
        dst_ref[h, HEAD_DIM:2 * HEAD_DIM, :] = data


def _std_columns(w_lo, w_hi, col, width):
    split = w_lo.shape[1]
    assert col + width <= split or col >= split
    return w_lo[:, col:col + width] if col < split else w_hi[:, col - split:col - split + width]


def _conv_branch(proj, u_sc, conv_w, conv_ref, w_pa, n):
    first = SUBLANES - (CONV_W - 1)
    u_sc[SUBLANES:SUBLANES + n, :] = proj(_S_GC, CONV_DIM) * proj(_S_XA, CONV_DIM)
    conv = sum(u_sc[first + i:first + i + n, :] * conv_w[i:i + 1, :] for i in range(CONV_W))
    new_conv = u_sc[first + n:SUBLANES + n, :]
    conv_ref[...] = new_conv
    u_sc[first:SUBLANES, :] = new_conv
    y_a = proj(_S_GB, CONV_DIM) * conv * jax.nn.silu(proj(_S_ZA, CONV_DIM))
    return jax.nn.sigmoid(proj(_S_GA, D_MODEL)) * jnp.dot(
        y_a.astype(BF16), w_pa[...], preferred_element_type=F32)


def _merge_and_norm(att, mix_a, h, silu_zb, gate_b, w_pb, w_out, ln_g, ln_b):
    y_b = att * silu_zb
    mixed = mix_a + gate_b * jnp.dot(
        y_b.astype(BF16), w_pb[...], preferred_element_type=F32)
    sub = jnp.dot(mixed.astype(BF16), w_out[...], preferred_element_type=F32)
    return _layer_norm(ALPHA * h + sub, ln_g[...], ln_b[...])


def _attention_t(j_lo, j_hi, j, tm, kaug_sc, qta_sc, vt_sc, attt_sc):
    n_key_tiles = j_hi + 1
    keys = n_key_tiles * tm
    key_in_tile = lax.broadcasted_iota(jnp.int32, (tm, tm), 0)
    query_in_tile = lax.broadcasted_iota(jnp.int32, (tm, tm), 1)
    tile_of_query = j if j_hi > j_lo else j_lo
    visible = {i: key_in_tile + (i - tile_of_query) * tm <= query_in_tile for i in range(j_lo, n_key_tiles)}

    def scores(hd):
        return jnp.dot(kaug_sc[0:keys, hd * HEAD_BLOCK:(hd + 1) * HEAD_BLOCK], qta_sc[hd],
                       preferred_element_type=F32)

    pending = [scores(hd) for hd in range(SCORE_LOOKAHEAD)]
    for hd in range(N_HEADS):
        s = pending.pop(0)
        if hd + SCORE_LOOKAHEAD < N_HEADS:
            pending.append(scores(hd + SCORE_LOOKAHEAD))
        blocks = [s[i * tm:(i + 1) * tm, :] for i in range(n_key_tiles)]
        for i, mask in visible.items():
            blocks[i] = jnp.where(mask, blocks[i], -jnp.inf)
        m = functools.reduce(jnp.maximum, [jnp.max(blk, axis=0, keepdims=True) for blk in blocks])
        o = jnp.zeros((V_ROWS, tm), F32)
        for i, blk in enumerate(blocks):
            o = o + jnp.dot(vt_sc[i, hd * V_ROWS:(hd + 1) * V_ROWS, :], jnp.exp2(blk - m).astype(BF16),
                            preferred_element_type=F32)
        attt_sc[hd * HEAD_DIM:(hd + 1) * HEAD_DIM, :] = o[0:HEAD_DIM, :] / o[HEAD_DIM:HEAD_DIM + 1, :]


def _prompt_kernel(x_ref, ln_in_g, ln_in_b, w_lo, w_hi, w_t, b_f_col, conv_w, w_pa, w_pb, w_out,
                   ln_g, ln_b, route, tri,
                   y_ref, kt_ref, vt_ref, lft_ref, conv_ref,
                   kaug_sc, vt_sc, h_sc, hb_sc, z_sc, qta_sc, attt_sc, zb_sc, gm_sc, u_sc, carryt_sc,
                   *, tm, n_tiles, n_total):
    t = pl.program_id(0)

    def norm_in():
        h = _layer_norm(x_ref[...], ln_in_g[...], ln_in_b[...])
        h_sc[lax.rem(t, PIPE_SLOTS)] = h
        hb_sc[lax.rem(t, PIPE_SLOTS)] = h.astype(BF16)

    def norm_out():
        y_ref[...] = _layer_norm(z_sc[...], ln_g[...], ln_b[...])

    @pl.when(t == 0)
    def _fill():
        for ref in (z_sc, kaug_sc, vt_sc, h_sc, hb_sc, attt_sc, zb_sc, gm_sc, u_sc, qta_sc, carryt_sc):
            ref[...] = jnp.zeros_like(ref)
        ones_row = jnp.where(lax.broadcasted_iota(jnp.int32, (V_ROWS - HEAD_DIM, tm), 0) == 0, 1.0, 0.0)
        for i in range(n_tiles):
            for hd in range(N_HEADS):
                vt_sc[i, hd * V_ROWS + HEAD_DIM:(hd + 1) * V_ROWS, :] = ones_row.astype(BF16)
        norm_in()

    @pl.when((t >= 1) & (t <= n_total + 1))
    def _mixers():
        tile_a = t - 1
        tile_b = t - 2
        ja = lax.rem(tile_a, n_tiles)
        jb = lax.rem(tile_b, n_tiles)
        slot_a = lax.rem(tile_a, PIPE_SLOTS)
        slot_b = lax.rem(t + 1, PIPE_SLOTS)
        gate_a = tile_a & 1
        gate_b = t & 1

        @pl.when(ja == 0)
        def _start_of_sequence():
            carryt_sc[...] = jnp.zeros_like(carryt_sc)

        @pl.when(jb == 0)
        def _start_of_conv():
            u_sc[0:SUBLANES, :] = jnp.zeros((SUBLANES, CONV_DIM), F32)

        spans = [(jj, jj) for jj in range(n_tiles - TAIL_SHARE)] + [(n_tiles - TAIL_SHARE, n_tiles - 1)]
        for j_lo, j_hi in spans:
            pl.when((t >= 2) & (jb >= j_lo) & (jb <= j_hi))(functools.partial(
                _attention_t, j_lo, j_hi, jb, tm, kaug_sc, qta_sc, vt_sc, attt_sc))

        def proj(slot, col, width):
            return jnp.dot(hb_sc[slot], _std_columns(w_lo, w_hi, col, width), preferred_element_type=F32)

        def proj_t(row, height):
            return lax.dot_general(w_t[row:row + height, :], hb_sc[slot_a], (((1,), (1,)), ((), ())),
                                   preferred_element_type=F32)

        norm_out()

        vf_t = proj_t(_T_V, ATT_DIM + F_ROWS)
        qk_t = proj_t(_T_Q, 2 * ATT_DIM)
        kt = qk_t[ATT_DIM:2 * ATT_DIM, :]
        vt = vf_t[0:ATT_DIM, :]
        kt_ref[...] = kt
        vt_ref[...] = vt
        logf_t = _log_sigmoid(vf_t[ATT_DIM:ATT_DIM + F_ROWS, :] + b_f_col[...])
        lft_ref[...] = logf_t[:N_HEADS, :]
        sums = jnp.dot(jnp.concatenate(_split3(logf_t), axis=0), tri[...], preferred_element_type=F32)

        mix_a = _conv_branch(functools.partial(proj, slot_b), u_sc, conv_w, conv_ref, w_pa, tm)

        zb_sc[gate_a] = jax.nn.silu(proj(slot_a, _S_ZB, ATT_DIM))
        gm_sc[gate_a] = jax.nn.sigmoid(proj(slot_a, _S_GM, D_MODEL))

        y_b = attt_sc[...].T * zb_sc[gate_b]
        mixed = mix_a + gm_sc[gate_b] * jnp.dot(y_b.astype(BF16), w_pb[...], preferred_element_type=F32)

        c_t = sum(sums[r * F_ROWS:(r + 1) * F_ROWS, :] for r in range(N_SPLIT)) + carryt_sc[:, 0:1]
        carryt_sc[...] = jnp.broadcast_to(c_t[:, tm - 1:tm], carryt_sc.shape)
        c_pieces = [piece.astype(F32) for piece in _split3(c_t * LOG2E)]
        qt = qk_t[0:ATT_DIM, :] * (SCALE * LOG2E)
        for hd in range(N_HEADS):
            _store_feature_major(qta_sc, hd, qt[hd * HEAD_DIM:(hd + 1) * HEAD_DIM, :].astype(BF16),
                                 _extra_rows(c_pieces, hd, tm, key_side=False))
            vt_sc[ja, hd * V_ROWS:hd * V_ROWS + HEAD_DIM, :] = (
                vt[hd * HEAD_DIM:(hd + 1) * HEAD_DIM, :].astype(BF16))
        ones_and_pad = jnp.where(lax.broadcasted_iota(jnp.int32, (LANES - ONES_LANE, tm), 0) == 0, 1.0, 0.0)
        packed_t = jnp.concatenate([piece[0:N_HEADS, :] for piece in c_pieces] + [ones_and_pad], axis=0)
        ex_k = jnp.dot(packed_t.T.astype(BF16), route[:, AUG_DIM:], preferred_element_type=F32)
        _store_token_major(kaug_sc, pl.ds(pl.multiple_of(ja * tm, tm), tm), kt.T, ex_k)

        z_sc[...] = ALPHA * h_sc[slot_b] + jnp.dot(mixed.astype(BF16), w_out[...], preferred_element_type=F32)

        norm_in()

    @pl.when(t == n_total + 2)
    def _drain():
        norm_out()


def _resident(shape):
    return pl.BlockSpec(shape, lambda *_: (0,) * len(shape), pipeline_mode=pl.Buffered(1))


def _run_prompt(x, params, *, tm):
    bsz, t, _ = x.shape
    n_tiles = t // tm
    assert n_tiles * tm == t
    n_total = bsz * n_tiles
    tile = lambda s, lag: jnp.clip(s - lag, 0, n_total - 1)
    per_tile = lambda width, lag: pl.BlockSpec(
        (None, tm, width), lambda s: (tile(s, lag) // n_tiles, tile(s, lag) % n_tiles, 0))
    per_tile_t = lambda rows: pl.BlockSpec(
        (None, rows, tm), lambda s: (tile(s, 1) // n_tiles, 0, tile(s, 1) % n_tiles))
    out_shape = (jax.ShapeDtypeStruct((bsz, t, D_MODEL), F32),
                 jax.ShapeDtypeStruct((bsz, ATT_DIM, t), F32),
                 jax.ShapeDtypeStruct((bsz, ATT_DIM, t), F32),
                 jax.ShapeDtypeStruct((bsz, N_HEADS, t), F32),
                 jax.ShapeDtypeStruct((bsz, CONV_W - 1, CONV_DIM), F32))
    out_specs = (per_tile(D_MODEL, 3), per_tile_t(ATT_DIM), per_tile_t(ATT_DIM), per_tile_t(N_HEADS),
                 pl.BlockSpec((None, CONV_W - 1, CONV_DIM), lambda s: (tile(s, 2) // n_tiles, 0, 0)))
    scratch = [
        pltpu.VMEM((t, AUG_DIM), BF16),
        pltpu.VMEM((n_tiles, N_HEADS * V_ROWS, tm), BF16),
        pltpu.VMEM((PIPE_SLOTS, tm, D_MODEL), F32),
        pltpu.VMEM((PIPE_SLOTS, tm, D_MODEL), BF16),
        pltpu.VMEM((tm, D_MODEL), F32),
        pltpu.VMEM((N_HEADS, HEAD_BLOCK, tm), BF16),
        pltpu.VMEM((ATT_DIM, tm), F32),
        pltpu.VMEM((2, tm, ATT_DIM), F32),
        pltpu.VMEM((2, tm, D_MODEL), F32),
        pltpu.VMEM((SUBLANES + tm, CONV_DIM), F32),
        pltpu.VMEM((F_ROWS, LANES), F32),
    ]
    return pl.pallas_call(
        functools.partial(_prompt_kernel, tm=tm, n_tiles=n_tiles, n_total=n_total),
        grid=(n_total + PIPE_SLOTS,),
        in_specs=[per_tile(D_MODEL, 0)] + [_resident(p.shape) for p in params],
        out_specs=out_specs,
        out_shape=out_shape,
        scratch_shapes=scratch,
        compiler_params=pltpu.CompilerParams(
            dimension_semantics=("arbitrary",),
            vmem_limit_bytes=VMEM_LIMIT_BYTES),
        name="trunk_prompt",
    )(x, *params)


def _sample_kernel(x_ref, conv0_ref, pkt_ref, pvt_ref, plft_ref,
                   ln_in_g, ln_in_b, w_lo, w_hi, w_fs, b_f_row, w_t, b_f_col, conv_w, w_pa, w_pb, w_out,
                   ln_g, ln_b, route, ones,
                   y_ref, kt_ref, vt_ref, lft_ref, conv_ref,
                   kta_sc, ktn_sc, hb_sc, qa_sc, att_sc, u_sc, *, ts, n_past):
    @pl.when(pl.program_id(0) == 0)
    def _zero_padding():
        kta_sc[...] = jnp.zeros_like(kta_sc)
        ktn_sc[...] = jnp.zeros_like(ktn_sc)
        hb_sc[...] = jnp.zeros_like(hb_sc)

    h = _layer_norm(x_ref[...], ln_in_g[...], ln_in_b[...])
    hb_sc[0:ts, :] = h.astype(BF16)

    def proj(col, width):
        return jnp.dot(hb_sc[0:ts, :], _std_columns(w_lo, w_hi, col, width), preferred_element_type=F32)

    u_sc[0:SUBLANES, :] = jnp.zeros((SUBLANES, CONV_DIM), F32)
    u_sc[SUBLANES - (CONV_W - 1):SUBLANES, :] = conv0_ref[...]
    mix_a = _conv_branch(proj, u_sc, conv_w, conv_ref, w_pa, ts)

    def proj_t(row, height):
        return lax.dot_general(w_t[row:row + height, :], hb_sc[...], (((1,), (1,)), ((), ())),
                               preferred_element_type=F32)

    kt = proj_t(_T_K, ATT_DIM)
    vf_t = proj_t(_T_V, ATT_DIM + F_ROWS)
    vt = vf_t[0:ATT_DIM, :]
    kt_ref[...] = kt[:, :ts]
    vt_ref[...] = vt[:, :ts]
    logf_t = _log_sigmoid(vf_t[ATT_DIM:ATT_DIM + SUBLANES, :] + b_f_col[0:SUBLANES, :])
    lft_ref[...] = logf_t[:, :ts]

    c_past = _cumsum(plft_ref[...], 1)
    c_last = c_past[:, n_past - 1:n_past]
    c_new_t = _cumsum(logf_t, 1) + c_last
    past_pieces = [piece.astype(F32) for piece in _split3(c_past)]
    new_pieces = [piece.astype(F32) for piece in _split3(c_new_t)]
    for hd in range(N_HEADS):
        rows = slice(hd * HEAD_DIM, (hd + 1) * HEAD_DIM)
        _store_feature_major(kta_sc, hd, pkt_ref[rows, :].astype(BF16),
                             _extra_rows(past_pieces, hd, n_past, key_side=True))
        _store_feature_major(ktn_sc, hd, kt[rows, :].astype(BF16),
                             _extra_rows(new_pieces, hd, LANES, key_side=True))

    eye = (lax.broadcasted_iota(jnp.int32, (SUBLANES, LANES), 0)
           == lax.broadcasted_iota(jnp.int32, (SUBLANES, LANES), 1))
    c_last_row = jnp.sum(jnp.where(eye, jnp.broadcast_to(c_last, (SUBLANES, LANES)), 0.0),
                         axis=0, keepdims=True)
    logf = _log_sigmoid(jnp.dot(hb_sc[0:ts, :], w_fs[...], preferred_element_type=F32) + b_f_row[...])
    c_q = _cumsum(logf, 0) + c_last_row
    ex_q = _extra_lanes(c_q, route[:, :AUG_DIM]) + ones[:, :AUG_DIM]
    _store_token_major(qa_sc, slice(None), proj(_S_Q, ATT_DIM) * SCALE, ex_q)

    causal = (lax.broadcasted_iota(jnp.int32, (ts, LANES), 1)
              <= lax.broadcasted_iota(jnp.int32, (ts, LANES), 0))
    lane = lax.broadcasted_iota(jnp.int32, (ts, HEAD_BLOCK), 1)
    contract_lanes = (((1,), (1,)), ((), ()))
    scores = []
    for hd in range(N_HEADS):
        qa_h = qa_sc[:, hd * HEAD_BLOCK:(hd + 1) * HEAD_BLOCK]
        scores.append((jnp.dot(qa_h, kta_sc[hd], preferred_element_type=F32),
                       jnp.dot(qa_h, ktn_sc[hd], preferred_element_type=F32)))
    for pair in range(N_HEADS // 2):
        rows = slice(pair * HEAD_BLOCK, (pair + 1) * HEAD_BLOCK)
        vt_past = pvt_ref[rows, :].astype(BF16)
        vt_new = vt[rows, :].astype(BF16)
        probs, sums = [], []
        for hd in (2 * pair, 2 * pair + 1):
            s_past = scores[hd][0]
            s_new = jnp.where(causal, scores[hd][1], -jnp.inf)
            m = jnp.maximum(jnp.max(s_past, axis=-1, keepdims=True), jnp.max(s_new, axis=-1, keepdims=True))
            p_past = jnp.exp(s_past - m)
            p_new = jnp.exp(s_new - m)
            sums.append(jnp.sum(p_past, axis=-1, keepdims=True) + jnp.sum(p_new, axis=-1, keepdims=True))
            probs.append((p_past.astype(BF16), p_new.astype(BF16)))
        o = (lax.dot_general(jnp.concatenate([probs[0][0], probs[1][0]], axis=0), vt_past, contract_lanes,
                             preferred_element_type=F32)
             + lax.dot_general(jnp.concatenate([probs[0][1], probs[1][1]], axis=0), vt_new, contract_lanes,
                               preferred_element_type=F32))
        att_sc[:, rows] = jnp.where(lane < HEAD_DIM, o[0:ts, :] / sums[0], o[ts:2 * ts, :] / sums[1])

    y_ref[...] = _merge_and_norm(att_sc[...], mix_a, h, jax.nn.silu(proj(_S_ZB, ATT_DIM)),
                                 jax.nn.sigmoid(proj(_S_GM, D_MODEL)), w_pb, w_out, ln_g, ln_b)


def _run_sample(x, conv0, pkt, pvt, plft, params):
    bsz, ts, _ = x.shape
    n_past = pkt.shape[2]
    assert ts <= LANES and ts % SUBLANES == 0
    per_batch = lambda rows, width: pl.BlockSpec((None, rows, width), lambda b: (b, 0, 0))
    out_shape = (jax.ShapeDtypeStruct((bsz, ts, D_MODEL), F32),
                 jax.ShapeDtypeStruct((bsz, ATT_DIM, ts), F32),
                 jax.ShapeDtypeStruct((bsz, ATT_DIM, ts), F32),
                 jax.ShapeDtypeStruct((bsz, N_HEADS, ts), F32),
                 jax.ShapeDtypeStruct((bsz, CONV_W - 1, CONV_DIM), F32))
    out_specs = (per_batch(ts, D_MODEL), per_batch(ATT_DIM, ts), per_batch(ATT_DIM, ts),
                 per_batch(N_HEADS, ts), per_batch(CONV_W - 1, CONV_DIM))
    in_specs = [per_batch(ts, D_MODEL), per_batch(CONV_W - 1, CONV_DIM), per_batch(ATT_DIM, n_past),
                per_batch(ATT_DIM, n_past), per_batch(N_HEADS, n_past)]
    scratch = [
        pltpu.VMEM((N_HEADS, HEAD_BLOCK, n_past), BF16),
        pltpu.VMEM((N_HEADS, HEAD_BLOCK, LANES), BF16),
        pltpu.VMEM((LANES, D_MODEL), BF16),
        pltpu.VMEM((ts, AUG_DIM), BF16),
        pltpu.VMEM((ts, ATT_DIM), F32),
        pltpu.VMEM((SUBLANES + ts, CONV_DIM), F32),
    ]
    return pl.pallas_call(
        functools.partial(_sample_kernel, ts=ts, n_past=n_past),
        grid=(bsz,),
        in_specs=in_specs + [_resident(p.shape) for p in params],
        out_specs=out_specs,
        out_shape=out_shape,
        scratch_shapes=scratch,
        compiler_params=pltpu.CompilerParams(
            dimension_semantics=("arbitrary",),
            vmem_limit_bytes=VMEM_LIMIT_BYTES),
        name="trunk_sample",
    )(x, conv0, pkt, pvt, plft, *params)


def kernel(x_prompt, x_sample, cache_k, cache_v, cache_logf, state_conv, ln_in_g, ln_in_b, w_in, b_f,
           conv_w, w_proj_a, w_proj_b, w_out, ln_g, ln_b):
    assert w_in.shape[0] == DEPTH
    route_np, ones_np = _routing_constants()
    w = w_in[0]
    row = lambda a: a.reshape(1, -1).astype(F32)
    w_t = jnp.concatenate([w[:, _W_Q:_W_F + N_HEADS],
                           jnp.zeros((D_MODEL, F_ROWS - N_HEADS), w.dtype)], axis=1).T.astype(BF16)
    tm = PROMPT_TILE
    p = dict(
        ln_in_g=row(ln_in_g), ln_in_b=row(ln_in_b),
        w_lo=w[:, :_W_V].astype(BF16), w_hi=w[:, _W_ZB:].astype(BF16),
        w_fs=jnp.pad(w[:, _W_F:_W_F + N_HEADS], ((0, 0), (0, LANES - N_HEADS))).astype(BF16),
        b_f_row=jnp.pad(row(b_f[0]), ((0, 0), (0, LANES - N_HEADS))),
        w_t=w_t,
        b_f_col=jnp.pad(b_f[0].astype(F32).reshape(N_HEADS, 1), ((0, F_ROWS - N_HEADS), (0, 0))),
        conv_w=conv_w[0].astype(F32),
        w_pa=w_proj_a[0].astype(BF16), w_pb=w_proj_b[0].astype(BF16), w_out=w_out[0].astype(BF16),
        ln_g=row(ln_g[0]), ln_b=row(ln_b[0]),
        route=jnp.asarray(route_np, BF16), ones=jnp.asarray(ones_np, F32),
        tri=jnp.asarray(np.triu(np.ones((tm, tm), np.float32)), BF16),
    )
    bs = x_sample.shape[0]
    n_past = cache_k.shape[2]
    feature_major = lambda a: a.transpose(0, 2, 3, 1).reshape(bs, ATT_DIM, n_past)

    y_p, kt_p, vt_p, lft_p, cv_p = _run_prompt(x_prompt, [p[name] for name in PROMPT_PARAMS], tm=tm)
    y_s, kt_s, vt_s, lft_s, cv_s = _run_sample(
        x_sample, state_conv[0], feature_major(cache_k[0]), feature_major(cache_v[0]),
        cache_logf[0].transpose(0, 2, 1), [p[name] for name in SAMPLE_PARAMS])

    heads_t = lambda a: a.reshape(a.shape[0], N_HEADS, HEAD_DIM, a.shape[2]).transpose(0, 3, 1, 2)[None]
    tokens_t = lambda a: a.transpose(0, 2, 1)[None]
    return (y_p, y_s, heads_t(kt_p), heads_t(vt_p), tokens_t(lft_p), cv_p[None],
            heads_t(kt_s), heads_t(vt_s), tokens_t(lft_s), cv_s[None])
```

```python
import functools

import numpy as np
import jax
import jax.numpy as jnp
from jax import lax
from jax.experimental import pallas as pl
from jax.experimental.pallas import tpu as pltpu

D_MODEL = 1024
CONV_DIM = 512
CONV_W = 3
N_HEADS = 8
HEAD_DIM = 64
ATT_DIM = N_HEADS * HEAD_DIM
LN_EPS = 1e-5
DEPTH = 1
ALPHA = (2.0 * DEPTH) ** 0.25
SCALE = HEAD_DIM ** -0.5

LANES = 128
SUBLANES = 8
HEAD_BLOCK = LANES
AUG_DIM = N_HEADS * HEAD_BLOCK
N_SPLIT = 3
ONES_LANE = N_SPLIT * N_HEADS
BF16_ROWS = 2 * SUBLANES
EXTRA_ROWS = BF16_ROWS
F_ROWS = BF16_ROWS
VMEM_LIMIT_BYTES = 56 * 1024 * 1024
V_ROWS = HEAD_DIM + BF16_ROWS
LOG2E = 1.4426950408889634
PIPE_SLOTS = 3
TAIL_SHARE = 1
SCORE_LOOKAHEAD = 4

_W_Q = 4 * CONV_DIM
_W_V = _W_Q + 2 * ATT_DIM
_W_F = _W_Q + 3 * ATT_DIM
_W_ZB = _W_F + N_HEADS
_S_XA, _S_GB, _S_GC, _S_ZA = (i * CONV_DIM for i in range(4))
_S_Q = 4 * CONV_DIM
_S_K = _S_Q + ATT_DIM
_S_ZB = _S_K + ATT_DIM
_S_GA = _S_ZB + ATT_DIM
_S_GM = _S_GA + D_MODEL
_T_Q, _T_K, _T_V, _T_F = (i * ATT_DIM for i in range(4))

PROMPT_TILE = 256
PROMPT_PARAMS = ("ln_in_g", "ln_in_b", "w_lo", "w_hi", "w_t", "b_f_col", "conv_w", "w_pa", "w_pb", "w_out",
                 "ln_g", "ln_b", "route", "tri")
SAMPLE_PARAMS = ("ln_in_g", "ln_in_b", "w_lo", "w_hi", "w_fs", "b_f_row", "w_t", "b_f_col", "conv_w", "w_pa",
                 "w_pb", "w_out", "ln_g", "ln_b", "route", "ones")

F32 = jnp.float32
BF16 = jnp.bfloat16


def _routing_constants():
    route = np.zeros((LANES, 2 * AUG_DIM), np.float32)
    ones = np.zeros((1, 2 * AUG_DIM), np.float32)
    for h in range(N_HEADS):
        base = h * HEAD_BLOCK + (HEAD_DIM if h % 2 == 0 else 0)
        for r in range(N_SPLIT):
            route[r * N_HEADS + h, base + 2 * r] = 1.0
            ones[0, base + 2 * r + 1] = 1.0
            ones[0, AUG_DIM + base + 2 * r] = 1.0
            route[r * N_HEADS + h, AUG_DIM + base + 2 * r + 1] = -1.0
    route[ONES_LANE, :] = ones[0, :]
    return route, ones


def _layer_norm(x, g, b):
    mu = jnp.mean(x, axis=-1, keepdims=True)
    xc = x - mu
    var = jnp.mean(xc * xc, axis=-1, keepdims=True)
    return xc * lax.rsqrt(var + LN_EPS) * g + b


def _log_sigmoid(x):
    return jnp.minimum(x, 0.0) - jnp.log1p(jnp.exp(-jnp.abs(x)))


def _split3(x):
    hi = x.astype(BF16)
    r1 = x - hi.astype(F32)
    mid = r1.astype(BF16)
    lo = (r1 - mid.astype(F32)).astype(BF16)
    return hi, mid, lo


def _cumsum(x, axis):
    n = x.shape[axis]
    idx = lax.broadcasted_iota(jnp.int32, x.shape, axis)
    s = 1
    while s < n:
        x = x + jnp.where(idx >= s, pltpu.roll(x, s, axis), 0.0)
        s *= 2
    return x


def _extra_lanes(c, route):
    lane = lax.broadcasted_iota(jnp.int32, c.shape, 1)
    valid = lane < N_HEADS
    packed = jnp.zeros_like(c)
    for r, piece in enumerate(_split3(c)):
        p32 = jnp.where(valid, piece.astype(F32), 0.0)
        packed = packed + (pltpu.roll(p32, r * N_HEADS, 1) if r else p32)
    return jnp.dot(packed.astype(BF16), route, preferred_element_type=F32)


def _store_token_major(dst_ref, rows, data, extras):
    n = data.shape[0]
    lane = lax.broadcasted_iota(jnp.int32, (n, HEAD_BLOCK), 1)
    for h in range(N_HEADS):
        pair = data[:, (h // 2) * HEAD_BLOCK:(h // 2 + 1) * HEAD_BLOCK]
        ex = extras[:, h * HEAD_BLOCK:(h + 1) * HEAD_BLOCK]
        keep = (lane < HEAD_DIM) if h % 2 == 0 else (lane >= HEAD_DIM)
        dst_ref[rows, h * HEAD_BLOCK:(h + 1) * HEAD_BLOCK] = jnp.where(keep, pair, ex).astype(BF16)


def _extra_rows(pieces, h, n, key_side):
    r = lax.broadcasted_iota(jnp.int32, (EXTRA_ROWS, n), 0)
    value_row, sign = (1, -1.0) if key_side else (0, 1.0)
    out = jnp.where((r < 2 * N_SPLIT) & ((r & 1) != value_row), 1.0, 0.0)
    for i, piece in enumerate(pieces):
        out = jnp.where(r == 2 * i + value_row,
                        jnp.broadcast_to(sign * piece[h:h + 1, :], (EXTRA_ROWS, n)), out)
    return out.astype(BF16)


def _store_feature_major(dst_ref, h, data, extra):
    if h % 2 == 0:
        dst_ref[h, 0:HEAD_DIM, :] = data
        dst_ref[h, HEAD_DIM:HEAD_DIM + EXTRA_ROWS, :] = extra
    else:
        dst_ref[h, 0:EXTRA_ROWS, :] = extra
        dst_ref[h, HEAD_DIM:2 * HEAD_DIM, :] = data


def _std_columns(w_lo, w_hi, col, width):
    split = w_lo.shape[1]
    assert col + width <= split or col >= split
    return w_lo[:, col:col + width] if col < split else w_hi[:, col - split:col - split + width]


def _conv_branch(proj, u_sc, conv_w, conv_ref, w_pa, n):
    first = SUBLANES - (CONV_W - 1)
    u_sc[SUBLANES:SUBLANES + n, :] = proj(_S_GC, CONV_DIM) * proj(_S_XA, CONV_DIM)
    conv = sum(u_sc[first + i:first + i + n, :] * conv_w[i:i + 1, :] for i in range(CONV_W))
    new_conv = u_sc[first + n:SUBLANES + n, :]
    conv_ref[...] = new_conv
    u_sc[first:SUBLANES, :] = new_conv
    y_a = proj(_S_GB, CONV_DIM) * conv * jax.nn.silu(proj(_S_ZA, CONV_DIM))
    return jax.nn.sigmoid(proj(_S_GA, D_MODEL)) * jnp.dot(
        y_a.astype(BF16), w_pa[...], preferred_element_type=F32)


def _merge_and_norm(att, mix_a, h, silu_zb, gate_b, w_pb, w_out, ln_g, ln_b):
    y_b = att * silu_zb
    mixed = mix_a + gate_b * jnp.dot(
        y_b.astype(BF16), w_pb[...], preferred_element_type=F32)
    sub = jnp.dot(mixed.astype(BF16), w_out[...], preferred_element_type=F32)
    return _layer_norm(ALPHA * h + sub, ln_g[...], ln_b[...])


def _attention_t(j_lo, j_hi, j, tm, kaug_sc, qta_sc, vt_sc, attt_sc):
    n_key_tiles = j_hi + 1
    keys = n_key_tiles * tm
    key_in_tile = lax.broadcasted_iota(jnp.int32, (tm, tm), 0)
    query_in_tile = lax.broadcasted_iota(jnp.int32, (tm, tm), 1)
    tile_of_query = j if j_hi > j_lo else j_lo
    visible = {i: key_in_tile + (i - tile_of_query) * tm <= query_in_tile for i in range(j_lo, n_key_tiles)}

    def scores(hd):
        return jnp.dot(kaug_sc[0:keys, hd * HEAD_BLOCK:(hd + 1) * HEAD_BLOCK], qta_sc[hd],
                       preferred_element_type=F32)

    pending = [scores(hd) for hd in range(SCORE_LOOKAHEAD)]
    for hd in range(N_HEADS):
        s = pending.pop(0)
        if hd + SCORE_LOOKAHEAD < N_HEADS:
            pending.append(scores(hd + SCORE_LOOKAHEAD))
        blocks = [s[i * tm:(i + 1) * tm, :] for i in range(n_key_tiles)]
        for i, mask in visible.items():
            blocks[i] = jnp.where(mask, blocks[i], -jnp.inf)
        m = functools.reduce(jnp.maximum, [jnp.max(blk, axis=0, keepdims=True) for blk in blocks])
        o = jnp.zeros((V_ROWS, tm), F32)
        for i, blk in enumerate(blocks):
            o = o + jnp.dot(vt_sc[i, hd * V_ROWS:(hd + 1) * V_ROWS, :], jnp.exp2(blk - m).astype(BF16),
                            preferred_element_type=F32)
        attt_sc[hd * HEAD_DIM:(hd + 1) * HEAD_DIM, :] = o[0:HEAD_DIM, :] / o[HEAD_DIM:HEAD_DIM + 1, :]


def _prompt_kernel(x_ref, ln_in_g, ln_in_b, w_lo, w_hi, w_t, b_f_col, conv_w, w_pa, w_pb, w_out,
                   ln_g, ln_b, route, tri,
                   y_ref, kt_ref, vt_ref, lft_ref, conv_ref,
                   kaug_sc, vt_sc, h_sc, hb_sc, z_sc, qta_sc, attt_sc, zb_sc, gm_sc, u_sc, carryt_sc,
                   *, tm, n_tiles, n_total):
    t = pl.program_id(0)

    def norm_in():
        h = _layer_norm(x_ref[...], ln_in_g[...], ln_in_b[...])
        h_sc[lax.rem(t, PIPE_SLOTS)] = h
        hb_sc[lax.rem(t, PIPE_SLOTS)] = h.astype(BF16)

    def norm_out():
        y_ref[...] = _layer_norm(z_sc[...], ln_g[...], ln_b[...])

    @pl.when(t == 0)
    def _fill():
        for ref in (z_sc, kaug_sc, vt_sc, h_sc, hb_sc, attt_sc, zb_sc, gm_sc, u_sc, qta_sc, carryt_sc):
            ref[...] = jnp.zeros_like(ref)
        ones_row = jnp.where(lax.broadcasted_iota(jnp.int32, (V_ROWS - HEAD_DIM, tm), 0) == 0, 1.0, 0.0)
        for i in range(n_tiles):
            for hd in range(N_HEADS):
                vt_sc[i, hd * V_ROWS + HEAD_DIM:(hd + 1) * V_ROWS, :] = ones_row.astype(BF16)
        norm_in()

    @pl.when((t >= 1) & (t <= n_total + 1))
    def _mixers():
        tile_a = t - 1
        tile_b = t - 2
        ja = lax.rem(tile_a, n_tiles)
        jb = lax.rem(tile_b, n_tiles)
        slot_a = lax.rem(tile_a, PIPE_SLOTS)
        slot_b = lax.rem(t + 1, PIPE_SLOTS)
        gate_a = tile_a & 1
        gate_b = t & 1

        @pl.when(ja == 0)
        def _start_of_sequence():
            carryt_sc[...] = jnp.zeros_like(carryt_sc)

        @pl.when(jb == 0)
        def _start_of_conv():
            u_sc[0:SUBLANES, :] = jnp.zeros((SUBLANES, CONV_DIM), F32)

        spans = [(jj, jj) for jj in range(n_tiles - TAIL_SHARE)] + [(n_tiles - TAIL_SHARE, n_tiles - 1)]
        for j_lo, j_hi in spans:
            pl.when((t >= 2) & (jb >= j_lo) & (jb <= j_hi))(functools.partial(
                _attention_t, j_lo, j_hi, jb, tm, kaug_sc, qta_sc, vt_sc, attt_sc))

        def proj(slot, col, width):
            return jnp.dot(hb_sc[slot], _std_columns(w_lo, w_hi, col, width), preferred_element_type=F32)

        def proj_t(row, height):
            return lax.dot_general(w_t[row:row + height, :], hb_sc[slot_a], (((1,), (1,)), ((), ())),
                                   preferred_element_type=F32)

        norm_out()

        vf_t = proj_t(_T_V, ATT_DIM + F_ROWS)
        qk_t = proj_t(_T_Q, 2 * ATT_DIM)
        kt = qk_t[ATT_DIM:2 * ATT_DIM, :]
        vt = vf_t[0:ATT_DIM, :]
        kt_ref[...] = kt
        vt_ref[...] = vt
        logf_t = _log_sigmoid(vf_t[ATT_DIM:ATT_DIM + F_ROWS, :] + b_f_col[...])
        lft_ref[...] = logf_t[:N_HEADS, :]
        sums = jnp.dot(jnp.concatenate(_split3(logf_t), axis=0), tri[...], preferred_element_type=F32)

        mix_a = _conv_branch(functools.partial(proj, slot_b), u_sc, conv_w, conv_ref, w_pa, tm)

        zb_sc[gate_a] = jax.nn.silu(proj(slot_a, _S_ZB, ATT_DIM))
        gm_sc[gate_a] = jax.nn.sigmoid(proj(slot_a, _S_GM, D_MODEL))

        y_b = attt_sc[...].T * zb_sc[gate_b]
        mixed = mix_a + gm_sc[gate_b] * jnp.dot(y_b.astype(BF16), w_pb[...], preferred_element_type=F32)

        c_t = sum(sums[r * F_ROWS:(r + 1) * F_ROWS, :] for r in range(N_SPLIT)) + carryt_sc[:, 0:1]
        carryt_sc[...] = jnp.broadcast_to(c_t[:, tm - 1:tm], carryt_sc.shape)
        c_pieces = [piece.astype(F32) for piece in _split3(c_t * LOG2E)]
        qt = qk_t[0:ATT_DIM, :] * (SCALE * LOG2E)
        for hd in range(N_HEADS):
            _store_feature_major(qta_sc, hd, qt[hd * HEAD_DIM:(hd + 1) * HEAD_DIM, :].astype(BF16),
                                 _extra_rows(c_pieces, hd, tm, key_side=False))
            vt_sc[ja, hd * V_ROWS:hd * V_ROWS + HEAD_DIM, :] = (
                vt[hd * HEAD_DIM:(hd + 1) * HEAD_DIM, :].astype(BF16))
        ones_and_pad = jnp.where(lax.broadcasted_iota(jnp.int32, (LANES - ONES_LANE, tm), 0) == 0, 1.0, 0.0)
        packed_t = jnp.concatenate([piece[0:N_HEADS, :] for piece in c_pieces] + [ones_and_pad], axis=0)
        ex_k = jnp.dot(packed_t.T.astype(BF16), route[:, AUG_DIM:], preferred_element_type=F32)
        _store_token_major(kaug_sc, pl.ds(pl.multiple_of(ja * tm, tm), tm), kt.T, ex_k)

        z_sc[...] = ALPHA * h_sc[slot_b] + jnp.dot(mixed.astype(BF16), w_out[...], preferred_element_type=F32)

        norm_in()

    @pl.when(t == n_total + 2)
    def _drain():
        norm_out()


def _resident(shape):
    return pl.BlockSpec(shape, lambda *_: (0,) * len(shape), pipeline_mode=pl.Buffered(1))


def _run_prompt(x, params, *, tm):
    bsz, t, _ = x.shape
    n_tiles = t // tm
    assert n_tiles * tm == t
    n_total = bsz * n_tiles
    tile = lambda s, lag: jnp.clip(s - lag, 0, n_total - 1)
    per_tile = lambda width, lag: pl.BlockSpec(
        (None, tm, width), lambda s: (tile(s, lag) // n_tiles, tile(s, lag) % n_tiles, 0))
    per_tile_t = lambda rows: pl.BlockSpec(
        (None, rows, tm), lambda s: (tile(s, 1) // n_tiles, 0, tile(s, 1) % n_tiles))
    out_shape = (jax.ShapeDtypeStruct((bsz, t, D_MODEL), F32),
                 jax.ShapeDtypeStruct((bsz, ATT_DIM, t), F32),
                 jax.ShapeDtypeStruct((bsz, ATT_DIM, t), F32),
                 jax.ShapeDtypeStruct((bsz, N_HEADS, t), F32),
                 jax.ShapeDtypeStruct((bsz, CONV_W - 1, CONV_DIM), F32))
    out_specs = (per_tile(D_MODEL, 3), per_tile_t(ATT_DIM), per_tile_t(ATT_DIM), per_tile_t(N_HEADS),
                 pl.BlockSpec((None, CONV_W - 1, CONV_DIM), lambda s: (tile(s, 2) // n_tiles, 0, 0)))
    scratch = [
        pltpu.VMEM((t, AUG_DIM), BF16),
        pltpu.VMEM((n_tiles, N_HEADS * V_ROWS, tm), BF16),
        pltpu.VMEM((PIPE_SLOTS, tm, D_MODEL), F32),
        pltpu.VMEM((PIPE_SLOTS, tm, D_MODEL), BF16),
        pltpu.VMEM((tm, D_MODEL), F32),
        pltpu.VMEM((N_HEADS, HEAD_BLOCK, tm), BF16),
        pltpu.VMEM((ATT_DIM, tm), F32),
        pltpu.VMEM((2, tm, ATT_DIM), F32),
        pltpu.VMEM((2, tm, D_MODEL), F32),
        pltpu.VMEM((SUBLANES + tm, CONV_DIM), F32),
        pltpu.VMEM((F_ROWS, LANES), F32),
    ]
    return pl.pallas_call(
        functools.partial(_prompt_kernel, tm=tm, n_tiles=n_tiles, n_total=n_total),
        grid=(n_total + PIPE_SLOTS,),
        in_specs=[per_tile(D_MODEL, 0)] + [_resident(p.shape) for p in params],
        out_specs=out_specs,
        out_shape=out_shape,
        scratch_shapes=scratch,
        compiler_params=pltpu.CompilerParams(
            dimension_semantics=("arbitrary",),
            vmem_limit_bytes=VMEM_LIMIT_BYTES),
        name="trunk_prompt",
    )(x, *params)


def _sample_kernel(x_ref, conv0_ref, pkt_ref, pvt_ref, plft_ref,
                   ln_in_g, ln_in_b, w_lo, w_hi, w_fs, b_f_row, w_t, b_f_col, conv_w, w_pa, w_pb, w_out,
                   ln_g, ln_b, route, ones,
                   y_ref, kt_ref, vt_ref, lft_ref, conv_ref,
                   kta_sc, ktn_sc, hb_sc, qa_sc, att_sc, u_sc, *, ts, n_past):
    @pl.when(pl.program_id(0) == 0)
    def _zero_padding():
        kta_sc[...] = jnp.zeros_like(kta_sc)
        ktn_sc[...] = jnp.zeros_like(ktn_sc)
        hb_sc[...] = jnp.zeros_like(hb_sc)

    h = _layer_norm(x_ref[...], ln_in_g[...], ln_in_b[...])
    hb_sc[0:ts, :] = h.astype(BF16)

    def proj(col, width):
        return jnp.dot(hb_sc[0:ts, :], _std_columns(w_lo, w_hi, col, width), preferred_element_type=F32)

    u_sc[0:SUBLANES, :] = jnp.zeros((SUBLANES, CONV_DIM), F32)
    u_sc[SUBLANES - (CONV_W - 1):SUBLANES, :] = conv0_ref[...]
    mix_a = _conv_branch(proj, u_sc, conv_w, conv_ref, w_pa, ts)

    def proj_t(row, height):
        return lax.dot_general(w_t[row:row + height, :], hb_sc[...], (((1,), (1,)), ((), ())),
                               preferred_element_type=F32)

    kt = proj_t(_T_K, ATT_DIM)
    vf_t = proj_t(_T_V, ATT_DIM + F_ROWS)
    vt = vf_t[0:ATT_DIM, :]
    kt_ref[...] = kt[:, :ts]
    vt_ref[...] = vt[:, :ts]
    logf_t = _log_sigmoid(vf_t[ATT_DIM:ATT_DIM + SUBLANES, :] + b_f_col[0:SUBLANES, :])
    lft_ref[...] = logf_t[:, :ts]

    c_past = _cumsum(plft_ref[...], 1)
    c_last = c_past[:, n_past - 1:n_past]
    c_new_t = _cumsum(logf_t, 1) + c_last
    past_pieces = [piece.astype(F32) for piece in _split3(c_past)]
    new_pieces = [piece.astype(F32) for piece in _split3(c_new_t)]
    for hd in range(N_HEADS):
        rows = slice(hd * HEAD_DIM, (hd + 1) * HEAD_DIM)
        _store_feature_major(kta_sc, hd, pkt_ref[rows, :].astype(BF16),
                             _extra_rows(past_pieces, hd, n_past, key_side=True))
        _store_feature_major(ktn_sc, hd, kt[rows, :].astype(BF16),
                             _extra_rows(new_pieces, hd, LANES, key_side=True))

    eye = (lax.broadcasted_iota(jnp.int32, (SUBLANES, LANES), 0)
           == lax.broadcasted_iota(jnp.int32, (SUBLANES, LANES), 1))
    c_last_row = jnp.sum(jnp.where(eye, jnp.broadcast_to(c_last, (SUBLANES, LANES)), 0.0),
                         axis=0, keepdims=True)
    logf = _log_sigmoid(jnp.dot(hb_sc[0:ts, :], w_fs[...], preferred_element_type=F32) + b_f_row[...])
    c_q = _cumsum(logf, 0) + c_last_row
    ex_q = _extra_lanes(c_q, route[:, :AUG_DIM]) + ones[:, :AUG_DIM]
    _store_token_major(qa_sc, slice(None), proj(_S_Q, ATT_DIM) * SCALE, ex_q)

    causal = (lax.broadcasted_iota(jnp.int32, (ts, LANES), 1)
              <= lax.broadcasted_iota(jnp.int32, (ts, LANES), 0))
    lane = lax.broadcasted_iota(jnp.int32, (ts, HEAD_BLOCK), 1)
    contract_lanes = (((1,), (1,)), ((), ()))
    silu_zb = jax.nn.silu(proj(_S_ZB, ATT_DIM))
    gate_b = jax.nn.sigmoid(proj(_S_GM, D_MODEL))
    scores = []
    for hd in range(N_HEADS):
        qa_h = qa_sc[:, hd * HEAD_BLOCK:(hd + 1) * HEAD_BLOCK]
        scores.append((jnp.dot(qa_h, kta_sc[hd], preferred_element_type=F32),
                       jnp.dot(qa_h, ktn_sc[hd], preferred_element_type=F32)))
    for pair in range(N_HEADS // 2):
        rows = slice(pair * HEAD_BLOCK, (pair + 1) * HEAD_BLOCK)
        vt_past = pvt_ref[rows, :].astype(BF16)
        vt_new = vt[rows, :].astype(BF16)
        probs, sums = [], []
        for hd in (2 * pair, 2 * pair + 1):
            s_past = scores[hd][0]
            s_new = jnp.where(causal, scores[hd][1], -jnp.inf)
            m = jnp.maximum(jnp.max(s_past, axis=-1, keepdims=True), jnp.max(s_new, axis=-1, keepdims=True))
            p_past = jnp.exp(s_past - m)
            p_new = jnp.exp(s_new - m)
            sums.append(jnp.sum(p_past, axis=-1, keepdims=True) + jnp.sum(p_new, axis=-1, keepdims=True))
            probs.append((p_past.astype(BF16), p_new.astype(BF16)))
        o = (lax.dot_general(jnp.concatenate([probs[0][0], probs[1][0]], axis=0), vt_past, contract_lanes,
                             preferred_element_type=F32)
             + lax.dot_general(jnp.concatenate([probs[0][1], probs[1][1]], axis=0), vt_new, contract_lanes,
                               preferred_element_type=F32))
        att_sc[:, rows] = jnp.where(lane < HEAD_DIM, o[0:ts, :] / sums[0], o[ts:2 * ts, :] / sums[1])

    y_ref[...] = _merge_and_norm(att_sc[...], mix_a, h, silu_zb, gate_b, w_pb, w_out, ln_g, ln_b)


def _run_sample(x, conv0, pkt, pvt, plft, params):
    bsz, ts, _ = x.shape
    n_past = pkt.shape[2]
    assert ts <= LANES and ts % SUBLANES == 0
    per_batch = lambda rows, width: pl.BlockSpec((None, rows, width), lambda b: (b, 0, 0))
    out_shape = (jax.ShapeDtypeStruct((bsz, ts, D_MODEL), F32),
                 jax.ShapeDtypeStruct((bsz, ATT_DIM, ts), F32),
                 jax.ShapeDtypeStruct((bsz, ATT_DIM, ts), F32),
                 jax.ShapeDtypeStruct((bsz, N_HEADS, ts), F32),
                 jax.ShapeDtypeStruct((bsz, CONV_W - 1, CONV_DIM), F32))
    out_specs = (per_batch(ts, D_MODEL), per_batch(ATT_DIM, ts), per_batch(ATT_DIM, ts),
                 per_batch(N_HEADS, ts), per_batch(CONV_W - 1, CONV_DIM))
    in_specs = [per_batch(ts, D_MODEL), per_batch(CONV_W - 1, CONV_DIM), per_batch(ATT_DIM, n_past),
                per_batch(ATT_DIM, n_past), per_batch(N_HEADS, n_past)]
    scratch = [
        pltpu.VMEM((N_HEADS, HEAD_BLOCK, n_past), BF16),
        pltpu.VMEM((N_HEADS, HEAD_BLOCK, LANES), BF16),
        pltpu.VMEM((LANES, D_MODEL), BF16),
        pltpu.VMEM((ts, AUG_DIM), BF16),
        pltpu.VMEM((ts, ATT_DIM), F32),
        pltpu.VMEM((SUBLANES + ts, CONV_DIM), F32),
    ]
    return pl.pallas_call(
        functools.partial(_sample_kernel, ts=ts, n_past=n_past),
        grid=(bsz,),
        in_specs=in_specs + [_resident(p.shape) for p in params],
        out_specs=out_specs,
        out_shape=out_shape,
        scratch_shapes=scratch,
        compiler_params=pltpu.CompilerParams(
            dimension_semantics=("arbitrary",),
            vmem_limit_bytes=VMEM_LIMIT_BYTES),
        name="trunk_sample",
    )(x, conv0, pkt, pvt, plft, *params)


def kernel(x_prompt, x_sample, cache_k, cache_v, cache_logf, state_conv, ln_in_g, ln_in_b, w_in, b_f,
           conv_w, w_proj_a, w_proj_b, w_out, ln_g, ln_b):
    assert w_in.shape[0] == DEPTH
    route_np, ones_np = _routing_constants()
    w = w_in[0]
    row = lambda a: a.reshape(1, -1).astype(F32)
    w_t = jnp.concatenate([w[:, _W_Q:_W_F + N_HEADS],
                           jnp.zeros((D_MODEL, F_ROWS - N_HEADS), w.dtype)], axis=1).T.astype(BF16)
    tm = PROMPT_TILE
    p = dict(
        ln_in_g=row(ln_in_g), ln_in_b=row(ln_in_b),
        w_lo=w[:, :_W_V].astype(BF16), w_hi=w[:, _W_ZB:].astype(BF16),
        w_fs=jnp.pad(w[:, _W_F:_W_F + N_HEADS], ((0, 0), (0, LANES - N_HEADS))).astype(BF16),
        b_f_row=jnp.pad(row(b_f[0]), ((0, 0), (0, LANES - N_HEADS))),
        w_t=w_t,
        b_f_col=jnp.pad(b_f[0].astype(F32).reshape(N_HEADS, 1), ((0, F_ROWS - N_HEADS), (0, 0))),
        conv_w=conv_w[0].astype(F32),
        w_pa=w_proj_a[0].astype(BF16), w_pb=w_proj_b[0].astype(BF16), w_out=w_out[0].astype(BF16),
        ln_g=row(ln_g[0]), ln_b=row(ln_b[0]),
        route=jnp.asarray(route_np, BF16), ones=jnp.asarray(ones_np, F32),
        tri=jnp.asarray(np.triu(np.ones((tm, tm), np.float32)), BF16),
    )
    bs = x_sample.shape[0]
    n_past = cache_k.shape[2]
    feature_major = lambda a: a.transpose(0, 2, 3, 1).reshape(bs, ATT_DIM, n_past)

    y_p, kt_p, vt_p, lft_p, cv_p = _run_prompt(x_prompt, [p[name] for name in PROMPT_PARAMS], tm=tm)
    y_s, kt_s, vt_s, lft_s, cv_s = _run_sample(
        x_sample, state_conv[0], feature_major(cache_k[0]), feature_major(cache_v[0]),
        cache_logf[0].transpose(0, 2, 1), [p[name] for name in SAMPLE_PARAMS])

    heads_t = lambda a: a.reshape(a.shape[0], N_HEADS, HEAD_DIM, a.shape[2]).transpose(0, 3, 1, 2)[None]
    tokens_t = lambda a: a.transpose(0, 2, 1)[None]
    return (y_p, y_s, heads_t(kt_p), heads_t(vt_p), tokens_t(lft_p), cv_p[None],
            heads_t(kt_s), heads_t(vt_s), tokens_t(lft_s), cv_s[None])
```

```python
import functools

import numpy as np
import jax
import jax.numpy as jnp
from jax import lax
from jax.experimental import pallas as pl
from jax.experimental.pallas import tpu as pltpu

D_MODEL = 1024
CONV_DIM = 512
CONV_W = 3
N_HEADS = 8
HEAD_DIM = 64
ATT_DIM = N_HEADS * HEAD_DIM
LN_EPS = 1e-5
DEPTH = 1
ALPHA = (2.0 * DEPTH) ** 0.25
SCALE = HEAD_DIM ** -0.5

LANES = 128
SUBLANES = 8
HEAD_BLOCK = LANES
AUG_DIM = N_HEADS * HEAD_BLOCK
N_SPLIT = 3
ONES_LANE = N_SPLIT * N_HEADS
EXTRA_ROWS = 16
F_ROWS = 16
VMEM_LIMIT_BYTES = 56 * 1024 * 1024
V_ROWS = HEAD_DIM + 16
LOG2E = 1.4426950408889634
UNDERFLOW_LOG2 = 180.0
NORM_SLACK = 1.05
PIPE_SLOTS = 3
SCORE_LOOKAHEAD = 4

_W_Q = 4 * CONV_DIM
_W_V = _W_Q + 2 * ATT_DIM
_W_F = _W_Q + 3 * ATT_DIM
_W_ZB = _W_F + N_HEADS
_S_XA, _S_GB, _S_GC, _S_ZA, _S_Q, _S_K, _S_ZB, _S_GA, _S_GM = (
    0, 512, 1024, 1536, 2048, 2560, 3072, 3584, 4608)
_T_Q, _T_K, _T_V, _T_F = 0, 512, 1024, 1536

PROMPT_TILE = 256
PROMPT_PARAMS = ("ln_in_g", "ln_in_b", "w_lo", "w_hi", "w_t", "b_f_col", "conv_w", "w_pa", "w_pb", "w_out",
                 "ln_g", "ln_b", "route", "tri")
SAMPLE_PARAMS = ("ln_in_g", "ln_in_b", "w_lo", "w_hi", "w_fs", "b_f_row", "w_t", "b_f_col", "conv_w", "w_pa",
                 "w_pb", "w_out", "ln_g", "ln_b", "route", "ones")

F32 = jnp.float32
BF16 = jnp.bfloat16


def _routing_constants():
    route = np.zeros((LANES, 2 * AUG_DIM), np.float32)
    ones = np.zeros((1, 2 * AUG_DIM), np.float32)
    for h in range(N_HEADS):
        base = h * HEAD_BLOCK + (HEAD_DIM if h % 2 == 0 else 0)
        for r in range(N_SPLIT):
            route[r * N_HEADS + h, base + 2 * r] = 1.0
            ones[0, base + 2 * r + 1] = 1.0
            ones[0, AUG_DIM + base + 2 * r] = 1.0
            route[r * N_HEADS + h, AUG_DIM + base + 2 * r + 1] = -1.0
    route[ONES_LANE, :] = ones[0, :]
    return route, ones


def _layer_norm(x, g, b):
    mu = jnp.mean(x, axis=-1, keepdims=True)
    xc = x - mu
    var = jnp.mean(xc * xc, axis=-1, keepdims=True)
    return xc * lax.rsqrt(var + LN_EPS) * g + b


def _log_sigmoid(x):
    return jnp.minimum(x, 0.0) - jnp.log1p(jnp.exp(-jnp.abs(x)))


def _split3(x):
    hi = x.astype(BF16)
    r1 = x - hi.astype(F32)
    mid = r1.astype(BF16)
    lo = (r1 - mid.astype(F32)).astype(BF16)
    return hi, mid, lo


def _cumsum(x, axis):
    n = x.shape[axis]
    idx = lax.broadcasted_iota(jnp.int32, x.shape, axis)
    s = 1
    while s < n:
        x = x + jnp.where(idx >= s, pltpu.roll(x, s, axis), 0.0)
        s *= 2
    return x


def _extra_lanes(c, route):
    lane = lax.broadcasted_iota(jnp.int32, c.shape, 1)
    valid = lane < N_HEADS
    packed = jnp.zeros_like(c)
    for r, piece in enumerate(_split3(c)):
        p32 = jnp.where(valid, piece.astype(F32), 0.0)
        packed = packed + (pltpu.roll(p32, r * N_HEADS, 1) if r else p32)
    return jnp.dot(packed.astype(BF16), route, preferred_element_type=F32)


def _store_token_major(dst_ref, rows, data, extras):
    n = data.shape[0]
    lane = lax.broadcasted_iota(jnp.int32, (n, HEAD_BLOCK), 1)
    for h in range(N_HEADS):
        pair = data[:, (h // 2) * HEAD_BLOCK:(h // 2 + 1) * HEAD_BLOCK]
        ex = extras[:, h * HEAD_BLOCK:(h + 1) * HEAD_BLOCK]
        keep = (lane < HEAD_DIM) if h % 2 == 0 else (lane >= HEAD_DIM)
        dst_ref[rows, h * HEAD_BLOCK:(h + 1) * HEAD_BLOCK] = jnp.where(keep, pair, ex).astype(BF16)


def _extra_rows(pieces, h, n, key_side):
    r = lax.broadcasted_iota(jnp.int32, (EXTRA_ROWS, n), 0)
    value_row, sign = (1, -1.0) if key_side else (0, 1.0)
    out = jnp.where((r < 2 * N_SPLIT) & ((r & 1) != value_row), 1.0, 0.0)
    for i, piece in enumerate(pieces):
        out = jnp.where(r == 2 * i + value_row,
                        jnp.broadcast_to(sign * piece[h:h + 1, :], (EXTRA_ROWS, n)), out)
    return out.astype(BF16)


def _store_feature_major(dst_ref, h, data, extra):
    if h % 2 == 0:
        dst_ref[h, 0:HEAD_DIM, :] = data
        dst_ref[h, HEAD_DIM:HEAD_DIM + EXTRA_ROWS, :] = extra
    else:
        dst_ref[h, 0:EXTRA_ROWS, :] = extra
        dst_ref[h, HEAD_DIM:2 * HEAD_DIM, :] = data


def _std_columns(w_lo, w_hi, col, width):
    split = w_lo.shape[1]
    assert col + width <= split or col >= split
    return w_lo[:, col:col + width] if col < split else w_hi[:, col - split:col - split + width]


def _conv_branch(proj, u_sc, conv_w, conv_ref, w_pa, n):
    first = SUBLANES - (CONV_W - 1)
    u_sc[SUBLANES:SUBLANES + n, :] = proj(_S_GC, CONV_DIM) * proj(_S_XA, CONV_DIM)
    conv = sum(u_sc[first + i:first + i + n, :] * conv_w[i:i + 1, :] for i in range(CONV_W))
    new_conv = u_sc[first + n:SUBLANES + n, :]
    conv_ref[...] = new_conv
    u_sc[first:SUBLANES, :] = new_conv
    y_a = proj(_S_GB, CONV_DIM) * conv * jax.nn.silu(proj(_S_ZA, CONV_DIM))
    return jax.nn.sigmoid(proj(_S_GA, D_MODEL)) * jnp.dot(
        y_a.astype(BF16), w_pa[...], preferred_element_type=F32)


def _merge_and_norm(att, mix_a, h, silu_zb, gate_b, w_pb, w_out, ln_g, ln_b):
    y_b = att * silu_zb
    mixed = mix_a + gate_b * jnp.dot(
        y_b.astype(BF16), w_pb[...], preferred_element_type=F32)
    sub = jnp.dot(mixed.astype(BF16), w_out[...], preferred_element_type=F32)
    return _layer_norm(ALPHA * h + sub, ln_g[...], ln_b[...])


def _head_max_sq_norm(xt):
    n = xt.shape[1]
    row = lax.broadcasted_iota(jnp.int32, (F_ROWS, n), 0)
    sq = xt * xt
    acc = jnp.zeros((F_ROWS, n), F32)
    for h in range(N_HEADS):
        norm = jnp.sum(sq[h * HEAD_DIM:(h + 1) * HEAD_DIM, :], axis=0, keepdims=True)
        acc = jnp.where(row == h, jnp.broadcast_to(norm, (F_ROWS, n)), acc)
    return jnp.max(acc, axis=1, keepdims=True)


def _dead_key_tiles(j, qt2, kt, c2_end, kn_sc, ce_sc):
    lane = lax.broadcasted_iota(jnp.int32, (F_ROWS, LANES), 1)
    row = lax.broadcasted_iota(jnp.int32, (F_ROWS, LANES), 0)
    spread = lambda col: jnp.broadcast_to(col, (F_ROWS, LANES))
    kn = jnp.where(lane == j, spread(_head_max_sq_norm(kt)), jnp.where(lane < j, kn_sc[...], 0.0))
    ce = jnp.where(lane == j, spread(c2_end), jnp.where(lane < j, ce_sc[...], 0.0))
    kn_sc[...] = kn
    ce_sc[...] = ce
    k_max = jnp.max(kn, axis=1, keepdims=True)
    ce_prev = jnp.sum(jnp.where(lane == j - 1, ce, 0.0), axis=1, keepdims=True)
    bound = NORM_SLACK * 2.0 * jnp.sqrt(_head_max_sq_norm(qt2) * k_max) + ce_prev - ce
    worst = jnp.max(jnp.where(row < N_HEADS, bound, -jnp.inf), axis=0, keepdims=True)
    dead = (worst < -UNDERFLOW_LOG2) & (lane[0:1, :] <= j - 2)
    return jnp.min(jnp.where(dead, LANES, lane[0:1, :]))


def _attention_t(n_key_tiles, first, tm, kaug_sc, qta_sc, vt_sc, attt_sc):
    keys = pl.ds(pl.multiple_of(first * tm, tm), n_key_tiles * tm)
    causal = (lax.broadcasted_iota(jnp.int32, (tm, tm), 0)
              <= lax.broadcasted_iota(jnp.int32, (tm, tm), 1))

    def scores(hd):
        return jnp.dot(kaug_sc[keys, hd * HEAD_BLOCK:(hd + 1) * HEAD_BLOCK], qta_sc[hd],
                       preferred_element_type=F32)

    pending = [scores(hd) for hd in range(SCORE_LOOKAHEAD)]
    for hd in range(N_HEADS):
        s = pending.pop(0)
        if hd + SCORE_LOOKAHEAD < N_HEADS:
            pending.append(scores(hd + SCORE_LOOKAHEAD))
        blocks = [s[i * tm:(i + 1) * tm, :] for i in range(n_key_tiles)]
        blocks[-1] = jnp.where(causal, blocks[-1], -jnp.inf)
        m = functools.reduce(jnp.maximum, [jnp.max(blk, axis=0, keepdims=True) for blk in blocks])
        o = jnp.zeros((V_ROWS, tm), F32)
        for i, blk in enumerate(blocks):
            o = o + jnp.dot(vt_sc[first + i, hd * V_ROWS:(hd + 1) * V_ROWS, :],
                            jnp.exp2(blk - m).astype(BF16), preferred_element_type=F32)
        attt_sc[hd * HEAD_DIM:(hd + 1) * HEAD_DIM, :] = o[0:HEAD_DIM, :] / o[HEAD_DIM:HEAD_DIM + 1, :]


def _prompt_kernel(x_ref, ln_in_g, ln_in_b, w_lo, w_hi, w_t, b_f_col, conv_w, w_pa, w_pb, w_out,
                   ln_g, ln_b, route, tri,
                   y_ref, kt_ref, vt_ref, lft_ref, conv_ref,
                   kaug_sc, vt_sc, h_sc, hb_sc, z_sc, qta_sc, attt_sc, zb_sc, gm_sc, u_sc, carryt_sc,
                   kn_sc, ce_sc, first_sm, *, tm, n_tiles, n_total):
    t = pl.program_id(0)

    def norm_in():
        h = _layer_norm(x_ref[...], ln_in_g[...], ln_in_b[...])
        h_sc[lax.rem(t, PIPE_SLOTS)] = h
        hb_sc[lax.rem(t, PIPE_SLOTS)] = h.astype(BF16)

    def norm_out():
        y_ref[...] = _layer_norm(z_sc[...], ln_g[...], ln_b[...])

    @pl.when(t == 0)
    def _fill():
        for ref in (z_sc, kaug_sc, vt_sc, h_sc, hb_sc, attt_sc, zb_sc, gm_sc, u_sc, qta_sc, carryt_sc,
                    kn_sc, ce_sc):
            ref[...] = jnp.zeros_like(ref)
        first_sm[0] = 0
        ones_row = jnp.where(lax.broadcasted_iota(jnp.int32, (V_ROWS - HEAD_DIM, tm), 0) == 0, 1.0, 0.0)
        for i in range(n_tiles):
            for hd in range(N_HEADS):
                vt_sc[i, hd * V_ROWS + HEAD_DIM:(hd + 1) * V_ROWS, :] = ones_row.astype(BF16)
        norm_in()

    @pl.when((t >= 1) & (t <= n_total + 1))
    def _mixers():
        tile_a = t - 1
        tile_b = t - 2
        ja = lax.rem(tile_a, n_tiles)
        jb = lax.rem(tile_b, n_tiles)
        slot_a = lax.rem(tile_a, PIPE_SLOTS)
        slot_b = lax.rem(t + 1, PIPE_SLOTS)
        gate_a = tile_a & 1
        gate_b = t & 1

        @pl.when(ja == 0)
        def _start_of_sequence():
            carryt_sc[...] = jnp.zeros_like(carryt_sc)

        @pl.when(jb == 0)
        def _start_of_conv():
            u_sc[0:SUBLANES, :] = jnp.zeros((SUBLANES, CONV_DIM), F32)

        first = first_sm[0]
        for n_key_tiles in range(1, n_tiles + 1):
            pl.when((t >= 2) & (jb - first + 1 == n_key_tiles))(functools.partial(
                _attention_t, n_key_tiles, first, tm, kaug_sc, qta_sc, vt_sc, attt_sc))

        def proj(slot, col, width):
            return jnp.dot(hb_sc[slot], _std_columns(w_lo, w_hi, col, width), preferred_element_type=F32)

        def proj_t(row, height):
            return lax.dot_general(w_t[row:row + height, :], hb_sc[slot_a], (((1,), (1,)), ((), ())),
                                   preferred_element_type=F32)

        norm_out()

        vf_t = proj_t(_T_V, ATT_DIM + F_ROWS)
        qk_t = proj_t(_T_Q, 2 * ATT_DIM)
        kt = qk_t[ATT_DIM:2 * ATT_DIM, :]
        vt = vf_t[0:ATT_DIM, :]
        kt_ref[...] = kt
        vt_ref[...] = vt
        logf_t = _log_sigmoid(vf_t[ATT_DIM:ATT_DIM + F_ROWS, :] + b_f_col[...])
        lft_ref[...] = logf_t[:N_HEADS, :]
        sums = jnp.dot(jnp.concatenate(_split3(logf_t), axis=0), tri[...], preferred_element_type=F32)

        mix_a = _conv_branch(functools.partial(proj, slot_b), u_sc, conv_w, conv_ref, w_pa, tm)

        zb_sc[gate_a] = jax.nn.silu(proj(slot_a, _S_ZB, ATT_DIM))
        gm_sc[gate_a] = jax.nn.sigmoid(proj(slot_a, _S_GM, D_MODEL))

        y_b = attt_sc[...].T * zb_sc[gate_b]
        mixed = mix_a + gm_sc[gate_b] * jnp.dot(y_b.astype(BF16), w_pb[...], preferred_element_type=F32)

        c_t = sum(sums[r * F_ROWS:(r + 1) * F_ROWS, :] for r in range(N_SPLIT)) + carryt_sc[:, 0:1]
        carryt_sc[...] = jnp.broadcast_to(c_t[:, tm - 1:tm], carryt_sc.shape)
        c_pieces = [piece.astype(F32) for piece in _split3(c_t * LOG2E)]
        qt = qk_t[0:ATT_DIM, :] * (SCALE * LOG2E)
        for hd in range(N_HEADS):
            _store_feature_major(qta_sc, hd, qt[hd * HEAD_DIM:(hd + 1) * HEAD_DIM, :].astype(BF16),
                                 _extra_rows(c_pieces, hd, tm, key_side=False))
            vt_sc[ja, hd * V_ROWS:hd * V_ROWS + HEAD_DIM, :] = (
                vt[hd * HEAD_DIM:(hd + 1) * HEAD_DIM, :].astype(BF16))
        ones_and_pad = jnp.where(lax.broadcasted_iota(jnp.int32, (LANES - ONES_LANE, tm), 0) == 0, 1.0, 0.0)
        packed_t = jnp.concatenate([piece[0:N_HEADS, :] for piece in c_pieces] + [ones_and_pad], axis=0)
        ex_k = jnp.dot(packed_t.T.astype(BF16), route[:, AUG_DIM:], preferred_element_type=F32)
        _store_token_major(kaug_sc, pl.ds(pl.multiple_of(ja * tm, tm), tm), kt.T, ex_k)
        first_sm[0] = _dead_key_tiles(ja, qt, kt, c_t[:, tm - 1:tm] * LOG2E, kn_sc, ce_sc)

        z_sc[...] = ALPHA * h_sc[slot_b] + jnp.dot(mixed.astype(BF16), w_out[...], preferred_element_type=F32)

        norm_in()

    @pl.when(t == n_total + 2)
    def _drain():
        norm_out()


def _resident(shape):
    return pl.BlockSpec(shape, lambda *_: (0,) * len(shape), pipeline_mode=pl.Buffered(1))


def _run_prompt(x, params, *, tm):
    bsz, t, _ = x.shape
    n_tiles = t // tm
    assert n_tiles * tm == t
    n_total = bsz * n_tiles
    tile = lambda s, lag: jnp.clip(s - lag, 0, n_total - 1)
    per_tile = lambda width, lag: pl.BlockSpec(
        (None, tm, width), lambda s: (tile(s, lag) // n_tiles, tile(s, lag) % n_tiles, 0))
    per_tile_t = lambda rows: pl.BlockSpec(
        (None, rows, tm), lambda s: (tile(s, 1) // n_tiles, 0, tile(s, 1) % n_tiles))
    out_shape = (jax.ShapeDtypeStruct((bsz, t, D_MODEL), F32),
                 jax.ShapeDtypeStruct((bsz, ATT_DIM, t), F32),
                 jax.ShapeDtypeStruct((bsz, ATT_DIM, t), F32),
                 jax.ShapeDtypeStruct((bsz, N_HEADS, t), F32),
                 jax.ShapeDtypeStruct((bsz, CONV_W - 1, CONV_DIM), F32))
    out_specs = (per_tile(D_MODEL, 3), per_tile_t(ATT_DIM), per_tile_t(ATT_DIM), per_tile_t(N_HEADS),
                 pl.BlockSpec((None, CONV_W - 1, CONV_DIM), lambda s: (tile(s, 2) // n_tiles, 0, 0)))
    scratch = [
        pltpu.VMEM((t, AUG_DIM), BF16),
        pltpu.VMEM((n_tiles, N_HEADS * V_ROWS, tm), BF16),
        pltpu.VMEM((PIPE_SLOTS, tm, D_MODEL), F32),
        pltpu.VMEM((PIPE_SLOTS, tm, D_MODEL), BF16),
        pltpu.VMEM((tm, D_MODEL), F32),
        pltpu.VMEM((N_HEADS, HEAD_BLOCK, tm), BF16),
        pltpu.VMEM((ATT_DIM, tm), F32),
        pltpu.VMEM((2, tm, ATT_DIM), F32),
        pltpu.VMEM((2, tm, D_MODEL), F32),
        pltpu.VMEM((SUBLANES + tm, CONV_DIM), F32),
        pltpu.VMEM((F_ROWS, LANES), F32),
        pltpu.VMEM((F_ROWS, LANES), F32),
        pltpu.VMEM((F_ROWS, LANES), F32),
        pltpu.SMEM((1,), jnp.int32),
    ]
    return pl.pallas_call(
        functools.partial(_prompt_kernel, tm=tm, n_tiles=n_tiles, n_total=n_total),
        grid=(n_total + PIPE_SLOTS,),
        in_specs=[per_tile(D_MODEL, 0)] + [_resident(p.shape) for p in params],
        out_specs=out_specs,
        out_shape=out_shape,
        scratch_shapes=scratch,
        compiler_params=pltpu.CompilerParams(
            dimension_semantics=("arbitrary",),
            vmem_limit_bytes=VMEM_LIMIT_BYTES),
        name="trunk_prompt",
    )(x, *params)


def _sample_kernel(x_ref, conv0_ref, pkt_ref, pvt_ref, plft_ref,
                   ln_in_g, ln_in_b, w_lo, w_hi, w_fs, b_f_row, w_t, b_f_col, conv_w, w_pa, w_pb, w_out,
                   ln_g, ln_b, route, ones,
                   y_ref, kt_ref, vt_ref, lft_ref, conv_ref,
                   kta_sc, ktn_sc, hb_sc, qa_sc, att_sc, u_sc, *, ts, n_past):
    @pl.when(pl.program_id(0) == 0)
    def _zero_padding():
        kta_sc[...] = jnp.zeros_like(kta_sc)
        ktn_sc[...] = jnp.zeros_like(ktn_sc)
        hb_sc[...] = jnp.zeros_like(hb_sc)

    h = _layer_norm(x_ref[...], ln_in_g[...], ln_in_b[...])
    hb_sc[0:ts, :] = h.astype(BF16)

    def proj(col, width):
        return jnp.dot(hb_sc[0:ts, :], _std_columns(w_lo, w_hi, col, width), preferred_element_type=F32)

    u_sc[0:SUBLANES, :] = jnp.zeros((SUBLANES, CONV_DIM), F32)
    u_sc[SUBLANES - (CONV_W - 1):SUBLANES, :] = conv0_ref[...]
    mix_a = _conv_branch(proj, u_sc, conv_w, conv_ref, w_pa, ts)

    def proj_t(row, height):
        return lax.dot_general(w_t[row:row + height, :], hb_sc[...], (((1,), (1,)), ((), ())),
                               preferred_element_type=F32)

    kt = proj_t(_T_K, ATT_DIM)
    vf_t = proj_t(_T_V, ATT_DIM + F_ROWS)
    vt = vf_t[0:ATT_DIM, :]
    kt_ref[...] = kt[:, :ts]
    vt_ref[...] = vt[:, :ts]
    logf_t = _log_sigmoid(vf_t[ATT_DIM:ATT_DIM + SUBLANES, :] + b_f_col[0:SUBLANES, :])
    lft_ref[...] = logf_t[:, :ts]

    c_past = _cumsum(plft_ref[...], 1)
    c_last = c_past[:, n_past - 1:n_past]
    c_new_t = _cumsum(logf_t, 1) + c_last
    past_pieces = [piece.astype(F32) for piece in _split3(c_past)]
    new_pieces = [piece.astype(F32) for piece in _split3(c_new_t)]
    for hd in range(N_HEADS):
        rows = slice(hd * HEAD_DIM, (hd + 1) * HEAD_DIM)
        _store_feature_major(kta_sc, hd, pkt_ref[rows, :].astype(BF16),
                             _extra_rows(past_pieces, hd, n_past, key_side=True))
        _store_feature_major(ktn_sc, hd, kt[rows, :].astype(BF16),
                             _extra_rows(new_pieces, hd, LANES, key_side=True))

    eye = (lax.broadcasted_iota(jnp.int32, (SUBLANES, LANES), 0)
           == lax.broadcasted_iota(jnp.int32, (SUBLANES, LANES), 1))
    c_last_row = jnp.sum(jnp.where(eye, jnp.broadcast_to(c_last, (SUBLANES, LANES)), 0.0),
                         axis=0, keepdims=True)
    logf = _log_sigmoid(jnp.dot(hb_sc[0:ts, :], w_fs[...], preferred_element_type=F32) + b_f_row[...])
    c_q = _cumsum(logf, 0) + c_last_row
    ex_q = _extra_lanes(c_q, route[:, :AUG_DIM]) + ones[:, :AUG_DIM]
    _store_token_major(qa_sc, slice(None), proj(_S_Q, ATT_DIM) * SCALE, ex_q)

    causal = (lax.broadcasted_iota(jnp.int32, (ts, LANES), 1)
              <= lax.broadcasted_iota(jnp.int32, (ts, LANES), 0))
    lane = lax.broadcasted_iota(jnp.int32, (ts, HEAD_BLOCK), 1)
    contract_lanes = (((1,), (1,)), ((), ()))
    scores = []
    for hd in range(N_HEADS):
        qa_h = qa_sc[:, hd * HEAD_BLOCK:(hd + 1) * HEAD_BLOCK]
        scores.append((jnp.dot(qa_h, kta_sc[hd], preferred_element_type=F32),
                       jnp.dot(qa_h, ktn_sc[hd], preferred_element_type=F32)))
    for pair in range(N_HEADS // 2):
        rows = slice(pair * HEAD_BLOCK, (pair + 1) * HEAD_BLOCK)
        vt_past = pvt_ref[rows, :].astype(BF16)
        vt_new = vt[rows, :].astype(BF16)
        probs, sums = [], []
        for hd in (2 * pair, 2 * pair + 1):
            s_past = scores[hd][0]
            s_new = jnp.where(causal, scores[hd][1], -jnp.inf)
            m = jnp.maximum(jnp.max(s_past, axis=-1, keepdims=True), jnp.max(s_new, axis=-1, keepdims=True))
            p_past = jnp.exp(s_past - m)
            p_new = jnp.exp(s_new - m)
            sums.append(jnp.sum(p_past, axis=-1, keepdims=True) + jnp.sum(p_new, axis=-1, keepdims=True))
            probs.append((p_past.astype(BF16), p_new.astype(BF16)))
        o = (lax.dot_general(jnp.concatenate([probs[0][0], probs[1][0]], axis=0), vt_past, contract_lanes,
                             preferred_element_type=F32)
             + lax.dot_general(jnp.concatenate([probs[0][1], probs[1][1]], axis=0), vt_new, contract_lanes,
                               preferred_element_type=F32))
        att_sc[:, rows] = jnp.where(lane < HEAD_DIM, o[0:ts, :] / sums[0], o[ts:2 * ts, :] / sums[1])

    y_ref[...] = _merge_and_norm(att_sc[...], mix_a, h, jax.nn.silu(proj(_S_ZB, ATT_DIM)),
                                 jax.nn.sigmoid(proj(_S_GM, D_MODEL)), w_pb, w_out, ln_g, ln_b)


def _run_sample(x, conv0, pkt, pvt, plft, params):
    bsz, ts, _ = x.shape
    n_past = pkt.shape[2]
    assert ts <= LANES and ts % SUBLANES == 0
    per_batch = lambda rows, width: pl.BlockSpec((None, rows, width), lambda b: (b, 0, 0))
    out_shape = (jax.ShapeDtypeStruct((bsz, ts, D_MODEL), F32),
                 jax.ShapeDtypeStruct((bsz, ATT_DIM, ts), F32),
                 jax.ShapeDtypeStruct((bsz, ATT_DIM, ts), F32),
                 jax.ShapeDtypeStruct((bsz, N_HEADS, ts), F32),
                 jax.ShapeDtypeStruct((bsz, CONV_W - 1, CONV_DIM), F32))
    out_specs = (per_batch(ts, D_MODEL), per_batch(ATT_DIM, ts), per_batch(ATT_DIM, ts),
                 per_batch(N_HEADS, ts), per_batch(CONV_W - 1, CONV_DIM))
    in_specs = [per_batch(ts, D_MODEL), per_batch(CONV_W - 1, CONV_DIM), per_batch(ATT_DIM, n_past),
                per_batch(ATT_DIM, n_past), per_batch(N_HEADS, n_past)]
    scratch = [
        pltpu.VMEM((N_HEADS, HEAD_BLOCK, n_past), BF16),
        pltpu.VMEM((N_HEADS, HEAD_BLOCK, LANES), BF16),
        pltpu.VMEM((LANES, D_MODEL), BF16),
        pltpu.VMEM((ts, AUG_DIM), BF16),
        pltpu.VMEM((ts, ATT_DIM), F32),
        pltpu.VMEM((SUBLANES + ts, CONV_DIM), F32),
    ]
    return pl.pallas_call(
        functools.partial(_sample_kernel, ts=ts, n_past=n_past),
        grid=(bsz,),
        in_specs=in_specs + [_resident(p.shape) for p in params],
        out_specs=out_specs,
        out_shape=out_shape,
        scratch_shapes=scratch,
        compiler_params=pltpu.CompilerParams(
            dimension_semantics=("arbitrary",),
            vmem_limit_bytes=VMEM_LIMIT_BYTES),
        name="trunk_sample",
    )(x, conv0, pkt, pvt, plft, *params)


def kernel(x_prompt, x_sample, cache_k, cache_v, cache_logf, state_conv, ln_in_g, ln_in_b, w_in, b_f,
           conv_w, w_proj_a, w_proj_b, w_out, ln_g, ln_b):
    assert w_in.shape[0] == DEPTH
    route_np, ones_np = _routing_constants()
    w = w_in[0]
    row = lambda a: a.reshape(1, -1).astype(F32)
    w_t = jnp.concatenate([w[:, _W_Q:_W_F + N_HEADS],
                           jnp.zeros((D_MODEL, F_ROWS - N_HEADS), w.dtype)], axis=1).T.astype(BF16)
    tm = PROMPT_TILE
    p = dict(
        ln_in_g=row(ln_in_g), ln_in_b=row(ln_in_b),
        w_lo=w[:, :_W_V].astype(BF16), w_hi=w[:, _W_ZB:].astype(BF16),
        w_fs=jnp.pad(w[:, _W_F:_W_F + N_HEADS], ((0, 0), (0, LANES - N_HEADS))).astype(BF16),
        b_f_row=jnp.pad(row(b_f[0]), ((0, 0), (0, LANES - N_HEADS))),
        w_t=w_t,
        b_f_col=jnp.pad(b_f[0].astype(F32).reshape(N_HEADS, 1), ((0, F_ROWS - N_HEADS), (0, 0))),
        conv_w=conv_w[0].astype(F32),
        w_pa=w_proj_a[0].astype(BF16), w_pb=w_proj_b[0].astype(BF16), w_out=w_out[0].astype(BF16),
        ln_g=row(ln_g[0]), ln_b=row(ln_b[0]),
        route=jnp.asarray(route_np, BF16), ones=jnp.asarray(ones_np, F32),
        tri=jnp.asarray(np.triu(np.ones((tm, tm), np.float32)), BF16),
    )
    bs = x_sample.shape[0]
    n_past = cache_k.shape[2]
    feature_major = lambda a: a.transpose(0, 2, 3, 1).reshape(bs, ATT_DIM, n_past)

    y_p, kt_p, vt_p, lft_p, cv_p = _run_prompt(x_prompt, [p[name] for name in PROMPT_PARAMS], tm=tm)
    y_s, kt_s, vt_s, lft_s, cv_s = _run_sample(
        x_sample, state_conv[0], feature_major(cache_k[0]), feature_major(cache_v[0]),
        cache_logf[0].transpose(0, 2, 1), [p[name] for name in SAMPLE_PARAMS])

    heads_t = lambda a: a.reshape(a.shape[0], N_HEADS, HEAD_DIM, a.shape[2]).transpose(0, 3, 1, 2)[None]
    tokens_t = lambda a: a.transpose(0, 2, 1)[None]
    return (y_p, y_s, heads_t(kt_p), heads_t(vt_p), tokens_t(lft_p), cv_p[None],
            heads_t(kt_s), heads_t(vt_s), tokens_t(lft_s), cv_s[None])
```

```python
import functools

import numpy as np
import jax
import jax.numpy as jnp
from jax import lax
from jax.experimental import pallas as pl
from jax.experimental.pallas import tpu as pltpu

D_MODEL = 1024
CONV_DIM = 512
CONV_W = 3
N_HEADS = 8
HEAD_DIM = 64
ATT_DIM = N_HEADS * HEAD_DIM
LN_EPS = 1e-5
DEPTH = 1
ALPHA = (2.0 * DEPTH) ** 0.25
SCALE = HEAD_DIM ** -0.5

LANES = 128
SUBLANES = 8
HEAD_BLOCK = LANES
AUG_DIM = N_HEADS * HEAD_BLOCK
N_SPLIT = 3
ONES_LANE = N_SPLIT * N_HEADS
EXTRA_ROWS = 16
F_ROWS = 16
VMEM_LIMIT_BYTES = 56 * 1024 * 1024
V_ROWS = HEAD_DIM + 16
LOG2E = 1.4426950408889634
UNDERFLOW_LOG2 = 156.0
NORM_SLACK = 1.02
PIPE_SLOTS = 3
SCORE_LOOKAHEAD = 4

_W_Q = 4 * CONV_DIM
_W_V = _W_Q + 2 * ATT_DIM
_W_F = _W_Q + 3 * ATT_DIM
_W_ZB = _W_F + N_HEADS
_S_XA, _S_GB, _S_GC, _S_ZA, _S_Q, _S_K, _S_ZB, _S_GA, _S_GM = (
    0, 512, 1024, 1536, 2048, 2560, 3072, 3584, 4608)
_T_Q, _T_K, _T_V, _T_F = 0, 512, 1024, 1536

PROMPT_TILE = 256
PROMPT_PARAMS = ("ln_in_g", "ln_in_b", "w_lo", "w_hi", "w_t", "b_f_col", "conv_w", "w_pa", "w_pb", "w_out",
                 "ln_g", "ln_b", "route", "tri")
SAMPLE_PARAMS = ("ln_in_g", "ln_in_b", "w_lo", "w_hi", "w_fs", "b_f_row", "w_t", "b_f_col", "conv_w", "w_pa",
                 "w_pb", "w_out", "ln_g", "ln_b", "route", "ones")

F32 = jnp.float32
BF16 = jnp.bfloat16


def _routing_constants():
    route = np.zeros((LANES, 2 * AUG_DIM), np.float32)
    ones = np.zeros((1, 2 * AUG_DIM), np.float32)
    for h in range(N_HEADS):
        base = h * HEAD_BLOCK + (HEAD_DIM if h % 2 == 0 else 0)
        for r in range(N_SPLIT):
            route[r * N_HEADS + h, base + 2 * r] = 1.0
            ones[0, base + 2 * r + 1] = 1.0
            ones[0, AUG_DIM + base + 2 * r] = 1.0
            route[r * N_HEADS + h, AUG_DIM + base + 2 * r + 1] = -1.0
    route[ONES_LANE, :] = ones[0, :]
    return route, ones


def _layer_norm(x, g, b):
    mu = jnp.mean(x, axis=-1, keepdims=True)
    xc = x - mu
    var = jnp.mean(xc * xc, axis=-1, keepdims=True)
    return xc * lax.rsqrt(var + LN_EPS) * g + b


def _log_sigmoid(x):
    return jnp.minimum(x, 0.0) - jnp.log1p(jnp.exp(-jnp.abs(x)))


def _split3(x):
    hi = x.astype(BF16)
    r1 = x - hi.astype(F32)
    mid = r1.astype(BF16)
    lo = (r1 - mid.astype(F32)).astype(BF16)
    return hi, mid, lo


def _cumsum(x, axis):
    n = x.shape[axis]
    idx = lax.broadcasted_iota(jnp.int32, x.shape, axis)
    s = 1
    while s < n:
        x = x + jnp.where(idx >= s, pltpu.roll(x, s, axis), 0.0)
        s *= 2
    return x


def _extra_lanes(c, route):
    lane = lax.broadcasted_iota(jnp.int32, c.shape, 1)
    valid = lane < N_HEADS
    packed = jnp.zeros_like(c)
    for r, piece in enumerate(_split3(c)):
        p32 = jnp.where(valid, piece.astype(F32), 0.0)
        packed = packed + (pltpu.roll(p32, r * N_HEADS, 1) if r else p32)
    return jnp.dot(packed.astype(BF16), route, preferred_element_type=F32)


def _store_token_major(dst_ref, rows, data, extras):
    n = data.shape[0]
    lane = lax.broadcasted_iota(jnp.int32, (n, HEAD_BLOCK), 1)
    for h in range(N_HEADS):
        pair = data[:, (h // 2) * HEAD_BLOCK:(h // 2 + 1) * HEAD_BLOCK]
        ex = extras[:, h * HEAD_BLOCK:(h + 1) * HEAD_BLOCK]
        keep = (lane < HEAD_DIM) if h % 2 == 0 else (lane >= HEAD_DIM)
        dst_ref[rows, h * HEAD_BLOCK:(h + 1) * HEAD_BLOCK] = jnp.where(keep, pair, ex).astype(BF16)


def _extra_rows(pieces, h, n, key_side):
    r = lax.broadcasted_iota(jnp.int32, (EXTRA_ROWS, n), 0)
    value_row, sign = (1, -1.0) if key_side else (0, 1.0)
    out = jnp.where((r < 2 * N_SPLIT) & ((r & 1) != value_row), 1.0, 0.0)
    for i, piece in enumerate(pieces):
        out = jnp.where(r == 2 * i + value_row,
                        jnp.broadcast_to(sign * piece[h:h + 1, :], (EXTRA_ROWS, n)), out)
    return out.astype(BF16)


def _store_feature_major(dst_ref, h, data, extra):
    if h % 2 == 0:
        dst_ref[h, 0:HEAD_DIM, :] = data
        dst_ref[h, HEAD_DIM:HEAD_DIM + EXTRA_ROWS, :] = extra
    else:
        dst_ref[h, 0:EXTRA_ROWS, :] = extra
        dst_ref[h, HEAD_DIM:2 * HEAD_DIM, :] = data


def _std_columns(w_lo, w_hi, col, width):
    split = w_lo.shape[1]
    assert col + width <= split or col >= split
    return w_lo[:, col:col + width] if col < split else w_hi[:, col - split:col - split + width]


def _conv_branch(proj, u_sc, conv_w, conv_ref, w_pa, n):
    first = SUBLANES - (CONV_W - 1)
    u_sc[SUBLANES:SUBLANES + n, :] = proj(_S_GC, CONV_DIM) * proj(_S_XA, CONV_DIM)
    conv = sum(u_sc[first + i:first + i + n, :] * conv_w[i:i + 1, :] for i in range(CONV_W))
    new_conv = u_sc[first + n:SUBLANES + n, :]
    conv_ref[...] = new_conv
    u_sc[first:SUBLANES, :] = new_conv
    y_a = proj(_S_GB, CONV_DIM) * conv * jax.nn.silu(proj(_S_ZA, CONV_DIM))
    return jax.nn.sigmoid(proj(_S_GA, D_MODEL)) * jnp.dot(
        y_a.astype(BF16), w_pa[...], preferred_element_type=F32)


def _merge_and_norm(att, mix_a, h, silu_zb, gate_b, w_pb, w_out, ln_g, ln_b):
    y_b = att * silu_zb
    mixed = mix_a + gate_b * jnp.dot(
        y_b.astype(BF16), w_pb[...], preferred_element_type=F32)
    sub = jnp.dot(mixed.astype(BF16), w_out[...], preferred_element_type=F32)
    return _layer_norm(ALPHA * h + sub, ln_g[...], ln_b[...])


def _head_dots(at, bt):
    n = at.shape[1]
    row = lax.broadcasted_iota(jnp.int32, (F_ROWS, n), 0)
    prod = at * bt
    acc = jnp.zeros((F_ROWS, n), F32)
    for h in range(N_HEADS):
        dot = jnp.sum(prod[h * HEAD_DIM:(h + 1) * HEAD_DIM, :], axis=0, keepdims=True)
        acc = jnp.where(row == h, jnp.broadcast_to(dot, (F_ROWS, n)), acc)
    return acc


def _dead_key_tiles(j, qt2, kt, c2_end, kn_sc, ce_sc):
    lane = lax.broadcasted_iota(jnp.int32, (F_ROWS, LANES), 1)
    row = lax.broadcasted_iota(jnp.int32, (F_ROWS, LANES), 0)
    spread = lambda col: jnp.broadcast_to(col, (F_ROWS, LANES))
    k_sq = jnp.max(_head_dots(kt, kt), axis=1, keepdims=True)
    q_sq = jnp.max(_head_dots(qt2, qt2), axis=1, keepdims=True)
    diag_min = jnp.min(_head_dots(qt2, kt), axis=1, keepdims=True)
    kn = jnp.where(lane == j, spread(k_sq), jnp.where(lane < j, kn_sc[...], 0.0))
    ce = jnp.where(lane == j, spread(c2_end), jnp.where(lane < j, ce_sc[...], 0.0))
    kn_sc[...] = kn
    ce_sc[...] = ce
    ce_prev = jnp.sum(jnp.where(lane == j - 1, ce, 0.0), axis=1, keepdims=True)
    bound = (NORM_SLACK * jnp.sqrt(q_sq * kn) + (NORM_SLACK - 1.0) * jnp.sqrt(q_sq * k_sq) - diag_min
             + ce_prev - ce)
    worst = jnp.max(jnp.where(row < N_HEADS, bound, -jnp.inf), axis=0, keepdims=True)
    dead = (worst < -UNDERFLOW_LOG2) & (lane[0:1, :] <= j - 2)
    return jnp.min(jnp.where(dead, LANES, lane[0:1, :]))


def _attention_t(n_key_tiles, first, tm, kaug_sc, qta_sc, vt_sc, attt_sc):
    keys = pl.ds(pl.multiple_of(first * tm, tm), n_key_tiles * tm)
    causal = (lax.broadcasted_iota(jnp.int32, (tm, tm), 0)
              <= lax.broadcasted_iota(jnp.int32, (tm, tm), 1))

    def scores(hd):
        return jnp.dot(kaug_sc[keys, hd * HEAD_BLOCK:(hd + 1) * HEAD_BLOCK], qta_sc[hd],
                       preferred_element_type=F32)

    pending = [scores(hd) for hd in range(SCORE_LOOKAHEAD)]
    for hd in range(N_HEADS):
        s = pending.pop(0)
        if hd + SCORE_LOOKAHEAD < N_HEADS:
            pending.append(scores(hd + SCORE_LOOKAHEAD))
        blocks = [s[i * tm:(i + 1) * tm, :] for i in range(n_key_tiles)]
        blocks[-1] = jnp.where(causal, blocks[-1], -jnp.inf)
        m = functools.reduce(jnp.maximum, [jnp.max(blk, axis=0, keepdims=True) for blk in blocks])
        o = jnp.zeros((V_ROWS, tm), F32)
        for i, blk in enumerate(blocks):
            o = o + jnp.dot(vt_sc[first + i, hd * V_ROWS:(hd + 1) * V_ROWS, :],
                            jnp.exp2(blk - m).astype(BF16), preferred_element_type=F32)
        attt_sc[hd * HEAD_DIM:(hd + 1) * HEAD_DIM, :] = o[0:HEAD_DIM, :] / o[HEAD_DIM:HEAD_DIM + 1, :]


def _prompt_kernel(x_ref, ln_in_g, ln_in_b, w_lo, w_hi, w_t, b_f_col, conv_w, w_pa, w_pb, w_out,
                   ln_g, ln_b, route, tri,
                   y_ref, kt_ref, vt_ref, lft_ref, conv_ref,
                   kaug_sc, vt_sc, h_sc, hb_sc, z_sc, qta_sc, attt_sc, zb_sc, gm_sc, u_sc, carryt_sc,
                   kn_sc, ce_sc, first_sm, *, tm, n_tiles, n_total):
    t = pl.program_id(0)

    def norm_in():
        h = _layer_norm(x_ref[...], ln_in_g[...], ln_in_b[...])
        h_sc[lax.rem(t, PIPE_SLOTS)] = h
        hb_sc[lax.rem(t, PIPE_SLOTS)] = h.astype(BF16)

    def norm_out():
        y_ref[...] = _layer_norm(z_sc[...], ln_g[...], ln_b[...])

    @pl.when(t == 0)
    def _fill():
        for ref in (z_sc, kaug_sc, vt_sc, h_sc, hb_sc, attt_sc, zb_sc, gm_sc, u_sc, qta_sc, carryt_sc,
                    kn_sc, ce_sc):
            ref[...] = jnp.zeros_like(ref)
        first_sm[0] = 0
        ones_row = jnp.where(lax.broadcasted_iota(jnp.int32, (V_ROWS - HEAD_DIM, tm), 0) == 0, 1.0, 0.0)
        for i in range(n_tiles):
            for hd in range(N_HEADS):
                vt_sc[i, hd * V_ROWS + HEAD_DIM:(hd + 1) * V_ROWS, :] = ones_row.astype(BF16)
        norm_in()

    @pl.when((t >= 1) & (t <= n_total + 1))
    def _mixers():
        tile_a = t - 1
        tile_b = t - 2
        ja = lax.rem(tile_a, n_tiles)
        jb = lax.rem(tile_b, n_tiles)
        slot_a = lax.rem(tile_a, PIPE_SLOTS)
        slot_b = lax.rem(t + 1, PIPE_SLOTS)
        gate_a = tile_a & 1
        gate_b = t & 1

        @pl.when(ja == 0)
        def _start_of_sequence():
            carryt_sc[...] = jnp.zeros_like(carryt_sc)

        @pl.when(jb == 0)
        def _start_of_conv():
            u_sc[0:SUBLANES, :] = jnp.zeros((SUBLANES, CONV_DIM), F32)

        first = first_sm[0]
        for n_key_tiles in range(1, n_tiles + 1):
            pl.when((t >= 2) & (jb - first + 1 == n_key_tiles))(functools.partial(
                _attention_t, n_key_tiles, first, tm, kaug_sc, qta_sc, vt_sc, attt_sc))

        def proj(slot, col, width):
            return jnp.dot(hb_sc[slot], _std_columns(w_lo, w_hi, col, width), preferred_element_type=F32)

        def proj_t(row, height):
            return lax.dot_general(w_t[row:row + height, :], hb_sc[slot_a], (((1,), (1,)), ((), ())),
                                   preferred_element_type=F32)

        norm_out()

        vf_t = proj_t(_T_V, ATT_DIM + F_ROWS)
        qk_t = proj_t(_T_Q, 2 * ATT_DIM)
        kt = qk_t[ATT_DIM:2 * ATT_DIM, :]
        vt = vf_t[0:ATT_DIM, :]
        kt_ref[...] = kt
        vt_ref[...] = vt
        logf_t = _log_sigmoid(vf_t[ATT_DIM:ATT_DIM + F_ROWS, :] + b_f_col[...])
        lft_ref[...] = logf_t[:N_HEADS, :]
        sums = jnp.dot(jnp.concatenate(_split3(logf_t), axis=0), tri[...], preferred_element_type=F32)

        mix_a = _conv_branch(functools.partial(proj, slot_b), u_sc, conv_w, conv_ref, w_pa, tm)

        zb_sc[gate_a] = jax.nn.silu(proj(slot_a, _S_ZB, ATT_DIM))
        gm_sc[gate_a] = jax.nn.sigmoid(proj(slot_a, _S_GM, D_MODEL))

        y_b = attt_sc[...].T * zb_sc[gate_b]
        mixed = mix_a + gm_sc[gate_b] * jnp.dot(y_b.astype(BF16), w_pb[...], preferred_element_type=F32)

        c_t = sum(sums[r * F_ROWS:(r + 1) * F_ROWS, :] for r in range(N_SPLIT)) + carryt_sc[:, 0:1]
        carryt_sc[...] = jnp.broadcast_to(c_t[:, tm - 1:tm], carryt_sc.shape)
        c_pieces = [piece.astype(F32) for piece in _split3(c_t * LOG2E)]
        qt = qk_t[0:ATT_DIM, :] * (SCALE * LOG2E)
        for hd in range(N_HEADS):
            _store_feature_major(qta_sc, hd, qt[hd * HEAD_DIM:(hd + 1) * HEAD_DIM, :].astype(BF16),
                                 _extra_rows(c_pieces, hd, tm, key_side=False))
            vt_sc[ja, hd * V_ROWS:hd * V_ROWS + HEAD_DIM, :] = (
                vt[hd * HEAD_DIM:(hd + 1) * HEAD_DIM, :].astype(BF16))
        ones_and_pad = jnp.where(lax.broadcasted_iota(jnp.int32, (LANES - ONES_LANE, tm), 0) == 0, 1.0, 0.0)
        packed_t = jnp.concatenate([piece[0:N_HEADS, :] for piece in c_pieces] + [ones_and_pad], axis=0)
        ex_k = jnp.dot(packed_t.T.astype(BF16), route[:, AUG_DIM:], preferred_element_type=F32)
        _store_token_major(kaug_sc, pl.ds(pl.multiple_of(ja * tm, tm), tm), kt.T, ex_k)
        first_sm[0] = _dead_key_tiles(ja, qt, kt, c_t[:, tm - 1:tm] * LOG2E, kn_sc, ce_sc)

        z_sc[...] = ALPHA * h_sc[slot_b] + jnp.dot(mixed.astype(BF16), w_out[...], preferred_element_type=F32)

        norm_in()

    @pl.when(t == n_total + 2)
    def _drain():
        norm_out()


def _resident(shape):
    return pl.BlockSpec(shape, lambda *_: (0,) * len(shape), pipeline_mode=pl.Buffered(1))


def _run_prompt(x, params, *, tm):
    bsz, t, _ = x.shape
    n_tiles = t // tm
    assert n_tiles * tm == t
    n_total = bsz * n_tiles
    tile = lambda s, lag: jnp.clip(s - lag, 0, n_total - 1)
    per_tile = lambda width, lag: pl.BlockSpec(
        (None, tm, width), lambda s: (tile(s, lag) // n_tiles, tile(s, lag) % n_tiles, 0))
    per_tile_t = lambda rows: pl.BlockSpec(
        (None, rows, tm), lambda s: (tile(s, 1) // n_tiles, 0, tile(s, 1) % n_tiles))
    out_shape = (jax.ShapeDtypeStruct((bsz, t, D_MODEL), F32),
                 jax.ShapeDtypeStruct((bsz, ATT_DIM, t), F32),
                 jax.ShapeDtypeStruct((bsz, ATT_DIM, t), F32),
                 jax.ShapeDtypeStruct((bsz, N_HEADS, t), F32),
                 jax.ShapeDtypeStruct((bsz, CONV_W - 1, CONV_DIM), F32))
    out_specs = (per_tile(D_MODEL, 3), per_tile_t(ATT_DIM), per_tile_t(ATT_DIM), per_tile_t(N_HEADS),
                 pl.BlockSpec((None, CONV_W - 1, CONV_DIM), lambda s: (tile(s, 2) // n_tiles, 0, 0)))
    scratch = [
        pltpu.VMEM((t, AUG_DIM), BF16),
        pltpu.VMEM((n_tiles, N_HEADS * V_ROWS, tm), BF16),
        pltpu.VMEM((PIPE_SLOTS, tm, D_MODEL), F32),
        pltpu.VMEM((PIPE_SLOTS, tm, D_MODEL), BF16),
        pltpu.VMEM((tm, D_MODEL), F32),
        pltpu.VMEM((N_HEADS, HEAD_BLOCK, tm), BF16),
        pltpu.VMEM((ATT_DIM, tm), F32),
        pltpu.VMEM((2, tm, ATT_DIM), F32),
        pltpu.VMEM((2, tm, D_MODEL), F32),
        pltpu.VMEM((SUBLANES + tm, CONV_DIM), F32),
        pltpu.VMEM((F_ROWS, LANES), F32),
        pltpu.VMEM((F_ROWS, LANES), F32),
        pltpu.VMEM((F_ROWS, LANES), F32),
        pltpu.SMEM((1,), jnp.int32),
    ]
    return pl.pallas_call(
        functools.partial(_prompt_kernel, tm=tm, n_tiles=n_tiles, n_total=n_total),
        grid=(n_total + PIPE_SLOTS,),
        in_specs=[per_tile(D_MODEL, 0)] + [_resident(p.shape) for p in params],
        out_specs=out_specs,
        out_shape=out_shape,
        scratch_shapes=scratch,
        compiler_params=pltpu.CompilerParams(
            dimension_semantics=("arbitrary",),
            vmem_limit_bytes=VMEM_LIMIT_BYTES),
        name="trunk_prompt",
    )(x, *params)


def _sample_kernel(x_ref, conv0_ref, pkt_ref, pvt_ref, plft_ref,
                   ln_in_g, ln_in_b, w_lo, w_hi, w_fs, b_f_row, w_t, b_f_col, conv_w, w_pa, w_pb, w_out,
                   ln_g, ln_b, route, ones,
                   y_ref, kt_ref, vt_ref, lft_ref, conv_ref,
                   kta_sc, ktn_sc, hb_sc, qa_sc, att_sc, u_sc, *, ts, n_past):
    @pl.when(pl.program_id(0) == 0)
    def _zero_padding():
        kta_sc[...] = jnp.zeros_like(kta_sc)
        ktn_sc[...] = jnp.zeros_like(ktn_sc)
        hb_sc[...] = jnp.zeros_like(hb_sc)

    h = _layer_norm(x_ref[...], ln_in_g[...], ln_in_b[...])
    hb_sc[0:ts, :] = h.astype(BF16)

    def proj(col, width):
        return jnp.dot(hb_sc[0:ts, :], _std_columns(w_lo, w_hi, col, width), preferred_element_type=F32)

    u_sc[0:SUBLANES, :] = jnp.zeros((SUBLANES, CONV_DIM), F32)
    u_sc[SUBLANES - (CONV_W - 1):SUBLANES, :] = conv0_ref[...]
    mix_a = _conv_branch(proj, u_sc, conv_w, conv_ref, w_pa, ts)

    def proj_t(row, height):
        return lax.dot_general(w_t[row:row + height, :], hb_sc[...], (((1,), (1,)), ((), ())),
                               preferred_element_type=F32)

    kt = proj_t(_T_K, ATT_DIM)
    vf_t = proj_t(_T_V, ATT_DIM + F_ROWS)
    vt = vf_t[0:ATT_DIM, :]
    kt_ref[...] = kt[:, :ts]
    vt_ref[...] = vt[:, :ts]
    logf_t = _log_sigmoid(vf_t[ATT_DIM:ATT_DIM + SUBLANES, :] + b_f_col[0:SUBLANES, :])
    lft_ref[...] = logf_t[:, :ts]

    c_past = _cumsum(plft_ref[...], 1)
    c_last = c_past[:, n_past - 1:n_past]
    c_new_t = _cumsum(logf_t, 1) + c_last
    past_pieces = [piece.astype(F32) for piece in _split3(c_past)]
    new_pieces = [piece.astype(F32) for piece in _split3(c_new_t)]
    for hd in range(N_HEADS):
        rows = slice(hd * HEAD_DIM, (hd + 1) * HEAD_DIM)
        _store_feature_major(kta_sc, hd, pkt_ref[rows, :].astype(BF16),
                             _extra_rows(past_pieces, hd, n_past, key_side=True))
        _store_feature_major(ktn_sc, hd, kt[rows, :].astype(BF16),
                             _extra_rows(new_pieces, hd, LANES, key_side=True))

    eye = (lax.broadcasted_iota(jnp.int32, (SUBLANES, LANES), 0)
           == lax.broadcasted_iota(jnp.int32, (SUBLANES, LANES), 1))
    c_last_row = jnp.sum(jnp.where(eye, jnp.broadcast_to(c_last, (SUBLANES, LANES)), 0.0),
                         axis=0, keepdims=True)
    logf = _log_sigmoid(jnp.dot(hb_sc[0:ts, :], w_fs[...], preferred_element_type=F32) + b_f_row[...])
    c_q = _cumsum(logf, 0) + c_last_row
    ex_q = _extra_lanes(c_q, route[:, :AUG_DIM]) + ones[:, :AUG_DIM]
    _store_token_major(qa_sc, slice(None), proj(_S_Q, ATT_DIM) * SCALE, ex_q)

    causal = (lax.broadcasted_iota(jnp.int32, (ts, LANES), 1)
              <= lax.broadcasted_iota(jnp.int32, (ts, LANES), 0))
    lane = lax.broadcasted_iota(jnp.int32, (ts, HEAD_BLOCK), 1)
    contract_lanes = (((1,), (1,)), ((), ()))
    scores = []
    for hd in range(N_HEADS):
        qa_h = qa_sc[:, hd * HEAD_BLOCK:(hd + 1) * HEAD_BLOCK]
        scores.append((jnp.dot(qa_h, kta_sc[hd], preferred_element_type=F32),
                       jnp.dot(qa_h, ktn_sc[hd], preferred_element_type=F32)))
    for pair in range(N_HEADS // 2):
        rows = slice(pair * HEAD_BLOCK, (pair + 1) * HEAD_BLOCK)
        vt_past = pvt_ref[rows, :].astype(BF16)
        vt_new = vt[rows, :].astype(BF16)
        probs, sums = [], []
        for hd in (2 * pair, 2 * pair + 1):
            s_past = scores[hd][0]
            s_new = jnp.where(causal, scores[hd][1], -jnp.inf)
            m = jnp.maximum(jnp.max(s_past, axis=-1, keepdims=True), jnp.max(s_new, axis=-1, keepdims=True))
            p_past = jnp.exp(s_past - m)
            p_new = jnp.exp(s_new - m)
            sums.append(jnp.sum(p_past, axis=-1, keepdims=True) + jnp.sum(p_new, axis=-1, keepdims=True))
            probs.append((p_past.astype(BF16), p_new.astype(BF16)))
        o = (lax.dot_general(jnp.concatenate([probs[0][0], probs[1][0]], axis=0), vt_past, contract_lanes,
                             preferred_element_type=F32)
             + lax.dot_general(jnp.concatenate([probs[0][1], probs[1][1]], axis=0), vt_new, contract_lanes,
                               preferred_element_type=F32))
        att_sc[:, rows] = jnp.where(lane < HEAD_DIM, o[0:ts, :] / sums[0], o[ts:2 * ts, :] / sums[1])

    y_ref[...] = _merge_and_norm(att_sc[...], mix_a, h, jax.nn.silu(proj(_S_ZB, ATT_DIM)),
                                 jax.nn.sigmoid(proj(_S_GM, D_MODEL)), w_pb, w_out, ln_g, ln_b)


def _run_sample(x, conv0, pkt, pvt, plft, params):
    bsz, ts, _ = x.shape
    n_past = pkt.shape[2]
    assert ts <= LANES and ts % SUBLANES == 0
    per_batch = lambda rows, width: pl.BlockSpec((None, rows, width), lambda b: (b, 0, 0))
    out_shape = (jax.ShapeDtypeStruct((bsz, ts, D_MODEL), F32),
                 jax.ShapeDtypeStruct((bsz, ATT_DIM, ts), F32),
                 jax.ShapeDtypeStruct((bsz, ATT_DIM, ts), F32),
                 jax.ShapeDtypeStruct((bsz, N_HEADS, ts), F32),
                 jax.ShapeDtypeStruct((bsz, CONV_W - 1, CONV_DIM), F32))
    out_specs = (per_batch(ts, D_MODEL), per_batch(ATT_DIM, ts), per_batch(ATT_DIM, ts),
                 per_batch(N_HEADS, ts), per_batch(CONV_W - 1, CONV_DIM))
    in_specs = [per_batch(ts, D_MODEL), per_batch(CONV_W - 1, CONV_DIM), per_batch(ATT_DIM, n_past),
                per_batch(ATT_DIM, n_past), per_batch(N_HEADS, n_past)]
    scratch = [
        pltpu.VMEM((N_HEADS, HEAD_BLOCK, n_past), BF16),
        pltpu.VMEM((N_HEADS, HEAD_BLOCK, LANES), BF16),
        pltpu.VMEM((LANES, D_MODEL), BF16),
        pltpu.VMEM((ts, AUG_DIM), BF16),
        pltpu.VMEM((ts, ATT_DIM), F32),
        pltpu.VMEM((SUBLANES + ts, CONV_DIM), F32),
    ]
    return pl.pallas_call(
        functools.partial(_sample_kernel, ts=ts, n_past=n_past),
        grid=(bsz,),
        in_specs=in_specs + [_resident(p.shape) for p in params],
        out_specs=out_specs,
        out_shape=out_shape,
        scratch_shapes=scratch,
        compiler_params=pltpu.CompilerParams(
            dimension_semantics=("arbitrary",),
            vmem_limit_bytes=VMEM_LIMIT_BYTES),
        name="trunk_sample",
    )(x, conv0, pkt, pvt, plft, *params)


def kernel(x_prompt, x_sample, cache_k, cache_v, cache_logf, state_conv, ln_in_g, ln_in_b, w_in, b_f,
           conv_w, w_proj_a, w_proj_b, w_out, ln_g, ln_b):
    assert w_in.shape[0] == DEPTH
    route_np, ones_np = _routing_constants()
    w = w_in[0]
    row = lambda a: a.reshape(1, -1).astype(F32)
    w_t = jnp.concatenate([w[:, _W_Q:_W_F + N_HEADS],
                           jnp.zeros((D_MODEL, F_ROWS - N_HEADS), w.dtype)], axis=1).T.astype(BF16)
    tm = PROMPT_TILE
    p = dict(
        ln_in_g=row(ln_in_g), ln_in_b=row(ln_in_b),
        w_lo=w[:, :_W_V].astype(BF16), w_hi=w[:, _W_ZB:].astype(BF16),
        w_fs=jnp.pad(w[:, _W_F:_W_F + N_HEADS], ((0, 0), (0, LANES - N_HEADS))).astype(BF16),
        b_f_row=jnp.pad(row(b_f[0]), ((0, 0), (0, LANES - N_HEADS))),
        w_t=w_t,
        b_f_col=jnp.pad(b_f[0].astype(F32).reshape(N_HEADS, 1), ((0, F_ROWS - N_HEADS), (0, 0))),
        conv_w=conv_w[0].astype(F32),
        w_pa=w_proj_a[0].astype(BF16), w_pb=w_proj_b[0].astype(BF16), w_out=w_out[0].astype(BF16),
        ln_g=row(ln_g[0]), ln_b=row(ln_b[0]),
        route=jnp.asarray(route_np, BF16), ones=jnp.asarray(ones_np, F32),
        tri=jnp.asarray(np.triu(np.ones((tm, tm), np.float32)), BF16),
    )
    bs = x_sample.shape[0]
    n_past = cache_k.shape[2]
    feature_major = lambda a: a.transpose(0, 2, 3, 1).reshape(bs, ATT_DIM, n_past)

    y_p, kt_p, vt_p, lft_p, cv_p = _run_prompt(x_prompt, [p[name] for name in PROMPT_PARAMS], tm=tm)
    y_s, kt_s, vt_s, lft_s, cv_s = _run_sample(
        x_sample, state_conv[0], feature_major(cache_k[0]), feature_major(cache_v[0]),
        cache_logf[0].transpose(0, 2, 1), [p[name] for name in SAMPLE_PARAMS])

    heads_t = lambda a: a.reshape(a.shape[0], N_HEADS, HEAD_DIM, a.shape[2]).transpose(0, 3, 1, 2)[None]
    tokens_t = lambda a: a.transpose(0, 2, 1)[None]
    return (y_p, y_s, heads_t(kt_p), heads_t(vt_p), tokens_t(lft_p), cv_p[None],
            heads_t(kt_s), heads_t(vt_s), tokens_t(lft_s), cv_s[None])
```

```python
import functools

import numpy as np
import jax
import jax.numpy as jnp
from jax import lax
from jax.experimental import pallas as pl
from jax.experimental.pallas import tpu as pltpu

D_MODEL = 1024
CONV_DIM = 512
CONV_W = 3
N_HEADS = 8
HEAD_DIM = 64
ATT_DIM = N_HEADS * HEAD_DIM
LN_EPS = 1e-5
DEPTH = 1
ALPHA = (2.0 * DEPTH) ** 0.25
SCALE = HEAD_DIM ** -0.5

LANES = 128
SUBLANES = 8
HEAD_BLOCK = LANES
AUG_DIM = N_HEADS * HEAD_BLOCK
N_SPLIT = 3
ONES_LANE = N_SPLIT * N_HEADS
EXTRA_ROWS = 16
F_ROWS = 16
VMEM_LIMIT_BYTES = 56 * 1024 * 1024
V_ROWS = HEAD_DIM + 16
LOG2E = 1.4426950408889634
UNDERFLOW_LOG2 = 156.0
NORM_SLACK = 1.02
PIPE_SLOTS = 3
SCORE_LOOKAHEAD = 4

_W_Q = 4 * CONV_DIM
_W_V = _W_Q + 2 * ATT_DIM
_W_F = _W_Q + 3 * ATT_DIM
_W_ZB = _W_F + N_HEADS
_S_XA, _S_GB, _S_GC, _S_ZA, _S_Q, _S_K, _S_ZB, _S_GA, _S_GM = (
    0, 512, 1024, 1536, 2048, 2560, 3072, 3584, 4608)
_T_Q, _T_K, _T_V, _T_F = 0, 512, 1024, 1536

PROMPT_TILE = 256
PROMPT_PARAMS = ("ln_in_g", "ln_in_b", "w_lo", "w_hi", "w_t", "b_f_col", "conv_w", "w_pa", "w_pb", "w_out",
                 "ln_g", "ln_b", "route", "tri")
SAMPLE_PARAMS = ("ln_in_g", "ln_in_b", "w_lo", "w_hi", "w_fs", "b_f_row", "w_t", "b_f_col", "conv_w", "w_pa",
                 "w_pb", "w_out", "ln_g", "ln_b", "route", "ones")

F32 = jnp.float32
BF16 = jnp.bfloat16


def _routing_constants():
    route = np.zeros((LANES, 2 * AUG_DIM), np.float32)
    ones = np.zeros((1, 2 * AUG_DIM), np.float32)
    for h in range(N_HEADS):
        base = h * HEAD_BLOCK + (HEAD_DIM if h % 2 == 0 else 0)
        for r in range(N_SPLIT):
            route[r * N_HEADS + h, base + 2 * r] = 1.0
            ones[0, base + 2 * r + 1] = 1.0
            ones[0, AUG_DIM + base + 2 * r] = 1.0
            route[r * N_HEADS + h, AUG_DIM + base + 2 * r + 1] = -1.0
    route[ONES_LANE, :] = ones[0, :]
    return route, ones


def _layer_norm(x, g, b):
    mu = jnp.mean(x, axis=-1, keepdims=True)
    xc = x - mu
    var = jnp.mean(xc * xc, axis=-1, keepdims=True)
    return xc * lax.rsqrt(var + LN_EPS) * g + b


def _log_sigmoid(x):
    return jnp.minimum(x, 0.0) - jnp.log1p(jnp.exp(-jnp.abs(x)))


def _split3(x):
    hi = x.astype(BF16)
    r1 = x - hi.astype(F32)
    mid = r1.astype(BF16)
    lo = (r1 - mid.astype(F32)).astype(BF16)
    return hi, mid, lo


def _cumsum(x, axis):
    n = x.shape[axis]
    idx = lax.broadcasted_iota(jnp.int32, x.shape, axis)
    s = 1
    while s < n:
        x = x + jnp.where(idx >= s, pltpu.roll(x, s, axis), 0.0)
        s *= 2
    return x


def _extra_lanes(c, route):
    lane = lax.broadcasted_iota(jnp.int32, c.shape, 1)
    valid = lane < N_HEADS
    packed = jnp.zeros_like(c)
    for r, piece in enumerate(_split3(c)):
        p32 = jnp.where(valid, piece.astype(F32), 0.0)
        packed = packed + (pltpu.roll(p32, r * N_HEADS, 1) if r else p32)
    return jnp.dot(packed.astype(BF16), route, preferred_element_type=F32)


def _store_token_major(dst_ref, rows, data, extras):
    n = data.shape[0]
    lane = lax.broadcasted_iota(jnp.int32, (n, HEAD_BLOCK), 1)
    for h in range(N_HEADS):
        pair = data[:, (h // 2) * HEAD_BLOCK:(h // 2 + 1) * HEAD_BLOCK]
        ex = extras[:, h * HEAD_BLOCK:(h + 1) * HEAD_BLOCK]
        keep = (lane < HEAD_DIM) if h % 2 == 0 else (lane >= HEAD_DIM)
        dst_ref[rows, h * HEAD_BLOCK:(h + 1) * HEAD_BLOCK] = jnp.where(keep, pair, ex).astype(BF16)


def _extra_rows(pieces, h, n, key_side):
    r = lax.broadcasted_iota(jnp.int32, (EXTRA_ROWS, n), 0)
    value_row, sign = (1, -1.0) if key_side else (0, 1.0)
    out = jnp.where((r < 2 * N_SPLIT) & ((r & 1) != value_row), 1.0, 0.0)
    for i, piece in enumerate(pieces):
        out = jnp.where(r == 2 * i + value_row,
                        jnp.broadcast_to(sign * piece[h:h + 1, :], (EXTRA_ROWS, n)), out)
    return out.astype(BF16)


def _store_feature_major(dst_ref, h, data, extra):
    if h % 2 == 0:
        dst_ref[h, 0:HEAD_DIM, :] = data
        dst_ref[h, HEAD_DIM:HEAD_DIM + EXTRA_ROWS, :] = extra
    else:
        dst_ref[h, 0:EXTRA_ROWS, :] = extra
        dst_ref[h, HEAD_DIM:2 * HEAD_DIM, :] = data


def _std_columns(w_lo, w_hi, col, width):
    split = w_lo.shape[1]
    assert col + width <= split or col >= split
    return w_lo[:, col:col + width] if col < split else w_hi[:, col - split:col - split + width]


def _conv_branch(proj, u_sc, conv_w, conv_ref, w_pa, n):
    first = SUBLANES - (CONV_W - 1)
    u_sc[SUBLANES:SUBLANES + n, :] = proj(_S_GC, CONV_DIM) * proj(_S_XA, CONV_DIM)
    conv = sum(u_sc[first + i:first + i + n, :] * conv_w[i:i + 1, :] for i in range(CONV_W))
    new_conv = u_sc[first + n:SUBLANES + n, :]
    conv_ref[...] = new_conv
    u_sc[first:SUBLANES, :] = new_conv
    y_a = proj(_S_GB, CONV_DIM) * conv * jax.nn.silu(proj(_S_ZA, CONV_DIM))
    return jax.nn.sigmoid(proj(_S_GA, D_MODEL)) * jnp.dot(
        y_a.astype(BF16), w_pa[...], preferred_element_type=F32)


def _merge_and_norm(att, mix_a, h, silu_zb, gate_b, w_pb, w_out, ln_g, ln_b):
    y_b = att * silu_zb
    mixed = mix_a + gate_b * jnp.dot(
        y_b.astype(BF16), w_pb[...], preferred_element_type=F32)
    sub = jnp.dot(mixed.astype(BF16), w_out[...], preferred_element_type=F32)
    return _layer_norm(ALPHA * h + sub, ln_g[...], ln_b[...])


def _head_max_sq_norm(xt):
    n = xt.shape[1]
    row = lax.broadcasted_iota(jnp.int32, (F_ROWS, n), 0)
    sq = xt * xt
    acc = jnp.zeros((F_ROWS, n), F32)
    for h in range(N_HEADS):
        norm = jnp.sum(sq[h * HEAD_DIM:(h + 1) * HEAD_DIM, :], axis=0, keepdims=True)
        acc = jnp.where(row == h, jnp.broadcast_to(norm, (F_ROWS, n)), acc)
    return jnp.max(acc, axis=1, keepdims=True)


def _dead_key_tiles(j, qt2, kt, c2_end, kn_sc, ce_sc):
    lane = lax.broadcasted_iota(jnp.int32, (F_ROWS, LANES), 1)
    row = lax.broadcasted_iota(jnp.int32, (F_ROWS, LANES), 0)
    spread = lambda col: jnp.broadcast_to(col, (F_ROWS, LANES))
    kn = jnp.where(lane == j, spread(_head_max_sq_norm(kt)), jnp.where(lane < j, kn_sc[...], 0.0))
    ce = jnp.where(lane == j, spread(c2_end), jnp.where(lane < j, ce_sc[...], 0.0))
    kn_sc[...] = kn
    ce_sc[...] = ce
    k_max = jnp.max(kn, axis=1, keepdims=True)
    ce_prev = jnp.sum(jnp.where(lane == j - 1, ce, 0.0), axis=1, keepdims=True)
    bound = NORM_SLACK * 2.0 * jnp.sqrt(_head_max_sq_norm(qt2) * k_max) + ce_prev - ce
    worst = jnp.max(jnp.where(row < N_HEADS, bound, -jnp.inf), axis=0, keepdims=True)
    dead = (worst < -UNDERFLOW_LOG2) & (lane[0:1, :] <= j - 2)
    return jnp.min(jnp.where(dead, LANES, lane[0:1, :]))


def _attention_t(n_key_tiles, first, tm, kaug_sc, qta_sc, vt_sc, attt_sc):
    keys = pl.ds(pl.multiple_of(first * tm, tm), n_key_tiles * tm)
    causal = (lax.broadcasted_iota(jnp.int32, (tm, tm), 0)
              <= lax.broadcasted_iota(jnp.int32, (tm, tm), 1))

    def scores(hd):
        return jnp.dot(kaug_sc[keys, hd * HEAD_BLOCK:(hd + 1) * HEAD_BLOCK], qta_sc[hd],
                       preferred_element_type=F32)

    pending = [scores(hd) for hd in range(SCORE_LOOKAHEAD)]
    for hd in range(N_HEADS):
        s = pending.pop(0)
        if hd + SCORE_LOOKAHEAD < N_HEADS:
            pending.append(scores(hd + SCORE_LOOKAHEAD))
        blocks = [s[i * tm:(i + 1) * tm, :] for i in range(n_key_tiles)]
        blocks[-1] = jnp.where(causal, blocks[-1], -jnp.inf)
        m = functools.reduce(jnp.maximum, [jnp.max(blk, axis=0, keepdims=True) for blk in blocks])
        o = jnp.zeros((V_ROWS, tm), F32)
        for i, blk in enumerate(blocks):
            o = o + jnp.dot(vt_sc[first + i, hd * V_ROWS:(hd + 1) * V_ROWS, :],
                            jnp.exp2(blk - m).astype(BF16), preferred_element_type=F32)
        attt_sc[hd * HEAD_DIM:(hd + 1) * HEAD_DIM, :] = o[0:HEAD_DIM, :] / o[HEAD_DIM:HEAD_DIM + 1, :]


def _prompt_kernel(x_ref, ln_in_g, ln_in_b, w_lo, w_hi, w_t, b_f_col, conv_w, w_pa, w_pb, w_out,
                   ln_g, ln_b, route, tri,
                   y_ref, kt_ref, vt_ref, lft_ref, conv_ref,
                   kaug_sc, vt_sc, h_sc, hb_sc, z_sc, qta_sc, attt_sc, zb_sc, gm_sc, u_sc, carryt_sc,
                   kn_sc, ce_sc, first_sm, *, tm, n_tiles, n_total):
    t = pl.program_id(0)

    def norm_in():
        h = _layer_norm(x_ref[...], ln_in_g[...], ln_in_b[...])
        h_sc[lax.rem(t, PIPE_SLOTS)] = h
        hb_sc[lax.rem(t, PIPE_SLOTS)] = h.astype(BF16)

    def norm_out():
        y_ref[...] = _layer_norm(z_sc[...], ln_g[...], ln_b[...])

    @pl.when(t == 0)
    def _fill():
        for ref in (z_sc, kaug_sc, vt_sc, h_sc, hb_sc, attt_sc, zb_sc, gm_sc, u_sc, qta_sc, carryt_sc,
                    kn_sc, ce_sc):
            ref[...] = jnp.zeros_like(ref)
        first_sm[0] = 0
        ones_row = jnp.where(lax.broadcasted_iota(jnp.int32, (V_ROWS - HEAD_DIM, tm), 0) == 0, 1.0, 0.0)
        for i in range(n_tiles):
            for hd in range(N_HEADS):
                vt_sc[i, hd * V_ROWS + HEAD_DIM:(hd + 1) * V_ROWS, :] = ones_row.astype(BF16)
        norm_in()

    @pl.when((t >= 1) & (t <= n_total + 1))
    def _mixers():
        tile_a = t - 1
        tile_b = t - 2
        ja = lax.rem(tile_a, n_tiles)
        jb = lax.rem(tile_b, n_tiles)
        slot_a = lax.rem(tile_a, PIPE_SLOTS)
        slot_b = lax.rem(t + 1, PIPE_SLOTS)
        gate_a = tile_a & 1
        gate_b = t & 1

        @pl.when(ja == 0)
        def _start_of_sequence():
            carryt_sc[...] = jnp.zeros_like(carryt_sc)

        @pl.when(jb == 0)
        def _start_of_conv():
            u_sc[0:SUBLANES, :] = jnp.zeros((SUBLANES, CONV_DIM), F32)

        first = first_sm[0]
        for n_key_tiles in range(1, n_tiles + 1):
            pl.when((t >= 2) & (jb - first + 1 == n_key_tiles))(functools.partial(
                _attention_t, n_key_tiles, first, tm, kaug_sc, qta_sc, vt_sc, attt_sc))

        def proj(slot, col, width):
            return jnp.dot(hb_sc[slot], _std_columns(w_lo, w_hi, col, width), preferred_element_type=F32)

        def proj_t(row, height):
            return lax.dot_general(w_t[row:row + height, :], hb_sc[slot_a], (((1,), (1,)), ((), ())),
                                   preferred_element_type=F32)

        norm_out()

        vf_t = proj_t(_T_V, ATT_DIM + F_ROWS)
        qk_t = proj_t(_T_Q, 2 * ATT_DIM)
        kt = qk_t[ATT_DIM:2 * ATT_DIM, :]
        vt = vf_t[0:ATT_DIM, :]
        kt_ref[...] = kt
        vt_ref[...] = vt
        logf_t = _log_sigmoid(vf_t[ATT_DIM:ATT_DIM + F_ROWS, :] + b_f_col[...])
        lft_ref[...] = logf_t[:N_HEADS, :]
        sums = jnp.dot(jnp.concatenate(_split3(logf_t), axis=0), tri[...], preferred_element_type=F32)

        mix_a = _conv_branch(functools.partial(proj, slot_b), u_sc, conv_w, conv_ref, w_pa, tm)

        zb_sc[gate_a] = jax.nn.silu(proj(slot_a, _S_ZB, ATT_DIM))
        gm_sc[gate_a] = jax.nn.sigmoid(proj(slot_a, _S_GM, D_MODEL))

        y_b = attt_sc[...].T * zb_sc[gate_b]
        mixed = mix_a + gm_sc[gate_b] * jnp.dot(y_b.astype(BF16), w_pb[...], preferred_element_type=F32)

        c_t = sum(sums[r * F_ROWS:(r + 1) * F_ROWS, :] for r in range(N_SPLIT)) + carryt_sc[:, 0:1]
        carryt_sc[...] = jnp.broadcast_to(c_t[:, tm - 1:tm], carryt_sc.shape)
        c_pieces = [piece.astype(F32) for piece in _split3(c_t * LOG2E)]
        qt = qk_t[0:ATT_DIM, :] * (SCALE * LOG2E)
        for hd in range(N_HEADS):
            _store_feature_major(qta_sc, hd, qt[hd * HEAD_DIM:(hd + 1) * HEAD_DIM, :].astype(BF16),
                                 _extra_rows(c_pieces, hd, tm, key_side=False))
            vt_sc[ja, hd * V_ROWS:hd * V_ROWS + HEAD_DIM, :] = (
                vt[hd * HEAD_DIM:(hd + 1) * HEAD_DIM, :].astype(BF16))
        ones_and_pad = jnp.where(lax.broadcasted_iota(jnp.int32, (LANES - ONES_LANE, tm), 0) == 0, 1.0, 0.0)
        packed_t = jnp.concatenate([piece[0:N_HEADS, :] for piece in c_pieces] + [ones_and_pad], axis=0)
        ex_k = jnp.dot(packed_t.T.astype(BF16), route[:, AUG_DIM:], preferred_element_type=F32)
        _store_token_major(kaug_sc, pl.ds(pl.multiple_of(ja * tm, tm), tm), kt.T, ex_k)
        first_sm[0] = _dead_key_tiles(ja, qt, kt, c_t[:, tm - 1:tm] * LOG2E, kn_sc, ce_sc)

        z_sc[...] = ALPHA * h_sc[slot_b] + jnp.dot(mixed.astype(BF16), w_out[...], preferred_element_type=F32)

        norm_in()

    @pl.when(t == n_total + 2)
    def _drain():
        norm_out()


def _resident(shape):
    return pl.BlockSpec(shape, lambda *_: (0,) * len(shape), pipeline_mode=pl.Buffered(1))


def _run_prompt(x, params, *, tm):
    bsz, t, _ = x.shape
    n_tiles = t // tm
    assert n_tiles * tm == t
    n_total = bsz * n_tiles
    tile = lambda s, lag: jnp.clip(s - lag, 0, n_total - 1)
    per_tile = lambda width, lag: pl.BlockSpec(
        (None, tm, width), lambda s: (tile(s, lag) // n_tiles, tile(s, lag) % n_tiles, 0))
    per_tile_t = lambda rows: pl.BlockSpec(
        (None, rows, tm), lambda s: (tile(s, 1) // n_tiles, 0, tile(s, 1) % n_tiles))
    out_shape = (jax.ShapeDtypeStruct((bsz, t, D_MODEL), F32),
                 jax.ShapeDtypeStruct((bsz, ATT_DIM, t), F32),
                 jax.ShapeDtypeStruct((bsz, ATT_DIM, t), F32),
                 jax.ShapeDtypeStruct((bsz, N_HEADS, t), F32),
                 jax.ShapeDtypeStruct((bsz, CONV_W - 1, CONV_DIM), F32))
    out_specs = (per_tile(D_MODEL, 3), per_tile_t(ATT_DIM), per_tile_t(ATT_DIM), per_tile_t(N_HEADS),
                 pl.BlockSpec((None, CONV_W - 1, CONV_DIM), lambda s: (tile(s, 2) // n_tiles, 0, 0)))
    scratch = [
        pltpu.VMEM((t, AUG_DIM), BF16),
        pltpu.VMEM((n_tiles, N_HEADS * V_ROWS, tm), BF16),
        pltpu.VMEM((PIPE_SLOTS, tm, D_MODEL), F32),
        pltpu.VMEM((PIPE_SLOTS, tm, D_MODEL), BF16),
        pltpu.VMEM((tm, D_MODEL), F32),
        pltpu.VMEM((N_HEADS, HEAD_BLOCK, tm), BF16),
        pltpu.VMEM((ATT_DIM, tm), F32),
        pltpu.VMEM((2, tm, ATT_DIM), F32),
        pltpu.VMEM((2, tm, D_MODEL), F32),
        pltpu.VMEM((SUBLANES + tm, CONV_DIM), F32),
        pltpu.VMEM((F_ROWS, LANES), F32),
        pltpu.VMEM((F_ROWS, LANES), F32),
        pltpu.VMEM((F_ROWS, LANES), F32),
        pltpu.SMEM((1,), jnp.int32),
    ]
    return pl.pallas_call(
        functools.partial(_prompt_kernel, tm=tm, n_tiles=n_tiles, n_total=n_total),
        grid=(n_total + PIPE_SLOTS,),
        in_specs=[per_tile(D_MODEL, 0)] + [_resident(p.shape) for p in params],
        out_specs=out_specs,
        out_shape=out_shape,
        scratch_shapes=scratch,
        compiler_params=pltpu.CompilerParams(
            dimension_semantics=("arbitrary",),
            vmem_limit_bytes=VMEM_LIMIT_BYTES),
        name="trunk_prompt",
    )(x, *params)


def _sample_kernel(x_ref, conv0_ref, pkt_ref, pvt_ref, plft_ref,
                   ln_in_g, ln_in_b, w_lo, w_hi, w_fs, b_f_row, w_t, b_f_col, conv_w, w_pa, w_pb, w_out,
                   ln_g, ln_b, route, ones,
                   y_ref, kt_ref, vt_ref, lft_ref, conv_ref,
                   kta_sc, ktn_sc, hb_sc, qa_sc, att_sc, u_sc, *, ts, n_past):
    @pl.when(pl.program_id(0) == 0)
    def _zero_padding():
        kta_sc[...] = jnp.zeros_like(kta_sc)
        ktn_sc[...] = jnp.zeros_like(ktn_sc)
        hb_sc[...] = jnp.zeros_like(hb_sc)

    h = _layer_norm(x_ref[...], ln_in_g[...], ln_in_b[...])
    hb_sc[0:ts, :] = h.astype(BF16)

    def proj(col, width):
        return jnp.dot(hb_sc[0:ts, :], _std_columns(w_lo, w_hi, col, width), preferred_element_type=F32)

    u_sc[0:SUBLANES, :] = jnp.zeros((SUBLANES, CONV_DIM), F32)
    u_sc[SUBLANES - (CONV_W - 1):SUBLANES, :] = conv0_ref[...]
    mix_a = _conv_branch(proj, u_sc, conv_w, conv_ref, w_pa, ts)

    def proj_t(row, height):
        return lax.dot_general(w_t[row:row + height, :], hb_sc[...], (((1,), (1,)), ((), ())),
                               preferred_element_type=F32)

    kt = proj_t(_T_K, ATT_DIM)
    vf_t = proj_t(_T_V, ATT_DIM + F_ROWS)
    vt = vf_t[0:ATT_DIM, :]
    kt_ref[...] = kt[:, :ts]
    vt_ref[...] = vt[:, :ts]
    logf_t = _log_sigmoid(vf_t[ATT_DIM:ATT_DIM + SUBLANES, :] + b_f_col[0:SUBLANES, :])
    lft_ref[...] = logf_t[:, :ts]

    c_past = _cumsum(plft_ref[...], 1)
    c_last = c_past[:, n_past - 1:n_past]
    c_new_t = _cumsum(logf_t, 1) + c_last
    past_pieces = [piece.astype(F32) for piece in _split3(c_past)]
    new_pieces = [piece.astype(F32) for piece in _split3(c_new_t)]
    for hd in range(N_HEADS):
        rows = slice(hd * HEAD_DIM, (hd + 1) * HEAD_DIM)
        _store_feature_major(kta_sc, hd, pkt_ref[rows, :].astype(BF16),
                             _extra_rows(past_pieces, hd, n_past, key_side=True))
        _store_feature_major(ktn_sc, hd, kt[rows, :].astype(BF16),
                             _extra_rows(new_pieces, hd, LANES, key_side=True))

    eye = (lax.broadcasted_iota(jnp.int32, (SUBLANES, LANES), 0)
           == lax.broadcasted_iota(jnp.int32, (SUBLANES, LANES), 1))
    c_last_row = jnp.sum(jnp.where(eye, jnp.broadcast_to(c_last, (SUBLANES, LANES)), 0.0),
                         axis=0, keepdims=True)
    logf = _log_sigmoid(jnp.dot(hb_sc[0:ts, :], w_fs[...], preferred_element_type=F32) + b_f_row[...])
    c_q = _cumsum(logf, 0) + c_last_row
    ex_q = _extra_lanes(c_q, route[:, :AUG_DIM]) + ones[:, :AUG_DIM]
    _store_token_major(qa_sc, slice(None), proj(_S_Q, ATT_DIM) * SCALE, ex_q)

    causal = (lax.broadcasted_iota(jnp.int32, (ts, LANES), 1)
              <= lax.broadcasted_iota(jnp.int32, (ts, LANES), 0))
    lane = lax.broadcasted_iota(jnp.int32, (ts, HEAD_BLOCK), 1)
    contract_lanes = (((1,), (1,)), ((), ()))
    scores = []
    for hd in range(N_HEADS):
        qa_h = qa_sc[:, hd * HEAD_BLOCK:(hd + 1) * HEAD_BLOCK]
        scores.append((jnp.dot(qa_h, kta_sc[hd], preferred_element_type=F32),
                       jnp.dot(qa_h, ktn_sc[hd], preferred_element_type=F32)))
    for pair in range(N_HEADS // 2):
        rows = slice(pair * HEAD_BLOCK, (pair + 1) * HEAD_BLOCK)
        vt_past = pvt_ref[rows, :].astype(BF16)
        vt_new = vt[rows, :].astype(BF16)
        probs, sums = [], []
        for hd in (2 * pair, 2 * pair + 1):
            s_past = scores[hd][0]
            s_new = jnp.where(causal, scores[hd][1], -jnp.inf)
            m = jnp.maximum(jnp.max(s_past, axis=-1, keepdims=True), jnp.max(s_new, axis=-1, keepdims=True))
            p_past = jnp.exp(s_past - m)
            p_new = jnp.exp(s_new - m)
            sums.append(jnp.sum(p_past, axis=-1, keepdims=True) + jnp.sum(p_new, axis=-1, keepdims=True))
            probs.append((p_past.astype(BF16), p_new.astype(BF16)))
        o = (lax.dot_general(jnp.concatenate([probs[0][0], probs[1][0]], axis=0), vt_past, contract_lanes,
                             preferred_element_type=F32)
             + lax.dot_general(jnp.concatenate([probs[0][1], probs[1][1]], axis=0), vt_new, contract_lanes,
                               preferred_element_type=F32))
        att_sc[:, rows] = jnp.where(lane < HEAD_DIM, o[0:ts, :] / sums[0], o[ts:2 * ts, :] / sums[1])

    y_ref[...] = _merge_and_norm(att_sc[...], mix_a, h, jax.nn.silu(proj(_S_ZB, ATT_DIM)),
                                 jax.nn.sigmoid(proj(_S_GM, D_MODEL)), w_pb, w_out, ln_g, ln_b)


def _run_sample(x, conv0, pkt, pvt, plft, params):
    bsz, ts, _ = x.shape
    n_past = pkt.shape[2]
    assert ts <= LANES and ts % SUBLANES == 0
    per_batch = lambda rows, width: pl.BlockSpec((None, rows, width), lambda b: (b, 0, 0))
    out_shape = (jax.ShapeDtypeStruct((bsz, ts, D_MODEL), F32),
                 jax.ShapeDtypeStruct((bsz, ATT_DIM, ts), F32),
                 jax.ShapeDtypeStruct((bsz, ATT_DIM, ts), F32),
                 jax.ShapeDtypeStruct((bsz, N_HEADS, ts), F32),
                 jax.ShapeDtypeStruct((bsz, CONV_W - 1, CONV_DIM), F32))
    out_specs = (per_batch(ts, D_MODEL), per_batch(ATT_DIM, ts), per_batch(ATT_DIM, ts),
                 per_batch(N_HEADS, ts), per_batch(CONV_W - 1, CONV_DIM))
    in_specs = [per_batch(ts, D_MODEL), per_batch(CONV_W - 1, CONV_DIM), per_batch(ATT_DIM, n_past),
                per_batch(ATT_DIM, n_past), per_batch(N_HEADS, n_past)]
    scratch = [
        pltpu.VMEM((N_HEADS, HEAD_BLOCK, n_past), BF16),
        pltpu.VMEM((N_HEADS, HEAD_BLOCK, LANES), BF16),
        pltpu.VMEM((LANES, D_MODEL), BF16),
        pltpu.VMEM((ts, AUG_DIM), BF16),
        pltpu.VMEM((ts, ATT_DIM), F32),
        pltpu.VMEM((SUBLANES + ts, CONV_DIM), F32),
    ]
    return pl.pallas_call(
        functools.partial(_sample_kernel, ts=ts, n_past=n_past),
        grid=(bsz,),
        in_specs=in_specs + [_resident(p.shape) for p in params],
        out_specs=out_specs,
        out_shape=out_shape,
        scratch_shapes=scratch,
        compiler_params=pltpu.CompilerParams(
            dimension_semantics=("arbitrary",),
            vmem_limit_bytes=VMEM_LIMIT_BYTES),
        name="trunk_sample",
    )(x, conv0, pkt, pvt, plft, *params)


def kernel(x_prompt, x_sample, cache_k, cache_v, cache_logf, state_conv, ln_in_g, ln_in_b, w_in, b_f,
           conv_w, w_proj_a, w_proj_b, w_out, ln_g, ln_b):
    assert w_in.shape[0] == DEPTH
    route_np, ones_np = _routing_constants()
    w = w_in[0]
    row = lambda a: a.reshape(1, -1).astype(F32)
    w_t = jnp.concatenate([w[:, _W_Q:_W_F + N_HEADS],
                           jnp.zeros((D_MODEL, F_ROWS - N_HEADS), w.dtype)], axis=1).T.astype(BF16)
    tm = PROMPT_TILE
    p = dict(
        ln_in_g=row(ln_in_g), ln_in_b=row(ln_in_b),
        w_lo=w[:, :_W_V].astype(BF16), w_hi=w[:, _W_ZB:].astype(BF16),
        w_fs=jnp.pad(w[:, _W_F:_W_F + N_HEADS], ((0, 0), (0, LANES - N_HEADS))).astype(BF16),
        b_f_row=jnp.pad(row(b_f[0]), ((0, 0), (0, LANES - N_HEADS))),
        w_t=w_t,
        b_f_col=jnp.pad(b_f[0].astype(F32).reshape(N_HEADS, 1), ((0, F_ROWS - N_HEADS), (0, 0))),
        conv_w=conv_w[0].astype(F32),
        w_pa=w_proj_a[0].astype(BF16), w_pb=w_proj_b[0].astype(BF16), w_out=w_out[0].astype(BF16),
        ln_g=row(ln_g[0]), ln_b=row(ln_b[0]),
        route=jnp.asarray(route_np, BF16), ones=jnp.asarray(ones_np, F32),
        tri=jnp.asarray(np.triu(np.ones((tm, tm), np.float32)), BF16),
    )
    bs = x_sample.shape[0]
    n_past = cache_k.shape[2]
    feature_major = lambda a: a.transpose(0, 2, 3, 1).reshape(bs, ATT_DIM, n_past)

    y_p, kt_p, vt_p, lft_p, cv_p = _run_prompt(x_prompt, [p[name] for name in PROMPT_PARAMS], tm=tm)
    y_s, kt_s, vt_s, lft_s, cv_s = _run_sample(
        x_sample, state_conv[0], feature_major(cache_k[0]), feature_major(cache_v[0]),
        cache_logf[0].transpose(0, 2, 1), [p[name] for name in SAMPLE_PARAMS])

    heads_t = lambda a: a.reshape(a.shape[0], N_HEADS, HEAD_DIM, a.shape[2]).transpose(0, 3, 1, 2)[None]
    tokens_t = lambda a: a.transpose(0, 2, 1)[None]
    return (y_p, y_s, heads_t(kt_p), heads_t(vt_p), tokens_t(lft_p), cv_p[None],
            heads_t(kt_s), heads_t(vt_s), tokens_t(lft_s), cv_s[None])
```

```python
import functools

import numpy as np
import jax
import jax.numpy as jnp
from jax import lax
from jax.experimental import pallas as pl
from jax.experimental.pallas import tpu as pltpu

D_MODEL = 1024
CONV_DIM = 512
CONV_W = 3
N_HEADS = 8
HEAD_DIM = 64
ATT_DIM = N_HEADS * HEAD_DIM
LN_EPS = 1e-5
DEPTH = 1
ALPHA = (2.0 * DEPTH) ** 0.25
SCALE = HEAD_DIM ** -0.5

LANES = 128
SUBLANES = 8
HEAD_BLOCK = LANES
AUG_DIM = N_HEADS * HEAD_BLOCK
N_SPLIT = 3
ONES_LANE = N_SPLIT * N_HEADS
EXTRA_ROWS = 16
F_ROWS = 16
VMEM_LIMIT_BYTES = 56 * 1024 * 1024
V_ROWS = HEAD_DIM + 16
LOG2E = 1.4426950408889634
UNDERFLOW_LOG2 = 156.0
NORM_SLACK = 1.02
PIPE_SLOTS = 3
SCORE_LOOKAHEAD = 4

_W_Q = 4 * CONV_DIM
_W_V = _W_Q + 2 * ATT_DIM
_W_F = _W_Q + 3 * ATT_DIM
_W_ZB = _W_F + N_HEADS
_S_XA, _S_GB, _S_GC, _S_ZA, _S_Q, _S_K, _S_ZB, _S_GA, _S_GM = (
    0, 512, 1024, 1536, 2048, 2560, 3072, 3584, 4608)
_T_Q, _T_K, _T_V, _T_F = 0, 512, 1024, 1536

PROMPT_TILE = 256
PROMPT_PARAMS = ("ln_in_g", "ln_in_b", "w_lo", "w_hi", "w_t", "b_f_col", "conv_w", "w_pa", "w_pb", "w_out",
                 "ln_g", "ln_b", "route", "tri")
SAMPLE_PARAMS = ("ln_in_g", "ln_in_b", "w_lo", "w_hi", "w_fs", "b_f_row", "w_t", "b_f_col", "conv_w", "w_pa",
                 "w_pb", "w_out", "ln_g", "ln_b", "route", "ones")

F32 = jnp.float32
BF16 = jnp.bfloat16


def _routing_constants():
    route = np.zeros((LANES, 2 * AUG_DIM), np.float32)
    ones = np.zeros((1, 2 * AUG_DIM), np.float32)
    for h in range(N_HEADS):
        base = h * HEAD_BLOCK + (HEAD_DIM if h % 2 == 0 else 0)
        for r in range(N_SPLIT):
            route[r * N_HEADS + h, base + 2 * r] = 1.0
            ones[0, base + 2 * r + 1] = 1.0
            ones[0, AUG_DIM + base + 2 * r] = 1.0
            route[r * N_HEADS + h, AUG_DIM + base + 2 * r + 1] = -1.0
    route[ONES_LANE, :] = ones[0, :]
    return route, ones


def _layer_norm(x, g, b):
    mu = jnp.mean(x, axis=-1, keepdims=True)
    xc = x - mu
    var = jnp.mean(xc * xc, axis=-1, keepdims=True)
    return xc * lax.rsqrt(var + LN_EPS) * g + b


def _log_sigmoid(x):
    return jnp.minimum(x, 0.0) - jnp.log1p(jnp.exp(-jnp.abs(x)))


def _split3(x):
    hi = x.astype(BF16)
    r1 = x - hi.astype(F32)
    mid = r1.astype(BF16)
    lo = (r1 - mid.astype(F32)).astype(BF16)
    return hi, mid, lo


def _cumsum(x, axis):
    n = x.shape[axis]
    idx = lax.broadcasted_iota(jnp.int32, x.shape, axis)
    s = 1
    while s < n:
        x = x + jnp.where(idx >= s, pltpu.roll(x, s, axis), 0.0)
        s *= 2
    return x


def _extra_lanes(c, route):
    lane = lax.broadcasted_iota(jnp.int32, c.shape, 1)
    valid = lane < N_HEADS
    packed = jnp.zeros_like(c)
    for r, piece in enumerate(_split3(c)):
        p32 = jnp.where(valid, piece.astype(F32), 0.0)
        packed = packed + (pltpu.roll(p32, r * N_HEADS, 1) if r else p32)
    return jnp.dot(packed.astype(BF16), route, preferred_element_type=F32)


def _store_token_major(dst_ref, rows, data, extras):
    n = data.shape[0]
    lane = lax.broadcasted_iota(jnp.int32, (n, HEAD_BLOCK), 1)
    for h in range(N_HEADS):
        pair = data[:, (h // 2) * HEAD_BLOCK:(h // 2 + 1) * HEAD_BLOCK]
        ex = extras[:, h * HEAD_BLOCK:(h + 1) * HEAD_BLOCK]
        keep = (lane < HEAD_DIM) if h % 2 == 0 else (lane >= HEAD_DIM)
        dst_ref[rows, h * HEAD_BLOCK:(h + 1) * HEAD_BLOCK] = jnp.where(keep, pair, ex).astype(BF16)


def _extra_rows(pieces, h, n, key_side):
    r = lax.broadcasted_iota(jnp.int32, (EXTRA_ROWS, n), 0)
    value_row, sign = (1, -1.0) if key_side else (0, 1.0)
    out = jnp.where((r < 2 * N_SPLIT) & ((r & 1) != value_row), 1.0, 0.0)
    for i, piece in enumerate(pieces):
        out = jnp.where(r == 2 * i + value_row,
                        jnp.broadcast_to(sign * piece[h:h + 1, :], (EXTRA_ROWS, n)), out)
    return out.astype(BF16)


def _store_feature_major(dst_ref, h, data, extra):
    if h % 2 == 0:
        dst_ref[h, 0:HEAD_DIM, :] = data
        dst_ref[h, HEAD_DIM:HEAD_DIM + EXTRA_ROWS, :] = extra
    else:
        dst_ref[h, 0:EXTRA_ROWS, :] = extra
        dst_ref[h, HEAD_DIM:2 * HEAD_DIM, :] = data


def _std_columns(w_lo, w_hi, col, width):
    split = w_lo.shape[1]
    assert col + width <= split or col >= split
    return w_lo[:, col:col + width] if col < split else w_hi[:, col - split:col - split + width]


def _conv_branch(proj, u_sc, conv_w, conv_ref, w_pa, n):
    first = SUBLANES - (CONV_W - 1)
    u_sc[SUBLANES:SUBLANES + n, :] = proj(_S_GC, CONV_DIM) * proj(_S_XA, CONV_DIM)
    conv = sum(u_sc[first + i:first + i + n, :] * conv_w[i:i + 1, :] for i in range(CONV_W))
    new_conv = u_sc[first + n:SUBLANES + n, :]
    conv_ref[...] = new_conv
    u_sc[first:SUBLANES, :] = new_conv
    y_a = proj(_S_GB, CONV_DIM) * conv * jax.nn.silu(proj(_S_ZA, CONV_DIM))
    return jax.nn.sigmoid(proj(_S_GA, D_MODEL)) * jnp.dot(
        y_a.astype(BF16), w_pa[...], preferred_element_type=F32)


def _merge_and_norm(att, mix_a, h, silu_zb, gate_b, w_pb, w_out, ln_g, ln_b):
    y_b = att * silu_zb
    mixed = mix_a + gate_b * jnp.dot(
        y_b.astype(BF16), w_pb[...], preferred_element_type=F32)
    sub = jnp.dot(mixed.astype(BF16), w_out[...], preferred_element_type=F32)
    return _layer_norm(ALPHA * h + sub, ln_g[...], ln_b[...])


def _head_max_sq_norm(xt):
    n = xt.shape[1]
    row = lax.broadcasted_iota(jnp.int32, (F_ROWS, n), 0)
    sq = xt * xt
    acc = jnp.zeros((F_ROWS, n), F32)
    for h in range(N_HEADS):
        norm = jnp.sum(sq[h * HEAD_DIM:(h + 1) * HEAD_DIM, :], axis=0, keepdims=True)
        acc = jnp.where(row == h, jnp.broadcast_to(norm, (F_ROWS, n)), acc)
    return jnp.max(acc, axis=1, keepdims=True)


def _dead_key_tiles(j, qt2, kt, c2_end, kn_sc, ce_sc):
    lane = lax.broadcasted_iota(jnp.int32, (F_ROWS, LANES), 1)
    row = lax.broadcasted_iota(jnp.int32, (F_ROWS, LANES), 0)
    spread = lambda col: jnp.broadcast_to(col, (F_ROWS, LANES))
    kn = jnp.where(lane == j, spread(_head_max_sq_norm(kt)), jnp.where(lane < j, kn_sc[...], 0.0))
    ce = jnp.where(lane == j, spread(c2_end), jnp.where(lane < j, ce_sc[...], 0.0))
    kn_sc[...] = kn
    ce_sc[...] = ce
    k_max = jnp.max(kn, axis=1, keepdims=True)
    ce_prev = jnp.sum(jnp.where(lane == j - 1, ce, 0.0), axis=1, keepdims=True)
    bound = NORM_SLACK * 2.0 * jnp.sqrt(_head_max_sq_norm(qt2) * k_max) + ce_prev - ce
    worst = jnp.max(jnp.where(row < N_HEADS, bound, -jnp.inf), axis=0, keepdims=True)
    dead = (worst < -UNDERFLOW_LOG2) & (lane[0:1, :] <= j - 2)
    return jnp.min(jnp.where(dead, LANES, lane[0:1, :]))


def _attention_t(n_key_tiles, first, tm, kaug_sc, qta_sc, vt_sc, attt_sc):
    keys = pl.ds(pl.multiple_of(first * tm, tm), n_key_tiles * tm)
    causal = (lax.broadcasted_iota(jnp.int32, (tm, tm), 0)
              <= lax.broadcasted_iota(jnp.int32, (tm, tm), 1))

    def scores(hd):
        return jnp.dot(kaug_sc[keys, hd * HEAD_BLOCK:(hd + 1) * HEAD_BLOCK], qta_sc[hd],
                       preferred_element_type=F32)

    pending = [scores(hd) for hd in range(SCORE_LOOKAHEAD)]
    for hd in range(N_HEADS):
        s = pending.pop(0)
        if hd + SCORE_LOOKAHEAD < N_HEADS:
            pending.append(scores(hd + SCORE_LOOKAHEAD))
        blocks = [s[i * tm:(i + 1) * tm, :] for i in range(n_key_tiles)]
        blocks[-1] = jnp.where(causal, blocks[-1], -jnp.inf)
        m = functools.reduce(jnp.maximum, [jnp.max(blk, axis=0, keepdims=True) for blk in blocks])
        o = jnp.zeros((V_ROWS, tm), F32)
        for i, blk in enumerate(blocks):
            o = o + jnp.dot(vt_sc[first + i, hd * V_ROWS:(hd + 1) * V_ROWS, :],
                            jnp.exp2(blk - m).astype(BF16), preferred_element_type=F32)
        attt_sc[hd * HEAD_DIM:(hd + 1) * HEAD_DIM, :] = o[0:HEAD_DIM, :] / o[HEAD_DIM:HEAD_DIM + 1, :]


def _prompt_kernel(x_ref, ln_in_g, ln_in_b, w_lo, w_hi, w_t, b_f_col, conv_w, w_pa, w_pb, w_out,
                   ln_g, ln_b, route, tri,
                   y_ref, kt_ref, vt_ref, lft_ref, conv_ref,
                   kaug_sc, vt_sc, h_sc, hb_sc, z_sc, qta_sc, attt_sc, zb_sc, gm_sc, u_sc, carryt_sc,
                   kn_sc, ce_sc, first_sm, *, tm, n_tiles, n_total):
    t = pl.program_id(0)

    def norm_in():
        h = _layer_norm(x_ref[...], ln_in_g[...], ln_in_b[...])
        h_sc[lax.rem(t, PIPE_SLOTS)] = h
        hb_sc[lax.rem(t, PIPE_SLOTS)] = h.astype(BF16)

    def norm_out():
        y_ref[...] = _layer_norm(z_sc[...], ln_g[...], ln_b[...])

    @pl.when(t == 0)
    def _fill():
        for ref in (z_sc, kaug_sc, vt_sc, h_sc, hb_sc, attt_sc, zb_sc, gm_sc, u_sc, qta_sc, carryt_sc,
                    kn_sc, ce_sc):
            ref[...] = jnp.zeros_like(ref)
        first_sm[0] = 0
        ones_row = jnp.where(lax.broadcasted_iota(jnp.int32, (V_ROWS - HEAD_DIM, tm), 0) == 0, 1.0, 0.0)
        for i in range(n_tiles):
            for hd in range(N_HEADS):
                vt_sc[i, hd * V_ROWS + HEAD_DIM:(hd + 1) * V_ROWS, :] = ones_row.astype(BF16)
        norm_in()

    @pl.when((t >= 1) & (t <= n_total + 1))
    def _mixers():
        tile_a = t - 1
        tile_b = t - 2
        ja = lax.rem(tile_a, n_tiles)
        jb = lax.rem(tile_b, n_tiles)
        slot_a = lax.rem(tile_a, PIPE_SLOTS)
        slot_b = lax.rem(t + 1, PIPE_SLOTS)
        gate_a = tile_a & 1
        gate_b = t & 1

        @pl.when(ja == 0)
        def _start_of_sequence():
            carryt_sc[...] = jnp.zeros_like(carryt_sc)

        @pl.when(jb == 0)
        def _start_of_conv():
            u_sc[0:SUBLANES, :] = jnp.zeros((SUBLANES, CONV_DIM), F32)

        first = first_sm[0]
        for n_key_tiles in range(1, n_tiles + 1):
            pl.when((t >= 2) & (jb - first + 1 == n_key_tiles))(functools.partial(
                _attention_t, n_key_tiles, first, tm, kaug_sc, qta_sc, vt_sc, attt_sc))

        def proj(slot, col, width):
            return jnp.dot(hb_sc[slot], _std_columns(w_lo, w_hi, col, width), preferred_element_type=F32)

        def proj_t(row, height):
            return lax.dot_general(w_t[row:row + height, :], hb_sc[slot_a], (((1,), (1,)), ((), ())),
                                   preferred_element_type=F32)

        norm_out()

        vf_t = proj_t(_T_V, ATT_DIM + F_ROWS)
        qk_t = proj_t(_T_Q, 2 * ATT_DIM)
        kt = qk_t[ATT_DIM:2 * ATT_DIM, :]
        vt = vf_t[0:ATT_DIM, :]
        kt_ref[...] = kt
        vt_ref[...] = vt
        logf_t = _log_sigmoid(vf_t[ATT_DIM:ATT_DIM + F_ROWS, :] + b_f_col[...])
        lft_ref[...] = logf_t[:N_HEADS, :]
        sums = jnp.dot(jnp.concatenate(_split3(logf_t), axis=0), tri[...], preferred_element_type=F32)

        mix_a = _conv_branch(functools.partial(proj, slot_b), u_sc, conv_w, conv_ref, w_pa, tm)

        zb_sc[gate_a] = jax.nn.silu(proj(slot_a, _S_ZB, ATT_DIM))
        gm_sc[gate_a] = jax.nn.sigmoid(proj(slot_a, _S_GM, D_MODEL))

        y_b = attt_sc[...].T * zb_sc[gate_b]
        mixed = mix_a + gm_sc[gate_b] * jnp.dot(y_b.astype(BF16), w_pb[...], preferred_element_type=F32)

        c_t = sum(sums[r * F_ROWS:(r + 1) * F_ROWS, :] for r in range(N_SPLIT)) + carryt_sc[0:F_ROWS, 0:1]
        carryt_sc[0:F_ROWS, :] = jnp.broadcast_to(c_t[:, tm - 1:tm], (F_ROWS, LANES))
        c_pieces = [piece.astype(F32) for piece in _split3(c_t * LOG2E)]
        qt = qk_t[0:ATT_DIM, :] * (SCALE * LOG2E)
        for hd in range(N_HEADS):
            _store_feature_major(qta_sc, hd, qt[hd * HEAD_DIM:(hd + 1) * HEAD_DIM, :].astype(BF16),
                                 _extra_rows(c_pieces, hd, tm, key_side=False))
            vt_sc[ja, hd * V_ROWS:hd * V_ROWS + HEAD_DIM, :] = (
                vt[hd * HEAD_DIM:(hd + 1) * HEAD_DIM, :].astype(BF16))
        ones_and_pad = jnp.where(lax.broadcasted_iota(jnp.int32, (LANES - ONES_LANE, tm), 0) == 0, 1.0, 0.0)
        packed_t = jnp.concatenate([piece[0:N_HEADS, :] for piece in c_pieces] + [ones_and_pad], axis=0)
        ex_k = jnp.dot(packed_t.T.astype(BF16), route[:, AUG_DIM:], preferred_element_type=F32)
        _store_token_major(kaug_sc, pl.ds(pl.multiple_of(ja * tm, tm), tm), kt.T, ex_k)
        first_sm[0] = _dead_key_tiles(ja, qt, kt, c_t[:, tm - 1:tm] * LOG2E, kn_sc, ce_sc)

        z_sc[...] = ALPHA * h_sc[slot_b] + jnp.dot(mixed.astype(BF16), w_out[...], preferred_element_type=F32)

        norm_in()

    @pl.when(t == n_total + 2)
    def _drain():
        norm_out()


def _resident(shape):
    return pl.BlockSpec(shape, lambda *_: (0,) * len(shape), pipeline_mode=pl.Buffered(1))


def _run_prompt(x, params, *, tm):
    bsz, t, _ = x.shape
    n_tiles = t // tm
    assert n_tiles * tm == t
    n_total = bsz * n_tiles
    tile = lambda s, lag: jnp.clip(s - lag, 0, n_total - 1)
    per_tile = lambda width, lag: pl.BlockSpec(
        (None, tm, width), lambda s: (tile(s, lag) // n_tiles, tile(s, lag) % n_tiles, 0))
    per_tile_t = lambda rows: pl.BlockSpec(
        (None, rows, tm), lambda s: (tile(s, 1) // n_tiles, 0, tile(s, 1) % n_tiles))
    out_shape = (jax.ShapeDtypeStruct((bsz, t, D_MODEL), F32),
                 jax.ShapeDtypeStruct((bsz, ATT_DIM, t), F32),
                 jax.ShapeDtypeStruct((bsz, ATT_DIM, t), F32),
                 jax.ShapeDtypeStruct((bsz, N_HEADS, t), F32),
                 jax.ShapeDtypeStruct((bsz, CONV_W - 1, CONV_DIM), F32))
    out_specs = (per_tile(D_MODEL, 3), per_tile_t(ATT_DIM), per_tile_t(ATT_DIM), per_tile_t(N_HEADS),
                 pl.BlockSpec((None, CONV_W - 1, CONV_DIM), lambda s: (tile(s, 2) // n_tiles, 0, 0)))
    scratch = [
        pltpu.VMEM((t, AUG_DIM), BF16),
        pltpu.VMEM((n_tiles, N_HEADS * V_ROWS, tm), BF16),
        pltpu.VMEM((PIPE_SLOTS, tm, D_MODEL), F32),
        pltpu.VMEM((PIPE_SLOTS, tm, D_MODEL), BF16),
        pltpu.VMEM((tm, D_MODEL), F32),
        pltpu.VMEM((N_HEADS, HEAD_BLOCK, tm), BF16),
        pltpu.VMEM((ATT_DIM, tm), F32),
        pltpu.VMEM((2, tm, ATT_DIM), F32),
        pltpu.VMEM((2, tm, D_MODEL), F32),
        pltpu.VMEM((SUBLANES + tm, CONV_DIM), F32),
        pltpu.VMEM((2 * F_ROWS, LANES), F32),
        pltpu.VMEM((F_ROWS, LANES), F32),
        pltpu.VMEM((F_ROWS, LANES), F32),
        pltpu.SMEM((1,), jnp.int32),
    ]
    return pl.pallas_call(
        functools.partial(_prompt_kernel, tm=tm, n_tiles=n_tiles, n_total=n_total),
        grid=(n_total + PIPE_SLOTS,),
        in_specs=[per_tile(D_MODEL, 0)] + [_resident(p.shape) for p in params],
        out_specs=out_specs,
        out_shape=out_shape,
        scratch_shapes=scratch,
        compiler_params=pltpu.CompilerParams(
            dimension_semantics=("arbitrary",),
            vmem_limit_bytes=VMEM_LIMIT_BYTES),
        name="trunk_prompt",
    )(x, *params)


def _sample_kernel(x_ref, conv0_ref, pkt_ref, pvt_ref, plft_ref,
                   ln_in_g, ln_in_b, w_lo, w_hi, w_fs, b_f_row, w_t, b_f_col, conv_w, w_pa, w_pb, w_out,
                   ln_g, ln_b, route, ones,
                   y_ref, kt_ref, vt_ref, lft_ref, conv_ref,
                   kta_sc, ktn_sc, hb_sc, qa_sc, att_sc, u_sc, *, ts, n_past):
    @pl.when(pl.program_id(0) == 0)
    def _zero_padding():
        kta_sc[...] = jnp.zeros_like(kta_sc)
        ktn_sc[...] = jnp.zeros_like(ktn_sc)
        hb_sc[...] = jnp.zeros_like(hb_sc)

    h = _layer_norm(x_ref[...], ln_in_g[...], ln_in_b[...])
    hb_sc[0:ts, :] = h.astype(BF16)

    def proj(col, width):
        return jnp.dot(hb_sc[0:ts, :], _std_columns(w_lo, w_hi, col, width), preferred_element_type=F32)

    u_sc[0:SUBLANES, :] = jnp.zeros((SUBLANES, CONV_DIM), F32)
    u_sc[SUBLANES - (CONV_W - 1):SUBLANES, :] = conv0_ref[...]
    mix_a = _conv_branch(proj, u_sc, conv_w, conv_ref, w_pa, ts)

    def proj_t(row, height):
        return lax.dot_general(w_t[row:row + height, :], hb_sc[...], (((1,), (1,)), ((), ())),
                               preferred_element_type=F32)

    kt = proj_t(_T_K, ATT_DIM)
    vf_t = proj_t(_T_V, ATT_DIM + F_ROWS)
    vt = vf_t[0:ATT_DIM, :]
    kt_ref[...] = kt[:, :ts]
    vt_ref[...] = vt[:, :ts]
    logf_t = _log_sigmoid(vf_t[ATT_DIM:ATT_DIM + SUBLANES, :] + b_f_col[0:SUBLANES, :])
    lft_ref[...] = logf_t[:, :ts]

    c_past = _cumsum(plft_ref[...], 1)
    c_last = c_past[:, n_past - 1:n_past]
    c_new_t = _cumsum(logf_t, 1) + c_last
    past_pieces = [piece.astype(F32) for piece in _split3(c_past)]
    new_pieces = [piece.astype(F32) for piece in _split3(c_new_t)]
    for hd in range(N_HEADS):
        rows = slice(hd * HEAD_DIM, (hd + 1) * HEAD_DIM)
        _store_feature_major(kta_sc, hd, pkt_ref[rows, :].astype(BF16),
                             _extra_rows(past_pieces, hd, n_past, key_side=True))
        _store_feature_major(ktn_sc, hd, kt[rows, :].astype(BF16),
                             _extra_rows(new_pieces, hd, LANES, key_side=True))

    eye = (lax.broadcasted_iota(jnp.int32, (SUBLANES, LANES), 0)
           == lax.broadcasted_iota(jnp.int32, (SUBLANES, LANES), 1))
    c_last_row = jnp.sum(jnp.where(eye, jnp.broadcast_to(c_last, (SUBLANES, LANES)), 0.0),
                         axis=0, keepdims=True)
    logf = _log_sigmoid(jnp.dot(hb_sc[0:ts, :], w_fs[...], preferred_element_type=F32) + b_f_row[...])
    c_q = _cumsum(logf, 0) + c_last_row
    ex_q = _extra_lanes(c_q, route[:, :AUG_DIM]) + ones[:, :AUG_DIM]
    _store_token_major(qa_sc, slice(None), proj(_S_Q, ATT_DIM) * SCALE, ex_q)

    causal = (lax.broadcasted_iota(jnp.int32, (ts, LANES), 1)
              <= lax.broadcasted_iota(jnp.int32, (ts, LANES), 0))
    lane = lax.broadcasted_iota(jnp.int32, (ts, HEAD_BLOCK), 1)
    contract_lanes = (((1,), (1,)), ((), ()))
    scores = []
    for hd in range(N_HEADS):
        qa_h = qa_sc[:, hd * HEAD_BLOCK:(hd + 1) * HEAD_BLOCK]
        scores.append((jnp.dot(qa_h, kta_sc[hd], preferred_element_type=F32),
                       jnp.dot(qa_h, ktn_sc[hd], preferred_element_type=F32)))
    for pair in range(N_HEADS // 2):
        rows = slice(pair * HEAD_BLOCK, (pair + 1) * HEAD_BLOCK)
        vt_past = pvt_ref[rows, :].astype(BF16)
        vt_new = vt[rows, :].astype(BF16)
        probs, sums = [], []
        for hd in (2 * pair, 2 * pair + 1):
            s_past = scores[hd][0]
            s_new = jnp.where(causal, scores[hd][1], -jnp.inf)
            m = jnp.maximum(jnp.max(s_past, axis=-1, keepdims=True), jnp.max(s_new, axis=-1, keepdims=True))
            p_past = jnp.exp(s_past - m)
            p_new = jnp.exp(s_new - m)
            sums.append(jnp.sum(p_past, axis=-1, keepdims=True) + jnp.sum(p_new, axis=-1, keepdims=True))
            probs.append((p_past.astype(BF16), p_new.astype(BF16)))
        o = (lax.dot_general(jnp.concatenate([probs[0][0], probs[1][0]], axis=0), vt_past, contract_lanes,
                             preferred_element_type=F32)
             + lax.dot_general(jnp.concatenate([probs[0][1], probs[1][1]], axis=0), vt_new, contract_lanes,
                               preferred_element_type=F32))
        att_sc[:, rows] = jnp.where(lane < HEAD_DIM, o[0:ts, :] / sums[0], o[ts:2 * ts, :] / sums[1])

    y_ref[...] = _merge_and_norm(att_sc[...], mix_a, h, jax.nn.silu(proj(_S_ZB, ATT_DIM)),
                                 jax.nn.sigmoid(proj(_S_GM, D_MODEL)), w_pb, w_out, ln_g, ln_b)


def _run_sample(x, conv0, pkt, pvt, plft, params):
    bsz, ts, _ = x.shape
    n_past = pkt.shape[2]
    assert ts <= LANES and ts % SUBLANES == 0
    per_batch = lambda rows, width: pl.BlockSpec((None, rows, width), lambda b: (b, 0, 0))
    out_shape = (jax.ShapeDtypeStruct((bsz, ts, D_MODEL), F32),
                 jax.ShapeDtypeStruct((bsz, ATT_DIM, ts), F32),
                 jax.ShapeDtypeStruct((bsz, ATT_DIM, ts), F32),
                 jax.ShapeDtypeStruct((bsz, N_HEADS, ts), F32),
                 jax.ShapeDtypeStruct((bsz, CONV_W - 1, CONV_DIM), F32))
    out_specs = (per_batch(ts, D_MODEL), per_batch(ATT_DIM, ts), per_batch(ATT_DIM, ts),
                 per_batch(N_HEADS, ts), per_batch(CONV_W - 1, CONV_DIM))
    in_specs = [per_batch(ts, D_MODEL), per_batch(CONV_W - 1, CONV_DIM), per_batch(ATT_DIM, n_past),
                per_batch(ATT_DIM, n_past), per_batch(N_HEADS, n_past)]
    scratch = [
        pltpu.VMEM((N_HEADS, HEAD_BLOCK, n_past), BF16),
        pltpu.VMEM((N_HEADS, HEAD_BLOCK, LANES), BF16),
        pltpu.VMEM((LANES, D_MODEL), BF16),
        pltpu.VMEM((ts, AUG_DIM), BF16),
        pltpu.VMEM((ts, ATT_DIM), F32),
        pltpu.VMEM((SUBLANES + ts, CONV_DIM), F32),
    ]
    return pl.pallas_call(
        functools.partial(_sample_kernel, ts=ts, n_past=n_past),
        grid=(bsz,),
        in_specs=in_specs + [_resident(p.shape) for p in params],
        out_specs=out_specs,
        out_shape=out_shape,
        scratch_shapes=scratch,
        compiler_params=pltpu.CompilerParams(
            dimension_semantics=("arbitrary",),
            vmem_limit_bytes=VMEM_LIMIT_BYTES),
        name="trunk_sample",
    )(x, conv0, pkt, pvt, plft, *params)


def kernel(x_prompt, x_sample, cache_k, cache_v, cache_logf, state_conv, ln_in_g, ln_in_b, w_in, b_f,
           conv_w, w_proj_a, w_proj_b, w_out, ln_g, ln_b):
    assert w_in.shape[0] == DEPTH
    route_np, ones_np = _routing_constants()
    w = w_in[0]
    row = lambda a: a.reshape(1, -1).astype(F32)
    w_t = jnp.concatenate([w[:, _W_Q:_W_F + N_HEADS],
                           jnp.zeros((D_MODEL, F_ROWS - N_HEADS), w.dtype)], axis=1).T.astype(BF16)
    tm = PROMPT_TILE
    p = dict(
        ln_in_g=row(ln_in_g), ln_in_b=row(ln_in_b),
        w_lo=w[:, :_W_V].astype(BF16), w_hi=w[:, _W_ZB:].astype(BF16),
        w_fs=jnp.pad(w[:, _W_F:_W_F + N_HEADS], ((0, 0), (0, LANES - N_HEADS))).astype(BF16),
        b_f_row=jnp.pad(row(b_f[0]), ((0, 0), (0, LANES - N_HEADS))),
        w_t=w_t,
        b_f_col=jnp.pad(b_f[0].astype(F32).reshape(N_HEADS, 1), ((0, F_ROWS - N_HEADS), (0, 0))),
        conv_w=conv_w[0].astype(F32),
        w_pa=w_proj_a[0].astype(BF16), w_pb=w_proj_b[0].astype(BF16), w_out=w_out[0].astype(BF16),
        ln_g=row(ln_g[0]), ln_b=row(ln_b[0]),
        route=jnp.asarray(route_np, BF16), ones=jnp.asarray(ones_np, F32),
        tri=jnp.asarray(np.triu(np.ones((tm, tm), np.float32)), BF16),
    )
    bs = x_sample.shape[0]
    n_past = cache_k.shape[2]
    feature_major = lambda a: a.transpose(0, 2, 3, 1).reshape(bs, ATT_DIM, n_past)

    y_p, kt_p, vt_p, lft_p, cv_p = _run_prompt(x_prompt, [p[name] for name in PROMPT_PARAMS], tm=tm)
    y_s, kt_s, vt_s, lft_s, cv_s = _run_sample(
        x_sample, state_conv[0], feature_major(cache_k[0]), feature_major(cache_v[0]),
        cache_logf[0].transpose(0, 2, 1), [p[name] for name in SAMPLE_PARAMS])

    heads_t = lambda a: a.reshape(a.shape[0], N_HEADS, HEAD_DIM, a.shape[2]).transpose(0, 3, 1, 2)[None]
    tokens_t = lambda a: a.transpose(0, 2, 1)[None]
    return (y_p, y_s, heads_t(kt_p), heads_t(vt_p), tokens_t(lft_p), cv_p[None],
            heads_t(kt_s), heads_t(vt_s), tokens_t(lft_s), cv_s[None])
```

```python
import functools

import numpy as np
import jax
import jax.numpy as jnp
from jax import lax
from jax.experimental import pallas as pl
from jax.experimental.pallas import tpu as pltpu

D_MODEL = 1024
CONV_DIM = 512
CONV_W = 3
N_HEADS = 8
HEAD_DIM = 64
ATT_DIM = N_HEADS * HEAD_DIM
LN_EPS = 1e-5
DEPTH = 1
ALPHA = (2.0 * DEPTH) ** 0.25
SCALE = HEAD_DIM ** -0.5

LANES = 128
SUBLANES = 8
HEAD_BLOCK = LANES
AUG_DIM = N_HEADS * HEAD_BLOCK
N_SPLIT = 3
ONES_LANE = N_SPLIT * N_HEADS
EXTRA_ROWS = 16
F_ROWS = 16
VMEM_LIMIT_BYTES = 56 * 1024 * 1024
V_ROWS = HEAD_DIM + 16
LOG2E = 1.4426950408889634
UNDERFLOW_LOG2 = 156.0
NORM_SLACK = 1.02
PIPE_SLOTS = 3
SCORE_LOOKAHEAD = 4

_W_Q = 4 * CONV_DIM
_W_V = _W_Q + 2 * ATT_DIM
_W_F = _W_Q + 3 * ATT_DIM
_W_ZB = _W_F + N_HEADS
_S_XA, _S_GB, _S_GC, _S_ZA, _S_Q, _S_K, _S_ZB, _S_GA, _S_GM = (
    0, 512, 1024, 1536, 2048, 2560, 3072, 3584, 4608)
_T_Q, _T_K, _T_V, _T_F = 0, 512, 1024, 1536

PROMPT_TILE = 256
PROMPT_PARAMS = ("ln_in_g", "ln_in_b", "w_lo", "w_hi", "w_t", "b_f_col", "conv_w", "w_pa", "w_pb", "w_out",
                 "ln_g", "ln_b", "route", "tri")
SAMPLE_PARAMS = ("ln_in_g", "ln_in_b", "w_lo", "w_hi", "w_fs", "b_f_row", "w_t", "b_f_col", "conv_w", "w_pa",
                 "w_pb", "w_out", "ln_g", "ln_b", "route", "ones")

F32 = jnp.float32
BF16 = jnp.bfloat16


def _routing_constants():
    route = np.zeros((LANES, 2 * AUG_DIM), np.float32)
    ones = np.zeros((1, 2 * AUG_DIM), np.float32)
    for h in range(N_HEADS):
        base = h * HEAD_BLOCK + (HEAD_DIM if h % 2 == 0 else 0)
        for r in range(N_SPLIT):
            route[r * N_HEADS + h, base + 2 * r] = 1.0
            ones[0, base + 2 * r + 1] = 1.0
            ones[0, AUG_DIM + base + 2 * r] = 1.0
            route[r * N_HEADS + h, AUG_DIM + base + 2 * r + 1] = -1.0
    route[ONES_LANE, :] = ones[0, :]
    return route, ones


def _layer_norm(x, g, b):
    mu = jnp.mean(x, axis=-1, keepdims=True)
    xc = x - mu
    var = jnp.mean(xc * xc, axis=-1, keepdims=True)
    return xc * lax.rsqrt(var + LN_EPS) * g + b


def _log_sigmoid(x):
    return jnp.minimum(x, 0.0) - jnp.log1p(jnp.exp(-jnp.abs(x)))


def _split3(x):
    hi = x.astype(BF16)
    r1 = x - hi.astype(F32)
    mid = r1.astype(BF16)
    lo = (r1 - mid.astype(F32)).astype(BF16)
    return hi, mid, lo


def _cumsum(x, axis):
    n = x.shape[axis]
    idx = lax.broadcasted_iota(jnp.int32, x.shape, axis)
    s = 1
    while s < n:
        x = x + jnp.where(idx >= s, pltpu.roll(x, s, axis), 0.0)
        s *= 2
    return x


def _extra_lanes(c, route):
    lane = lax.broadcasted_iota(jnp.int32, c.shape, 1)
    valid = lane < N_HEADS
    packed = jnp.zeros_like(c)
    for r, piece in enumerate(_split3(c)):
        p32 = jnp.where(valid, piece.astype(F32), 0.0)
        packed = packed + (pltpu.roll(p32, r * N_HEADS, 1) if r else p32)
    return jnp.dot(packed.astype(BF16), route, preferred_element_type=F32)


def _store_token_major(dst_ref, rows, data, extras):
    n = data.shape[0]
    lane = lax.broadcasted_iota(jnp.int32, (n, HEAD_BLOCK), 1)
    for h in range(N_HEADS):
        pair = data[:, (h // 2) * HEAD_BLOCK:(h // 2 + 1) * HEAD_BLOCK]
        ex = extras[:, h * HEAD_BLOCK:(h + 1) * HEAD_BLOCK]
        keep = (lane < HEAD_DIM) if h % 2 == 0 else (lane >= HEAD_DIM)
        dst_ref[rows, h * HEAD_BLOCK:(h + 1) * HEAD_BLOCK] = jnp.where(keep, pair, ex).astype(BF16)


def _extra_rows(pieces, h, n, key_side):
    r = lax.broadcasted_iota(jnp.int32, (EXTRA_ROWS, n), 0)
    value_row, sign = (1, -1.0) if key_side else (0, 1.0)
    out = jnp.where((r < 2 * N_SPLIT) & ((r & 1) != value_row), 1.0, 0.0)
    for i, piece in enumerate(pieces):
        out = jnp.where(r == 2 * i + value_row,
                        jnp.broadcast_to(sign * piece[h:h + 1, :], (EXTRA_ROWS, n)), out)
    return out.astype(BF16)


def _store_feature_major(dst_ref, h, data, extra):
    if h % 2 == 0:
        dst_ref[h, 0:HEAD_DIM, :] = data
        dst_ref[h, HEAD_DIM:HEAD_DIM + EXTRA_ROWS, :] = extra
    else:
        dst_ref[h, 0:EXTRA_ROWS, :] = extra
        dst_ref[h, HEAD_DIM:2 * HEAD_DIM, :] = data


def _std_columns(w_lo, w_hi, col, width):
    split = w_lo.shape[1]
    assert col + width <= split or col >= split
    return w_lo[:, col:col + width] if col < split else w_hi[:, col - split:col - split + width]


def _conv_branch(proj, u_sc, conv_w, conv_ref, w_pa, n):
    first = SUBLANES - (CONV_W - 1)
    u_sc[SUBLANES:SUBLANES + n, :] = proj(_S_GC, CONV_DIM) * proj(_S_XA, CONV_DIM)
    conv = sum(u_sc[first + i:first + i + n, :] * conv_w[i:i + 1, :] for i in range(CONV_W))
    new_conv = u_sc[first + n:SUBLANES + n, :]
    conv_ref[...] = new_conv
    u_sc[first:SUBLANES, :] = new_conv
    y_a = proj(_S_GB, CONV_DIM) * conv * jax.nn.silu(proj(_S_ZA, CONV_DIM))
    return jax.nn.sigmoid(proj(_S_GA, D_MODEL)) * jnp.dot(
        y_a.astype(BF16), w_pa[...], preferred_element_type=F32)


def _merge_and_norm(att, mix_a, h, silu_zb, gate_b, w_pb, w_out, ln_g, ln_b):
    y_b = att * silu_zb
    mixed = mix_a + gate_b * jnp.dot(
        y_b.astype(BF16), w_pb[...], preferred_element_type=F32)
    sub = jnp.dot(mixed.astype(BF16), w_out[...], preferred_element_type=F32)
    return _layer_norm(ALPHA * h + sub, ln_g[...], ln_b[...])


def _head_max_sq_norm(xt):
    n = xt.shape[1]
    row = lax.broadcasted_iota(jnp.int32, (F_ROWS, n), 0)
    sq = xt * xt
    acc = jnp.zeros((F_ROWS, n), F32)
    for h in range(N_HEADS):
        norm = jnp.sum(sq[h * HEAD_DIM:(h + 1) * HEAD_DIM, :], axis=0, keepdims=True)
        acc = jnp.where(row == h, jnp.broadcast_to(norm, (F_ROWS, n)), acc)
    return jnp.max(acc, axis=1, keepdims=True)


def _dead_key_tiles(j, qt2, kt, c2_end, kn_sc, ce_sc):
    lane = lax.broadcasted_iota(jnp.int32, (F_ROWS, LANES), 1)
    row = lax.broadcasted_iota(jnp.int32, (F_ROWS, LANES), 0)
    spread = lambda col: jnp.broadcast_to(col, (F_ROWS, LANES))
    kn = jnp.where(lane == j, spread(_head_max_sq_norm(kt)), jnp.where(lane < j, kn_sc[...], 0.0))
    ce = jnp.where(lane == j, spread(c2_end), jnp.where(lane < j, ce_sc[...], 0.0))
    kn_sc[...] = kn
    ce_sc[...] = ce
    k_max = jnp.max(kn, axis=1, keepdims=True)
    ce_prev = jnp.sum(jnp.where(lane == j - 1, ce, 0.0), axis=1, keepdims=True)
    bound = NORM_SLACK * 2.0 * jnp.sqrt(_head_max_sq_norm(qt2) * k_max) + ce_prev - ce
    worst = jnp.max(jnp.where(row < N_HEADS, bound, -jnp.inf), axis=0, keepdims=True)
    dead = (worst < -UNDERFLOW_LOG2) & (lane[0:1, :] <= j - 2)
    return jnp.min(jnp.where(dead, LANES, lane[0:1, :]))


def _attention_t(n_key_tiles, first, tm, kaug_sc, qta_sc, vt_sc, attt_sc):
    keys = pl.ds(pl.multiple_of(first * tm, tm), n_key_tiles * tm)
    causal = (lax.broadcasted_iota(jnp.int32, (tm, tm), 0)
              <= lax.broadcasted_iota(jnp.int32, (tm, tm), 1))

    def scores(hd):
        return jnp.dot(kaug_sc[keys, hd * HEAD_BLOCK:(hd + 1) * HEAD_BLOCK], qta_sc[hd],
                       preferred_element_type=F32)

    pending = [scores(hd) for hd in range(SCORE_LOOKAHEAD)]
    for hd in range(N_HEADS):
        s = pending.pop(0)
        if hd + SCORE_LOOKAHEAD < N_HEADS:
            pending.append(scores(hd + SCORE_LOOKAHEAD))
        blocks = [s[i * tm:(i + 1) * tm, :] for i in range(n_key_tiles)]
        blocks[-1] = jnp.where(causal, blocks[-1], -jnp.inf)
        m = functools.reduce(jnp.maximum, [jnp.max(blk, axis=0, keepdims=True) for blk in blocks])
        o = jnp.zeros((V_ROWS, tm), F32)
        for i, blk in enumerate(blocks):
            o = o + jnp.dot(vt_sc[first + i, hd * V_ROWS:(hd + 1) * V_ROWS, :],
                            jnp.exp2(blk - m).astype(BF16), preferred_element_type=F32)
        attt_sc[hd * HEAD_DIM:(hd + 1) * HEAD_DIM, :] = o[0:HEAD_DIM, :] / o[HEAD_DIM:HEAD_DIM + 1, :]


def _prompt_kernel(x_ref, ln_in_g, ln_in_b, w_lo, w_hi, w_t, b_f_col, conv_w, w_pa, w_pb, w_out,
                   ln_g, ln_b, route, tri,
                   y_ref, kt_ref, vt_ref, lft_ref, conv_ref,
                   kaug_sc, vt_sc, h_sc, hb_sc, z_sc, qta_sc, attt_sc, zb_sc, gm_sc, u_sc, carryt_sc,
                   kn_sc, ce_sc, first_sm, *, tm, n_tiles, n_total):
    t = pl.program_id(0)

    def norm_in():
        h = _layer_norm(x_ref[...], ln_in_g[...], ln_in_b[...])
        h_sc[lax.rem(t, PIPE_SLOTS)] = h
        hb_sc[lax.rem(t, PIPE_SLOTS)] = h.astype(BF16)

    def norm_out():
        y_ref[...] = _layer_norm(z_sc[...], ln_g[...], ln_b[...])

    @pl.when(t == 0)
    def _fill():
        for ref in (z_sc, kaug_sc, vt_sc, h_sc, hb_sc, attt_sc, zb_sc, gm_sc, u_sc, qta_sc, carryt_sc,
                    kn_sc, ce_sc):
            ref[...] = jnp.zeros_like(ref)
        first_sm[0] = 0
        ones_row = jnp.where(lax.broadcasted_iota(jnp.int32, (V_ROWS - HEAD_DIM, tm), 0) == 0, 1.0, 0.0)
        for i in range(n_tiles):
            for hd in range(N_HEADS):
                vt_sc[i, hd * V_ROWS + HEAD_DIM:(hd + 1) * V_ROWS, :] = ones_row.astype(BF16)
        norm_in()

    @pl.when((t >= 1) & (t <= n_total + 1))
    def _mixers():
        tile_a = t - 1
        tile_b = t - 2
        ja = lax.rem(tile_a, n_tiles)
        jb = lax.rem(tile_b, n_tiles)
        slot_a = lax.rem(tile_a, PIPE_SLOTS)
        slot_b = lax.rem(t + 1, PIPE_SLOTS)
        gate_a = tile_a & 1
        gate_b = t & 1

        @pl.when(ja == 0)
        def _start_of_sequence():
            carryt_sc[...] = jnp.zeros_like(carryt_sc)

        @pl.when(jb == 0)
        def _start_of_conv():
            u_sc[0:SUBLANES, :] = jnp.zeros((SUBLANES, CONV_DIM), F32)

        first = first_sm[0]
        for n_key_tiles in range(1, n_tiles + 1):
            pl.when((t >= 2) & (jb - first + 1 == n_key_tiles))(functools.partial(
                _attention_t, n_key_tiles, first, tm, kaug_sc, qta_sc, vt_sc, attt_sc))

        def proj(slot, col, width):
            return jnp.dot(hb_sc[slot], _std_columns(w_lo, w_hi, col, width), preferred_element_type=F32)

        def proj_t(row, height):
            return lax.dot_general(w_t[row:row + height, :], hb_sc[slot_a], (((1,), (1,)), ((), ())),
                                   preferred_element_type=F32)

        norm_out()

        vf_t = proj_t(_T_V, ATT_DIM + F_ROWS)
        qk_t = proj_t(_T_Q, 2 * ATT_DIM)
        kt = qk_t[ATT_DIM:2 * ATT_DIM, :]
        vt = vf_t[0:ATT_DIM, :]
        kt_ref[...] = kt
        vt_ref[...] = vt
        logf_t = _log_sigmoid(vf_t[ATT_DIM:ATT_DIM + F_ROWS, :] + b_f_col[...])
        lft_ref[...] = logf_t[:N_HEADS, :]
        sums = jnp.dot(jnp.concatenate(_split3(logf_t), axis=0), tri[...], preferred_element_type=F32)

        mix_a = _conv_branch(functools.partial(proj, slot_b), u_sc, conv_w, conv_ref, w_pa, tm)

        y_b = attt_sc[...].T * zb_sc[gate_b]
        mixed = mix_a + gm_sc[gate_b] * jnp.dot(y_b.astype(BF16), w_pb[...], preferred_element_type=F32)
        z_sc[...] = ALPHA * h_sc[slot_b] + jnp.dot(mixed.astype(BF16), w_out[...], preferred_element_type=F32)

        zb_sc[gate_a] = jax.nn.silu(proj(slot_a, _S_ZB, ATT_DIM))
        gm_sc[gate_a] = jax.nn.sigmoid(proj(slot_a, _S_GM, D_MODEL))

        c_t = sum(sums[r * F_ROWS:(r + 1) * F_ROWS, :] for r in range(N_SPLIT)) + carryt_sc[:, 0:1]
        carryt_sc[...] = jnp.broadcast_to(c_t[:, tm - 1:tm], carryt_sc.shape)
        c_pieces = [piece.astype(F32) for piece in _split3(c_t * LOG2E)]
        qt = qk_t[0:ATT_DIM, :] * (SCALE * LOG2E)
        for hd in range(N_HEADS):
            _store_feature_major(qta_sc, hd, qt[hd * HEAD_DIM:(hd + 1) * HEAD_DIM, :].astype(BF16),
                                 _extra_rows(c_pieces, hd, tm, key_side=False))
            vt_sc[ja, hd * V_ROWS:hd * V_ROWS + HEAD_DIM, :] = (
                vt[hd * HEAD_DIM:(hd + 1) * HEAD_DIM, :].astype(BF16))
        ones_and_pad = jnp.where(lax.broadcasted_iota(jnp.int32, (LANES - ONES_LANE, tm), 0) == 0, 1.0, 0.0)
        packed_t = jnp.concatenate([piece[0:N_HEADS, :] for piece in c_pieces] + [ones_and_pad], axis=0)
        ex_k = jnp.dot(packed_t.T.astype(BF16), route[:, AUG_DIM:], preferred_element_type=F32)
        _store_token_major(kaug_sc, pl.ds(pl.multiple_of(ja * tm, tm), tm), kt.T, ex_k)
        first_sm[0] = _dead_key_tiles(ja, qt, kt, c_t[:, tm - 1:tm] * LOG2E, kn_sc, ce_sc)

        norm_in()

    @pl.when(t == n_total + 2)
    def _drain():
        norm_out()


def _resident(shape):
    return pl.BlockSpec(shape, lambda *_: (0,) * len(shape), pipeline_mode=pl.Buffered(1))


def _run_prompt(x, params, *, tm):
    bsz, t, _ = x.shape
    n_tiles = t // tm
    assert n_tiles * tm == t
    n_total = bsz * n_tiles
    tile = lambda s, lag: jnp.clip(s - lag, 0, n_total - 1)
    per_tile = lambda width, lag: pl.BlockSpec(
        (None, tm, width), lambda s: (tile(s, lag) // n_tiles, tile(s, lag) % n_tiles, 0))
    per_tile_t = lambda rows: pl.BlockSpec(
        (None, rows, tm), lambda s: (tile(s, 1) // n_tiles, 0, tile(s, 1) % n_tiles))
    out_shape = (jax.ShapeDtypeStruct((bsz, t, D_MODEL), F32),
                 jax.ShapeDtypeStruct((bsz, ATT_DIM, t), F32),
                 jax.ShapeDtypeStruct((bsz, ATT_DIM, t), F32),
                 jax.ShapeDtypeStruct((bsz, N_HEADS, t), F32),
                 jax.ShapeDtypeStruct((bsz, CONV_W - 1, CONV_DIM), F32))
    out_specs = (per_tile(D_MODEL, 3), per_tile_t(ATT_DIM), per_tile_t(ATT_DIM), per_tile_t(N_HEADS),
                 pl.BlockSpec((None, CONV_W - 1, CONV_DIM), lambda s: (tile(s, 2) // n_tiles, 0, 0)))
    scratch = [
        pltpu.VMEM((t, AUG_DIM), BF16),
        pltpu.VMEM((n_tiles, N_HEADS * V_ROWS, tm), BF16),
        pltpu.VMEM((PIPE_SLOTS, tm, D_MODEL), F32),
        pltpu.VMEM((PIPE_SLOTS, tm, D_MODEL), BF16),
        pltpu.VMEM((tm, D_MODEL), F32),
        pltpu.VMEM((N_HEADS, HEAD_BLOCK, tm), BF16),
        pltpu.VMEM((ATT_DIM, tm), F32),
        pltpu.VMEM((2, tm, ATT_DIM), F32),
        pltpu.VMEM((2, tm, D_MODEL), F32),
        pltpu.VMEM((SUBLANES + tm, CONV_DIM), F32),
        pltpu.VMEM((F_ROWS, LANES), F32),
        pltpu.VMEM((F_ROWS, LANES), F32),
        pltpu.VMEM((F_ROWS, LANES), F32),
        pltpu.SMEM((1,), jnp.int32),
    ]
    return pl.pallas_call(
        functools.partial(_prompt_kernel, tm=tm, n_tiles=n_tiles, n_total=n_total),
        grid=(n_total + PIPE_SLOTS,),
        in_specs=[per_tile(D_MODEL, 0)] + [_resident(p.shape) for p in params],
        out_specs=out_specs,
        out_shape=out_shape,
        scratch_shapes=scratch,
        compiler_params=pltpu.CompilerParams(
            dimension_semantics=("arbitrary",),
            vmem_limit_bytes=VMEM_LIMIT_BYTES),
        name="trunk_prompt",
    )(x, *params)


def _sample_kernel(x_ref, conv0_ref, pkt_ref, pvt_ref, plft_ref,
                   ln_in_g, ln_in_b, w_lo, w_hi, w_fs, b_f_row, w_t, b_f_col, conv_w, w_pa, w_pb, w_out,
                   ln_g, ln_b, route, ones,
                   y_ref, kt_ref, vt_ref, lft_ref, conv_ref,
                   kta_sc, ktn_sc, hb_sc, qa_sc, att_sc, u_sc, *, ts, n_past):
    @pl.when(pl.program_id(0) == 0)
    def _zero_padding():
        kta_sc[...] = jnp.zeros_like(kta_sc)
        ktn_sc[...] = jnp.zeros_like(ktn_sc)
        hb_sc[...] = jnp.zeros_like(hb_sc)

    h = _layer_norm(x_ref[...], ln_in_g[...], ln_in_b[...])
    hb_sc[0:ts, :] = h.astype(BF16)

    def proj(col, width):
        return jnp.dot(hb_sc[0:ts, :], _std_columns(w_lo, w_hi, col, width), preferred_element_type=F32)

    u_sc[0:SUBLANES, :] = jnp.zeros((SUBLANES, CONV_DIM), F32)
    u_sc[SUBLANES - (CONV_W - 1):SUBLANES, :] = conv0_ref[...]
    mix_a = _conv_branch(proj, u_sc, conv_w, conv_ref, w_pa, ts)

    def proj_t(row, height):
        return lax.dot_general(w_t[row:row + height, :], hb_sc[...], (((1,), (1,)), ((), ())),
                               preferred_element_type=F32)

    kt = proj_t(_T_K, ATT_DIM)
    vf_t = proj_t(_T_V, ATT_DIM + F_ROWS)
    vt = vf_t[0:ATT_DIM, :]
    kt_ref[...] = kt[:, :ts]
    vt_ref[...] = vt[:, :ts]
    logf_t = _log_sigmoid(vf_t[ATT_DIM:ATT_DIM + SUBLANES, :] + b_f_col[0:SUBLANES, :])
    lft_ref[...] = logf_t[:, :ts]

    c_past = _cumsum(plft_ref[...], 1)
    c_last = c_past[:, n_past - 1:n_past]
    c_new_t = _cumsum(logf_t, 1) + c_last
    past_pieces = [piece.astype(F32) for piece in _split3(c_past)]
    new_pieces = [piece.astype(F32) for piece in _split3(c_new_t)]
    for hd in range(N_HEADS):
        rows = slice(hd * HEAD_DIM, (hd + 1) * HEAD_DIM)
        _store_feature_major(kta_sc, hd, pkt_ref[rows, :].astype(BF16),
                             _extra_rows(past_pieces, hd, n_past, key_side=True))
        _store_feature_major(ktn_sc, hd, kt[rows, :].astype(BF16),
                             _extra_rows(new_pieces, hd, LANES, key_side=True))

    eye = (lax.broadcasted_iota(jnp.int32, (SUBLANES, LANES), 0)
           == lax.broadcasted_iota(jnp.int32, (SUBLANES, LANES), 1))
    c_last_row = jnp.sum(jnp.where(eye, jnp.broadcast_to(c_last, (SUBLANES, LANES)), 0.0),
                         axis=0, keepdims=True)
    logf = _log_sigmoid(jnp.dot(hb_sc[0:ts, :], w_fs[...], preferred_element_type=F32) + b_f_row[...])
    c_q = _cumsum(logf, 0) + c_last_row
    ex_q = _extra_lanes(c_q, route[:, :AUG_DIM]) + ones[:, :AUG_DIM]
    _store_token_major(qa_sc, slice(None), proj(_S_Q, ATT_DIM) * SCALE, ex_q)

    causal = (lax.broadcasted_iota(jnp.int32, (ts, LANES), 1)
              <= lax.broadcasted_iota(jnp.int32, (ts, LANES), 0))
    lane = lax.broadcasted_iota(jnp.int32, (ts, HEAD_BLOCK), 1)
    contract_lanes = (((1,), (1,)), ((), ()))
    scores = []
    for hd in range(N_HEADS):
        qa_h = qa_sc[:, hd * HEAD_BLOCK:(hd + 1) * HEAD_BLOCK]
        scores.append((jnp.dot(qa_h, kta_sc[hd], preferred_element_type=F32),
                       jnp.dot(qa_h, ktn_sc[hd], preferred_element_type=F32)))
    for pair in range(N_HEADS // 2):
        rows = slice(pair * HEAD_BLOCK, (pair + 1) * HEAD_BLOCK)
        vt_past = pvt_ref[rows, :].astype(BF16)
        vt_new = vt[rows, :].astype(BF16)
        probs, sums = [], []
        for hd in (2 * pair, 2 * pair + 1):
            s_past = scores[hd][0]
            s_new = jnp.where(causal, scores[hd][1], -jnp.inf)
            m = jnp.maximum(jnp.max(s_past, axis=-1, keepdims=True), jnp.max(s_new, axis=-1, keepdims=True))
            p_past = jnp.exp(s_past - m)
            p_new = jnp.exp(s_new - m)
            sums.append(jnp.sum(p_past, axis=-1, keepdims=True) + jnp.sum(p_new, axis=-1, keepdims=True))
            probs.append((p_past.astype(BF16), p_new.astype(BF16)))
        o = (lax.dot_general(jnp.concatenate([probs[0][0], probs[1][0]], axis=0), vt_past, contract_lanes,
                             preferred_element_type=F32)
             + lax.dot_general(jnp.concatenate([probs[0][1], probs[1][1]], axis=0), vt_new, contract_lanes,
                               preferred_element_type=F32))
        att_sc[:, rows] = jnp.where(lane < HEAD_DIM, o[0:ts, :] / sums[0], o[ts:2 * ts, :] / sums[1])

    y_ref[...] = _merge_and_norm(att_sc[...], mix_a, h, jax.nn.silu(proj(_S_ZB, ATT_DIM)),
                                 jax.nn.sigmoid(proj(_S_GM, D_MODEL)), w_pb, w_out, ln_g, ln_b)


def _run_sample(x, conv0, pkt, pvt, plft, params):
    bsz, ts, _ = x.shape
    n_past = pkt.shape[2]
    assert ts <= LANES and ts % SUBLANES == 0
    per_batch = lambda rows, width: pl.BlockSpec((None, rows, width), lambda b: (b, 0, 0))
    out_shape = (jax.ShapeDtypeStruct((bsz, ts, D_MODEL), F32),
                 jax.ShapeDtypeStruct((bsz, ATT_DIM, ts), F32),
                 jax.ShapeDtypeStruct((bsz, ATT_DIM, ts), F32),
                 jax.ShapeDtypeStruct((bsz, N_HEADS, ts), F32),
                 jax.ShapeDtypeStruct((bsz, CONV_W - 1, CONV_DIM), F32))
    out_specs = (per_batch(ts, D_MODEL), per_batch(ATT_DIM, ts), per_batch(ATT_DIM, ts),
                 per_batch(N_HEADS, ts), per_batch(CONV_W - 1, CONV_DIM))
    in_specs = [per_batch(ts, D_MODEL), per_batch(CONV_W - 1, CONV_DIM), per_batch(ATT_DIM, n_past),
                per_batch(ATT_DIM, n_past), per_batch(N_HEADS, n_past)]
    scratch = [
        pltpu.VMEM((N_HEADS, HEAD_BLOCK, n_past), BF16),
        pltpu.VMEM((N_HEADS, HEAD_BLOCK, LANES), BF16),
        pltpu.VMEM((LANES, D_MODEL), BF16),
        pltpu.VMEM((ts, AUG_DIM), BF16),
        pltpu.VMEM((ts, ATT_DIM), F32),
        pltpu.VMEM((SUBLANES + ts, CONV_DIM), F32),
    ]
    return pl.pallas_call(
        functools.partial(_sample_kernel, ts=ts, n_past=n_past),
        grid=(bsz,),
        in_specs=in_specs + [_resident(p.shape) for p in params],
        out_specs=out_specs,
        out_shape=out_shape,
        scratch_shapes=scratch,
        compiler_params=pltpu.CompilerParams(
            dimension_semantics=("arbitrary",),
            vmem_limit_bytes=VMEM_LIMIT_BYTES),
        name="trunk_sample",
    )(x, conv0, pkt, pvt, plft, *params)


def kernel(x_prompt, x_sample, cache_k, cache_v, cache_logf, state_conv, ln_in_g, ln_in_b, w_in, b_f,
           conv_w, w_proj_a, w_proj_b, w_out, ln_g, ln_b):
    assert w_in.shape[0] == DEPTH
    route_np, ones_np = _routing_constants()
    w = w_in[0]
    row = lambda a: a.reshape(1, -1).astype(F32)
    w_t = jnp.concatenate([w[:, _W_Q:_W_F + N_HEADS],
                           jnp.zeros((D_MODEL, F_ROWS - N_HEADS), w.dtype)], axis=1).T.astype(BF16)
    tm = PROMPT_TILE
    p = dict(
        ln_in_g=row(ln_in_g), ln_in_b=row(ln_in_b),
        w_lo=w[:, :_W_V].astype(BF16), w_hi=w[:, _W_ZB:].astype(BF16),
        w_fs=jnp.pad(w[:, _W_F:_W_F + N_HEADS], ((0, 0), (0, LANES - N_HEADS))).astype(BF16),
        b_f_row=jnp.pad(row(b_f[0]), ((0, 0), (0, LANES - N_HEADS))),
        w_t=w_t,
        b_f_col=jnp.pad(b_f[0].astype(F32).reshape(N_HEADS, 1), ((0, F_ROWS - N_HEADS), (0, 0))),
        conv_w=conv_w[0].astype(F32),
        w_pa=w_proj_a[0].astype(BF16), w_pb=w_proj_b[0].astype(BF16), w_out=w_out[0].astype(BF16),
        ln_g=row(ln_g[0]), ln_b=row(ln_b[0]),
        route=jnp.asarray(route_np, BF16), ones=jnp.asarray(ones_np, F32),
        tri=jnp.asarray(np.triu(np.ones((tm, tm), np.float32)), BF16),
    )
    bs = x_sample.shape[0]
    n_past = cache_k.shape[2]
    feature_major = lambda a: a.transpose(0, 2, 3, 1).reshape(bs, ATT_DIM, n_past)

    y_p, kt_p, vt_p, lft_p, cv_p = _run_prompt(x_prompt, [p[name] for name in PROMPT_PARAMS], tm=tm)
    y_s, kt_s, vt_s, lft_s, cv_s = _run_sample(
        x_sample, state_conv[0], feature_major(cache_k[0]), feature_major(cache_v[0]),
        cache_logf[0].transpose(0, 2, 1), [p[name] for name in SAMPLE_PARAMS])

    heads_t = lambda a: a.reshape(a.shape[0], N_HEADS, HEAD_DIM, a.shape[2]).transpose(0, 3, 1, 2)[None]
    tokens_t = lambda a: a.transpose(0, 2, 1)[None]
    return (y_p, y_s, heads_t(kt_p), heads_t(vt_p), tokens_t(lft_p), cv_p[None],
            heads_t(kt_s), heads_t(vt_s), tokens_t(lft_s), cv_s[None])
```

```python
import functools

import numpy as np
import jax
import jax.numpy as jnp
from jax import lax
from jax.experimental import pallas as pl
from jax.experimental.pallas import tpu as pltpu

D_MODEL = 1024
CONV_DIM = 512
CONV_W = 3
N_HEADS = 8
HEAD_DIM = 64
ATT_DIM = N_HEADS * HEAD_DIM
LN_EPS = 1e-5
DEPTH = 1
ALPHA = (2.0 * DEPTH) ** 0.25
SCALE = HEAD_DIM ** -0.5

LANES = 128
SUBLANES = 8
HEAD_BLOCK = LANES
AUG_DIM = N_HEADS * HEAD_BLOCK
N_SPLIT = 3
ONES_LANE = N_SPLIT * N_HEADS
EXTRA_ROWS = 16
F_ROWS = 16
VMEM_LIMIT_BYTES = 56 * 1024 * 1024
V_ROWS = HEAD_DIM + 16
LOG2E = 1.4426950408889634
UNDERFLOW_LOG2 = 156.0
NORM_SLACK = 1.02
PIPE_SLOTS = 3
SCORE_LOOKAHEAD = 4

_W_Q = 4 * CONV_DIM
_W_V = _W_Q + 2 * ATT_DIM
_W_F = _W_Q + 3 * ATT_DIM
_W_ZB = _W_F + N_HEADS
_S_XA, _S_GB, _S_GC, _S_ZA, _S_Q, _S_K, _S_ZB, _S_GA, _S_GM = (
    0, 512, 1024, 1536, 2048, 2560, 3072, 3584, 4608)
_T_Q, _T_K, _T_V, _T_F = 0, 512, 1024, 1536

PROMPT_TILE = 256
PROMPT_PARAMS = ("ln_in_g", "ln_in_b", "w_lo", "w_hi", "w_t", "b_f_col", "conv_w", "w_pa", "w_pb", "w_out",
                 "ln_g", "ln_b", "route", "tri")
SAMPLE_PARAMS = ("ln_in_g", "ln_in_b", "w_lo", "w_hi", "w_fs", "b_f_row", "w_t", "b_f_col", "conv_w", "w_pa",
                 "w_pb", "w_out", "ln_g", "ln_b", "route", "ones")

F32 = jnp.float32
BF16 = jnp.bfloat16


def _routing_constants():
    route = np.zeros((LANES, 2 * AUG_DIM), np.float32)
    ones = np.zeros((1, 2 * AUG_DIM), np.float32)
    for h in range(N_HEADS):
        base = h * HEAD_BLOCK + (HEAD_DIM if h % 2 == 0 else 0)
        for r in range(N_SPLIT):
            route[r * N_HEADS + h, base + 2 * r] = 1.0
            ones[0, base + 2 * r + 1] = 1.0
            ones[0, AUG_DIM + base + 2 * r] = 1.0
            route[r * N_HEADS + h, AUG_DIM + base + 2 * r + 1] = -1.0
    route[ONES_LANE, :] = ones[0, :]
    return route, ones


def _layer_norm(x, g, b):
    mu = jnp.mean(x, axis=-1, keepdims=True)
    xc = x - mu
    var = jnp.mean(xc * xc, axis=-1, keepdims=True)
    return xc * lax.rsqrt(var + LN_EPS) * g + b


def _log_sigmoid(x):
    return jnp.minimum(x, 0.0) - jnp.log1p(jnp.exp(-jnp.abs(x)))


def _split3(x):
    hi = x.astype(BF16)
    r1 = x - hi.astype(F32)
    mid = r1.astype(BF16)
    lo = (r1 - mid.astype(F32)).astype(BF16)
    return hi, mid, lo


def _cumsum(x, axis, segment=None):
    n = segment or x.shape[axis]
    idx = lax.broadcasted_iota(jnp.int32, x.shape, axis)
    if segment:
        idx = lax.rem(idx, segment)
    s = 1
    while s < n:
        x = x + jnp.where(idx >= s, pltpu.roll(x, s, axis), 0.0)
        s *= 2
    return x


def _extra_lanes(c, route):
    lane = lax.broadcasted_iota(jnp.int32, c.shape, 1)
    valid = lane < N_HEADS
    packed = jnp.zeros_like(c)
    for r, piece in enumerate(_split3(c)):
        p32 = jnp.where(valid, piece.astype(F32), 0.0)
        packed = packed + (pltpu.roll(p32, r * N_HEADS, 1) if r else p32)
    return jnp.dot(packed.astype(BF16), route, preferred_element_type=F32)


def _store_token_major(dst_ref, rows, data, extras):
    n = data.shape[0]
    lane = lax.broadcasted_iota(jnp.int32, (n, HEAD_BLOCK), 1)
    for h in range(N_HEADS):
        pair = data[:, (h // 2) * HEAD_BLOCK:(h // 2 + 1) * HEAD_BLOCK]
        ex = extras[:, h * HEAD_BLOCK:(h + 1) * HEAD_BLOCK]
        keep = (lane < HEAD_DIM) if h % 2 == 0 else (lane >= HEAD_DIM)
        dst_ref[rows, h * HEAD_BLOCK:(h + 1) * HEAD_BLOCK] = jnp.where(keep, pair, ex).astype(BF16)


def _extra_rows(pieces, h, n, key_side):
    r = lax.broadcasted_iota(jnp.int32, (EXTRA_ROWS, n), 0)
    value_row, sign = (1, -1.0) if key_side else (0, 1.0)
    out = jnp.where((r < 2 * N_SPLIT) & ((r & 1) != value_row), 1.0, 0.0)
    for i, piece in enumerate(pieces):
        out = jnp.where(r == 2 * i + value_row,
                        jnp.broadcast_to(sign * piece[h:h + 1, :], (EXTRA_ROWS, n)), out)
    return out.astype(BF16)


def _store_feature_major(dst_ref, h, data, extra):
    if h % 2 == 0:
        dst_ref[h, 0:HEAD_DIM, :] = data
        dst_ref[h, HEAD_DIM:HEAD_DIM + EXTRA_ROWS, :] = extra
    else:
        dst_ref[h, 0:EXTRA_ROWS, :] = extra
        dst_ref[h, HEAD_DIM:2 * HEAD_DIM, :] = data


def _std_columns(w_lo, w_hi, col, width):
    split = w_lo.shape[1]
    assert col + width <= split or col >= split
    return w_lo[:, col:col + width] if col < split else w_hi[:, col - split:col - split + width]


def _conv_branch(proj, u_sc, conv_w, conv_ref, w_pa, n):
    first = SUBLANES - (CONV_W - 1)
    u_sc[SUBLANES:SUBLANES + n, :] = proj(_S_GC, CONV_DIM) * proj(_S_XA, CONV_DIM)
    conv = sum(u_sc[first + i:first + i + n, :] * conv_w[i:i + 1, :] for i in range(CONV_W))
    new_conv = u_sc[first + n:SUBLANES + n, :]
    conv_ref[...] = new_conv
    u_sc[first:SUBLANES, :] = new_conv
    y_a = proj(_S_GB, CONV_DIM) * conv * jax.nn.silu(proj(_S_ZA, CONV_DIM))
    return jax.nn.sigmoid(proj(_S_GA, D_MODEL)) * jnp.dot(
        y_a.astype(BF16), w_pa[...], preferred_element_type=F32)


def _merge_and_norm(att, mix_a, h, silu_zb, gate_b, w_pb, w_out, ln_g, ln_b):
    y_b = att * silu_zb
    mixed = mix_a + gate_b * jnp.dot(
        y_b.astype(BF16), w_pb[...], preferred_element_type=F32)
    sub = jnp.dot(mixed.astype(BF16), w_out[...], preferred_element_type=F32)
    return _layer_norm(ALPHA * h + sub, ln_g[...], ln_b[...])


def _head_max_sq_norm(xt):
    n = xt.shape[1]
    row = lax.broadcasted_iota(jnp.int32, (F_ROWS, n), 0)
    sq = xt * xt
    acc = jnp.zeros((F_ROWS, n), F32)
    for h in range(N_HEADS):
        norm = jnp.sum(sq[h * HEAD_DIM:(h + 1) * HEAD_DIM, :], axis=0, keepdims=True)
        acc = jnp.where(row == h, jnp.broadcast_to(norm, (F_ROWS, n)), acc)
    return jnp.max(acc, axis=1, keepdims=True)


def _dead_key_tiles(j, qt2, kt, c2_end, kn_sc, ce_sc):
    lane = lax.broadcasted_iota(jnp.int32, (F_ROWS, LANES), 1)
    row = lax.broadcasted_iota(jnp.int32, (F_ROWS, LANES), 0)
    spread = lambda col: jnp.broadcast_to(col, (F_ROWS, LANES))
    kn = jnp.where(lane == j, spread(_head_max_sq_norm(kt)), jnp.where(lane < j, kn_sc[...], 0.0))
    ce = jnp.where(lane == j, spread(c2_end), jnp.where(lane < j, ce_sc[...], 0.0))
    kn_sc[...] = kn
    ce_sc[...] = ce
    k_max = jnp.max(kn, axis=1, keepdims=True)
    ce_prev = jnp.sum(jnp.where(lane == j - 1, ce, 0.0), axis=1, keepdims=True)
    bound = NORM_SLACK * 2.0 * jnp.sqrt(_head_max_sq_norm(qt2) * k_max) + ce_prev - ce
    worst = jnp.max(jnp.where(row < N_HEADS, bound, -jnp.inf), axis=0, keepdims=True)
    dead = (worst < -UNDERFLOW_LOG2) & (lane[0:1, :] <= j - 2)
    return jnp.min(jnp.where(dead, LANES, lane[0:1, :]))


def _attention_t(n_key_tiles, first, tm, kaug_sc, qta_sc, vt_sc, attt_sc):
    keys = pl.ds(pl.multiple_of(first * tm, tm), n_key_tiles * tm)
    causal = (lax.broadcasted_iota(jnp.int32, (tm, tm), 0)
              <= lax.broadcasted_iota(jnp.int32, (tm, tm), 1))

    def scores(hd):
        return jnp.dot(kaug_sc[keys, hd * HEAD_BLOCK:(hd + 1) * HEAD_BLOCK], qta_sc[hd],
                       preferred_element_type=F32)

    pending = [scores(hd) for hd in range(SCORE_LOOKAHEAD)]
    for hd in range(N_HEADS):
        s = pending.pop(0)
        if hd + SCORE_LOOKAHEAD < N_HEADS:
            pending.append(scores(hd + SCORE_LOOKAHEAD))
        blocks = [s[i * tm:(i + 1) * tm, :] for i in range(n_key_tiles)]
        blocks[-1] = jnp.where(causal, blocks[-1], -jnp.inf)
        m = functools.reduce(jnp.maximum, [jnp.max(blk, axis=0, keepdims=True) for blk in blocks])
        o = jnp.zeros((V_ROWS, tm), F32)
        for i, blk in enumerate(blocks):
            o = o + jnp.dot(vt_sc[first + i, hd * V_ROWS:(hd + 1) * V_ROWS, :],
                            jnp.exp2(blk - m).astype(BF16), preferred_element_type=F32)
        attt_sc[hd * HEAD_DIM:(hd + 1) * HEAD_DIM, :] = o[0:HEAD_DIM, :] / o[HEAD_DIM:HEAD_DIM + 1, :]


def _prompt_kernel(x_ref, ln_in_g, ln_in_b, w_lo, w_hi, w_t, b_f_col, conv_w, w_pa, w_pb, w_out,
                   ln_g, ln_b, route, tri,
                   y_ref, kt_ref, vt_ref, lft_ref, conv_ref,
                   kaug_sc, vt_sc, h_sc, hb_sc, z_sc, qta_sc, attt_sc, zb_sc, gm_sc, u_sc, carryt_sc,
                   kn_sc, ce_sc, first_sm, *, tm, n_tiles, n_total):
    t = pl.program_id(0)

    def norm_in():
        h = _layer_norm(x_ref[...], ln_in_g[...], ln_in_b[...])
        h_sc[lax.rem(t, PIPE_SLOTS)] = h
        hb_sc[lax.rem(t, PIPE_SLOTS)] = h.astype(BF16)

    def norm_out():
        y_ref[...] = _layer_norm(z_sc[...], ln_g[...], ln_b[...])

    @pl.when(t == 0)
    def _fill():
        for ref in (z_sc, kaug_sc, vt_sc, h_sc, hb_sc, attt_sc, zb_sc, gm_sc, u_sc, qta_sc, carryt_sc,
                    kn_sc, ce_sc):
            ref[...] = jnp.zeros_like(ref)
        first_sm[0] = 0
        ones_row = jnp.where(lax.broadcasted_iota(jnp.int32, (V_ROWS - HEAD_DIM, tm), 0) == 0, 1.0, 0.0)
        for i in range(n_tiles):
            for hd in range(N_HEADS):
                vt_sc[i, hd * V_ROWS + HEAD_DIM:(hd + 1) * V_ROWS, :] = ones_row.astype(BF16)
        norm_in()

    @pl.when((t >= 1) & (t <= n_total + 1))
    def _mixers():
        tile_a = t - 1
        tile_b = t - 2
        ja = lax.rem(tile_a, n_tiles)
        jb = lax.rem(tile_b, n_tiles)
        slot_a = lax.rem(tile_a, PIPE_SLOTS)
        slot_b = lax.rem(t + 1, PIPE_SLOTS)
        gate_a = tile_a & 1
        gate_b = t & 1

        @pl.when(ja == 0)
        def _start_of_sequence():
            carryt_sc[...] = jnp.zeros_like(carryt_sc)

        @pl.when(jb == 0)
        def _start_of_conv():
            u_sc[0:SUBLANES, :] = jnp.zeros((SUBLANES, CONV_DIM), F32)

        first = first_sm[0]
        for n_key_tiles in range(1, n_tiles + 1):
            pl.when((t >= 2) & (jb - first + 1 == n_key_tiles))(functools.partial(
                _attention_t, n_key_tiles, first, tm, kaug_sc, qta_sc, vt_sc, attt_sc))

        def proj(slot, col, width):
            return jnp.dot(hb_sc[slot], _std_columns(w_lo, w_hi, col, width), preferred_element_type=F32)

        def proj_t(row, height):
            return lax.dot_general(w_t[row:row + height, :], hb_sc[slot_a], (((1,), (1,)), ((), ())),
                                   preferred_element_type=F32)

        norm_out()

        vf_t = proj_t(_T_V, ATT_DIM + F_ROWS)
        qk_t = proj_t(_T_Q, 2 * ATT_DIM)
        kt = qk_t[ATT_DIM:2 * ATT_DIM, :]
        vt = vf_t[0:ATT_DIM, :]
        kt_ref[...] = kt
        vt_ref[...] = vt
        logf_t = _log_sigmoid(vf_t[ATT_DIM:ATT_DIM + F_ROWS, :] + b_f_col[...])
        lft_ref[...] = logf_t[:N_HEADS, :]
        sums = jnp.dot(jnp.concatenate(_split3(logf_t), axis=0), tri[...], preferred_element_type=F32)

        mix_a = _conv_branch(functools.partial(proj, slot_b), u_sc, conv_w, conv_ref, w_pa, tm)

        y_b = attt_sc[...].T * zb_sc[gate_b]
        mixed = mix_a + gm_sc[gate_b] * jnp.dot(y_b.astype(BF16), w_pb[...], preferred_element_type=F32)
        z_sc[...] = ALPHA * h_sc[slot_b] + jnp.dot(mixed.astype(BF16), w_out[...], preferred_element_type=F32)

        zb_sc[gate_a] = jax.nn.silu(proj(slot_a, _S_ZB, ATT_DIM))
        gm_sc[gate_a] = jax.nn.sigmoid(proj(slot_a, _S_GM, D_MODEL))

        c_t = sum(sums[r * F_ROWS:(r + 1) * F_ROWS, :] for r in range(N_SPLIT)) + carryt_sc[:, 0:1]
        carryt_sc[...] = jnp.broadcast_to(c_t[:, tm - 1:tm], carryt_sc.shape)
        c_pieces = [piece.astype(F32) for piece in _split3(c_t * LOG2E)]
        qt = qk_t[0:ATT_DIM, :] * (SCALE * LOG2E)
        for hd in range(N_HEADS):
            _store_feature_major(qta_sc, hd, qt[hd * HEAD_DIM:(hd + 1) * HEAD_DIM, :].astype(BF16),
                                 _extra_rows(c_pieces, hd, tm, key_side=False))
            vt_sc[ja, hd * V_ROWS:hd * V_ROWS + HEAD_DIM, :] = (
                vt[hd * HEAD_DIM:(hd + 1) * HEAD_DIM, :].astype(BF16))
        ones_and_pad = jnp.where(lax.broadcasted_iota(jnp.int32, (LANES - ONES_LANE, tm), 0) == 0, 1.0, 0.0)
        packed_t = jnp.concatenate([piece[0:N_HEADS, :] for piece in c_pieces] + [ones_and_pad], axis=0)
        ex_k = jnp.dot(packed_t.T.astype(BF16), route[:, AUG_DIM:], preferred_element_type=F32)
        _store_token_major(kaug_sc, pl.ds(pl.multiple_of(ja * tm, tm), tm), kt.T, ex_k)
        first_sm[0] = _dead_key_tiles(ja, qt, kt, c_t[:, tm - 1:tm] * LOG2E, kn_sc, ce_sc)

        norm_in()

    @pl.when(t == n_total + 2)
    def _drain():
        norm_out()


def _resident(shape):
    return pl.BlockSpec(shape, lambda *_: (0,) * len(shape), pipeline_mode=pl.Buffered(1))


def _run_prompt(x, params, *, tm):
    bsz, t, _ = x.shape
    n_tiles = t // tm
    assert n_tiles * tm == t
    n_total = bsz * n_tiles
    tile = lambda s, lag: jnp.clip(s - lag, 0, n_total - 1)
    per_tile = lambda width, lag: pl.BlockSpec(
        (None, tm, width), lambda s: (tile(s, lag) // n_tiles, tile(s, lag) % n_tiles, 0))
    per_tile_t = lambda rows: pl.BlockSpec(
        (None, rows, tm), lambda s: (tile(s, 1) // n_tiles, 0, tile(s, 1) % n_tiles))
    out_shape = (jax.ShapeDtypeStruct((bsz, t, D_MODEL), F32),
                 jax.ShapeDtypeStruct((bsz, ATT_DIM, t), F32),
                 jax.ShapeDtypeStruct((bsz, ATT_DIM, t), F32),
                 jax.ShapeDtypeStruct((bsz, N_HEADS, t), F32),
                 jax.ShapeDtypeStruct((bsz, CONV_W - 1, CONV_DIM), F32))
    out_specs = (per_tile(D_MODEL, 3), per_tile_t(ATT_DIM), per_tile_t(ATT_DIM), per_tile_t(N_HEADS),
                 pl.BlockSpec((None, CONV_W - 1, CONV_DIM), lambda s: (tile(s, 2) // n_tiles, 0, 0)))
    scratch = [
        pltpu.VMEM((t, AUG_DIM), BF16),
        pltpu.VMEM((n_tiles, N_HEADS * V_ROWS, tm), BF16),
        pltpu.VMEM((PIPE_SLOTS, tm, D_MODEL), F32),
        pltpu.VMEM((PIPE_SLOTS, tm, D_MODEL), BF16),
        pltpu.VMEM((tm, D_MODEL), F32),
        pltpu.VMEM((N_HEADS, HEAD_BLOCK, tm), BF16),
        pltpu.VMEM((ATT_DIM, tm), F32),
        pltpu.VMEM((2, tm, ATT_DIM), F32),
        pltpu.VMEM((2, tm, D_MODEL), F32),
        pltpu.VMEM((SUBLANES + tm, CONV_DIM), F32),
        pltpu.VMEM((F_ROWS, LANES), F32),
        pltpu.VMEM((F_ROWS, LANES), F32),
        pltpu.VMEM((F_ROWS, LANES), F32),
        pltpu.SMEM((1,), jnp.int32),
    ]
    return pl.pallas_call(
        functools.partial(_prompt_kernel, tm=tm, n_tiles=n_tiles, n_total=n_total),
        grid=(n_total + PIPE_SLOTS,),
        in_specs=[per_tile(D_MODEL, 0)] + [_resident(p.shape) for p in params],
        out_specs=out_specs,
        out_shape=out_shape,
        scratch_shapes=scratch,
        compiler_params=pltpu.CompilerParams(
            dimension_semantics=("arbitrary",),
            vmem_limit_bytes=VMEM_LIMIT_BYTES),
        name="trunk_prompt",
    )(x, *params)


def _sample_kernel(x_ref, conv0_ref, pkt_ref, pvt_ref, plft_ref,
                   ln_in_g, ln_in_b, w_lo, w_hi, w_fs, b_f_row, w_t, b_f_col, conv_w, w_pa, w_pb, w_out,
                   ln_g, ln_b, route, ones,
                   y_ref, kt_ref, vt_ref, lft_ref, conv_ref,
                   kta_sc, ktn_sc, hb_sc, qa_sc, att_sc, u_sc, h_sc, mixa_sc, cpast_sc, vtn_sc,
                   *, ts, n_past, group):
    stream = lax.rem(pl.program_id(0), group)
    first = SUBLANES - (CONV_W - 1)

    @pl.when(pl.program_id(0) == 0)
    def _zero_padding():
        kta_sc[...] = jnp.zeros_like(kta_sc)
        ktn_sc[...] = jnp.zeros_like(ktn_sc)

    def proj(col, width):
        return jnp.dot(hb_sc[...], _std_columns(w_lo, w_hi, col, width), preferred_element_type=F32)

    @pl.when(stream == 0)
    def _group_projections():
        h = _layer_norm(jnp.concatenate([x_ref[s] for s in range(group)], axis=0), ln_in_g[...], ln_in_b[...])
        h_sc[...] = h
        hb_sc[...] = h.astype(BF16)

        u = proj(_S_GC, CONV_DIM) * proj(_S_XA, CONV_DIM)
        convs = []
        for s in range(group):
            u_sc[s, first:SUBLANES, :] = conv0_ref[s]
            u_sc[s, SUBLANES:SUBLANES + ts, :] = u[s * ts:(s + 1) * ts, :]
            convs.append(sum(u_sc[s, first + i:first + i + ts, :] * conv_w[i:i + 1, :] for i in range(CONV_W)))
            conv_ref[s] = u_sc[s, first + ts:SUBLANES + ts, :]
        y_a = proj(_S_GB, CONV_DIM) * jnp.concatenate(convs, axis=0) * jax.nn.silu(proj(_S_ZA, CONV_DIM))
        mixa_sc[...] = jax.nn.sigmoid(proj(_S_GA, D_MODEL)) * jnp.dot(
            y_a.astype(BF16), w_pa[...], preferred_element_type=F32)

        def proj_t(row, height):
            return lax.dot_general(w_t[row:row + height, :], hb_sc[...], (((1,), (1,)), ((), ())),
                                   preferred_element_type=F32)

        kt = proj_t(_T_K, ATT_DIM)
        vf_t = proj_t(_T_V, ATT_DIM + F_ROWS)
        vt = vf_t[0:ATT_DIM, :]
        vtn_sc[...] = vt.astype(BF16)
        logf_t = _log_sigmoid(vf_t[ATT_DIM:ATT_DIM + SUBLANES, :] + b_f_col[0:SUBLANES, :])

        eye = (lax.broadcasted_iota(jnp.int32, (SUBLANES, LANES), 0)
               == lax.broadcasted_iota(jnp.int32, (SUBLANES, LANES), 1))
        lane_stream = lax.broadcasted_iota(jnp.int32, (SUBLANES, LANES), 1) // ts
        c_last_lanes = jnp.zeros((SUBLANES, LANES), F32)
        c_last_rows = []
        for s in range(group):
            kt_ref[s] = kt[:, s * ts:(s + 1) * ts]
            vt_ref[s] = vt[:, s * ts:(s + 1) * ts]
            lft_ref[s] = logf_t[:, s * ts:(s + 1) * ts]
            c_past = _cumsum(plft_ref[s], 1)
            cpast_sc[s] = c_past
            c_last = jnp.broadcast_to(c_past[:, n_past - 1:n_past], (SUBLANES, LANES))
            c_last_lanes = jnp.where(lane_stream == s, c_last, c_last_lanes)
            c_last_rows.append(jnp.broadcast_to(
                jnp.sum(jnp.where(eye, c_last, 0.0), axis=0, keepdims=True), (ts, LANES)))
        c_new_t = _cumsum(logf_t, 1, segment=ts) + c_last_lanes
        new_pieces = [piece.astype(F32) for piece in _split3(c_new_t)]
        for hd in range(N_HEADS):
            rows = slice(hd * HEAD_DIM, (hd + 1) * HEAD_DIM)
            _store_feature_major(ktn_sc, hd, kt[rows, :].astype(BF16),
                                 _extra_rows(new_pieces, hd, LANES, key_side=True))

        logf = _log_sigmoid(jnp.dot(hb_sc[...], w_fs[...], preferred_element_type=F32) + b_f_row[...])
        c_q = _cumsum(logf, 0, segment=ts) + jnp.concatenate(c_last_rows, axis=0)
        ex_q = _extra_lanes(c_q, route[:, :AUG_DIM]) + ones[:, :AUG_DIM]
        _store_token_major(qa_sc, slice(None), proj(_S_Q, ATT_DIM) * SCALE, ex_q)

    past_pieces = [piece.astype(F32) for piece in _split3(cpast_sc[stream])]
    for hd in range(N_HEADS):
        rows = slice(hd * HEAD_DIM, (hd + 1) * HEAD_DIM)
        _store_feature_major(kta_sc, hd, pkt_ref[rows, :].astype(BF16),
                             _extra_rows(past_pieces, hd, n_past, key_side=True))

    lane_key = lax.broadcasted_iota(jnp.int32, (ts, LANES), 1) - stream * ts
    causal = (lane_key >= 0) & (lane_key <= lax.broadcasted_iota(jnp.int32, (ts, LANES), 0))
    lane = lax.broadcasted_iota(jnp.int32, (ts, HEAD_BLOCK), 1)
    contract_lanes = (((1,), (1,)), ((), ()))
    stream_rows = pl.ds(pl.multiple_of(stream * ts, ts), ts)
    scores = []
    for hd in range(N_HEADS):
        qa_h = qa_sc[stream_rows, hd * HEAD_BLOCK:(hd + 1) * HEAD_BLOCK]
        scores.append((jnp.dot(qa_h, kta_sc[hd], preferred_element_type=F32),
                       jnp.dot(qa_h, ktn_sc[hd], preferred_element_type=F32)))
    for pair in range(N_HEADS // 2):
        rows = slice(pair * HEAD_BLOCK, (pair + 1) * HEAD_BLOCK)
        vt_past = pvt_ref[rows, :].astype(BF16)
        vt_new = vtn_sc[rows, :]
        probs, sums = [], []
        for hd in (2 * pair, 2 * pair + 1):
            s_past = scores[hd][0]
            s_new = jnp.where(causal, scores[hd][1], -jnp.inf)
            m = jnp.maximum(jnp.max(s_past, axis=-1, keepdims=True), jnp.max(s_new, axis=-1, keepdims=True))
            p_past = jnp.exp(s_past - m)
            p_new = jnp.exp(s_new - m)
            sums.append(jnp.sum(p_past, axis=-1, keepdims=True) + jnp.sum(p_new, axis=-1, keepdims=True))
            probs.append((p_past.astype(BF16), p_new.astype(BF16)))
        o = (lax.dot_general(jnp.concatenate([probs[0][0], probs[1][0]], axis=0), vt_past, contract_lanes,
                             preferred_element_type=F32)
             + lax.dot_general(jnp.concatenate([probs[0][1], probs[1][1]], axis=0), vt_new, contract_lanes,
                               preferred_element_type=F32))
        att_sc[stream_rows, rows] = jnp.where(lane < HEAD_DIM, o[0:ts, :] / sums[0], o[ts:2 * ts, :] / sums[1])

    @pl.when(stream == group - 1)
    def _group_merge():
        y = _merge_and_norm(att_sc[...], mixa_sc[...], h_sc[...], jax.nn.silu(proj(_S_ZB, ATT_DIM)),
                            jax.nn.sigmoid(proj(_S_GM, D_MODEL)), w_pb, w_out, ln_g, ln_b)
        for s in range(group):
            y_ref[s] = y[s * ts:(s + 1) * ts, :]


def _run_sample(x, conv0, pkt, pvt, plft, params):
    bsz, ts, _ = x.shape
    n_past = pkt.shape[2]
    assert LANES % ts == 0 and ts % (2 * SUBLANES) == 0
    group = LANES // ts
    assert bsz % group == 0
    per_batch = lambda rows, width: pl.BlockSpec((None, rows, width), lambda b: (b, 0, 0))
    per_group = lambda rows, width: pl.BlockSpec((group, rows, width), lambda b: (b // group, 0, 0))
    out_shape = (jax.ShapeDtypeStruct((bsz, ts, D_MODEL), F32),
                 jax.ShapeDtypeStruct((bsz, ATT_DIM, ts), F32),
                 jax.ShapeDtypeStruct((bsz, ATT_DIM, ts), F32),
                 jax.ShapeDtypeStruct((bsz, N_HEADS, ts), F32),
                 jax.ShapeDtypeStruct((bsz, CONV_W - 1, CONV_DIM), F32))
    out_specs = (per_group(ts, D_MODEL), per_group(ATT_DIM, ts), per_group(ATT_DIM, ts),
                 per_group(N_HEADS, ts), per_group(CONV_W - 1, CONV_DIM))
    in_specs = [per_group(ts, D_MODEL), per_group(CONV_W - 1, CONV_DIM), per_batch(ATT_DIM, n_past),
                per_batch(ATT_DIM, n_past), per_group(N_HEADS, n_past)]
    scratch = [
        pltpu.VMEM((N_HEADS, HEAD_BLOCK, n_past), BF16),
        pltpu.VMEM((N_HEADS, HEAD_BLOCK, LANES), BF16),
        pltpu.VMEM((LANES, D_MODEL), BF16),
        pltpu.VMEM((LANES, AUG_DIM), BF16),
        pltpu.VMEM((LANES, ATT_DIM), F32),
        pltpu.VMEM((group, SUBLANES + ts, CONV_DIM), F32),
        pltpu.VMEM((LANES, D_MODEL), F32),
        pltpu.VMEM((LANES, D_MODEL), F32),
        pltpu.VMEM((group, N_HEADS, n_past), F32),
        pltpu.VMEM((ATT_DIM, LANES), BF16),
    ]
    return pl.pallas_call(
        functools.partial(_sample_kernel, ts=ts, n_past=n_past, group=group),
        grid=(bsz,),
        in_specs=in_specs + [_resident(p.shape) for p in params],
        out_specs=out_specs,
        out_shape=out_shape,
        scratch_shapes=scratch,
        compiler_params=pltpu.CompilerParams(
            dimension_semantics=("arbitrary",),
            vmem_limit_bytes=VMEM_LIMIT_BYTES),
        name="trunk_sample",
    )(x, conv0, pkt, pvt, plft, *params)


def kernel(x_prompt, x_sample, cache_k, cache_v, cache_logf, state_conv, ln_in_g, ln_in_b, w_in, b_f,
           conv_w, w_proj_a, w_proj_b, w_out, ln_g, ln_b):
    assert w_in.shape[0] == DEPTH
    route_np, ones_np = _routing_constants()
    w = w_in[0]
    row = lambda a: a.reshape(1, -1).astype(F32)
    w_t = jnp.concatenate([w[:, _W_Q:_W_F + N_HEADS],
                           jnp.zeros((D_MODEL, F_ROWS - N_HEADS), w.dtype)], axis=1).T.astype(BF16)
    tm = PROMPT_TILE
    p = dict(
        ln_in_g=row(ln_in_g), ln_in_b=row(ln_in_b),
        w_lo=w[:, :_W_V].astype(BF16), w_hi=w[:, _W_ZB:].astype(BF16),
        w_fs=jnp.pad(w[:, _W_F:_W_F + N_HEADS], ((0, 0), (0, LANES - N_HEADS))).astype(BF16),
        b_f_row=jnp.pad(row(b_f[0]), ((0, 0), (0, LANES - N_HEADS))),
        w_t=w_t,
        b_f_col=jnp.pad(b_f[0].astype(F32).reshape(N_HEADS, 1), ((0, F_ROWS - N_HEADS), (0, 0))),
        conv_w=conv_w[0].astype(F32),
        w_pa=w_proj_a[0].astype(BF16), w_pb=w_proj_b[0].astype(BF16), w_out=w_out[0].astype(BF16),
        ln_g=row(ln_g[0]), ln_b=row(ln_b[0]),
        route=jnp.asarray(route_np, BF16), ones=jnp.asarray(ones_np, F32),
        tri=jnp.asarray(np.triu(np.ones((tm, tm), np.float32)), BF16),
    )
    bs = x_sample.shape[0]
    n_past = cache_k.shape[2]
    feature_major = lambda a: a.transpose(0, 2, 3, 1).reshape(bs, ATT_DIM, n_past)

    y_p, kt_p, vt_p, lft_p, cv_p = _run_prompt(x_prompt, [p[name] for name in PROMPT_PARAMS], tm=tm)
    y_s, kt_s, vt_s, lft_s, cv_s = _run_sample(
        x_sample, state_conv[0], feature_major(cache_k[0]), feature_major(cache_v[0]),
        cache_logf[0].transpose(0, 2, 1), [p[name] for name in SAMPLE_PARAMS])

    heads_t = lambda a: a.reshape(a.shape[0], N_HEADS, HEAD_DIM, a.shape[2]).transpose(0, 3, 1, 2)[None]
    tokens_t = lambda a: a.transpose(0, 2, 1)[None]
    return (y_p, y_s, heads_t(kt_p), heads_t(vt_p), tokens_t(lft_p), cv_p[None],
            heads_t(kt_s), heads_t(vt_s), tokens_t(lft_s), cv_s[None])
```

```python
import functools

import numpy as np
import jax
import jax.numpy as jnp
from jax import lax
from jax.experimental import pallas as pl
from jax.experimental.pallas import tpu as pltpu

D_MODEL = 1024
CONV_DIM = 512
CONV_W = 3
N_HEADS = 8
HEAD_DIM = 64
ATT_DIM = N_HEADS * HEAD_DIM
LN_EPS = 1e-5
DEPTH = 1
ALPHA = (2.0 * DEPTH) ** 0.25
SCALE = HEAD_DIM ** -0.5

LANES = 128
SUBLANES = 8
HEAD_BLOCK = LANES
AUG_DIM = N_HEADS * HEAD_BLOCK
N_SPLIT = 3
ONES_LANE = N_SPLIT * N_HEADS
EXTRA_ROWS = 16
F_ROWS = 16
VMEM_LIMIT_BYTES = 56 * 1024 * 1024
V_ROWS = HEAD_DIM + 16
LOG2E = 1.4426950408889634
UNDERFLOW_LOG2 = 156.0
NORM_SLACK = 1.02
PIPE_SLOTS = 3
SCORE_LOOKAHEAD = 4

_W_Q = 4 * CONV_DIM
_W_V = _W_Q + 2 * ATT_DIM
_W_F = _W_Q + 3 * ATT_DIM
_W_ZB = _W_F + N_HEADS
_S_XA, _S_GB, _S_GC, _S_ZA, _S_Q, _S_K, _S_ZB, _S_GA, _S_GM = (
    0, 512, 1024, 1536, 2048, 2560, 3072, 3584, 4608)
_T_Q, _T_K, _T_V, _T_F = 0, 512, 1024, 1536

PROMPT_TILE = 256
PROMPT_PARAMS = ("ln_in_g", "ln_in_b", "w_lo", "w_hi", "w_t", "b_f_col", "conv_w", "w_pa", "w_pb", "w_out",
                 "ln_g", "ln_b", "route", "tri")
SAMPLE_PARAMS = ("ln_in_g", "ln_in_b", "w_lo", "w_hi", "w_fs", "b_f_row", "w_t", "b_f_col", "conv_w", "w_pa",
                 "w_pb", "w_out", "ln_g", "ln_b", "route", "ones")

F32 = jnp.float32
BF16 = jnp.bfloat16


def _routing_constants():
    route = np.zeros((LANES, 2 * AUG_DIM), np.float32)
    ones = np.zeros((1, 2 * AUG_DIM), np.float32)
    for h in range(N_HEADS):
        base = h * HEAD_BLOCK + (HEAD_DIM if h % 2 == 0 else 0)
        for r in range(N_SPLIT):
            route[r * N_HEADS + h, base + 2 * r] = 1.0
            ones[0, base + 2 * r + 1] = 1.0
            ones[0, AUG_DIM + base + 2 * r] = 1.0
            route[r * N_HEADS + h, AUG_DIM + base + 2 * r + 1] = -1.0
    route[ONES_LANE, :] = ones[0, :]
    return route, ones


def _layer_norm(x, g, b):
    mu = jnp.mean(x, axis=-1, keepdims=True)
    xc = x - mu
    var = jnp.mean(xc * xc, axis=-1, keepdims=True)
    return xc * lax.rsqrt(var + LN_EPS) * g + b


def _log_sigmoid(x):
    return jnp.minimum(x, 0.0) - jnp.log1p(jnp.exp(-jnp.abs(x)))


def _split3(x):
    hi = x.astype(BF16)
    r1 = x - hi.astype(F32)
    mid = r1.astype(BF16)
    lo = (r1 - mid.astype(F32)).astype(BF16)
    return hi, mid, lo


def _cumsum(x, axis, segment=None):
    n = segment or x.shape[axis]
    idx = lax.broadcasted_iota(jnp.int32, x.shape, axis)
    if segment:
        idx = lax.rem(idx, segment)
    s = 1
    while s < n:
        x = x + jnp.where(idx >= s, pltpu.roll(x, s, axis), 0.0)
        s *= 2
    return x


def _extra_lanes(c, route):
    lane = lax.broadcasted_iota(jnp.int32, c.shape, 1)
    valid = lane < N_HEADS
    packed = jnp.zeros_like(c)
    for r, piece in enumerate(_split3(c)):
        p32 = jnp.where(valid, piece.astype(F32), 0.0)
        packed = packed + (pltpu.roll(p32, r * N_HEADS, 1) if r else p32)
    return jnp.dot(packed.astype(BF16), route, preferred_element_type=F32)


def _store_token_major(dst_ref, rows, data, extras):
    n = data.shape[0]
    lane = lax.broadcasted_iota(jnp.int32, (n, HEAD_BLOCK), 1)
    for h in range(N_HEADS):
        pair = data[:, (h // 2) * HEAD_BLOCK:(h // 2 + 1) * HEAD_BLOCK]
        ex = extras[:, h * HEAD_BLOCK:(h + 1) * HEAD_BLOCK]
        keep = (lane < HEAD_DIM) if h % 2 == 0 else (lane >= HEAD_DIM)
        dst_ref[rows, h * HEAD_BLOCK:(h + 1) * HEAD_BLOCK] = jnp.where(keep, pair, ex).astype(BF16)


def _extra_rows(pieces, h, n, key_side):
    r = lax.broadcasted_iota(jnp.int32, (EXTRA_ROWS, n), 0)
    value_row, sign = (1, -1.0) if key_side else (0, 1.0)
    out = jnp.where((r < 2 * N_SPLIT) & ((r & 1) != value_row), 1.0, 0.0)
    for i, piece in enumerate(pieces):
        out = jnp.where(r == 2 * i + value_row,
                        jnp.broadcast_to(sign * piece[h:h + 1, :], (EXTRA_ROWS, n)), out)
    return out.astype(BF16)


def _store_feature_major(dst_ref, h, data, extra):
    if h % 2 == 0:
        dst_ref[h, 0:HEAD_DIM, :] = data
        dst_ref[h, HEAD_DIM:HEAD_DIM + EXTRA_ROWS, :] = extra
    else:
        dst_ref[h, 0:EXTRA_ROWS, :] = extra
        dst_ref[h, HEAD_DIM:2 * HEAD_DIM, :] = data


def _std_columns(w_lo, w_hi, col, width):
    split = w_lo.shape[1]
    assert col + width <= split or col >= split
    return w_lo[:, col:col + width] if col < split else w_hi[:, col - split:col - split + width]


def _conv_branch(proj, u_sc, conv_w, conv_ref, w_pa, n):
    first = SUBLANES - (CONV_W - 1)
    u_sc[SUBLANES:SUBLANES + n, :] = proj(_S_GC, CONV_DIM) * proj(_S_XA, CONV_DIM)
    conv = sum(u_sc[first + i:first + i + n, :] * conv_w[i:i + 1, :] for i in range(CONV_W))
    new_conv = u_sc[first + n:SUBLANES + n, :]
    conv_ref[...] = new_conv
    u_sc[first:SUBLANES, :] = new_conv
    y_a = proj(_S_GB, CONV_DIM) * conv * jax.nn.silu(proj(_S_ZA, CONV_DIM))
    return jax.nn.sigmoid(proj(_S_GA, D_MODEL)) * jnp.dot(
        y_a.astype(BF16), w_pa[...], preferred_element_type=F32)


def _merge_and_norm(att, mix_a, h, silu_zb, gate_b, w_pb, w_out, ln_g, ln_b):
    y_b = att * silu_zb
    mixed = mix_a + gate_b * jnp.dot(
        y_b.astype(BF16), w_pb[...], preferred_element_type=F32)
    sub = jnp.dot(mixed.astype(BF16), w_out[...], preferred_element_type=F32)
    return _layer_norm(ALPHA * h + sub, ln_g[...], ln_b[...])


def _head_max_sq_norm(xt):
    n = xt.shape[1]
    row = lax.broadcasted_iota(jnp.int32, (F_ROWS, n), 0)
    sq = xt * xt
    acc = jnp.zeros((F_ROWS, n), F32)
    for h in range(N_HEADS):
        norm = jnp.sum(sq[h * HEAD_DIM:(h + 1) * HEAD_DIM, :], axis=0, keepdims=True)
        acc = jnp.where(row == h, jnp.broadcast_to(norm, (F_ROWS, n)), acc)
    return jnp.max(acc, axis=1, keepdims=True)


def _dead_key_tiles(j, qt2, kt, c2_end, kn_sc, ce_sc):
    lane = lax.broadcasted_iota(jnp.int32, (F_ROWS, LANES), 1)
    row = lax.broadcasted_iota(jnp.int32, (F_ROWS, LANES), 0)
    spread = lambda col: jnp.broadcast_to(col, (F_ROWS, LANES))
    kn = jnp.where(lane == j, spread(_head_max_sq_norm(kt)), jnp.where(lane < j, kn_sc[...], 0.0))
    ce = jnp.where(lane == j, spread(c2_end), jnp.where(lane < j, ce_sc[...], 0.0))
    kn_sc[...] = kn
    ce_sc[...] = ce
    k_max = jnp.max(kn, axis=1, keepdims=True)
    ce_prev = jnp.sum(jnp.where(lane == j - 1, ce, 0.0), axis=1, keepdims=True)
    bound = NORM_SLACK * 2.0 * jnp.sqrt(_head_max_sq_norm(qt2) * k_max) + ce_prev - ce
    worst = jnp.max(jnp.where(row < N_HEADS, bound, -jnp.inf), axis=0, keepdims=True)
    dead = (worst < -UNDERFLOW_LOG2) & (lane[0:1, :] <= j - 2)
    return jnp.min(jnp.where(dead, LANES, lane[0:1, :]))


def _attention_t(n_key_tiles, first, tm, kaug_sc, qta_sc, vt_sc, attt_sc):
    keys = pl.ds(pl.multiple_of(first * tm, tm), n_key_tiles * tm)
    causal = (lax.broadcasted_iota(jnp.int32, (tm, tm), 0)
              <= lax.broadcasted_iota(jnp.int32, (tm, tm), 1))

    def scores(hd):
        return jnp.dot(kaug_sc[keys, hd * HEAD_BLOCK:(hd + 1) * HEAD_BLOCK], qta_sc[hd],
                       preferred_element_type=F32)

    pending = [scores(hd) for hd in range(SCORE_LOOKAHEAD)]
    for hd in range(N_HEADS):
        s = pending.pop(0)
        if hd + SCORE_LOOKAHEAD < N_HEADS:
            pending.append(scores(hd + SCORE_LOOKAHEAD))
        blocks = [s[i * tm:(i + 1) * tm, :] for i in range(n_key_tiles)]
        blocks[-1] = jnp.where(causal, blocks[-1], -jnp.inf)
        m = functools.reduce(jnp.maximum, [jnp.max(blk, axis=0, keepdims=True) for blk in blocks])
        o = jnp.zeros((V_ROWS, tm), F32)
        for i, blk in enumerate(blocks):
            o = o + jnp.dot(vt_sc[first + i, hd * V_ROWS:(hd + 1) * V_ROWS, :],
                            jnp.exp2(blk - m).astype(BF16), preferred_element_type=F32)
        attt_sc[hd * HEAD_DIM:(hd + 1) * HEAD_DIM, :] = o[0:HEAD_DIM, :] / o[HEAD_DIM:HEAD_DIM + 1, :]


def _prompt_kernel(x_ref, ln_in_g, ln_in_b, w_lo, w_hi, w_t, b_f_col, conv_w, w_pa, w_pb, w_out,
                   ln_g, ln_b, route, tri,
                   y_ref, kt_ref, vt_ref, lft_ref, conv_ref,
                   kaug_sc, vt_sc, h_sc, hb_sc, z_sc, qta_sc, attt_sc, zb_sc, gm_sc, u_sc, carryt_sc,
                   kn_sc, ce_sc, first_sm, *, tm, n_tiles, n_total):
    t = pl.program_id(0)

    def norm_in():
        h = _layer_norm(x_ref[...], ln_in_g[...], ln_in_b[...])
        h_sc[lax.rem(t, PIPE_SLOTS)] = h
        hb_sc[lax.rem(t, PIPE_SLOTS)] = h.astype(BF16)

    def norm_out():
        y_ref[...] = _layer_norm(z_sc[...], ln_g[...], ln_b[...])

    @pl.when(t == 0)
    def _fill():
        for ref in (z_sc, kaug_sc, vt_sc, h_sc, hb_sc, attt_sc, zb_sc, gm_sc, u_sc, qta_sc, carryt_sc,
                    kn_sc, ce_sc):
            ref[...] = jnp.zeros_like(ref)
        first_sm[0] = 0
        ones_row = jnp.where(lax.broadcasted_iota(jnp.int32, (V_ROWS - HEAD_DIM, tm), 0) == 0, 1.0, 0.0)
        for i in range(n_tiles):
            for hd in range(N_HEADS):
                vt_sc[i, hd * V_ROWS + HEAD_DIM:(hd + 1) * V_ROWS, :] = ones_row.astype(BF16)
        norm_in()

    @pl.when((t >= 1) & (t <= n_total + 1))
    def _mixers():
        tile_a = t - 1
        tile_b = t - 2
        ja = lax.rem(tile_a, n_tiles)
        jb = lax.rem(tile_b, n_tiles)
        slot_a = lax.rem(tile_a, PIPE_SLOTS)
        slot_b = lax.rem(t + 1, PIPE_SLOTS)
        gate_a = tile_a & 1
        gate_b = t & 1

        @pl.when(ja == 0)
        def _start_of_sequence():
            carryt_sc[...] = jnp.zeros_like(carryt_sc)

        @pl.when(jb == 0)
        def _start_of_conv():
            u_sc[0:SUBLANES, :] = jnp.zeros((SUBLANES, CONV_DIM), F32)

        first = first_sm[0]
        for n_key_tiles in range(1, n_tiles + 1):
            pl.when((t >= 2) & (jb - first + 1 == n_key_tiles))(functools.partial(
                _attention_t, n_key_tiles, first, tm, kaug_sc, qta_sc, vt_sc, attt_sc))

        def proj(slot, col, width):
            return jnp.dot(hb_sc[slot], _std_columns(w_lo, w_hi, col, width), preferred_element_type=F32)

        def proj_t(row, height):
            return lax.dot_general(w_t[row:row + height, :], hb_sc[slot_a], (((1,), (1,)), ((), ())),
                                   preferred_element_type=F32)

        norm_out()

        vf_t = proj_t(_T_V, ATT_DIM + F_ROWS)
        qk_t = proj_t(_T_Q, 2 * ATT_DIM)
        kt = qk_t[ATT_DIM:2 * ATT_DIM, :]
        vt = vf_t[0:ATT_DIM, :]
        kt_ref[...] = kt
        vt_ref[...] = vt
        logf_t = _log_sigmoid(vf_t[ATT_DIM:ATT_DIM + F_ROWS, :] + b_f_col[...])
        lft_ref[...] = logf_t[:N_HEADS, :]
        sums = jnp.dot(jnp.concatenate(_split3(logf_t), axis=0), tri[...], preferred_element_type=F32)

        mix_a = _conv_branch(functools.partial(proj, slot_b), u_sc, conv_w, conv_ref, w_pa, tm)

        y_b = attt_sc[...].T * zb_sc[gate_b]
        mixed = mix_a + gm_sc[gate_b] * jnp.dot(y_b.astype(BF16), w_pb[...], preferred_element_type=F32)
        z_sc[...] = ALPHA * h_sc[slot_b] + jnp.dot(mixed.astype(BF16), w_out[...], preferred_element_type=F32)

        zb_sc[gate_a] = jax.nn.silu(proj(slot_a, _S_ZB, ATT_DIM))
        gm_sc[gate_a] = jax.nn.sigmoid(proj(slot_a, _S_GM, D_MODEL))

        c_t = sum(sums[r * F_ROWS:(r + 1) * F_ROWS, :] for r in range(N_SPLIT)) + carryt_sc[:, 0:1]
        carryt_sc[...] = jnp.broadcast_to(c_t[:, tm - 1:tm], carryt_sc.shape)
        c_pieces = [piece.astype(F32) for piece in _split3(c_t * LOG2E)]
        qt = qk_t[0:ATT_DIM, :] * (SCALE * LOG2E)
        for hd in range(N_HEADS):
            _store_feature_major(qta_sc, hd, qt[hd * HEAD_DIM:(hd + 1) * HEAD_DIM, :].astype(BF16),
                                 _extra_rows(c_pieces, hd, tm, key_side=False))
            vt_sc[ja, hd * V_ROWS:hd * V_ROWS + HEAD_DIM, :] = (
                vt[hd * HEAD_DIM:(hd + 1) * HEAD_DIM, :].astype(BF16))
        ones_and_pad = jnp.where(lax.broadcasted_iota(jnp.int32, (LANES - ONES_LANE, tm), 0) == 0, 1.0, 0.0)
        packed_t = jnp.concatenate([piece[0:N_HEADS, :] for piece in c_pieces] + [ones_and_pad], axis=0)
        ex_k = jnp.dot(packed_t.T.astype(BF16), route[:, AUG_DIM:], preferred_element_type=F32)
        _store_token_major(kaug_sc, pl.ds(pl.multiple_of(ja * tm, tm), tm), kt.T, ex_k)
        first_sm[0] = _dead_key_tiles(ja, qt, kt, c_t[:, tm - 1:tm] * LOG2E, kn_sc, ce_sc)

        norm_in()

    @pl.when(t == n_total + 2)
    def _drain():
        norm_out()


def _resident(shape):
    return pl.BlockSpec(shape, lambda *_: (0,) * len(shape), pipeline_mode=pl.Buffered(1))


def _run_prompt(x, params, *, tm):
    bsz, t, _ = x.shape
    n_tiles = t // tm
    assert n_tiles * tm == t
    n_total = bsz * n_tiles
    tile = lambda s, lag: jnp.clip(s - lag, 0, n_total - 1)
    per_tile = lambda width, lag: pl.BlockSpec(
        (None, tm, width), lambda s: (tile(s, lag) // n_tiles, tile(s, lag) % n_tiles, 0))
    per_tile_t = lambda rows: pl.BlockSpec(
        (None, rows, tm), lambda s: (tile(s, 1) // n_tiles, 0, tile(s, 1) % n_tiles))
    out_shape = (jax.ShapeDtypeStruct((bsz, t, D_MODEL), F32),
                 jax.ShapeDtypeStruct((bsz, ATT_DIM, t), F32),
                 jax.ShapeDtypeStruct((bsz, ATT_DIM, t), F32),
                 jax.ShapeDtypeStruct((bsz, N_HEADS, t), F32),
                 jax.ShapeDtypeStruct((bsz, CONV_W - 1, CONV_DIM), F32))
    out_specs = (per_tile(D_MODEL, 3), per_tile_t(ATT_DIM), per_tile_t(ATT_DIM), per_tile_t(N_HEADS),
                 pl.BlockSpec((None, CONV_W - 1, CONV_DIM), lambda s: (tile(s, 2) // n_tiles, 0, 0)))
    scratch = [
        pltpu.VMEM((t, AUG_DIM), BF16),
        pltpu.VMEM((n_tiles, N_HEADS * V_ROWS, tm), BF16),
        pltpu.VMEM((PIPE_SLOTS, tm, D_MODEL), F32),
        pltpu.VMEM((PIPE_SLOTS, tm, D_MODEL), BF16),
        pltpu.VMEM((tm, D_MODEL), F32),
        pltpu.VMEM((N_HEADS, HEAD_BLOCK, tm), BF16),
        pltpu.VMEM((ATT_DIM, tm), F32),
        pltpu.VMEM((2, tm, ATT_DIM), F32),
        pltpu.VMEM((2, tm, D_MODEL), F32),
        pltpu.VMEM((SUBLANES + tm, CONV_DIM), F32),
        pltpu.VMEM((F_ROWS, LANES), F32),
        pltpu.VMEM((F_ROWS, LANES), F32),
        pltpu.VMEM((F_ROWS, LANES), F32),
        pltpu.SMEM((1,), jnp.int32),
    ]
    return pl.pallas_call(
        functools.partial(_prompt_kernel, tm=tm, n_tiles=n_tiles, n_total=n_total),
        grid=(n_total + PIPE_SLOTS,),
        in_specs=[per_tile(D_MODEL, 0)] + [_resident(p.shape) for p in params],
        out_specs=out_specs,
        out_shape=out_shape,
        scratch_shapes=scratch,
        compiler_params=pltpu.CompilerParams(
            dimension_semantics=("arbitrary",),
            vmem_limit_bytes=VMEM_LIMIT_BYTES),
        name="trunk_prompt",
    )(x, *params)


def _sample_kernel(x_ref, conv0_ref, pkt_ref, pvt_ref, plft_ref,
                   ln_in_g, ln_in_b, w_lo, w_hi, w_fs, b_f_row, w_t, b_f_col, conv_w, w_pa, w_pb, w_out,
                   ln_g, ln_b, route, ones,
                   y_ref, kt_ref, vt_ref, lft_ref, conv_ref,
                   kta_sc, ktn_sc, hb_sc, qa_sc, att_sc, u_sc, h_sc, mixa_sc, cpast_sc, vtn_sc,
                   *, ts, n_past, group):
    stream = lax.rem(pl.program_id(0), group)
    first = SUBLANES - (CONV_W - 1)

    @pl.when(pl.program_id(0) == 0)
    def _zero_padding():
        kta_sc[...] = jnp.zeros_like(kta_sc)
        ktn_sc[...] = jnp.zeros_like(ktn_sc)

    def proj(col, width):
        return jnp.dot(hb_sc[...], _std_columns(w_lo, w_hi, col, width), preferred_element_type=F32)

    @pl.when(stream == 0)
    def _group_projections():
        h = _layer_norm(jnp.concatenate([x_ref[s] for s in range(group)], axis=0), ln_in_g[...], ln_in_b[...])
        h_sc[...] = h
        hb_sc[...] = h.astype(BF16)

        u = proj(_S_GC, CONV_DIM) * proj(_S_XA, CONV_DIM)
        convs = []
        for s in range(group):
            u_sc[s, first:SUBLANES, :] = conv0_ref[s]
            u_sc[s, SUBLANES:SUBLANES + ts, :] = u[s * ts:(s + 1) * ts, :]
            convs.append(sum(u_sc[s, first + i:first + i + ts, :] * conv_w[i:i + 1, :] for i in range(CONV_W)))
            conv_ref[s] = u_sc[s, first + ts:SUBLANES + ts, :]
        y_a = proj(_S_GB, CONV_DIM) * jnp.concatenate(convs, axis=0) * jax.nn.silu(proj(_S_ZA, CONV_DIM))
        mixa_sc[...] = jax.nn.sigmoid(proj(_S_GA, D_MODEL)) * jnp.dot(
            y_a.astype(BF16), w_pa[...], preferred_element_type=F32)

        def proj_t(row, height):
            return lax.dot_general(w_t[row:row + height, :], hb_sc[...], (((1,), (1,)), ((), ())),
                                   preferred_element_type=F32)

        kt = proj_t(_T_K, ATT_DIM)
        vf_t = proj_t(_T_V, ATT_DIM + F_ROWS)
        vt = vf_t[0:ATT_DIM, :]
        vtn_sc[...] = vt.astype(BF16)
        logf_t = _log_sigmoid(vf_t[ATT_DIM:ATT_DIM + SUBLANES, :] + b_f_col[0:SUBLANES, :])

        eye = (lax.broadcasted_iota(jnp.int32, (SUBLANES, LANES), 0)
               == lax.broadcasted_iota(jnp.int32, (SUBLANES, LANES), 1))
        lane_stream = lax.broadcasted_iota(jnp.int32, (SUBLANES, LANES), 1) // ts
        c_last_lanes = jnp.zeros((SUBLANES, LANES), F32)
        c_last_rows = []
        for s in range(group):
            kt_ref[s] = kt[:, s * ts:(s + 1) * ts]
            vt_ref[s] = vt[:, s * ts:(s + 1) * ts]
            lft_ref[s] = logf_t[:, s * ts:(s + 1) * ts]
            c_past = _cumsum(plft_ref[s], 1)
            cpast_sc[s] = c_past
            c_last = jnp.broadcast_to(c_past[:, n_past - 1:n_past], (SUBLANES, LANES))
            c_last_lanes = jnp.where(lane_stream == s, c_last, c_last_lanes)
            c_last_rows.append(jnp.broadcast_to(
                jnp.sum(jnp.where(eye, c_last, 0.0), axis=0, keepdims=True), (ts, LANES)))
        c_new_t = _cumsum(logf_t, 1, segment=ts) + c_last_lanes
        new_pieces = [piece.astype(F32) for piece in _split3(c_new_t)]
        for hd in range(N_HEADS):
            rows = slice(hd * HEAD_DIM, (hd + 1) * HEAD_DIM)
            _store_feature_major(ktn_sc, hd, kt[rows, :].astype(BF16),
                                 _extra_rows(new_pieces, hd, LANES, key_side=True))

        logf = _log_sigmoid(jnp.dot(hb_sc[...], w_fs[...], preferred_element_type=F32) + b_f_row[...])
        c_q = _cumsum(logf, 0, segment=ts) + jnp.concatenate(c_last_rows, axis=0)
        ex_q = _extra_lanes(c_q, route[:, :AUG_DIM]) + ones[:, :AUG_DIM]
        _store_token_major(qa_sc, slice(None), proj(_S_Q, ATT_DIM) * SCALE, ex_q)

    past_pieces = [piece.astype(F32) for piece in _split3(cpast_sc[stream])]
    for hd in range(N_HEADS):
        rows = slice(hd * HEAD_DIM, (hd + 1) * HEAD_DIM)
        _store_feature_major(kta_sc, hd, pkt_ref[rows, :].astype(BF16),
                             _extra_rows(past_pieces, hd, n_past, key_side=True))

    lane_key = lax.broadcasted_iota(jnp.int32, (ts, LANES), 1) - stream * ts
    causal = (lane_key >= 0) & (lane_key <= lax.broadcasted_iota(jnp.int32, (ts, LANES), 0))
    lane = lax.broadcasted_iota(jnp.int32, (ts, HEAD_BLOCK), 1)
    contract_lanes = (((1,), (1,)), ((), ()))
    stream_rows = pl.ds(pl.multiple_of(stream * ts, ts), ts)
    scores = []
    for hd in range(N_HEADS):
        qa_h = qa_sc[stream_rows, hd * HEAD_BLOCK:(hd + 1) * HEAD_BLOCK]
        scores.append((jnp.dot(qa_h, kta_sc[hd], preferred_element_type=F32),
                       jnp.dot(qa_h, ktn_sc[hd], preferred_element_type=F32)))
    for pair in range(N_HEADS // 2):
        rows = slice(pair * HEAD_BLOCK, (pair + 1) * HEAD_BLOCK)
        vt_past = pvt_ref[rows, :].astype(BF16)
        vt_new = vtn_sc[rows, :]
        probs, sums = [], []
        for hd in (2 * pair, 2 * pair + 1):
            s_past = scores[hd][0]
            s_new = jnp.where(causal, scores[hd][1], -jnp.inf)
            m = jnp.maximum(jnp.max(s_past, axis=-1, keepdims=True), jnp.max(s_new, axis=-1, keepdims=True))
            p_past = jnp.exp(s_past - m)
            p_new = jnp.exp(s_new - m)
            sums.append(jnp.sum(p_past, axis=-1, keepdims=True) + jnp.sum(p_new, axis=-1, keepdims=True))
            probs.append((p_past.astype(BF16), p_new.astype(BF16)))
        o = (lax.dot_general(jnp.concatenate([probs[0][0], probs[1][0]], axis=0), vt_past, contract_lanes,
                             preferred_element_type=F32)
             + lax.dot_general(jnp.concatenate([probs[0][1], probs[1][1]], axis=0), vt_new, contract_lanes,
                               preferred_element_type=F32))
        att_sc[stream_rows, rows] = jnp.where(lane < HEAD_DIM, o[0:ts, :] / sums[0], o[ts:2 * ts, :] / sums[1])

    @pl.when(stream == group - 1)
    def _group_merge():
        y = _merge_and_norm(att_sc[...], mixa_sc[...], h_sc[...], jax.nn.silu(proj(_S_ZB, ATT_DIM)),
                            jax.nn.sigmoid(proj(_S_GM, D_MODEL)), w_pb, w_out, ln_g, ln_b)
        for s in range(group):
            y_ref[s] = y[s * ts:(s + 1) * ts, :]


def _run_sample(x, conv0, pkt, pvt, plft, params):
    bsz, ts, _ = x.shape
    n_past = pkt.shape[2]
    assert LANES % ts == 0 and ts % (2 * SUBLANES) == 0
    group = LANES // ts
    assert bsz % group == 0
    per_batch = lambda rows, width: pl.BlockSpec((None, rows, width), lambda b: (b, 0, 0))
    per_group = lambda rows, width: pl.BlockSpec((group, rows, width), lambda b: (b // group, 0, 0))
    out_shape = (jax.ShapeDtypeStruct((bsz, ts, D_MODEL), F32),
                 jax.ShapeDtypeStruct((bsz, ATT_DIM, ts), F32),
                 jax.ShapeDtypeStruct((bsz, ATT_DIM, ts), F32),
                 jax.ShapeDtypeStruct((bsz, N_HEADS, ts), F32),
                 jax.ShapeDtypeStruct((bsz, CONV_W - 1, CONV_DIM), F32))
    out_specs = (per_group(ts, D_MODEL), per_group(ATT_DIM, ts), per_group(ATT_DIM, ts),
                 per_group(N_HEADS, ts), per_group(CONV_W - 1, CONV_DIM))
    in_specs = [per_group(ts, D_MODEL), per_group(CONV_W - 1, CONV_DIM), per_batch(ATT_DIM, n_past),
                per_batch(ATT_DIM, n_past), per_group(N_HEADS, n_past)]
    scratch = [
        pltpu.VMEM((N_HEADS, HEAD_BLOCK, n_past), BF16),
        pltpu.VMEM((N_HEADS, HEAD_BLOCK, LANES), BF16),
        pltpu.VMEM((LANES, D_MODEL), BF16),
        pltpu.VMEM((LANES, AUG_DIM), BF16),
        pltpu.VMEM((LANES, ATT_DIM), F32),
        pltpu.VMEM((group, SUBLANES + ts, CONV_DIM), F32),
        pltpu.VMEM((LANES, D_MODEL), F32),
        pltpu.VMEM((LANES, D_MODEL), F32),
        pltpu.VMEM((group, N_HEADS, n_past), F32),
        pltpu.VMEM((ATT_DIM, LANES), BF16),
    ]
    return pl.pallas_call(
        functools.partial(_sample_kernel, ts=ts, n_past=n_past, group=group),
        grid=(bsz,),
        in_specs=in_specs + [_resident(p.shape) for p in params],
        out_specs=out_specs,
        out_shape=out_shape,
        scratch_shapes=scratch,
        compiler_params=pltpu.CompilerParams(
            dimension_semantics=("arbitrary",),
            vmem_limit_bytes=VMEM_LIMIT_BYTES),
        name="trunk_sample",
    )(x, conv0, pkt, pvt, plft, *params)


WEIGHT_BLOCK = 512


def _relayout_kernel(a_ref, b_ref, lo_ref, hi_ref, *, n_lo, skip):
    step = pl.program_id(0)

    @pl.when(step < n_lo)
    def _before_gap():
        lo_ref[...] = a_ref[...].T.astype(BF16)

    @pl.when(step >= n_lo)
    def _after_gap():
        rows = jnp.concatenate([a_ref[skip:, :], b_ref[...]], axis=0)
        hi_ref[...] = rows.T.astype(BF16)


def _relayout_weights(w_rows):
    n_features = w_rows.shape[0]
    gap = _W_ZB - _W_V
    skip = gap % WEIGHT_BLOCK
    shift = gap // WEIGHT_BLOCK
    assert _W_V % WEIGHT_BLOCK == 0 and (n_features - _W_ZB) % WEIGHT_BLOCK == 0
    assert skip > 0 and skip % SUBLANES == 0 and WEIGHT_BLOCK % skip == 0
    n_lo = _W_V // WEIGHT_BLOCK
    n_hi = (n_features - _W_ZB) // WEIGHT_BLOCK
    a_index = lambda j: (jnp.where(j < n_lo, j, j + shift), 0)
    b_index = lambda j: ((j + shift + 1) * (WEIGHT_BLOCK // skip), 0)
    return pl.pallas_call(
        functools.partial(_relayout_kernel, n_lo=n_lo, skip=skip),
        grid=(n_lo + n_hi,),
        in_specs=[pl.BlockSpec((WEIGHT_BLOCK, D_MODEL), a_index), pl.BlockSpec((skip, D_MODEL), b_index)],
        out_specs=(pl.BlockSpec((D_MODEL, WEIGHT_BLOCK), lambda j: (0, jnp.minimum(j, n_lo - 1))),
                   pl.BlockSpec((D_MODEL, WEIGHT_BLOCK), lambda j: (0, jnp.maximum(j - n_lo, 0)))),
        out_shape=(jax.ShapeDtypeStruct((D_MODEL, _W_V), BF16),
                   jax.ShapeDtypeStruct((D_MODEL, n_features - _W_ZB), BF16)),
        compiler_params=pltpu.CompilerParams(dimension_semantics=("arbitrary",)),
        name="weight_relayout",
    )(w_rows, w_rows)


def kernel(x_prompt, x_sample, cache_k, cache_v, cache_logf, state_conv, ln_in_g, ln_in_b, w_in, b_f,
           conv_w, w_proj_a, w_proj_b, w_out, ln_g, ln_b):
    assert w_in.shape[0] == DEPTH
    route_np, ones_np = _routing_constants()
    w = w_in[0]
    row = lambda a: a.reshape(1, -1).astype(F32)
    w_t = jnp.concatenate([w[:, _W_Q:_W_F + N_HEADS],
                           jnp.zeros((D_MODEL, F_ROWS - N_HEADS), w.dtype)], axis=1).T.astype(BF16)
    tm = PROMPT_TILE
    w_lo, w_hi = _relayout_weights(w.T)
    p = dict(
        ln_in_g=row(ln_in_g), ln_in_b=row(ln_in_b),
        w_lo=w_lo, w_hi=w_hi,
        w_fs=jnp.pad(w[:, _W_F:_W_F + N_HEADS], ((0, 0), (0, LANES - N_HEADS))).astype(BF16),
        b_f_row=jnp.pad(row(b_f[0]), ((0, 0), (0, LANES - N_HEADS))),
        w_t=w_t,
        b_f_col=jnp.pad(b_f[0].astype(F32).reshape(N_HEADS, 1), ((0, F_ROWS - N_HEADS), (0, 0))),
        conv_w=conv_w[0].astype(F32),
        w_pa=w_proj_a[0].astype(BF16), w_pb=w_proj_b[0].astype(BF16), w_out=w_out[0].astype(BF16),
        ln_g=row(ln_g[0]), ln_b=row(ln_b[0]),
        route=jnp.asarray(route_np, BF16), ones=jnp.asarray(ones_np, F32),
        tri=jnp.asarray(np.triu(np.ones((tm, tm), np.float32)), BF16),
    )
    bs = x_sample.shape[0]
    n_past = cache_k.shape[2]
    feature_major = lambda a: a.transpose(0, 2, 3, 1).reshape(bs, ATT_DIM, n_past)

    y_p, kt_p, vt_p, lft_p, cv_p = _run_prompt(x_prompt, [p[name] for name in PROMPT_PARAMS], tm=tm)
    y_s, kt_s, vt_s, lft_s, cv_s = _run_sample(
        x_sample, state_conv[0], feature_major(cache_k[0]), feature_major(cache_v[0]),
        cache_logf[0].transpose(0, 2, 1), [p[name] for name in SAMPLE_PARAMS])

    heads_t = lambda a: a.reshape(a.shape[0], N_HEADS, HEAD_DIM, a.shape[2]).transpose(0, 3, 1, 2)[None]
    tokens_t = lambda a: a.transpose(0, 2, 1)[None]
    return (y_p, y_s, heads_t(kt_p), heads_t(vt_p), tokens_t(lft_p), cv_p[None],
            heads_t(kt_s), heads_t(vt_s), tokens_t(lft_s), cv_s[None])
```

```python
import functools

import numpy as np
import jax
import jax.numpy as jnp
from jax import lax
from jax.experimental import pallas as pl
from jax.experimental.pallas import tpu as pltpu

D_MODEL = 1024
CONV_DIM = 512
CONV_W = 3
N_HEADS = 8
HEAD_DIM = 64
ATT_DIM = N_HEADS * HEAD_DIM
LN_EPS = 1e-5
DEPTH = 1
ALPHA = (2.0 * DEPTH) ** 0.25
SCALE = HEAD_DIM ** -0.5

LANES = 128
SUBLANES = 8
HEAD_BLOCK = LANES
AUG_DIM = N_HEADS * HEAD_BLOCK
N_SPLIT = 3
ONES_LANE = N_SPLIT * N_HEADS
EXTRA_ROWS = 16
F_ROWS = 16
VMEM_LIMIT_BYTES = 56 * 1024 * 1024
V_ROWS = HEAD_DIM + 16
LOG2E = 1.4426950408889634
UNDERFLOW_LOG2 = 156.0
NORM_SLACK = 1.02
PIPE_SLOTS = 3
SCORE_LOOKAHEAD = 4

_W_Q = 4 * CONV_DIM
_W_V = _W_Q + 2 * ATT_DIM
_W_F = _W_Q + 3 * ATT_DIM
_W_ZB = _W_F + N_HEADS
_S_XA, _S_GB, _S_GC, _S_ZA, _S_Q, _S_K, _S_ZB, _S_GA, _S_GM = (
    0, 512, 1024, 1536, 2048, 2560, 3072, 3584, 4608)
_T_Q, _T_K, _T_V, _T_F = 0, 512, 1024, 1536

PROMPT_TILE = 256
PROMPT_PARAMS = ("ln_in_g", "ln_in_b", "w_lo", "w_hi", "w_t", "b_f_col", "conv_w", "w_pa", "w_pb", "w_out",
                 "ln_g", "ln_b", "route", "tri")
SAMPLE_PARAMS = ("ln_in_g", "ln_in_b", "w_lo", "w_hi", "w_fs", "b_f_row", "w_t", "b_f_col", "conv_w", "w_pa",
                 "w_pb", "w_out", "ln_g", "ln_b", "route", "ones")

F32 = jnp.float32
BF16 = jnp.bfloat16


def _routing_constants():
    route = np.zeros((LANES, 2 * AUG_DIM), np.float32)
    ones = np.zeros((1, 2 * AUG_DIM), np.float32)
    for h in range(N_HEADS):
        base = h * HEAD_BLOCK + (HEAD_DIM if h % 2 == 0 else 0)
        for r in range(N_SPLIT):
            route[r * N_HEADS + h, base + 2 * r] = 1.0
            ones[0, base + 2 * r + 1] = 1.0
            ones[0, AUG_DIM + base + 2 * r] = 1.0
            route[r * N_HEADS + h, AUG_DIM + base + 2 * r + 1] = -1.0
    route[ONES_LANE, :] = ones[0, :]
    return route, ones


def _layer_norm(x, g, b):
    mu = jnp.mean(x, axis=-1, keepdims=True)
    xc = x - mu
    var = jnp.mean(xc * xc, axis=-1, keepdims=True)
    return xc * lax.rsqrt(var + LN_EPS) * g + b


def _log_sigmoid(x):
    return jnp.minimum(x, 0.0) - jnp.log1p(jnp.exp(-jnp.abs(x)))


def _split3(x):
    hi = x.astype(BF16)
    r1 = x - hi.astype(F32)
    mid = r1.astype(BF16)
    lo = (r1 - mid.astype(F32)).astype(BF16)
    return hi, mid, lo


def _cumsum(x, axis, segment=None):
    n = segment or x.shape[axis]
    idx = lax.broadcasted_iota(jnp.int32, x.shape, axis)
    if segment:
        idx = lax.rem(idx, segment)
    s = 1
    while s < n:
        x = x + jnp.where(idx >= s, pltpu.roll(x, s, axis), 0.0)
        s *= 2
    return x


def _extra_lanes(c, route):
    lane = lax.broadcasted_iota(jnp.int32, c.shape, 1)
    valid = lane < N_HEADS
    packed = jnp.zeros_like(c)
    for r, piece in enumerate(_split3(c)):
        p32 = jnp.where(valid, piece.astype(F32), 0.0)
        packed = packed + (pltpu.roll(p32, r * N_HEADS, 1) if r else p32)
    return jnp.dot(packed.astype(BF16), route, preferred_element_type=F32)


def _store_token_major(dst_ref, rows, data, extras):
    n = data.shape[0]
    lane = lax.broadcasted_iota(jnp.int32, (n, HEAD_BLOCK), 1)
    for h in range(N_HEADS):
        pair = data[:, (h // 2) * HEAD_BLOCK:(h // 2 + 1) * HEAD_BLOCK]
        ex = extras[:, h * HEAD_BLOCK:(h + 1) * HEAD_BLOCK]
        keep = (lane < HEAD_DIM) if h % 2 == 0 else (lane >= HEAD_DIM)
        dst_ref[rows, h * HEAD_BLOCK:(h + 1) * HEAD_BLOCK] = jnp.where(keep, pair, ex).astype(BF16)


def _extra_rows(pieces, h, n, key_side):
    r = lax.broadcasted_iota(jnp.int32, (EXTRA_ROWS, n), 0)
    value_row, sign = (1, -1.0) if key_side else (0, 1.0)
    out = jnp.where((r < 2 * N_SPLIT) & ((r & 1) != value_row), 1.0, 0.0)
    for i, piece in enumerate(pieces):
        out = jnp.where(r == 2 * i + value_row,
                        jnp.broadcast_to(sign * piece[h:h + 1, :], (EXTRA_ROWS, n)), out)
    return out.astype(BF16)


def _store_feature_major(dst_ref, h, data, extra):
    if h % 2 == 0:
        dst_ref[h, 0:HEAD_DIM, :] = data
        dst_ref[h, HEAD_DIM:HEAD_DIM + EXTRA_ROWS, :] = extra
    else:
        dst_ref[h, 0:EXTRA_ROWS, :] = extra
        dst_ref[h, HEAD_DIM:2 * HEAD_DIM, :] = data


def _std_columns(w_lo, w_hi, col, width):
    split = w_lo.shape[1]
    assert col + width <= split or col >= split
    return w_lo[:, col:col + width] if col < split else w_hi[:, col - split:col - split + width]


def _conv_branch(proj, u_sc, conv_w, conv_ref, w_pa, n):
    first = SUBLANES - (CONV_W - 1)
    u_sc[SUBLANES:SUBLANES + n, :] = proj(_S_GC, CONV_DIM) * proj(_S_XA, CONV_DIM)
    conv = sum(u_sc[first + i:first + i + n, :] * conv_w[i:i + 1, :] for i in range(CONV_W))
    new_conv = u_sc[first + n:SUBLANES + n, :]
    conv_ref[...] = new_conv
    u_sc[first:SUBLANES, :] = new_conv
    y_a = proj(_S_GB, CONV_DIM) * conv * jax.nn.silu(proj(_S_ZA, CONV_DIM))
    return jax.nn.sigmoid(proj(_S_GA, D_MODEL)) * jnp.dot(
        y_a.astype(BF16), w_pa[...], preferred_element_type=F32)


def _merge_and_norm(att, mix_a, h, silu_zb, gate_b, w_pb, w_out, ln_g, ln_b):
    y_b = att * silu_zb
    mixed = mix_a + gate_b * jnp.dot(
        y_b.astype(BF16), w_pb[...], preferred_element_type=F32)
    sub = jnp.dot(mixed.astype(BF16), w_out[...], preferred_element_type=F32)
    return _layer_norm(ALPHA * h + sub, ln_g[...], ln_b[...])


def _head_max_sq_norm(xt):
    n = xt.shape[1]
    row = lax.broadcasted_iota(jnp.int32, (F_ROWS, n), 0)
    sq = xt * xt
    acc = jnp.zeros((F_ROWS, n), F32)
    for h in range(N_HEADS):
        norm = jnp.sum(sq[h * HEAD_DIM:(h + 1) * HEAD_DIM, :], axis=0, keepdims=True)
        acc = jnp.where(row == h, jnp.broadcast_to(norm, (F_ROWS, n)), acc)
    return jnp.max(acc, axis=1, keepdims=True)


def _dead_key_tiles(j, qt2, kt, c2_end, kn_sc, ce_sc):
    lane = lax.broadcasted_iota(jnp.int32, (F_ROWS, LANES), 1)
    row = lax.broadcasted_iota(jnp.int32, (F_ROWS, LANES), 0)
    spread = lambda col: jnp.broadcast_to(col, (F_ROWS, LANES))
    kn = jnp.where(lane == j, spread(_head_max_sq_norm(kt)), jnp.where(lane < j, kn_sc[...], 0.0))
    ce = jnp.where(lane == j, spread(c2_end), jnp.where(lane < j, ce_sc[...], 0.0))
    kn_sc[...] = kn
    ce_sc[...] = ce
    k_max = jnp.max(kn, axis=1, keepdims=True)
    ce_prev = jnp.sum(jnp.where(lane == j - 1, ce, 0.0), axis=1, keepdims=True)
    bound = NORM_SLACK * 2.0 * jnp.sqrt(_head_max_sq_norm(qt2) * k_max) + ce_prev - ce
    worst = jnp.max(jnp.where(row < N_HEADS, bound, -jnp.inf), axis=0, keepdims=True)
    dead = (worst < -UNDERFLOW_LOG2) & (lane[0:1, :] <= j - 2)
    return jnp.min(jnp.where(dead, LANES, lane[0:1, :]))


def _attention_t(n_key_tiles, first, tm, kaug_sc, qta_sc, vt_sc, attt_sc):
    keys = pl.ds(pl.multiple_of(first * tm, tm), n_key_tiles * tm)
    causal = (lax.broadcasted_iota(jnp.int32, (tm, tm), 0)
              <= lax.broadcasted_iota(jnp.int32, (tm, tm), 1))

    def scores(hd):
        return jnp.dot(kaug_sc[keys, hd * HEAD_BLOCK:(hd + 1) * HEAD_BLOCK], qta_sc[hd],
                       preferred_element_type=F32)

    pending = [scores(hd) for hd in range(SCORE_LOOKAHEAD)]
    for hd in range(N_HEADS):
        s = pending.pop(0)
        if hd + SCORE_LOOKAHEAD < N_HEADS:
            pending.append(scores(hd + SCORE_LOOKAHEAD))
        blocks = [s[i * tm:(i + 1) * tm, :] for i in range(n_key_tiles)]
        blocks[-1] = jnp.where(causal, blocks[-1], -jnp.inf)
        m = functools.reduce(jnp.maximum, [jnp.max(blk, axis=0, keepdims=True) for blk in blocks])
        o = jnp.zeros((V_ROWS, tm), F32)
        for i, blk in enumerate(blocks):
            o = o + jnp.dot(vt_sc[first + i, hd * V_ROWS:(hd + 1) * V_ROWS, :],
                            jnp.exp2(blk - m).astype(BF16), preferred_element_type=F32)
        attt_sc[hd * HEAD_DIM:(hd + 1) * HEAD_DIM, :] = o[0:HEAD_DIM, :] / o[HEAD_DIM:HEAD_DIM + 1, :]


def _prompt_kernel(x_ref, ln_in_g, ln_in_b, w_lo, w_hi, w_t, b_f_col, conv_w, w_pa, w_pb, w_out,
                   ln_g, ln_b, route, tri,
                   y_ref, kt_ref, vt_ref, lft_ref, conv_ref,
                   kaug_sc, vt_sc, h_sc, hb_sc, z_sc, qta_sc, attt_sc, zb_sc, gm_sc, u_sc, carryt_sc,
                   kn_sc, ce_sc, first_sm, *, tm, n_tiles, n_total):
    t = pl.program_id(0)

    def norm_in():
        h = _layer_norm(x_ref[...], ln_in_g[...], ln_in_b[...])
        h_sc[lax.rem(t, PIPE_SLOTS)] = h
        hb_sc[lax.rem(t, PIPE_SLOTS)] = h.astype(BF16)

    def norm_out():
        y_ref[...] = _layer_norm(z_sc[...], ln_g[...], ln_b[...])

    @pl.when(t == 0)
    def _fill():
        for ref in (z_sc, kaug_sc, vt_sc, h_sc, hb_sc, attt_sc, zb_sc, gm_sc, u_sc, qta_sc, carryt_sc,
                    kn_sc, ce_sc):
            ref[...] = jnp.zeros_like(ref)
        first_sm[0] = 0
        ones_row = jnp.where(lax.broadcasted_iota(jnp.int32, (V_ROWS - HEAD_DIM, tm), 0) == 0, 1.0, 0.0)
        for i in range(n_tiles):
            for hd in range(N_HEADS):
                vt_sc[i, hd * V_ROWS + HEAD_DIM:(hd + 1) * V_ROWS, :] = ones_row.astype(BF16)
        norm_in()

    @pl.when((t >= 1) & (t <= n_total + 1))
    def _mixers():
        tile_a = t - 1
        tile_b = t - 2
        ja = lax.rem(tile_a, n_tiles)
        jb = lax.rem(tile_b, n_tiles)
        slot_a = lax.rem(tile_a, PIPE_SLOTS)
        slot_b = lax.rem(t + 1, PIPE_SLOTS)
        gate_a = tile_a & 1
        gate_b = t & 1

        @pl.when(ja == 0)
        def _start_of_sequence():
            carryt_sc[...] = jnp.zeros_like(carryt_sc)

        @pl.when(jb == 0)
        def _start_of_conv():
            u_sc[0:SUBLANES, :] = jnp.zeros((SUBLANES, CONV_DIM), F32)

        first = first_sm[0]
        for n_key_tiles in range(1, n_tiles + 1):
            pl.when((t >= 2) & (jb - first + 1 == n_key_tiles))(functools.partial(
                _attention_t, n_key_tiles, first, tm, kaug_sc, qta_sc, vt_sc, attt_sc))

        def proj(slot, col, width):
            return jnp.dot(hb_sc[slot], _std_columns(w_lo, w_hi, col, width), preferred_element_type=F32)

        def proj_t(row, height):
            return lax.dot_general(w_t[row:row + height, :], hb_sc[slot_a], (((1,), (1,)), ((), ())),
                                   preferred_element_type=F32)

        norm_out()

        vf_t = proj_t(_T_V, ATT_DIM + F_ROWS)
        qk_t = proj_t(_T_Q, 2 * ATT_DIM)
        kt = qk_t[ATT_DIM:2 * ATT_DIM, :]
        vt = vf_t[0:ATT_DIM, :]
        kt_ref[...] = kt
        vt_ref[...] = vt
        logf_t = _log_sigmoid(vf_t[ATT_DIM:ATT_DIM + F_ROWS, :] + b_f_col[...])
        lft_ref[...] = logf_t[:N_HEADS, :]
        sums = jnp.dot(jnp.concatenate(_split3(logf_t), axis=0), tri[...], preferred_element_type=F32)

        mix_a = _conv_branch(functools.partial(proj, slot_b), u_sc, conv_w, conv_ref, w_pa, tm)

        y_b = attt_sc[...].T * zb_sc[gate_b]
        mixed = mix_a + gm_sc[gate_b] * jnp.dot(y_b.astype(BF16), w_pb[...], preferred_element_type=F32)
        z_sc[...] = ALPHA * h_sc[slot_b] + jnp.dot(mixed.astype(BF16), w_out[...], preferred_element_type=F32)

        zb_sc[gate_a] = jax.nn.silu(proj(slot_a, _S_ZB, ATT_DIM))
        gm_sc[gate_a] = jax.nn.sigmoid(proj(slot_a, _S_GM, D_MODEL))

        c_t = sum(sums[r * F_ROWS:(r + 1) * F_ROWS, :] for r in range(N_SPLIT)) + carryt_sc[:, 0:1]
        carryt_sc[...] = jnp.broadcast_to(c_t[:, tm - 1:tm], carryt_sc.shape)
        c_pieces = [piece.astype(F32) for piece in _split3(c_t * LOG2E)]
        qt = qk_t[0:ATT_DIM, :] * (SCALE * LOG2E)
        for hd in range(N_HEADS):
            _store_feature_major(qta_sc, hd, qt[hd * HEAD_DIM:(hd + 1) * HEAD_DIM, :].astype(BF16),
                                 _extra_rows(c_pieces, hd, tm, key_side=False))
            vt_sc[ja, hd * V_ROWS:hd * V_ROWS + HEAD_DIM, :] = (
                vt[hd * HEAD_DIM:(hd + 1) * HEAD_DIM, :].astype(BF16))
        ones_and_pad = jnp.where(lax.broadcasted_iota(jnp.int32, (LANES - ONES_LANE, tm), 0) == 0, 1.0, 0.0)
        packed_t = jnp.concatenate([piece[0:N_HEADS, :] for piece in c_pieces] + [ones_and_pad], axis=0)
        ex_k = jnp.dot(packed_t.T.astype(BF16), route[:, AUG_DIM:], preferred_element_type=F32)
        _store_token_major(kaug_sc, pl.ds(pl.multiple_of(ja * tm, tm), tm), kt.T, ex_k)
        first_sm[0] = _dead_key_tiles(ja, qt, kt, c_t[:, tm - 1:tm] * LOG2E, kn_sc, ce_sc)

        norm_in()

    @pl.when(t == n_total + 2)
    def _drain():
        norm_out()


def _resident(shape):
    return pl.BlockSpec(shape, lambda *_: (0,) * len(shape), pipeline_mode=pl.Buffered(1))


def _run_prompt(x, params, *, tm):
    bsz, t, _ = x.shape
    n_tiles = t // tm
    assert n_tiles * tm == t
    n_total = bsz * n_tiles
    tile = lambda s, lag: jnp.clip(s - lag, 0, n_total - 1)
    per_tile = lambda width, lag: pl.BlockSpec(
        (None, tm, width), lambda s: (tile(s, lag) // n_tiles, tile(s, lag) % n_tiles, 0))
    per_tile_t = lambda rows: pl.BlockSpec(
        (None, rows, tm), lambda s: (tile(s, 1) // n_tiles, 0, tile(s, 1) % n_tiles))
    out_shape = (jax.ShapeDtypeStruct((bsz, t, D_MODEL), F32),
                 jax.ShapeDtypeStruct((bsz, ATT_DIM, t), F32),
                 jax.ShapeDtypeStruct((bsz, ATT_DIM, t), F32),
                 jax.ShapeDtypeStruct((bsz, N_HEADS, t), F32),
                 jax.ShapeDtypeStruct((bsz, CONV_W - 1, CONV_DIM), F32))
    out_specs = (per_tile(D_MODEL, 3), per_tile_t(ATT_DIM), per_tile_t(ATT_DIM), per_tile_t(N_HEADS),
                 pl.BlockSpec((None, CONV_W - 1, CONV_DIM), lambda s: (tile(s, 2) // n_tiles, 0, 0)))
    scratch = [
        pltpu.VMEM((t, AUG_DIM), BF16),
        pltpu.VMEM((n_tiles, N_HEADS * V_ROWS, tm), BF16),
        pltpu.VMEM((PIPE_SLOTS, tm, D_MODEL), F32),
        pltpu.VMEM((PIPE_SLOTS, tm, D_MODEL), BF16),
        pltpu.VMEM((tm, D_MODEL), F32),
        pltpu.VMEM((N_HEADS, HEAD_BLOCK, tm), BF16),
        pltpu.VMEM((ATT_DIM, tm), F32),
        pltpu.VMEM((2, tm, ATT_DIM), F32),
        pltpu.VMEM((2, tm, D_MODEL), F32),
        pltpu.VMEM((SUBLANES + tm, CONV_DIM), F32),
        pltpu.VMEM((F_ROWS, LANES), F32),
        pltpu.VMEM((F_ROWS, LANES), F32),
        pltpu.VMEM((F_ROWS, LANES), F32),
        pltpu.SMEM((1,), jnp.int32),
    ]
    return pl.pallas_call(
        functools.partial(_prompt_kernel, tm=tm, n_tiles=n_tiles, n_total=n_total),
        grid=(n_total + PIPE_SLOTS,),
        in_specs=[per_tile(D_MODEL, 0)] + [_resident(p.shape) for p in params],
        out_specs=out_specs,
        out_shape=out_shape,
        scratch_shapes=scratch,
        compiler_params=pltpu.CompilerParams(
            dimension_semantics=("arbitrary",),
            vmem_limit_bytes=VMEM_LIMIT_BYTES),
        name="trunk_prompt",
    )(x, *params)


def _sample_kernel(x_ref, conv0_ref, pkt_ref, pvt_ref, plft_ref,
                   ln_in_g, ln_in_b, w_lo, w_hi, w_fs, b_f_row, w_t, b_f_col, conv_w, w_pa, w_pb, w_out,
                   ln_g, ln_b, route, ones,
                   y_ref, kt_ref, vt_ref, lft_ref, conv_ref,
                   kta_sc, ktn_sc, hb_sc, qa_sc, att_sc, u_sc, h_sc, mixa_sc, cpast_sc, vtn_sc,
                   *, ts, n_past, group):
    stream = lax.rem(pl.program_id(0), group)
    first = SUBLANES - (CONV_W - 1)

    @pl.when(pl.program_id(0) == 0)
    def _zero_padding():
        kta_sc[...] = jnp.zeros_like(kta_sc)
        ktn_sc[...] = jnp.zeros_like(ktn_sc)

    def proj(col, width):
        return jnp.dot(hb_sc[...], _std_columns(w_lo, w_hi, col, width), preferred_element_type=F32)

    @pl.when(stream == 0)
    def _group_projections():
        h = _layer_norm(jnp.concatenate([x_ref[s] for s in range(group)], axis=0), ln_in_g[...], ln_in_b[...])
        h_sc[...] = h
        hb_sc[...] = h.astype(BF16)

        u = proj(_S_GC, CONV_DIM) * proj(_S_XA, CONV_DIM)
        convs = []
        for s in range(group):
            u_sc[s, first:SUBLANES, :] = conv0_ref[s]
            u_sc[s, SUBLANES:SUBLANES + ts, :] = u[s * ts:(s + 1) * ts, :]
            convs.append(sum(u_sc[s, first + i:first + i + ts, :] * conv_w[i:i + 1, :] for i in range(CONV_W)))
            conv_ref[s] = u_sc[s, first + ts:SUBLANES + ts, :]
        y_a = proj(_S_GB, CONV_DIM) * jnp.concatenate(convs, axis=0) * jax.nn.silu(proj(_S_ZA, CONV_DIM))
        mixa_sc[...] = jax.nn.sigmoid(proj(_S_GA, D_MODEL)) * jnp.dot(
            y_a.astype(BF16), w_pa[...], preferred_element_type=F32)

        def proj_t(row, height):
            return lax.dot_general(w_t[row:row + height, :], hb_sc[...], (((1,), (1,)), ((), ())),
                                   preferred_element_type=F32)

        kt = proj_t(_T_K, ATT_DIM)
        vf_t = proj_t(_T_V, ATT_DIM + F_ROWS)
        vt = vf_t[0:ATT_DIM, :]
        vtn_sc[...] = vt.astype(BF16)
        logf_t = _log_sigmoid(vf_t[ATT_DIM:ATT_DIM + SUBLANES, :] + b_f_col[0:SUBLANES, :])

        eye = (lax.broadcasted_iota(jnp.int32, (SUBLANES, LANES), 0)
               == lax.broadcasted_iota(jnp.int32, (SUBLANES, LANES), 1))
        lane_stream = lax.broadcasted_iota(jnp.int32, (SUBLANES, LANES), 1) // ts
        c_last_lanes = jnp.zeros((SUBLANES, LANES), F32)
        c_last_rows = []
        for s in range(group):
            kt_ref[s] = kt[:, s * ts:(s + 1) * ts]
            vt_ref[s] = vt[:, s * ts:(s + 1) * ts]
            lft_ref[s] = logf_t[:, s * ts:(s + 1) * ts]
            c_past = _cumsum(plft_ref[s], 1)
            cpast_sc[s] = c_past
            c_last = jnp.broadcast_to(c_past[:, n_past - 1:n_past], (SUBLANES, LANES))
            c_last_lanes = jnp.where(lane_stream == s, c_last, c_last_lanes)
            c_last_rows.append(jnp.broadcast_to(
                jnp.sum(jnp.where(eye, c_last, 0.0), axis=0, keepdims=True), (ts, LANES)))
        c_new_t = _cumsum(logf_t, 1, segment=ts) + c_last_lanes
        new_pieces = [piece.astype(F32) for piece in _split3(c_new_t)]
        for hd in range(N_HEADS):
            rows = slice(hd * HEAD_DIM, (hd + 1) * HEAD_DIM)
            _store_feature_major(ktn_sc, hd, kt[rows, :].astype(BF16),
                                 _extra_rows(new_pieces, hd, LANES, key_side=True))

        logf = _log_sigmoid(jnp.dot(hb_sc[...], w_fs[...], preferred_element_type=F32) + b_f_row[...])
        c_q = _cumsum(logf, 0, segment=ts) + jnp.concatenate(c_last_rows, axis=0)
        ex_q = _extra_lanes(c_q, route[:, :AUG_DIM]) + ones[:, :AUG_DIM]
        _store_token_major(qa_sc, slice(None), proj(_S_Q, ATT_DIM) * SCALE, ex_q)

    past_pieces = [piece.astype(F32) for piece in _split3(cpast_sc[stream])]
    for hd in range(N_HEADS):
        rows = slice(hd * HEAD_DIM, (hd + 1) * HEAD_DIM)
        _store_feature_major(kta_sc, hd, pkt_ref[rows, :].astype(BF16),
                             _extra_rows(past_pieces, hd, n_past, key_side=True))

    lane_key = lax.broadcasted_iota(jnp.int32, (ts, LANES), 1) - stream * ts
    causal = (lane_key >= 0) & (lane_key <= lax.broadcasted_iota(jnp.int32, (ts, LANES), 0))
    lane = lax.broadcasted_iota(jnp.int32, (ts, HEAD_BLOCK), 1)
    contract_lanes = (((1,), (1,)), ((), ()))
    stream_rows = pl.ds(pl.multiple_of(stream * ts, ts), ts)
    scores = []
    for hd in range(N_HEADS):
        qa_h = qa_sc[stream_rows, hd * HEAD_BLOCK:(hd + 1) * HEAD_BLOCK]
        scores.append((jnp.dot(qa_h, kta_sc[hd], preferred_element_type=F32),
                       jnp.dot(qa_h, ktn_sc[hd], preferred_element_type=F32)))
    for pair in range(N_HEADS // 2):
        rows = slice(pair * HEAD_BLOCK, (pair + 1) * HEAD_BLOCK)
        vt_past = pvt_ref[rows, :].astype(BF16)
        vt_new = vtn_sc[rows, :]
        probs, sums = [], []
        for hd in (2 * pair, 2 * pair + 1):
            s_past = scores[hd][0]
            s_new = jnp.where(causal, scores[hd][1], -jnp.inf)
            m = jnp.maximum(jnp.max(s_past, axis=-1, keepdims=True), jnp.max(s_new, axis=-1, keepdims=True))
            p_past = jnp.exp(s_past - m)
            p_new = jnp.exp(s_new - m)
            sums.append(jnp.sum(p_past, axis=-1, keepdims=True) + jnp.sum(p_new, axis=-1, keepdims=True))
            probs.append((p_past.astype(BF16), p_new.astype(BF16)))
        o = (lax.dot_general(jnp.concatenate([probs[0][0], probs[1][0]], axis=0), vt_past, contract_lanes,
                             preferred_element_type=F32)
             + lax.dot_general(jnp.concatenate([probs[0][1], probs[1][1]], axis=0), vt_new, contract_lanes,
                               preferred_element_type=F32))
        att_sc[stream_rows, rows] = jnp.where(lane < HEAD_DIM, o[0:ts, :] / sums[0], o[ts:2 * ts, :] / sums[1])

    @pl.when(stream == group - 1)
    def _group_merge():
        y = _merge_and_norm(att_sc[...], mixa_sc[...], h_sc[...], jax.nn.silu(proj(_S_ZB, ATT_DIM)),
                            jax.nn.sigmoid(proj(_S_GM, D_MODEL)), w_pb, w_out, ln_g, ln_b)
        for s in range(group):
            y_ref[s] = y[s * ts:(s + 1) * ts, :]


def _run_sample(x, conv0, pkt, pvt, plft, params):
    bsz, ts, _ = x.shape
    n_past = pkt.shape[2]
    assert LANES % ts == 0 and ts % (2 * SUBLANES) == 0
    group = LANES // ts
    assert bsz % group == 0
    per_batch = lambda rows, width: pl.BlockSpec((None, rows, width), lambda b: (b, 0, 0))
    per_group = lambda rows, width: pl.BlockSpec((group, rows, width), lambda b: (b // group, 0, 0))
    out_shape = (jax.ShapeDtypeStruct((bsz, ts, D_MODEL), F32),
                 jax.ShapeDtypeStruct((bsz, ATT_DIM, ts), F32),
                 jax.ShapeDtypeStruct((bsz, ATT_DIM, ts), F32),
                 jax.ShapeDtypeStruct((bsz, N_HEADS, ts), F32),
                 jax.ShapeDtypeStruct((bsz, CONV_W - 1, CONV_DIM), F32))
    out_specs = (per_group(ts, D_MODEL), per_group(ATT_DIM, ts), per_group(ATT_DIM, ts),
                 per_group(N_HEADS, ts), per_group(CONV_W - 1, CONV_DIM))
    in_specs = [per_group(ts, D_MODEL), per_group(CONV_W - 1, CONV_DIM), per_batch(ATT_DIM, n_past),
                per_batch(ATT_DIM, n_past), per_group(N_HEADS, n_past)]
    scratch = [
        pltpu.VMEM((N_HEADS, HEAD_BLOCK, n_past), BF16),
        pltpu.VMEM((N_HEADS, HEAD_BLOCK, LANES), BF16),
        pltpu.VMEM((LANES, D_MODEL), BF16),
        pltpu.VMEM((LANES, AUG_DIM), BF16),
        pltpu.VMEM((LANES, ATT_DIM), F32),
        pltpu.VMEM((group, SUBLANES + ts, CONV_DIM), F32),
        pltpu.VMEM((LANES, D_MODEL), F32),
        pltpu.VMEM((LANES, D_MODEL), F32),
        pltpu.VMEM((group, N_HEADS, n_past), F32),
        pltpu.VMEM((ATT_DIM, LANES), BF16),
    ]
    return pl.pallas_call(
        functools.partial(_sample_kernel, ts=ts, n_past=n_past, group=group),
        grid=(bsz,),
        in_specs=in_specs + [_resident(p.shape) for p in params],
        out_specs=out_specs,
        out_shape=out_shape,
        scratch_shapes=scratch,
        compiler_params=pltpu.CompilerParams(
            dimension_semantics=("arbitrary",),
            vmem_limit_bytes=VMEM_LIMIT_BYTES),
        name="trunk_sample",
    )(x, conv0, pkt, pvt, plft, *params)


WEIGHT_BLOCK = 512


def _relayout_kernel(a_ref, b_ref, c_ref, lo_ref, hi_ref, wt_ref, *, n_lo, skip, n_t, t_rows):
    step = pl.program_id(0)

    @pl.when(step < n_t)
    def _feature_major_rows():
        row = lax.broadcasted_iota(jnp.int32, c_ref.shape, 0) + step * WEIGHT_BLOCK
        wt_ref[...] = jnp.where(row < t_rows, c_ref[...], 0.0).astype(BF16)

    @pl.when(step < n_lo)
    def _before_gap():
        lo_ref[...] = a_ref[...].T.astype(BF16)

    @pl.when(step >= n_lo)
    def _after_gap():
        rows = jnp.concatenate([a_ref[skip:, :], b_ref[...]], axis=0)
        hi_ref[...] = rows.T.astype(BF16)


def _relayout_weights(w_rows):
    n_features = w_rows.shape[0]
    gap = _W_ZB - _W_V
    skip = gap % WEIGHT_BLOCK
    shift = gap // WEIGHT_BLOCK
    assert _W_V % WEIGHT_BLOCK == 0 and (n_features - _W_ZB) % WEIGHT_BLOCK == 0
    assert skip > 0 and skip % SUBLANES == 0 and WEIGHT_BLOCK % skip == 0
    n_lo = _W_V // WEIGHT_BLOCK
    n_hi = (n_features - _W_ZB) // WEIGHT_BLOCK
    a_index = lambda j: (jnp.where(j < n_lo, j, j + shift), 0)
    b_index = lambda j: ((j + shift + 1) * (WEIGHT_BLOCK // skip), 0)
    t_rows = _W_ZB - _W_Q
    n_t = pl.cdiv(_T_F + F_ROWS, WEIGHT_BLOCK)
    assert _W_Q % WEIGHT_BLOCK == 0 and n_t <= n_lo + n_hi and t_rows <= _T_F + F_ROWS
    t_index = lambda j: (jnp.minimum(j, n_t - 1), 0)
    c_index = lambda j: (_W_Q // WEIGHT_BLOCK + jnp.minimum(j, n_t - 1), 0)
    return pl.pallas_call(
        functools.partial(_relayout_kernel, n_lo=n_lo, skip=skip, n_t=n_t, t_rows=t_rows),
        grid=(n_lo + n_hi,),
        in_specs=[pl.BlockSpec((WEIGHT_BLOCK, D_MODEL), a_index), pl.BlockSpec((skip, D_MODEL), b_index),
                  pl.BlockSpec((WEIGHT_BLOCK, D_MODEL), c_index)],
        out_specs=(pl.BlockSpec((D_MODEL, WEIGHT_BLOCK), lambda j: (0, jnp.minimum(j, n_lo - 1))),
                   pl.BlockSpec((D_MODEL, WEIGHT_BLOCK), lambda j: (0, jnp.maximum(j - n_lo, 0))),
                   pl.BlockSpec((WEIGHT_BLOCK, D_MODEL), t_index)),
        out_shape=(jax.ShapeDtypeStruct((D_MODEL, _W_V), BF16),
                   jax.ShapeDtypeStruct((D_MODEL, n_features - _W_ZB), BF16),
                   jax.ShapeDtypeStruct((_T_F + F_ROWS, D_MODEL), BF16)),
        compiler_params=pltpu.CompilerParams(dimension_semantics=("arbitrary",)),
        name="weight_relayout",
    )(w_rows, w_rows, w_rows)


def kernel(x_prompt, x_sample, cache_k, cache_v, cache_logf, state_conv, ln_in_g, ln_in_b, w_in, b_f,
           conv_w, w_proj_a, w_proj_b, w_out, ln_g, ln_b):
    assert w_in.shape[0] == DEPTH
    route_np, ones_np = _routing_constants()
    w = w_in[0]
    row = lambda a: a.reshape(1, -1).astype(F32)
    tm = PROMPT_TILE
    w_lo, w_hi, w_t = _relayout_weights(w.T)
    p = dict(
        ln_in_g=row(ln_in_g), ln_in_b=row(ln_in_b),
        w_lo=w_lo, w_hi=w_hi,
        w_fs=jnp.pad(w[:, _W_F:_W_F + N_HEADS], ((0, 0), (0, LANES - N_HEADS))).astype(BF16),
        b_f_row=jnp.pad(row(b_f[0]), ((0, 0), (0, LANES - N_HEADS))),
        w_t=w_t,
        b_f_col=jnp.pad(b_f[0].astype(F32).reshape(N_HEADS, 1), ((0, F_ROWS - N_HEADS), (0, 0))),
        conv_w=conv_w[0].astype(F32),
        w_pa=w_proj_a[0].astype(BF16), w_pb=w_proj_b[0].astype(BF16), w_out=w_out[0].astype(BF16),
        ln_g=row(ln_g[0]), ln_b=row(ln_b[0]),
        route=jnp.asarray(route_np, BF16), ones=jnp.asarray(ones_np, F32),
        tri=jnp.asarray(np.triu(np.ones((tm, tm), np.float32)), BF16),
    )
    bs = x_sample.shape[0]
    n_past = cache_k.shape[2]
    feature_major = lambda a: a.transpose(0, 2, 3, 1).reshape(bs, ATT_DIM, n_past)

    y_p, kt_p, vt_p, lft_p, cv_p = _run_prompt(x_prompt, [p[name] for name in PROMPT_PARAMS], tm=tm)
    y_s, kt_s, vt_s, lft_s, cv_s = _run_sample(
        x_sample, state_conv[0], feature_major(cache_k[0]), feature_major(cache_v[0]),
        cache_logf[0].transpose(0, 2, 1), [p[name] for name in SAMPLE_PARAMS])

    heads_t = lambda a: a.reshape(a.shape[0], N_HEADS, HEAD_DIM, a.shape[2]).transpose(0, 3, 1, 2)[None]
    tokens_t = lambda a: a.transpose(0, 2, 1)[None]
    return (y_p, y_s, heads_t(kt_p), heads_t(vt_p), tokens_t(lft_p), cv_p[None],
            heads_t(kt_s), heads_t(vt_s), tokens_t(lft_s), cv_s[None])
```
